```python
import math
import jax
import jax.numpy as jnp
from jax import lax
import numpy as np

D_MODEL = 1024
BATCH = 4
SEQ = 4096
DEPTH = 2
DEC_BATCH = 2
DEC_SEQ = 8192
PAST_LEN = 128

EPS = 1e-6
MIX_WIDTH = D_MODEL // 2
N_BRANCH = 4
LRU_WIDTH = MIX_WIDTH
LRU_BLOCKS = 8
LRU_BLOCK = LRU_WIDTH // LRU_BLOCKS
LRU_CONV = 4
LRU_C = 8.0
HY_WIDTH = MIX_WIDTH
HY_ORDER = 2
HY_CONV = 3
HY_EMB = 33
HY_BANDS = (HY_EMB - 1) // 2
HY_HIDDEN = 64
HY_DECAY_TARGET = 1e-2
HY_FAST_PCT = 0.3
HY_SLOW_PCT = 1.5
GLA_HEADS = 4
GLA_DK = MIX_WIDTH // 8
GLA_DV = MIX_WIDTH // GLA_HEADS
GLA_WIDTH = GLA_HEADS * GLA_DV
GLA_RANK = 16
GLA_TAU = 16.0
GLA_CHUNK = 64
S5_WIDTH = MIX_WIDTH
S5_GROUP = 16
S5_GROUPS = S5_WIDTH // S5_GROUP
S5_STATE = 64
S5_DT_MIN = 1e-3
S5_DT_MAX = 1e-1
MOE_GROUPS = 4
MOE_EXPERTS = 4
MOE_TOPK = 2
MOE_FF = D_MODEL // 4
N_IN = 2 * LRU_WIDTH + (HY_ORDER + 1) * HY_WIDTH + 2 * GLA_HEADS * GLA_DK + 2 * GLA_WIDTH + 2 * GLA_RANK + S5_WIDTH

kernel_name = 'hybrid_bidir_lru_hyena_gla_s5_hiermoe'


def rms_norm(x, g):
    xf = x.astype(jnp.float32)
    y = xf * lax.rsqrt(jnp.mean(xf * xf, axis=-1, keepdims=True) + EPS)
    return (y * g.astype(jnp.float32)).astype(x.dtype)


def depthwise_conv(x, w, b, pad_left):
    k_width = w.shape[0]
    seq = x.shape[1]
    xp = jnp.pad(x, ((0, 0), (pad_left, k_width - 1 - pad_left), (0, 0)))
    y = b.astype(x.dtype) + xp[:, 0:seq] * w[0].astype(x.dtype)
    for j in range(1, k_width):
        y = y + xp[:, j:j + seq] * w[j].astype(x.dtype)
    return y


def _linear_combine(e1, e2):
    a1, b1 = e1
    a2, b2 = e2
    return a1 * a2, a2 * b1 + b2


def linear_scan(a, b, reverse):
    return lax.associative_scan(_linear_combine, (a, b), axis=1, reverse=reverse)[1]


def rg_lru(x, wa, ba, wx, bx, lam, reverse):
    f32 = jnp.float32
    bsz, seq, width = x.shape
    xb = x.reshape(bsz, seq, LRU_BLOCKS, LRU_BLOCK)
    gate_r = jax.nn.sigmoid(jnp.einsum('blni,nij->blnj', xb, wa.astype(f32)).reshape(bsz, seq, width) + ba.astype(f32))
    gate_i = jax.nn.sigmoid(jnp.einsum('blni,nij->blnj', xb, wx.astype(f32)).reshape(bsz, seq, width) + bx.astype(f32))
    log_a = -LRU_C * gate_r * jax.nn.softplus(-lam.astype(f32))
    a = jnp.exp(log_a)
    b = jnp.sqrt(-jnp.expm1(2.0 * log_a)) * gate_i * x
    return linear_scan(a, b, reverse)


def lru_mixer(xa, ga, conv_w, conv_b, wa, ba, wx, bx, lam):
    xc = depthwise_conv(xa.astype(jnp.float32), conv_w, conv_b, LRU_CONV // 2)
    h = rg_lru(xc, wa[0], ba[0], wx[0], bx[0], lam[0], False) + rg_lru(xc, wa[1], ba[1], wx[1], bx[1], lam[1], True)
    return h * jax.nn.gelu(ga.astype(jnp.float32))


def hyena_filters(seq, w1, b1, w2, b2, w3, freq):
    f32 = jnp.float32
    t = jnp.linspace(0.0, 1.0, seq, dtype=f32)[:, None]
    omega = (2.0 * math.pi / seq) * jnp.arange(seq, dtype=f32)
    bands = jnp.linspace(1e-4, HY_BANDS - 1, HY_BANDS, dtype=f32)
    ang = omega[:, None] * bands[None, :]
    z = jnp.concatenate([t, jnp.cos(ang), -jnp.sin(ang)], axis=-1)
    fr = freq.astype(f32)
    hid = jnp.sin(fr * (z @ w1.astype(f32) + b1.astype(f32)))
    hid = jnp.sin(fr * (hid @ w2.astype(f32) + b2.astype(f32)))
    filt = (hid @ w3.astype(f32)).reshape(seq, HY_ORDER, 2, HY_WIDTH)
    max_decay = math.log(HY_DECAY_TARGET) / HY_FAST_PCT
    min_decay = math.log(HY_DECAY_TARGET) / HY_SLOW_PCT
    deltas = jnp.abs(jnp.linspace(min_decay, max_decay, HY_WIDTH, dtype=f32))
    filt = filt * jnp.exp(-t * deltas)[:, None, None, :]
    fwd = filt[:, :, 0]
    bwd = filt[:seq - 1, :, 1][::-1]
    two_sided = jnp.concatenate([fwd, jnp.zeros((1, HY_ORDER, HY_WIDTH), f32), bwd], axis=0)
    two_sided = two_sided * lax.rsqrt(jnp.sum(two_sided * two_sided, axis=0, keepdims=True) + EPS)
    return jnp.fft.rfft(two_sided, axis=0)


def fft_conv(u, k_f, bias):
    seq = u.shape[1]
    u_f = jnp.fft.rfft(u, n=2 * seq, axis=1)
    y = jnp.fft.irfft(u_f * k_f[None], n=2 * seq, axis=1)[:, :seq]
    return y + u * bias.astype(jnp.float32)


def hyena_mixer(hy_in, conv_w, conv_b, w1, b1, w2, b2, w3, freq, bias):
    seq = hy_in.shape[1]
    zc = depthwise_conv(hy_in.astype(jnp.float32), conv_w, conv_b, HY_CONV // 2)
    v, x1, x2 = jnp.split(zc, 3, axis=-1)
    k_f = hyena_filters(seq, w1, b1, w2, b2, w3, freq)
    z = fft_conv(v, k_f[:, 0], bias[0]) * x1
    z = fft_conv(z, k_f[:, 1], bias[1]) * x2
    return z


def _state_combine(e1, e2):
    g1, s1 = e1
    g2, s2 = e2
    return g1 * g2, g2[..., None] * s1 + s2


def gla_chunked(q, k, v, log_alpha):
    bsz, seq = q.shape[:2]
    n_chunks = seq // GLA_CHUNK
    def chunk(t):
        return t.reshape(bsz, n_chunks, GLA_CHUNK, GLA_HEADS, t.shape[-1])
    q, k, v, log_alpha = chunk(q), chunk(k), chunk(v), chunk(log_alpha)
    b = jnp.cumsum(log_alpha, axis=2)
    b_last = b[:, :, -1:]
    q_e = q * jnp.exp(b)
    k_e = k * jnp.exp(-b)
    scores = jnp.einsum('bnihd,bnjhd->bnhij', q_e, k_e)
    mask = jnp.tril(jnp.ones((GLA_CHUNK, GLA_CHUNK), dtype=bool))
    scores = jnp.where(mask, scores, 0.0)
    o_intra = jnp.einsum('bnhij,bnjhv->bnihv', scores, v)
    d_state = jnp.einsum('bnjhd,bnjhv->bnhdv', k * jnp.exp(b_last - b), v)
    g_chunk = jnp.exp(b_last[:, :, 0])
    _, s_incl = lax.associative_scan(_state_combine, (g_chunk, d_state), axis=1)
    s_prev = jnp.concatenate([jnp.zeros_like(s_incl[:, :1]), s_incl[:, :-1]], axis=1)
    o_inter = jnp.einsum('bnihd,bnhdv->bnihv', q_e, s_prev)
    return (o_intra + o_inter).reshape(bsz, seq, GLA_HEADS, GLA_DV)


def gla_mixer(q, k, v, g, lr, wg2, bg, norm_g):
    f32 = jnp.float32
    bsz, seq = q.shape[:2]
    q = q.astype(f32).reshape(bsz, seq, GLA_HEADS, GLA_DK) * (GLA_DK ** -0.5)
    k = k.astype(f32).reshape(bsz, seq, GLA_HEADS, GLA_DK)
    v = v.astype(f32).reshape(bsz, seq, GLA_HEADS, GLA_DV)
    lr = lr.astype(f32).reshape(bsz, seq, 2, GLA_RANK)
    log_alpha = jax.nn.log_sigmoid(jnp.einsum('bltr,trk->bltk', lr, wg2.astype(f32)) + bg.astype(f32)) / GLA_TAU
    log_alpha = log_alpha.reshape(bsz, seq, 2, GLA_HEADS, GLA_DK)
    def flip(t):
        return jnp.flip(t, axis=1)
    o = gla_chunked(q, k, v, log_alpha[:, :, 0]) + flip(gla_chunked(flip(q), flip(k), flip(v), flip(log_alpha[:, :, 1])))
    o = o * lax.rsqrt(jnp.mean(o * o, axis=-1, keepdims=True) + EPS) * norm_g.astype(f32)
    return o.reshape(bsz, seq, GLA_WIDTH) * jax.nn.silu(g.astype(f32))


def s5_scan(ug, lam_re, lam_im, log_dt, b_re, b_im, c_re, c_im, reverse):
    f32 = jnp.float32
    lam = lax.complex(lam_re.astype(f32), lam_im.astype(f32))
    dt = jnp.exp(log_dt.astype(f32))[:, None]
    lam_bar = jnp.exp(lam * dt)
    b_bar = ((lam_bar - 1.0) / lam)[..., None] * lax.complex(b_re.astype(f32), b_im.astype(f32))
    bu = jnp.einsum('blgh,gph->blgp', ug.astype(jnp.complex64), b_bar)
    xs = linear_scan(jnp.broadcast_to(lam_bar, bu.shape), bu, reverse)
    c = lax.complex(c_re.astype(f32), c_im.astype(f32))
    return jnp.einsum('blgp,ghp->blgh', xs, c).real


def s5_mixer(u, lam_re, lam_im, log_dt, b_re, b_im, c_re, c_im, d, glu_w, glu_b):
    f32 = jnp.float32
    bsz, seq = u.shape[:2]
    ug = u.astype(f32).reshape(bsz, seq, S5_GROUPS, S5_GROUP)
    y = ug * d.astype(f32).reshape(S5_GROUPS, S5_GROUP)
    for direction in range(2):
        y = y + s5_scan(ug, lam_re[direction], lam_im[direction], log_dt[direction], b_re[direction],
                        b_im[direction], c_re[direction], c_im[direction], direction == 1)
    y = jax.nn.gelu(y.reshape(bsz, seq, S5_WIDTH))
    return y * jax.nn.sigmoid(y @ glu_w.astype(f32) + glu_b.astype(f32))


def mixer_layer(h, w, l):
    proj = h @ w['w_in'][l]
    sizes = (LRU_WIDTH, LRU_WIDTH, (HY_ORDER + 1) * HY_WIDTH, GLA_HEADS * GLA_DK, GLA_HEADS * GLA_DK,
             GLA_WIDTH, GLA_WIDTH, 2 * GLA_RANK, S5_WIDTH)
    cuts = np.cumsum(sizes)[:-1].tolist()
    xa, ga, hy_in, q, k, v, g, lr, s5_in = jnp.split(proj, cuts, axis=-1)
    y_a = lru_mixer(xa, ga, w['lru_conv_w'][l], w['lru_conv_b'][l], w['lru_wa'][l], w['lru_ba'][l],
                    w['lru_wx'][l], w['lru_bx'][l], w['lru_lambda'][l])
    y_b = hyena_mixer(hy_in, w['hy_conv_w'][l], w['hy_conv_b'][l], w['hy_w1'][l], w['hy_b1'][l], w['hy_w2'][l],
                      w['hy_b2'][l], w['hy_w3'][l], w['hy_freq'][l], w['hy_bias'][l])
    y_c = gla_mixer(q, k, v, g, lr, w['gla_wg2'][l], w['gla_bg'][l], w['gla_norm_g'][l])
    y_d = s5_mixer(s5_in, w['s5_lam_re'][l], w['s5_lam_im'][l], w['s5_log_dt'][l], w['s5_b_re'][l], w['s5_b_im'][l],
                   w['s5_c_re'][l], w['s5_c_im'][l], w['s5_d'][l], w['s5_glu_w'][l], w['s5_glu_b'][l])
    merged = jnp.zeros_like(h)
    for idx, y in enumerate((y_a, y_b, y_c, y_d)):
        gate = jax.nn.sigmoid(h @ w['w_gate'][l, idx] + w['b_gate'][l, idx])
        merged = merged + gate * (y.astype(h.dtype) @ w['w_branch'][l, idx])
    return merged @ w['w_out'][l]


def hier_moe(h, w_rg, b_rg, w_re, b_re, w_gate, w_up, w_down):
    f32 = jnp.float32
    hf = h.astype(f32)
    group_logits = hf @ w_rg.astype(f32) + b_rg.astype(f32)
    group_idx = jnp.argmax(group_logits, axis=-1)
    group_prob = jnp.take_along_axis(jax.nn.softmax(group_logits, axis=-1), group_idx[..., None], axis=-1)
    expert_logits = jnp.einsum('bld,gde->blge', hf, w_re.astype(f32)) + b_re.astype(f32)
    sel = jnp.take_along_axis(expert_logits, group_idx[..., None, None], axis=2)[:, :, 0]
    top_val, top_idx = lax.top_k(sel, MOE_TOPK)
    top_w = jax.nn.softmax(top_val, axis=-1) * group_prob
    expert_w = jnp.sum(jax.nn.one_hot(top_idx, MOE_EXPERTS, dtype=f32) * top_w[..., None], axis=-2)
    group_mask = jax.nn.one_hot(group_idx, MOE_GROUPS, dtype=f32)
    out = jnp.zeros_like(h)
    for grp in range(MOE_GROUPS):
        wg = (expert_w * group_mask[..., grp:grp + 1]).astype(h.dtype)
        gate = jnp.einsum('bld,edf->blef', h, w_gate[grp])
        up = jnp.einsum('bld,edf->blef', h, w_up[grp])
        act = jax.nn.silu(gate) * up * wg[..., None]
        out = out + jnp.einsum('blef,efd->bld', act, w_down[grp])
    return out


def encoder(x, w):
    for l in range(DEPTH):
        h = rms_norm(x, w['norm_mix_g'][l])
        x = x + mixer_layer(h, w, l)
        h = rms_norm(x, w['norm_ffn_g'][l])
        x = x + hier_moe(h, w['w_router_group'][l], w['b_router_group'][l], w['w_router_expert'][l],
                         w['b_router_expert'][l], w['w_e_gate'][l], w['w_e_up'][l], w['w_e_down'][l])
    return rms_norm(x, w['final_norm_g'])


def setup_inputs(seed: int = 0) -> dict:
    key = jax.random.key(seed)
    keys = iter(jax.random.split(key, 64))
    f32 = jnp.float32
    def normal(shape, scale):
        return scale * jax.random.normal(next(keys), shape, f32)
    def gain(shape):
        return 1.0 + 0.01 * jax.random.normal(next(keys), shape, f32)
    def uniform(shape, lo, hi):
        return jax.random.uniform(next(keys), shape, f32, lo, hi)
    nl = DEPTH
    lru_u = uniform((nl, 2, LRU_WIDTH), 0.9, 0.999)
    lru_a = lru_u ** (1.0 / LRU_C)
    s5_n = jnp.arange(S5_STATE, dtype=f32)
    return {
        'x_prompt': normal((BATCH, SEQ, D_MODEL), 1.0),
        'x_sample': normal((DEC_BATCH, DEC_SEQ, D_MODEL), 1.0),
        'norm_mix_g': gain((nl, D_MODEL)),
        'w_in': normal((nl, D_MODEL, N_IN), D_MODEL ** -0.5),
        'lru_conv_w': normal((nl, LRU_CONV, LRU_WIDTH), LRU_CONV ** -0.5),
        'lru_conv_b': normal((nl, LRU_WIDTH), 0.01),
        'lru_wa': normal((nl, 2, LRU_BLOCKS, LRU_BLOCK, LRU_BLOCK), LRU_BLOCK ** -0.5),
        'lru_ba': normal((nl, 2, LRU_WIDTH), 0.01),
        'lru_wx': normal((nl, 2, LRU_BLOCKS, LRU_BLOCK, LRU_BLOCK), LRU_BLOCK ** -0.5),
        'lru_bx': normal((nl, 2, LRU_WIDTH), 0.01),
        'lru_lambda': jnp.log(lru_a) - jnp.log1p(-lru_a),
        'hy_conv_w': normal((nl, HY_CONV, (HY_ORDER + 1) * HY_WIDTH), HY_CONV ** -0.5),
        'hy_conv_b': normal((nl, (HY_ORDER + 1) * HY_WIDTH), 0.01),
        'hy_w1': normal((nl, HY_EMB, HY_HIDDEN), HY_EMB ** -0.5),
        'hy_b1': normal((nl, HY_HIDDEN), 0.01),
        'hy_w2': normal((nl, HY_HIDDEN, HY_HIDDEN), HY_HIDDEN ** -0.5),
        'hy_b2': normal((nl, HY_HIDDEN), 0.01),
        'hy_w3': normal((nl, HY_HIDDEN, HY_ORDER * 2 * HY_WIDTH), HY_HIDDEN ** -0.5),
        'hy_freq': gain((nl, HY_HIDDEN)),
        'hy_bias': normal((nl, HY_ORDER, HY_WIDTH), 0.5),
        'gla_wg2': normal((nl, 2, GLA_RANK, GLA_HEADS * GLA_DK), GLA_RANK ** -0.5),
        'gla_bg': normal((nl, 2, GLA_HEADS * GLA_DK), 0.01),
        'gla_norm_g': gain((nl, GLA_DV)),
        's5_lam_re': -0.5 + normal((nl, 2, S5_GROUPS, S5_STATE), 0.01),
        's5_lam_im': math.pi * s5_n + normal((nl, 2, S5_GROUPS, S5_STATE), 0.01),
        's5_log_dt': uniform((nl, 2, S5_GROUPS), math.log(S5_DT_MIN), math.log(S5_DT_MAX)),
        's5_b_re': normal((nl, 2, S5_GROUPS, S5_STATE, S5_GROUP), (2 * S5_GROUP) ** -0.5),
        's5_b_im': normal((nl, 2, S5_GROUPS, S5_STATE, S5_GROUP), (2 * S5_GROUP) ** -0.5),
        's5_c_re': normal((nl, 2, S5_GROUPS, S5_GROUP, S5_STATE), S5_STATE ** -0.5),
        's5_c_im': normal((nl, 2, S5_GROUPS, S5_GROUP, S5_STATE), S5_STATE ** -0.5),
        's5_d': normal((nl, S5_WIDTH), 1.0),
        's5_glu_w': normal((nl, S5_WIDTH, S5_WIDTH), S5_WIDTH ** -0.5),
        's5_glu_b': normal((nl, S5_WIDTH), 0.01),
        'w_branch': normal((nl, N_BRANCH, MIX_WIDTH, D_MODEL), MIX_WIDTH ** -0.5),
        'w_gate': normal((nl, N_BRANCH, D_MODEL, D_MODEL), D_MODEL ** -0.5),
        'b_gate': normal((nl, N_BRANCH, D_MODEL), 0.01),
        'w_out': normal((nl, D_MODEL, D_MODEL), D_MODEL ** -0.5),
        'norm_ffn_g': gain((nl, D_MODEL)),
        'w_router_group': normal((nl, D_MODEL, MOE_GROUPS), D_MODEL ** -0.5),
        'b_router_group': normal((nl, MOE_GROUPS), 0.01),
        'w_router_expert': normal((nl, MOE_GROUPS, D_MODEL, MOE_EXPERTS), D_MODEL ** -0.5),
        'b_router_expert': normal((nl, MOE_GROUPS, MOE_EXPERTS), 0.01),
        'w_e_gate': normal((nl, MOE_GROUPS, MOE_EXPERTS, D_MODEL, MOE_FF), D_MODEL ** -0.5),
        'w_e_up': normal((nl, MOE_GROUPS, MOE_EXPERTS, D_MODEL, MOE_FF), D_MODEL ** -0.5),
        'w_e_down': normal((nl, MOE_GROUPS, MOE_EXPERTS, MOE_FF, D_MODEL), MOE_FF ** -0.5),
        'final_norm_g': gain((D_MODEL,)),
    }


def reference(x_prompt, x_sample, norm_mix_g, w_in, lru_conv_w, lru_conv_b, lru_wa, lru_ba, lru_wx, lru_bx,
              lru_lambda, hy_conv_w, hy_conv_b, hy_w1, hy_b1, hy_w2, hy_b2, hy_w3, hy_freq, hy_bias,
              gla_wg2, gla_bg, gla_norm_g, s5_lam_re, s5_lam_im, s5_log_dt, s5_b_re, s5_b_im, s5_c_re, s5_c_im,
              s5_d, s5_glu_w, s5_glu_b, w_branch, w_gate, b_gate, w_out, norm_ffn_g, w_router_group,
              b_router_group, w_router_expert, b_router_expert, w_e_gate, w_e_up, w_e_down, final_norm_g):
    w = dict(norm_mix_g=norm_mix_g, w_in=w_in, lru_conv_w=lru_conv_w, lru_conv_b=lru_conv_b, lru_wa=lru_wa,
             lru_ba=lru_ba, lru_wx=lru_wx, lru_bx=lru_bx, lru_lambda=lru_lambda, hy_conv_w=hy_conv_w,
             hy_conv_b=hy_conv_b, hy_w1=hy_w1, hy_b1=hy_b1, hy_w2=hy_w2, hy_b2=hy_b2, hy_w3=hy_w3,
             hy_freq=hy_freq, hy_bias=hy_bias, gla_wg2=gla_wg2, gla_bg=gla_bg, gla_norm_g=gla_norm_g,
             s5_lam_re=s5_lam_re, s5_lam_im=s5_lam_im, s5_log_dt=s5_log_dt, s5_b_re=s5_b_re, s5_b_im=s5_b_im,
             s5_c_re=s5_c_re, s5_c_im=s5_c_im, s5_d=s5_d, s5_glu_w=s5_glu_w, s5_glu_b=s5_glu_b,
             w_branch=w_branch, w_gate=w_gate, b_gate=b_gate, w_out=w_out, norm_ffn_g=norm_ffn_g,
             w_router_group=w_router_group, b_router_group=b_router_group, w_router_expert=w_router_expert,
             b_router_expert=b_router_expert, w_e_gate=w_e_gate, w_e_up=w_e_up, w_e_down=w_e_down,
             final_norm_g=final_norm_g)
    y_prompt = encoder(x_prompt, w)
    y_sample = encoder(x_sample, w)
    return (y_prompt, y_sample)
```

```python
import functools
import math

import numpy as np
import jax
import jax.numpy as jnp
from jax import lax
from jax.experimental import pallas as pl
from jax.experimental.pallas import tpu as pltpu

F32 = jnp.float32
BF16 = jnp.bfloat16

D_MODEL = 1024
DEPTH = 2
EPS = 1e-6
MIX_WIDTH = D_MODEL // 2
LRU_BLOCKS = 8
LRU_BLOCK = MIX_WIDTH // LRU_BLOCKS
LRU_CONV = 4
LRU_C = 8.0
HY_ORDER = 2
HY_CONV = 3
HY_EMB = 33
HY_BANDS = (HY_EMB - 1) // 2
HY_HIDDEN = 64
HY_DECAY_TARGET = 1e-2
HY_FAST_PCT = 0.3
HY_SLOW_PCT = 1.5
GLA_HEADS = 4
GLA_DK = MIX_WIDTH // 8
GLA_DV = MIX_WIDTH // GLA_HEADS
GLA_RANK = 16
GLA_TAU = 16.0
GLA_CHUNK = 64
S5_GROUP = 16
S5_GROUPS = MIX_WIDTH // S5_GROUP
S5_STATE = 64
MOE_GROUPS = 4
MOE_EXPERTS = 4
MOE_FF = D_MODEL // 4

V7X_LANES = 128
V7X_SUBLANES = 8
V7X_VMEM_BYTES = 64 * 2**20
MIB = 2**20

GLA_PACK = 4 * V7X_LANES * 2 + 512 + 512 + V7X_LANES
PK_LRU = (0, 1024)
PK_HY = (1024, 2560)
PK_GLA = (2560, 2560 + GLA_PACK)
PK_S5 = (PK_GLA[1], PK_GLA[1] + 512)
N_PACK = PK_S5[1]

TOK_TILE = 256
LRU_TILE = 256
CONV_TILE = 512
GLA_TILE = 256
S5_CHUNK = 16
FFT_N1 = 64
FFT_ROWS = 128
FFT_COLS = 256
FFT_KB = 11


def _cparams(sem, vmem_mib):
    return pltpu.CompilerParams(dimension_semantics=sem, vmem_limit_bytes=int(vmem_mib * MIB))


def _rms(x, g):
    return x * lax.rsqrt(jnp.mean(x * x, axis=-1, keepdims=True) + EPS) * g


def _sigmoid(x):
    return 1.0 / (1.0 + jnp.exp(-x))


def _softplus(x):
    return jnp.maximum(x, 0.0) + jnp.log(1.0 + jnp.exp(-jnp.abs(x)))


def _gelu_tanh(x):
    return 0.5 * x * (1.0 + jnp.tanh(math.sqrt(2.0 / math.pi) * (x + 0.044715 * (x * x * x))))


def _silu(x):
    return x * _sigmoid(x)


def _dot(a, b):
    return jnp.dot(a, b, preferred_element_type=F32)


def _dot_hi(a, b):
    return jnp.dot(a, b, preferred_element_type=F32, precision=lax.Precision.HIGHEST)


def _inproj_kernel(x_ref, g_ref, w_ref, lru_ref, hy_ref, gla_ref, s5_ref):
    h = _rms(x_ref[...], g_ref[...]).astype(BF16)
    lru_ref[...] = _dot(h, w_ref[:, PK_LRU[0]:PK_LRU[1]])
    hy_ref[...] = _dot(h, w_ref[:, PK_HY[0]:PK_HY[1]])
    gla_ref[...] = _dot(h, w_ref[:, PK_GLA[0]:PK_GLA[1]])
    s5_ref[...] = _dot(h, w_ref[:, PK_S5[0]:PK_S5[1]])


def _inproj(xf, g, w_pack):
    t = xf.shape[0]
    widths = [PK_LRU[1] - PK_LRU[0], PK_HY[1] - PK_HY[0], PK_GLA[1] - PK_GLA[0], PK_S5[1] - PK_S5[0]]
    return pl.pallas_call(
        _inproj_kernel,
        grid=(t // TOK_TILE,),
        in_specs=[pl.BlockSpec((TOK_TILE, D_MODEL), lambda i: (i, 0)),
                  pl.BlockSpec((1, D_MODEL), lambda i: (0, 0)),
                  pl.BlockSpec((D_MODEL, N_PACK), lambda i: (0, 0))],
        out_specs=[pl.BlockSpec((TOK_TILE, w), lambda i: (i, 0)) for w in widths],
        out_shape=[jax.ShapeDtypeStruct((t, w), F32) for w in widths],
        compiler_params=_cparams(("parallel",), 48),
        name="inproj",
    )(xf, g, w_pack)


def _fill_ext(ext_ref, main, prev8, next8, first, last, tile):
    ext_ref[0:8, :] = jnp.where(first, 0.0, prev8)
    ext_ref[8:8 + tile, :] = main
    ext_ref[8 + tile:16 + tile, :] = jnp.where(last, 0.0, next8)


def _linear_scan_tile(a, b, reverse):
    n = a.shape[0]
    row = lax.broadcasted_iota(jnp.int32, a.shape, 0)
    d = 1
    while d < n:
        if reverse:
            a_s = pltpu.roll(a, n - d, 0)
            b_s = pltpu.roll(b, n - d, 0)
            valid = row < n - d
        else:
            a_s = pltpu.roll(a, d, 0)
            b_s = pltpu.roll(b, d, 0)
            valid = row >= d
        b = jnp.where(valid, a * b_s + b, b)
        a = jnp.where(valid, a * a_s, a)
        d *= 2
    return a, b


def _lru_kernel(mf_ref, pf_ref, nf_ref, mb_ref, pb_ref, nb_ref, cw_ref, cb_ref, wg_ref, bg_ref, lam_ref,
                of_ref, ob_ref, extf_ref, extb_ref, carry_ref):
    c = pl.program_id(1)
    nc = pl.num_programs(1)
    tile = LRU_TILE

    @pl.when(c == 0)
    def _():
        carry_ref[...] = jnp.zeros_like(carry_ref)

    def one(m_ref, p_ref, n_ref, ext_ref, d, first, last, o_ref):
        x = m_ref[0, :, 0:MIX_WIDTH]
        ga = m_ref[0, :, MIX_WIDTH:2 * MIX_WIDTH]
        _fill_ext(ext_ref, x, p_ref[0], n_ref[0], first, last, tile)
        xc = cb_ref[...] + ext_ref[6:6 + tile, :] * cw_ref[0:1, :]
        xc = xc + ext_ref[7:7 + tile, :] * cw_ref[1:2, :]
        xc = xc + ext_ref[8:8 + tile, :] * cw_ref[2:3, :]
        xc = xc + ext_ref[9:9 + tile, :] * cw_ref[3:4, :]
        z = _dot(xc.astype(BF16), wg_ref[d]) + bg_ref[d]
        gate_r = _sigmoid(z[:, 0:MIX_WIDTH])
        gate_i = _sigmoid(z[:, MIX_WIDTH:2 * MIX_WIDTH])
        log_a = -LRU_C * gate_r * _softplus(-lam_ref[d])
        a = jnp.exp(log_a)
        b = jnp.sqrt(1.0 - a * a) * gate_i * xc
        a_cum, b_cum = _linear_scan_tile(a, b, reverse=(d == 1))
        h = b_cum + a_cum * carry_ref[d:d + 1, :]
        carry_ref[d:d + 1, :] = h[0:1, :] if d == 1 else h[tile - 1:tile, :]
        o_ref[0] = h * _gelu_tanh(ga)

    one(mf_ref, pf_ref, nf_ref, extf_ref, 0, c == 0, c == nc - 1, of_ref)
    one(mb_ref, pb_ref, nb_ref, extb_ref, 1, c == nc - 1, c == 0, ob_ref)


def _lru(lru_in, cw, cb, wg, bg, lam):
    bsz, seq, _ = lru_in.shape
    tile = LRU_TILE
    nc = seq // tile
    r8 = tile // 8
    last8 = seq // 8 - 1

    def fwd(c):
        return c

    def bwd(c):
        return nc - 1 - c

    def specs(ch):
        return [pl.BlockSpec((1, tile, 2 * MIX_WIDTH), lambda b, c: (b, ch(c), 0)),
                pl.BlockSpec((1, 8, MIX_WIDTH), lambda b, c: (b, jnp.maximum(ch(c) * r8 - 1, 0), 0)),
                pl.BlockSpec((1, 8, MIX_WIDTH), lambda b, c: (b, jnp.minimum((ch(c) + 1) * r8, last8), 0))]

    const = lambda shape: pl.BlockSpec(shape, lambda b, c: (0,) * len(shape))
    return pl.pallas_call(
        _lru_kernel,
        grid=(bsz, nc),
        in_specs=specs(fwd) + specs(bwd) + [const(cw.shape), const(cb.shape), const(wg.shape), const(bg.shape),
                                            const(lam.shape)],
        out_specs=[pl.BlockSpec((1, tile, MIX_WIDTH), lambda b, c: (b, c, 0)),
                   pl.BlockSpec((1, tile, MIX_WIDTH), lambda b, c: (b, nc - 1 - c, 0))],
        out_shape=[jax.ShapeDtypeStruct((bsz, seq, MIX_WIDTH), F32)] * 2,
        scratch_shapes=[pltpu.VMEM((tile + 16, MIX_WIDTH), F32), pltpu.VMEM((tile + 16, MIX_WIDTH), F32),
                        pltpu.VMEM((8, MIX_WIDTH), F32)],
        compiler_params=_cparams(("parallel", "arbitrary"), 40),
        name="lru",
    )(lru_in, lru_in, lru_in, lru_in, lru_in, lru_in, cw, cb, wg, bg, lam)


def _hyconv_kernel(m_ref, p_ref, n_ref, cw_ref, cb_ref, o_ref, ext_ref):
    c = pl.program_id(1)
    nc = pl.num_programs(1)
    tile = CONV_TILE
    _fill_ext(ext_ref, m_ref[0], p_ref[0], n_ref[0], c == 0, c == nc - 1, tile)
    y = cb_ref[...] + ext_ref[7:7 + tile, :] * cw_ref[0:1, :]
    y = y + ext_ref[8:8 + tile, :] * cw_ref[1:2, :]
    y = y + ext_ref[9:9 + tile, :] * cw_ref[2:3, :]
    o_ref[0] = y


def _hyconv(hy_in, cw, cb):
    bsz, seq, width = hy_in.shape
    tile = CONV_TILE
    r8 = tile // 8
    last8 = seq // 8 - 1
    const = lambda shape: pl.BlockSpec(shape, lambda b, c: (0,) * len(shape))
    return pl.pallas_call(
        _hyconv_kernel,
        grid=(bsz, seq // tile),
        in_specs=[pl.BlockSpec((1, tile, width), lambda b, c: (b, c, 0)),
                  pl.BlockSpec((1, 8, width), lambda b, c: (b, jnp.maximum(c * r8 - 1, 0), 0)),
                  pl.BlockSpec((1, 8, width), lambda b, c: (b, jnp.minimum((c + 1) * r8, last8), 0)),
                  const(cw.shape), const(cb.shape)],
        out_specs=pl.BlockSpec((1, tile, width), lambda b, c: (b, c, 0)),
        out_shape=jax.ShapeDtypeStruct((bsz, seq, width), F32),
        scratch_shapes=[pltpu.VMEM((tile + 16, width), F32)],
        compiler_params=_cparams(("parallel", "parallel"), 40),
        name="hyconv",
    )(hy_in, hy_in, hy_in, cw, cb)


class _FftPlan:
    def __init__(self, seq):
        n = 2 * seq
        n1 = FFT_N1
        n2 = n // n1
        assert n1 * n2 == n and n2 % 16 == 0
        h1 = n1 // 2 + 1
        nh = n1 // 2
        self.n, self.n1, self.n2, self.h1, self.nh = n, n1, n2, h1, nh
        self.rows = min(FFT_ROWS, n2)
        assert n2 % self.rows == 0 and h1 % FFT_KB == 0
        k1 = np.arange(h1, dtype=np.float64)
        m1 = np.arange(nh, dtype=np.float64)
        r = np.arange(8, dtype=np.float64)
        ang = -2.0 * np.pi * (k1[:, None, None] * m1[None, None, :] / n1 + r[None, :, None] * k1[:, None, None] / n)
        e = np.exp(1j * ang)
        fa = np.zeros((h1, 8, nh, 8), np.complex128)
        for rr in range(8):
            fa[:, rr, :, rr] = e[:, rr, :]
        fa = fa.reshape(h1 * 8, nh * 8)
        self.fa = jnp.asarray(np.concatenate([fa.real, fa.imag], axis=0), BF16)
        ck = np.where((k1 == 0) | (k1 == n1 // 2), 1.0, 2.0)
        ec = np.conj(e) * ck[:, None, None] / n
        fc = np.zeros((nh, 8, 2, h1, 8), np.float64)
        for rr in range(8):
            fc[:, rr, 0, :, rr] = ec[:, rr, :].real.T
            fc[:, rr, 1, :, rr] = -ec[:, rr, :].imag.T
        self.fc = jnp.asarray(fc.reshape(nh * 8, 2 * h1 * 8), BF16)
        rg = np.arange(n2 // 8, dtype=np.float64)
        tw = np.exp(-2j * np.pi * 8.0 * rg[None, :] * k1[:, None] / n)
        self.tw = jnp.asarray(np.concatenate([tw.real, tw.imag], axis=0), F32)
        q = np.arange(n2, dtype=np.float64)
        f2 = np.exp(-2j * np.pi * np.outer(q, q) / n2)
        fr, fi = f2.real, f2.imag
        self.gb = jnp.asarray(np.block([[fr, -fi], [fi, fr]]), BF16)
        self.gbi = jnp.asarray(np.block([[fr, fi], [-fi, fr]]), BF16)


def _fft_a_kernel(tw_ref, x_ref, fa_ref, a_ref, *, h1, nh, rows):
    rb = pl.program_id(2)
    cols = x_ref.shape[-1]

    def stage(rg):
        xg = x_ref[0, :, pl.ds(pl.multiple_of(rg * 8, 8), 8), :].reshape(nh * 8, cols).astype(BF16)
        return _dot(fa_ref[...], xg)

    def body(i, carry):
        p0 = stage(2 * i)
        p1 = stage(2 * i + 1)
        g0 = rb * (rows // 8) + 2 * i
        for k in range(h1):
            outs = []
            for p, g in ((p0, g0), (p1, g0 + 1)):
                pr = p[k * 8:(k + 1) * 8, :]
                pi = p[(h1 + k) * 8:(h1 + k + 1) * 8, :]
                tr = tw_ref[k, g]
                ti = tw_ref[h1 + k, g]
                outs.append((pr * tr - pi * ti, pr * ti + pi * tr))
            dst = pl.ds(pl.multiple_of(i * 16, 16), 16)
            a_ref[0, k, 0, dst, :] = jnp.concatenate([outs[0][0], outs[1][0]], axis=0).astype(BF16)
            a_ref[0, k, 1, dst, :] = jnp.concatenate([outs[0][1], outs[1][1]], axis=0).astype(BF16)
        return carry

    lax.fori_loop(0, rows // 16, body, 0)


def _fft_a(x4, plan, col_off=0, ncols=None):
    bq, nh, n2, width = x4.shape
    ncols = width if ncols is None else ncols
    cb0 = col_off // FFT_COLS
    rows = plan.rows
    kern = functools.partial(_fft_a_kernel, h1=plan.h1, nh=nh, rows=rows)
    return pl.pallas_call(
        kern,
        grid=(bq, ncols // FFT_COLS, n2 // rows),
        in_specs=[pl.BlockSpec(memory_space=pltpu.SMEM),
                  pl.BlockSpec((1, nh, rows, FFT_COLS), lambda b, c, r: (b, 0, r, cb0 + c)),
                  pl.BlockSpec(plan.fa.shape, lambda b, c, r: (0, 0))],
        out_specs=pl.BlockSpec((1, plan.h1, 2, rows, FFT_COLS), lambda b, c, r: (b, 0, 0, r, c)),
        out_shape=jax.ShapeDtypeStruct((bq, plan.h1, 2, n2, ncols), BF16),
        compiler_params=_cparams(("parallel", "parallel", "parallel"), 40),
        name="fft_outer_fwd",
    )(plan.tw, x4, plan.fa)


def _fft_mid_kernel(a_ref, kf_ref, gb_ref, gbi_ref, o_ref, *, n2):
    kb = a_ref.shape[1]
    cols = a_ref.shape[-1]
    for k in range(kb):
        y = _dot(gb_ref[...], a_ref[0, k].reshape(2 * n2, cols))
        yr, yi = y[0:n2, :], y[n2:2 * n2, :]
        kr, ki = kf_ref[0, k, 0], kf_ref[0, k, 1]
        z = jnp.concatenate([yr * kr - yi * ki, yr * ki + yi * kr], axis=0).astype(BF16)
        o_ref[0, k] = _dot(gbi_ref[...], z).reshape(2, n2, cols).astype(BF16)


def _fft_mid(a, kf, order, plan):
    bq, h1, _, n2, width = a.shape
    kern = functools.partial(_fft_mid_kernel, n2=n2)
    blk = (1, FFT_KB, 2, n2, FFT_COLS)
    return pl.pallas_call(
        kern,
        grid=(bq, width // FFT_COLS, h1 // FFT_KB),
        in_specs=[pl.BlockSpec(blk, lambda b, c, k: (b, k, 0, 0, c)),
                  pl.BlockSpec(blk, lambda b, c, k: (order, k, 0, 0, c)),
                  pl.BlockSpec(plan.gb.shape, lambda b, c, k: (0, 0)),
                  pl.BlockSpec(plan.gbi.shape, lambda b, c, k: (0, 0))],
        out_specs=pl.BlockSpec(blk, lambda b, c, k: (b, k, 0, 0, c)),
        out_shape=jax.ShapeDtypeStruct(a.shape, BF16),
        compiler_params=_cparams(("parallel", "parallel", "parallel"), 40),
        name="fft_inner_mul",
    )(a, kf, plan.gb, plan.gbi)


def _fft_c_kernel(tw_ref, b_ref, u_ref, g_ref, bias_ref, fc_ref, o_ref, *, h1, nh, rows):
    rb = pl.program_id(2)
    cols = o_ref.shape[-1]

    def body(i, carry):
        src = pl.ds(pl.multiple_of(i * 16, 16), 16)
        tiles = [[b_ref[0, k, p, src, :].astype(F32) for p in range(2)] for k in range(h1)]
        for half in range(2):
            g = rb * (rows // 8) + 2 * i + half
            re_rows, im_rows = [], []
            for k in range(h1):
                br = tiles[k][0][half * 8:(half + 1) * 8, :]
                bi = tiles[k][1][half * 8:(half + 1) * 8, :]
                tr = tw_ref[k, g]
                ti = tw_ref[h1 + k, g]
                re_rows.append(br * tr + bi * ti)
                im_rows.append(bi * tr - br * ti)
            s = jnp.concatenate(re_rows + im_rows, axis=0).astype(BF16)
            y = _dot(fc_ref[...], s).reshape(nh, 8, cols)
            dst = pl.ds(pl.multiple_of((2 * i + half) * 8, 8), 8)
            u = u_ref[0, :, dst, :]
            o_ref[0, :, dst, :] = (y + u * bias_ref[...]) * g_ref[0, :, dst, :]
        return carry

    lax.fori_loop(0, rows // 16, body, 0)


def _fft_c(bm, u4, u_off, g4, g_off, bias, plan):
    bq, h1, _, n2, width = bm.shape
    nh = plan.nh
    rows = plan.rows
    ub, gbk = u_off // FFT_COLS, g_off // FFT_COLS
    kern = functools.partial(_fft_c_kernel, h1=h1, nh=nh, rows=rows)
    xblk = (1, nh, rows, FFT_COLS)
    return pl.pallas_call(
        kern,
        grid=(bq, width // FFT_COLS, n2 // rows),
        in_specs=[pl.BlockSpec(memory_space=pltpu.SMEM),
                  pl.BlockSpec((1, h1, 2, rows, FFT_COLS), lambda b, c, r: (b, 0, 0, r, c)),
                  pl.BlockSpec(xblk, lambda b, c, r: (b, 0, r, ub + c)),
                  pl.BlockSpec(xblk, lambda b, c, r: (b, 0, r, gbk + c)),
                  pl.BlockSpec((1, FFT_COLS), lambda b, c, r: (0, c)),
                  pl.BlockSpec(plan.fc.shape, lambda b, c, r: (0, 0))],
        out_specs=pl.BlockSpec(xblk, lambda b, c, r: (b, 0, r, c)),
        out_shape=jax.ShapeDtypeStruct((bq, nh, n2, width), F32),
        compiler_params=_cparams(("parallel", "parallel", "parallel"), 48),
        name="fft_outer_inv",
    )(plan.tw, bm, u4, g4, bias, plan.fc)


def _hyfilt_gen_kernel(w1_ref, b1_ref, w2_ref, b2_ref, w3_ref, fr_ref, o_ref, ss_ref, *, seq):
    j = pl.program_id(0)
    rblk = pl.program_id(1)
    tile, cols = o_ref.shape[1], o_ref.shape[2]
    irow = lax.broadcasted_iota(jnp.int32, (tile, V7X_LANES), 0) + rblk * tile
    row = irow.astype(F32)
    lane = lax.broadcasted_iota(jnp.int32, (tile, V7X_LANES), 1)
    t = row / (seq - 1.0)
    omega = (2.0 * math.pi / seq) * row
    band_step = (HY_BANDS - 1 - 1e-4) / (HY_BANDS - 1)
    is_cos = (lane >= 1) & (lane <= HY_BANDS)
    is_sin = (lane > HY_BANDS) & (lane <= 2 * HY_BANDS)
    bidx = jnp.where(is_cos, lane - 1, lane - 1 - HY_BANDS).astype(F32)
    ang = omega * (1e-4 + band_step * bidx)
    z = jnp.where(lane == 0, t, jnp.where(is_cos, jnp.cos(ang), jnp.where(is_sin, -jnp.sin(ang), 0.0)))
    fr = fr_ref[...]
    hid = jnp.sin(fr * (_dot_hi(z, w1_ref[...]) + b1_ref[...]))
    hid = jnp.sin(fr * (_dot_hi(hid, w2_ref[...]) + b2_ref[...]))
    filt = _dot_hi(hid, w3_ref[...])
    chan = (lax.broadcasted_iota(jnp.int32, (1, cols), 1) + (j * cols) % MIX_WIDTH).astype(F32)
    max_decay = math.log(HY_DECAY_TARGET) / HY_FAST_PCT
    min_decay = math.log(HY_DECAY_TARGET) / HY_SLOW_PCT
    delta = jnp.abs(min_decay + (max_decay - min_decay) / (MIX_WIDTH - 1) * chan)
    filt = filt * jnp.exp(-t[:, 0:1] * delta)
    is_bwd = ((j * cols) // MIX_WIDTH) % 2 == 1
    rows_c = lax.broadcasted_iota(jnp.int32, (tile, cols), 0) + rblk * tile
    filt = jnp.where(is_bwd & (rows_c == seq - 1), 0.0, filt)
    o_ref[0] = filt

    @pl.when(rblk == 0)
    def _():
        ss_ref[...] = jnp.zeros_like(ss_ref)

    ss_ref[...] += jnp.sum(filt * filt, axis=0, keepdims=True)


def _hyfilt_gen(seq, w1p, b1, w2, b2, w3, freq):
    ncol = w3.shape[1]
    tile = min(seq, 1024)
    const = lambda shape: pl.BlockSpec(shape, lambda j, r: (0,) * len(shape))
    kern = functools.partial(_hyfilt_gen_kernel, seq=seq)
    return pl.pallas_call(
        kern,
        grid=(ncol // FFT_COLS, seq // tile),
        in_specs=[const(w1p.shape), const(b1.shape), const(w2.shape), const(b2.shape),
                  pl.BlockSpec((HY_HIDDEN, FFT_COLS), lambda j, r: (0, j)), const(freq.shape)],
        out_specs=[pl.BlockSpec((1, tile, FFT_COLS), lambda j, r: (0, r, j)),
                   pl.BlockSpec((1, FFT_COLS), lambda j, r: (0, j))],
        out_shape=[jax.ShapeDtypeStruct((1, seq, ncol), F32), jax.ShapeDtypeStruct((1, ncol), F32)],
        compiler_params=_cparams(("parallel", "arbitrary"), 40),
        name="hyena_filter_gen",
    )(w1p, b1, w2, b2, w3, freq)


def _hyfilt_spec_kernel(a0_ref, a1_ref, ss0_ref, ss1_ref, gb_ref, o_ref, *, n, n1, n2):
    kblk = pl.program_id(2)
    kb = a0_ref.shape[1]
    cols = a0_ref.shape[-1]
    scale = lax.rsqrt(ss0_ref[...] + ss1_ref[...] + EPS)
    k2 = lax.broadcasted_iota(jnp.int32, (n2, cols), 0).astype(F32)
    for k in range(kb):
        y0 = _dot(gb_ref[...], a0_ref[0, k].reshape(2 * n2, cols))
        y1 = _dot(gb_ref[...], a1_ref[0, k].reshape(2 * n2, cols))
        k1 = (kblk * kb + k).astype(F32)
        ang = (-2.0 * math.pi / n) * (k1 + n1 * k2)
        wr, wi = jnp.cos(ang), jnp.sin(ang)
        y1r, y1i = y1[0:n2, :], y1[n2:2 * n2, :]
        o_ref[0, k, 0] = (y0[0:n2, :] + (wr * y1r - wi * y1i)) * scale
        o_ref[0, k, 1] = (y0[n2:2 * n2, :] - (wr * y1i + wi * y1r)) * scale


def _hyfilt_spec(af, ss, plan):
    _, h1, _, n2, _ = af.shape
    ncb = MIX_WIDTH // FFT_COLS
    blk = (1, FFT_KB, 2, n2, FFT_COLS)
    kern = functools.partial(_hyfilt_spec_kernel, n=plan.n, n1=plan.n1, n2=n2)
    return pl.pallas_call(
        kern,
        grid=(HY_ORDER, ncb, h1 // FFT_KB),
        in_specs=[pl.BlockSpec(blk, lambda o, c, k: (0, k, 0, 0, o * 2 * ncb + c)),
                  pl.BlockSpec(blk, lambda o, c, k: (0, k, 0, 0, o * 2 * ncb + ncb + c)),
                  pl.BlockSpec((1, FFT_COLS), lambda o, c, k: (0, o * 2 * ncb + c)),
                  pl.BlockSpec((1, FFT_COLS), lambda o, c, k: (0, o * 2 * ncb + ncb + c)),
                  pl.BlockSpec(plan.gb.shape, lambda o, c, k: (0, 0))],
        out_specs=pl.BlockSpec(blk, lambda o, c, k: (o, k, 0, 0, c)),
        out_shape=jax.ShapeDtypeStruct((HY_ORDER, h1, 2, n2, MIX_WIDTH), F32),
        compiler_params=_cparams(("parallel", "parallel", "parallel"), 40),
        name="hyena_filter_spectrum",
    )(af, af, ss, ss, plan.gb)


def _hyena(hy_in, lw, plan):
    bsz, seq, _ = hy_in.shape
    zc = _hyconv(hy_in, lw["hy_conv_w"], lw["hy_conv_b"])
    filt, ss = _hyfilt_gen(seq, lw["hy_w1p"], lw["hy_b1"], lw["hy_w2"], lw["hy_b2"], lw["hy_w3"], lw["hy_freq"])
    af = _fft_a(filt.reshape(1, plan.nh, plan.n2, filt.shape[-1]), plan)
    kf = _hyfilt_spec(af, ss, plan)
    zc4 = zc.reshape(bsz, plan.nh, plan.n2, 3 * MIX_WIDTH)
    a = _fft_a(zc4, plan, col_off=0, ncols=MIX_WIDTH)
    bm = _fft_mid(a, kf, 0, plan)
    z1 = _fft_c(bm, zc4, 0, zc4, MIX_WIDTH, lw["hy_bias"][0:1], plan)
    a = _fft_a(z1, plan)
    bm = _fft_mid(a, kf, 1, plan)
    z2 = _fft_c(bm, z1, 0, zc4, 2 * MIX_WIDTH, lw["hy_bias"][1:2], plan)
    return z2.reshape(bsz, seq, MIX_WIDTH)


def _gla_kernel(*refs, reverse):
    if reverse:
        x_ref, of_ref, wla_ref, bla_ref, ng_ref, o_ref, st_ref = refs
    else:
        x_ref, wla_ref, bla_ref, o_ref, st_ref = refs
    c = pl.program_id(1)
    tile = GLA_TILE
    ck = GLA_CHUNK
    nck = tile // ck
    hw = V7X_LANES
    nh = GLA_HEADS

    @pl.when(c == 0)
    def _():
        st_ref[...] = jnp.zeros_like(st_ref)

    q = x_ref[0, :, 0:nh * hw] * (GLA_DK ** -0.5)
    k = x_ref[0, :, nh * hw:2 * nh * hw]
    v = x_ref[0, :, 2 * nh * hw:3 * nh * hw]
    lr = x_ref[0, :, 4 * nh * hw:4 * nh * hw + hw]
    zl = _dot(lr.astype(BF16), wla_ref[...]) + bla_ref[...]
    la = (jnp.minimum(zl, 0.0) - jnp.log(1.0 + jnp.exp(-jnp.abs(zl)))) / GLA_TAU

    row = lax.broadcasted_iota(jnp.int32, la.shape, 0) % ck
    bcum = la
    d = 1
    while d < ck:
        if reverse:
            bcum = bcum + jnp.where(row < ck - d, pltpu.roll(bcum, tile - d, 0), 0.0)
        else:
            bcum = bcum + jnp.where(row >= d, pltpu.roll(bcum, d, 0), 0.0)
        d *= 2
    b3 = bcum.reshape(nck, ck, nh * hw)
    blast = b3[:, 0:1, :] if reverse else b3[:, ck - 1:ck, :]
    q_e = (q * jnp.exp(bcum)).astype(BF16)
    k_e = (k * jnp.exp(-bcum)).astype(BF16)
    k_d = (k.reshape(nck, ck, nh * hw) * jnp.exp(blast - b3)).reshape(tile, nh * hw).astype(BF16)
    gch = jnp.exp(blast)
    vb = v.astype(BF16)

    ri = lax.broadcasted_iota(jnp.int32, (ck, ck), 0)
    ci = lax.broadcasted_iota(jnp.int32, (ck, ck), 1)
    mask = (ri <= ci) if reverse else (ri >= ci)
    order = range(nck - 1, -1, -1) if reverse else range(nck)
    outs = [None] * nck
    for n in order:
        rs = slice(n * ck, (n + 1) * ck)
        heads = []
        for h in range(nh):
            ls = slice(h * hw, (h + 1) * hw)
            qe, ke, kd, vh = q_e[rs, ls], k_e[rs, ls], k_d[rs, ls], vb[rs, ls]
            st = st_ref[h]
            sc = lax.dot_general(qe, ke, (((1,), (1,)), ((), ())), preferred_element_type=F32)
            sc = jnp.where(mask, sc, 0.0).astype(BF16)
            o = _dot(sc, vh) + lax.dot_general(qe, st.astype(BF16), (((1,), (1,)), ((), ())),
                                               preferred_element_type=F32)
            upd = lax.dot_general(vh, kd, (((0,), (0,)), ((), ())), preferred_element_type=F32)
            st_ref[h] = st * gch[n, :, ls] + upd
            heads.append(o)
        outs[n] = jnp.concatenate(heads, axis=1)
    o_dir = jnp.concatenate(outs, axis=0)
    if not reverse:
        o_ref[0] = o_dir
        return
    o = of_ref[0] + o_dir
    g = x_ref[0, :, 3 * nh * hw:4 * nh * hw]
    normed = []
    for h in range(nh):
        oh = o[:, h * hw:(h + 1) * hw]
        normed.append(oh * lax.rsqrt(jnp.mean(oh * oh, axis=-1, keepdims=True) + EPS))
    o = jnp.concatenate(normed, axis=1) * ng_ref[...]
    o_ref[0] = o * _silu(g)


def _gla(gla_in, wla, bla, norm_g):
    bsz, seq, width = gla_in.shape
    tile = GLA_TILE
    nc = seq // tile
    const = lambda shape: pl.BlockSpec(shape, lambda b, c: (0,) * len(shape))
    out_shape = jax.ShapeDtypeStruct((bsz, seq, MIX_WIDTH), F32)
    scratch = [pltpu.VMEM((GLA_HEADS, GLA_DV, V7X_LANES), F32)]
    o_f = pl.pallas_call(
        functools.partial(_gla_kernel, reverse=False),
        grid=(bsz, nc),
        in_specs=[pl.BlockSpec((1, tile, width), lambda b, c: (b, c, 0)), const(wla.shape[1:]), const(bla.shape[1:])],
        out_specs=pl.BlockSpec((1, tile, MIX_WIDTH), lambda b, c: (b, c, 0)),
        out_shape=out_shape,
        scratch_shapes=scratch,
        compiler_params=_cparams(("parallel", "arbitrary"), 40),
        name="gla_fwd",
    )(gla_in, wla[0], bla[0])
    return pl.pallas_call(
        functools.partial(_gla_kernel, reverse=True),
        grid=(bsz, nc),
        in_specs=[pl.BlockSpec((1, tile, width), lambda b, c: (b, nc - 1 - c, 0)),
                  pl.BlockSpec((1, tile, MIX_WIDTH), lambda b, c: (b, nc - 1 - c, 0)),
                  const(wla.shape[1:]), const(bla.shape[1:]), const(norm_g.shape)],
        out_specs=pl.BlockSpec((1, tile, MIX_WIDTH), lambda b, c: (b, nc - 1 - c, 0)),
        out_shape=out_shape,
        scratch_shapes=scratch,
        compiler_params=_cparams(("parallel", "arbitrary"), 40),
        name="gla_bwd",
    )(gla_in, o_f, wla[1], bla[1], norm_g)


def _s5_kernel(u_ref, t_ref, e_ref, o_ref_w, mu_ref, y_ref):
    nrow = u_ref.shape[2]
    half = V7X_LANES
    u = u_ref[0, 0].astype(BF16)
    y = _dot(u, t_ref[0])
    he = _dot(u, e_ref[0])
    row = lax.broadcasted_iota(jnp.int32, (nrow, half), 0)
    states = []
    for d in range(2):
        h = he[:, d * half:(d + 1) * half]
        step = 1
        s = 0
        while step < nrow:
            if d == 0:
                hs = jnp.where(row >= step, pltpu.roll(h, step, 0), 0.0)
            else:
                hs = jnp.where(row < nrow - step, pltpu.roll(h, nrow - step, 0), 0.0)
            h = h + hs * mu_ref[0, d, s, 0:1, :] + pltpu.roll(hs, half // 2, 1) * mu_ref[0, d, s, 1:2, :]
            step *= 2
            s += 1
        if d == 0:
            h = jnp.where(row >= 1, pltpu.roll(h, 1, 0), 0.0)
        else:
            h = jnp.where(row < nrow - 1, pltpu.roll(h, nrow - 1, 0), 0.0)
        states.append(h)
    hp = jnp.concatenate(states, axis=1).astype(BF16)
    y_ref[0, 0] = y + _dot(hp, o_ref_w[0])


def _s5(u2, tmat, emat, omat, mu):
    ng, bsz, nrow, width = u2.shape
    wspec = lambda arr: pl.BlockSpec((1,) + arr.shape[1:], lambda g, b: (g,) + (0,) * (arr.ndim - 1))
    return pl.pallas_call(
        _s5_kernel,
        grid=(ng, bsz),
        in_specs=[pl.BlockSpec((1, 1, nrow, width), lambda g, b: (g, b, 0, 0)),
                  wspec(tmat), wspec(emat), wspec(omat), wspec(mu)],
        out_specs=pl.BlockSpec((1, 1, nrow, width), lambda g, b: (g, b, 0, 0)),
        out_shape=jax.ShapeDtypeStruct(u2.shape, F32),
        compiler_params=_cparams(("parallel", "parallel"), 32),
        name="s5",
    )(u2, tmat, emat, omat, mu)


def _s5_tables(lam_re, lam_im, log_dt, b_re, b_im, c_re, c_im, nrow_max_log2):
    ch = S5_CHUNK
    hi = lax.Precision.HIGHEST
    cmul = lambda x, y: (x[0] * y[0] - x[1] * y[1], x[0] * y[1] + x[1] * y[0])
    lr_, li_ = lam_re.astype(F32), lam_im.astype(F32)
    dt = jnp.exp(log_dt.astype(F32))[..., None]
    ar, ai = lr_ * dt, li_ * dt

    def lam_pow(tau):
        t = jnp.asarray(tau, F32)
        t = t.reshape(t.shape + (1,) * 3)
        mag = jnp.exp(t * ar)
        return mag * jnp.cos(t * ai), mag * jnp.sin(t * ai)

    lb = lam_pow(jnp.ones((), F32))
    num = (lb[0] - 1.0, lb[1])
    den = lr_ * lr_ + li_ * li_
    ratio = ((num[0] * lr_ + num[1] * li_) / den, (num[1] * lr_ - num[0] * li_) / den)
    b_bar = cmul((ratio[0][..., None], ratio[1][..., None]), (b_re.astype(F32), b_im.astype(F32)))
    cc = (c_re.astype(F32), c_im.astype(F32))

    lp = lam_pow(jnp.arange(ch + 1, dtype=F32))
    cb = cmul((cc[0][..., None], cc[1][..., None]), (b_bar[0][:, :, None], b_bar[1][:, :, None]))
    kern = (jnp.einsum("dghpk,tdgp->tdghk", cb[0], lp[0][:ch], precision=hi)
            - jnp.einsum("dghpk,tdgp->tdghk", cb[1], lp[1][:ch], precision=hi))
    idx = jnp.arange(ch)
    lag = idx[None, :] - idx[:, None]
    kf = jnp.where((lag >= 0)[..., None, None, None], kern[jnp.clip(lag, 0, ch - 1), 0], 0.0)
    kb = jnp.where((lag <= 0)[..., None, None, None], kern[jnp.clip(-lag, 0, ch - 1), 1], 0.0)
    tmat = jnp.transpose(kf + kb, (2, 0, 4, 1, 3)).reshape(S5_GROUPS, ch * S5_GROUP, ch * S5_GROUP)

    def e_cols(powers, d):
        e = cmul((powers[0][..., None], powers[1][..., None]), (b_bar[0][d][None], b_bar[1][d][None]))
        return [jnp.transpose(x, (1, 0, 3, 2)).reshape(S5_GROUPS, ch * S5_GROUP, S5_STATE) for x in e]

    emat = jnp.concatenate(e_cols((lp[0][ch - 1 - idx, 0], lp[1][ch - 1 - idx, 0]), 0)
                           + e_cols((lp[0][idx, 1], lp[1][idx, 1]), 1), axis=-1)

    def o_rows(powers, d):
        o = cmul((cc[0][d][None], cc[1][d][None]), (powers[0][:, :, None, :], powers[1][:, :, None, :]))
        o = [jnp.transpose(x, (1, 3, 0, 2)).reshape(S5_GROUPS, S5_STATE, ch * S5_GROUP) for x in o]
        return [o[0], -o[1]]

    omat = jnp.concatenate(o_rows((lp[0][idx + 1, 0], lp[1][idx + 1, 0]), 0)
                           + o_rows((lp[0][ch - idx, 1], lp[1][ch - idx, 1]), 1), axis=1)
    mp = lam_pow(ch * (2.0 ** jnp.arange(nrow_max_log2, dtype=F32)))
    mu = jnp.stack([jnp.concatenate([mp[0], mp[0]], axis=-1), jnp.concatenate([-mp[1], mp[1]], axis=-1)],
                   axis=-2)
    mu = jnp.transpose(mu, (2, 1, 0, 3, 4))
    return tmat.astype(BF16), emat.astype(BF16), omat.astype(BF16), mu.astype(F32)


def _s5_mixer(s5_in, tables):
    bsz, seq, _ = s5_in.shape
    ch = S5_CHUNK
    u2 = s5_in.reshape(bsz, seq // ch, ch, S5_GROUPS, S5_GROUP)
    u2 = jnp.transpose(u2, (3, 0, 1, 2, 4)).reshape(S5_GROUPS, bsz, seq // ch, ch * S5_GROUP)
    y2 = _s5(u2, *tables)
    y2 = jnp.transpose(y2.reshape(S5_GROUPS, bsz, seq // ch, ch, S5_GROUP), (1, 2, 3, 0, 4))
    return y2.reshape(bsz, seq, MIX_WIDTH)


def _merge_kernel(x_ref, ng_ref, yaf_ref, yab_ref, yb_ref, yc_ref, y5_ref, u5_ref, d5_ref, gluw_ref, glub_ref,
                  wgate_ref, bgate_ref, wbr_ref, wout_ref, o_ref):
    x = x_ref[...]
    h = _rms(x, ng_ref[...]).astype(BF16)
    y_d = _gelu_tanh(u5_ref[...] * d5_ref[...] + y5_ref[...])
    y_d = y_d * _sigmoid(_dot(y_d.astype(BF16), gluw_ref[...]) + glub_ref[...])
    branches = (yaf_ref[...] + yab_ref[...], yb_ref[...], yc_ref[...], y_d)
    merged = jnp.zeros(x.shape, F32)
    for i, y in enumerate(branches):
        gate = _sigmoid(_dot(h, wgate_ref[i]) + bgate_ref[i])
        merged = merged + gate * _dot(y.astype(BF16), wbr_ref[i])
    o_ref[...] = x + _dot(merged.astype(BF16), wout_ref[...])


def _merge(xf, lw, yaf, yab, yb, yc, y5, u5):
    t = xf.shape[0]
    tok = lambda w: pl.BlockSpec((TOK_TILE, w), lambda i: (i, 0))
    const = lambda arr: pl.BlockSpec(arr.shape, lambda i: (0,) * arr.ndim, pipeline_mode=pl.Buffered(1))
    weights = [lw["s5_d"], lw["s5_glu_w"], lw["s5_glu_b"], lw["w_gate"], lw["b_gate"], lw["w_branch"], lw["w_out"]]
    return pl.pallas_call(
        _merge_kernel,
        grid=(t // TOK_TILE,),
        in_specs=[tok(D_MODEL), const(lw["norm_mix_g"])] + [tok(MIX_WIDTH)] * 6 + [const(w) for w in weights],
        out_specs=tok(D_MODEL),
        out_shape=jax.ShapeDtypeStruct((t, D_MODEL), F32),
        compiler_params=_cparams(("parallel",), 48),
        name="merge",
    )(xf, lw["norm_mix_g"], yaf, yab, yb, yc, y5, u5, *weights)


def _moe_kernel(x_ref, ng_ref, wr_ref, br_ref, wg_ref, wu_ref, wd_ref, fg_ref, o_ref, *, final):
    x = x_ref[...]
    hf = _rms(x, ng_ref[...])
    logits = _dot_hi(hf, wr_ref[...]) + br_ref[...]
    lane = lax.broadcasted_iota(jnp.int32, logits.shape, 1).astype(F32)
    neg = -jnp.inf
    big = float(V7X_LANES)
    ng, ne = MOE_GROUPS, MOE_EXPERTS
    gl = jnp.where(lane < ng, logits, neg)
    gmax = jnp.max(gl, axis=1, keepdims=True)
    gidx = jnp.min(jnp.where(gl == gmax, lane, big), axis=1, keepdims=True)
    gprob = 1.0 / jnp.sum(jnp.exp(gl - gmax), axis=1, keepdims=True)
    lo = ng + ne * gidx
    sel = (lane >= lo) & (lane < lo + ne)
    m1 = jnp.max(jnp.where(sel, logits, neg), axis=1, keepdims=True)
    i1 = jnp.min(jnp.where(sel & (logits == m1), lane, big), axis=1, keepdims=True)
    sel2 = sel & (lane != i1)
    m2 = jnp.max(jnp.where(sel2, logits, neg), axis=1, keepdims=True)
    i2 = jnp.min(jnp.where(sel2 & (logits == m2), lane, big), axis=1, keepdims=True)
    e2 = jnp.exp(m2 - m1)
    w1 = gprob / (1.0 + e2)
    w2 = gprob * e2 / (1.0 + e2)
    ew = jnp.where(lane == i1, w1, jnp.where(lane == i2, w2, 0.0))
    h = hf.astype(BF16)
    acc = jnp.zeros(x.shape, F32)
    for idx in range(ng * ne):
        wcol = jnp.sum(jnp.where(lane == ng + idx, ew, 0.0), axis=1, keepdims=True)
        gate = _dot(h, wg_ref[idx])
        up = _dot(h, wu_ref[idx])
        act = _silu(gate) * up * wcol
        acc = acc + _dot(act.astype(BF16), wd_ref[idx])
    y = x + acc
    if final:
        y = _rms(y, fg_ref[...])
    o_ref[...] = y


def _moe(xf, lw, final_g, final):
    t = xf.shape[0]
    tok = pl.BlockSpec((TOK_TILE, D_MODEL), lambda i: (i, 0))
    const = lambda arr: pl.BlockSpec(arr.shape, lambda i: (0,) * arr.ndim, pipeline_mode=pl.Buffered(1))
    weights = [lw["norm_ffn_g"], lw["w_router"], lw["b_router"], lw["w_e_gate"], lw["w_e_up"], lw["w_e_down"],
               final_g]
    return pl.pallas_call(
        functools.partial(_moe_kernel, final=final),
        grid=(t // TOK_TILE,),
        in_specs=[tok] + [const(w) for w in weights],
        out_specs=tok,
        out_shape=jax.ShapeDtypeStruct((t, D_MODEL), F32),
        compiler_params=_cparams(("parallel",), 56),
        name="moe",
    )(xf, *weights)


def _block_diag(blocks):
    nb, bs, _ = blocks.shape
    eye = jnp.eye(nb, dtype=blocks.dtype)
    return jnp.einsum("nij,nm->nimj", blocks, eye).reshape(nb * bs, nb * bs)


def _pad_heads(w, axis=-1):
    shape = w.shape[:-1] + (GLA_HEADS, GLA_DK)
    w = w.reshape(shape)
    pad = [(0, 0)] * (w.ndim - 1) + [(0, V7X_LANES - GLA_DK)]
    return jnp.pad(w, pad).reshape(w.shape[:-2] + (GLA_HEADS * V7X_LANES,))


def _prep_layer(w, l, nrow_max_log2):
    f = lambda name: w[name][l]
    w_in = f("w_in")
    cuts = np.cumsum([MIX_WIDTH, MIX_WIDTH, 3 * MIX_WIDTH, GLA_HEADS * GLA_DK, GLA_HEADS * GLA_DK, MIX_WIDTH,
                      MIX_WIDTH, 2 * GLA_RANK]).tolist()
    xa, ga, hy, q, k, v, g, lr, s5 = jnp.split(w_in, cuts, axis=-1)
    lr = jnp.pad(lr, ((0, 0), (0, V7X_LANES - 2 * GLA_RANK)))
    w_pack = jnp.concatenate([xa, ga, hy, _pad_heads(q), _pad_heads(k), v, g, lr, s5], axis=-1).astype(BF16)
    assert w_pack.shape[1] == N_PACK
    lw = {"w_pack": w_pack, "norm_mix_g": f("norm_mix_g")[None]}
    lw["lru_conv_w"] = f("lru_conv_w")
    lw["lru_conv_b"] = f("lru_conv_b")[None]
    wa, wx = f("lru_wa"), f("lru_wx")
    lw["lru_wg"] = jnp.stack([jnp.concatenate([_block_diag(wa[d]), _block_diag(wx[d])], axis=1)
                              for d in range(2)]).astype(BF16)
    lw["lru_bg"] = jnp.concatenate([f("lru_ba"), f("lru_bx")], axis=-1)[:, None, :]
    lw["lru_lam"] = f("lru_lambda")[:, None, :]
    lw["hy_conv_w"] = f("hy_conv_w")
    lw["hy_conv_b"] = f("hy_conv_b")[None]
    lw["hy_w1p"] = jnp.pad(f("hy_w1"), ((0, V7X_LANES - HY_EMB), (0, 0)))
    lw["hy_b1"] = f("hy_b1")[None]
    lw["hy_w2"] = f("hy_w2")
    lw["hy_b2"] = f("hy_b2")[None]
    lw["hy_w3"] = f("hy_w3")
    lw["hy_freq"] = f("hy_freq")[None]
    lw["hy_bias"] = f("hy_bias")
    wg2 = _pad_heads(f("gla_wg2"))
    wla = jnp.zeros((2, V7X_LANES, GLA_HEADS * V7X_LANES), F32)
    wla = wla.at[0, 0:GLA_RANK].set(wg2[0]).at[1, GLA_RANK:2 * GLA_RANK].set(wg2[1])
    lw["gla_wla"] = wla.astype(BF16)
    lw["gla_bla"] = _pad_heads(f("gla_bg"))[:, None, :]
    lw["gla_norm_g"] = jnp.tile(f("gla_norm_g"), GLA_HEADS)[None]
    lw["s5_tables"] = _s5_tables(f("s5_lam_re"), f("s5_lam_im"), f("s5_log_dt"), f("s5_b_re"), f("s5_b_im"),
                                 f("s5_c_re"), f("s5_c_im"), nrow_max_log2)
    lw["s5_d"] = f("s5_d")[None]
    lw["s5_glu_w"] = f("s5_glu_w").astype(BF16)
    lw["s5_glu_b"] = f("s5_glu_b")[None]
    lw["w_gate"] = f("w_gate").astype(BF16)
    lw["b_gate"] = f("b_gate")[:, None, :]
    lw["w_branch"] = f("w_branch").astype(BF16)
    lw["w_out"] = f("w_out").astype(BF16)
    wr = jnp.concatenate([f("w_router_group"), jnp.transpose(f("w_router_expert"), (1, 0, 2)).reshape(D_MODEL, -1)],
                         axis=1)
    br = jnp.concatenate([f("b_router_group"), f("b_router_expert").reshape(-1)])
    nr = MOE_GROUPS + MOE_GROUPS * MOE_EXPERTS
    lw["w_router"] = jnp.pad(wr, ((0, 0), (0, V7X_LANES - nr)))
    lw["b_router"] = jnp.pad(br, (0, V7X_LANES - nr))[None]
    lw["norm_ffn_g"] = f("norm_ffn_g")[None]
    ne = MOE_GROUPS * MOE_EXPERTS
    lw["w_e_gate"] = f("w_e_gate").reshape(ne, D_MODEL, MOE_FF).astype(BF16)
    lw["w_e_up"] = f("w_e_up").reshape(ne, D_MODEL, MOE_FF).astype(BF16)
    lw["w_e_down"] = f("w_e_down").reshape(ne, MOE_FF, D_MODEL).astype(BF16)
    return lw


def _encoder(x, layers, final_g):
    bsz, seq, _ = x.shape
    plan = _FftPlan(seq)
    xf = x.reshape(bsz * seq, D_MODEL)
    s5_steps = int(math.log2(seq // S5_CHUNK))
    for l, lw in enumerate(layers):
        lru_in, hy_in, gla_in, s5_in = _inproj(xf, lw["norm_mix_g"], lw["w_pack"])
        shp = lambda a: a.reshape(bsz, seq, a.shape[-1])
        yaf, yab = _lru(shp(lru_in), lw["lru_conv_w"], lw["lru_conv_b"], lw["lru_wg"], lw["lru_bg"], lw["lru_lam"])
        yb = _hyena(shp(hy_in), lw, plan)
        yc = _gla(shp(gla_in), lw["gla_wla"], lw["gla_bla"], lw["gla_norm_g"])
        tm, em, om, mu = lw["s5_tables"]
        y5 = _s5_mixer(shp(s5_in), (tm, em, om, mu[:, :, :s5_steps]))
        flat = lambda a: a.reshape(bsz * seq, MIX_WIDTH)
        xf = _merge(xf, lw, flat(yaf), flat(yab), flat(yb), flat(yc), flat(y5), s5_in)
        xf = _moe(xf, lw, final_g, final=(l == len(layers) - 1))
    return xf.reshape(bsz, seq, D_MODEL)


def kernel(x_prompt, x_sample, norm_mix_g, w_in, lru_conv_w, lru_conv_b, lru_wa, lru_ba, lru_wx, lru_bx,
           lru_lambda, hy_conv_w, hy_conv_b, hy_w1, hy_b1, hy_w2, hy_b2, hy_w3, hy_freq, hy_bias,
           gla_wg2, gla_bg, gla_norm_g, s5_lam_re, s5_lam_im, s5_log_dt, s5_b_re, s5_b_im, s5_c_re, s5_c_im,
           s5_d, s5_glu_w, s5_glu_b, w_branch, w_gate, b_gate, w_out, norm_ffn_g, w_router_group,
           b_router_group, w_router_expert, b_router_expert, w_e_gate, w_e_up, w_e_down, final_norm_g):
    w = dict(norm_mix_g=norm_mix_g, w_in=w_in, lru_conv_w=lru_conv_w, lru_conv_b=lru_conv_b, lru_wa=lru_wa,
             lru_ba=lru_ba, lru_wx=lru_wx, lru_bx=lru_bx, lru_lambda=lru_lambda, hy_conv_w=hy_conv_w,
             hy_conv_b=hy_conv_b, hy_w1=hy_w1, hy_b1=hy_b1, hy_w2=hy_w2, hy_b2=hy_b2, hy_w3=hy_w3,
             hy_freq=hy_freq, hy_bias=hy_bias, gla_wg2=gla_wg2, gla_bg=gla_bg, gla_norm_g=gla_norm_g,
             s5_lam_re=s5_lam_re, s5_lam_im=s5_lam_im, s5_log_dt=s5_log_dt, s5_b_re=s5_b_re, s5_b_im=s5_b_im,
             s5_c_re=s5_c_re, s5_c_im=s5_c_im, s5_d=s5_d, s5_glu_w=s5_glu_w, s5_glu_b=s5_glu_b,
             w_branch=w_branch, w_gate=w_gate, b_gate=b_gate, w_out=w_out, norm_ffn_g=norm_ffn_g,
             w_router_group=w_router_group, b_router_group=b_router_group, w_router_expert=w_router_expert,
             b_router_expert=b_router_expert, w_e_gate=w_e_gate, w_e_up=w_e_up, w_e_down=w_e_down)
    max_seq = max(x_prompt.shape[1], x_sample.shape[1])
    nlog = int(math.log2(max_seq // S5_CHUNK))
    layers = [_prep_layer(w, l, nlog) for l in range(norm_mix_g.shape[0])]
    fg = final_norm_g[None]
    return (_encoder(x_prompt, layers, fg), _encoder(x_sample, layers, fg))
```

```python
import functools
import math

import numpy as np
import jax
import jax.numpy as jnp
from jax import lax
from jax.experimental import pallas as pl
from jax.experimental.pallas import tpu as pltpu

F32 = jnp.float32
BF16 = jnp.bfloat16

D_MODEL = 1024
DEPTH = 2
EPS = 1e-6
MIX_WIDTH = D_MODEL // 2
LRU_BLOCKS = 8
LRU_BLOCK = MIX_WIDTH // LRU_BLOCKS
LRU_CONV = 4
LRU_C = 8.0
HY_ORDER = 2
HY_CONV = 3
HY_EMB = 33
HY_BANDS = (HY_EMB - 1) // 2
HY_HIDDEN = 64
HY_DECAY_TARGET = 1e-2
HY_FAST_PCT = 0.3
HY_SLOW_PCT = 1.5
GLA_HEADS = 4
GLA_DK = MIX_WIDTH // 8
GLA_DV = MIX_WIDTH // GLA_HEADS
GLA_RANK = 16
GLA_TAU = 16.0
GLA_CHUNK = 64
S5_GROUP = 16
S5_GROUPS = MIX_WIDTH // S5_GROUP
S5_STATE = 64
MOE_GROUPS = 4
MOE_EXPERTS = 4
MOE_FF = D_MODEL // 4

V7X_LANES = 128
V7X_SUBLANES = 8
V7X_VMEM_BYTES = 64 * 2**20
MIB = 2**20

GLA_PACK = 4 * V7X_LANES * 2 + 512 + 512 + V7X_LANES
PK_LRU = (0, 1024)
PK_HY = (1024, 2560)
PK_GLA = (2560, 2560 + GLA_PACK)
PK_S5 = (PK_GLA[1], PK_GLA[1] + 512)
N_PACK = PK_S5[1]

TOK_TILE = 256
LRU_TILE = 256
CONV_TILE = 512
GLA_TILE = 256
S5_CHUNK = 16
S5_LANE_GROUPS = V7X_LANES // S5_GROUP
FFT_N1 = 64
FFT_ROWS = 128
FFT_COLS = 256
FFT_KB = 11


def _cparams(sem, vmem_mib):
    return pltpu.CompilerParams(dimension_semantics=sem, vmem_limit_bytes=int(vmem_mib * MIB))


def _rms(x, g):
    return x * lax.rsqrt(jnp.mean(x * x, axis=-1, keepdims=True) + EPS) * g


def _sigmoid(x):
    return 1.0 / (1.0 + jnp.exp(-x))


def _softplus(x):
    return jnp.maximum(x, 0.0) + jnp.log(1.0 + jnp.exp(-jnp.abs(x)))


def _gelu_tanh(x):
    return 0.5 * x * (1.0 + jnp.tanh(math.sqrt(2.0 / math.pi) * (x + 0.044715 * (x * x * x))))


def _silu(x):
    return x * _sigmoid(x)


def _dot(a, b):
    return jnp.dot(a, b, preferred_element_type=F32)


def _dot_hi(a, b):
    return jnp.dot(a, b, preferred_element_type=F32, precision=lax.Precision.HIGHEST)


def _inproj_kernel(x_ref, g_ref, w_ref, lru_ref, hy_ref, gla_ref, s5_ref):
    h = _rms(x_ref[...], g_ref[...]).astype(BF16)
    lru_ref[...] = _dot(h, w_ref[:, PK_LRU[0]:PK_LRU[1]])
    hy_ref[...] = _dot(h, w_ref[:, PK_HY[0]:PK_HY[1]])
    gla_ref[...] = _dot(h, w_ref[:, PK_GLA[0]:PK_GLA[1]])
    s5_ref[...] = _dot(h, w_ref[:, PK_S5[0]:PK_S5[1]])


def _inproj(xf, g, w_pack):
    t = xf.shape[0]
    widths = [PK_LRU[1] - PK_LRU[0], PK_HY[1] - PK_HY[0], PK_GLA[1] - PK_GLA[0], PK_S5[1] - PK_S5[0]]
    return pl.pallas_call(
        _inproj_kernel,
        grid=(t // TOK_TILE,),
        in_specs=[pl.BlockSpec((TOK_TILE, D_MODEL), lambda i: (i, 0)),
                  pl.BlockSpec((1, D_MODEL), lambda i: (0, 0)),
                  pl.BlockSpec((D_MODEL, N_PACK), lambda i: (0, 0))],
        out_specs=[pl.BlockSpec((TOK_TILE, w), lambda i: (i, 0)) for w in widths],
        out_shape=[jax.ShapeDtypeStruct((t, w), F32) for w in widths],
        compiler_params=_cparams(("parallel",), 48),
        name="inproj",
    )(xf, g, w_pack)


def _fill_ext(ext_ref, main, prev8, next8, first, last, tile):
    ext_ref[0:8, :] = jnp.where(first, 0.0, prev8)
    ext_ref[8:8 + tile, :] = main
    ext_ref[8 + tile:16 + tile, :] = jnp.where(last, 0.0, next8)


def _linear_scan_tile(a, b, reverse):
    n = a.shape[0]
    row = lax.broadcasted_iota(jnp.int32, a.shape, 0)
    d = 1
    while d < n:
        if reverse:
            a_s = pltpu.roll(a, n - d, 0)
            b_s = pltpu.roll(b, n - d, 0)
            valid = row < n - d
        else:
            a_s = pltpu.roll(a, d, 0)
            b_s = pltpu.roll(b, d, 0)
            valid = row >= d
        b = jnp.where(valid, a * b_s + b, b)
        a = jnp.where(valid, a * a_s, a)
        d *= 2
    return a, b


def _lru_kernel(mf_ref, pf_ref, nf_ref, mb_ref, pb_ref, nb_ref, cw_ref, cb_ref, wg_ref, bg_ref, lam_ref,
                of_ref, ob_ref, extf_ref, extb_ref, carry_ref):
    c = pl.program_id(1)
    nc = pl.num_programs(1)
    tile = LRU_TILE

    @pl.when(c == 0)
    def _():
        carry_ref[...] = jnp.zeros_like(carry_ref)

    def one(m_ref, p_ref, n_ref, ext_ref, d, first, last, o_ref):
        x = m_ref[0, :, 0:MIX_WIDTH]
        ga = m_ref[0, :, MIX_WIDTH:2 * MIX_WIDTH]
        _fill_ext(ext_ref, x, p_ref[0], n_ref[0], first, last, tile)
        xc = cb_ref[...] + ext_ref[6:6 + tile, :] * cw_ref[0:1, :]
        xc = xc + ext_ref[7:7 + tile, :] * cw_ref[1:2, :]
        xc = xc + ext_ref[8:8 + tile, :] * cw_ref[2:3, :]
        xc = xc + ext_ref[9:9 + tile, :] * cw_ref[3:4, :]
        z = _dot(xc.astype(BF16), wg_ref[d]) + bg_ref[d]
        gate_r = _sigmoid(z[:, 0:MIX_WIDTH])
        gate_i = _sigmoid(z[:, MIX_WIDTH:2 * MIX_WIDTH])
        log_a = -LRU_C * gate_r * _softplus(-lam_ref[d])
        a = jnp.exp(log_a)
        b = jnp.sqrt(1.0 - a * a) * gate_i * xc
        a_cum, b_cum = _linear_scan_tile(a, b, reverse=(d == 1))
        h = b_cum + a_cum * carry_ref[d:d + 1, :]
        carry_ref[d:d + 1, :] = h[0:1, :] if d == 1 else h[tile - 1:tile, :]
        o_ref[0] = h * _gelu_tanh(ga)

    one(mf_ref, pf_ref, nf_ref, extf_ref, 0, c == 0, c == nc - 1, of_ref)
    one(mb_ref, pb_ref, nb_ref, extb_ref, 1, c == nc - 1, c == 0, ob_ref)


def _lru(lru_in, cw, cb, wg, bg, lam):
    bsz, seq, _ = lru_in.shape
    tile = LRU_TILE
    nc = seq // tile
    r8 = tile // 8
    last8 = seq // 8 - 1

    def fwd(c):
        return c

    def bwd(c):
        return nc - 1 - c

    def specs(ch):
        return [pl.BlockSpec((1, tile, 2 * MIX_WIDTH), lambda b, c: (b, ch(c), 0)),
                pl.BlockSpec((1, 8, MIX_WIDTH), lambda b, c: (b, jnp.maximum(ch(c) * r8 - 1, 0), 0)),
                pl.BlockSpec((1, 8, MIX_WIDTH), lambda b, c: (b, jnp.minimum((ch(c) + 1) * r8, last8), 0))]

    const = lambda shape: pl.BlockSpec(shape, lambda b, c: (0,) * len(shape))
    return pl.pallas_call(
        _lru_kernel,
        grid=(bsz, nc),
        in_specs=specs(fwd) + specs(bwd) + [const(cw.shape), const(cb.shape), const(wg.shape), const(bg.shape),
                                            const(lam.shape)],
        out_specs=[pl.BlockSpec((1, tile, MIX_WIDTH), lambda b, c: (b, c, 0)),
                   pl.BlockSpec((1, tile, MIX_WIDTH), lambda b, c: (b, nc - 1 - c, 0))],
        out_shape=[jax.ShapeDtypeStruct((bsz, seq, MIX_WIDTH), F32)] * 2,
        scratch_shapes=[pltpu.VMEM((tile + 16, MIX_WIDTH), F32), pltpu.VMEM((tile + 16, MIX_WIDTH), F32),
                        pltpu.VMEM((8, MIX_WIDTH), F32)],
        compiler_params=_cparams(("parallel", "arbitrary"), 40),
        name="lru",
    )(lru_in, lru_in, lru_in, lru_in, lru_in, lru_in, cw, cb, wg, bg, lam)


def _hyconv_kernel(m_ref, p_ref, n_ref, cw_ref, cb_ref, o_ref, ext_ref):
    c = pl.program_id(1)
    nc = pl.num_programs(1)
    tile = CONV_TILE
    _fill_ext(ext_ref, m_ref[0], p_ref[0], n_ref[0], c == 0, c == nc - 1, tile)
    y = cb_ref[...] + ext_ref[7:7 + tile, :] * cw_ref[0:1, :]
    y = y + ext_ref[8:8 + tile, :] * cw_ref[1:2, :]
    y = y + ext_ref[9:9 + tile, :] * cw_ref[2:3, :]
    o_ref[0] = y


def _hyconv(hy_in, cw, cb):
    bsz, seq, width = hy_in.shape
    tile = CONV_TILE
    r8 = tile // 8
    last8 = seq // 8 - 1
    const = lambda shape: pl.BlockSpec(shape, lambda b, c: (0,) * len(shape))
    return pl.pallas_call(
        _hyconv_kernel,
        grid=(bsz, seq // tile),
        in_specs=[pl.BlockSpec((1, tile, width), lambda b, c: (b, c, 0)),
                  pl.BlockSpec((1, 8, width), lambda b, c: (b, jnp.maximum(c * r8 - 1, 0), 0)),
                  pl.BlockSpec((1, 8, width), lambda b, c: (b, jnp.minimum((c + 1) * r8, last8), 0)),
                  const(cw.shape), const(cb.shape)],
        out_specs=pl.BlockSpec((1, tile, width), lambda b, c: (b, c, 0)),
        out_shape=jax.ShapeDtypeStruct((bsz, seq, width), F32),
        scratch_shapes=[pltpu.VMEM((tile + 16, width), F32)],
        compiler_params=_cparams(("parallel", "parallel"), 40),
        name="hyconv",
    )(hy_in, hy_in, hy_in, cw, cb)


class _FftPlan:
    def __init__(self, seq):
        n = 2 * seq
        n1 = FFT_N1
        n2 = n // n1
        assert n1 * n2 == n and n2 % 16 == 0
        h1 = n1 // 2 + 1
        nh = n1 // 2
        self.n, self.n1, self.n2, self.h1, self.nh = n, n1, n2, h1, nh
        self.rows = min(FFT_ROWS, n2)
        assert n2 % self.rows == 0 and h1 % FFT_KB == 0
        k1 = np.arange(h1, dtype=np.float64)
        m1 = np.arange(nh, dtype=np.float64)
        r = np.arange(8, dtype=np.float64)
        ang = -2.0 * np.pi * (k1[:, None, None] * m1[None, None, :] / n1 + r[None, :, None] * k1[:, None, None] / n)
        e = np.exp(1j * ang)
        fa = np.zeros((h1, 8, nh, 8), np.complex128)
        for rr in range(8):
            fa[:, rr, :, rr] = e[:, rr, :]
        fa = fa.reshape(h1 * 8, nh * 8)
        self.fa = jnp.asarray(np.concatenate([fa.real, fa.imag], axis=0), BF16)
        ck = np.where((k1 == 0) | (k1 == n1 // 2), 1.0, 2.0)
        ec = np.conj(e) * ck[:, None, None] / n
        fc = np.zeros((nh, 8, 2, h1, 8), np.float64)
        for rr in range(8):
            fc[:, rr, 0, :, rr] = ec[:, rr, :].real.T
            fc[:, rr, 1, :, rr] = -ec[:, rr, :].imag.T
        self.fc = jnp.asarray(fc.reshape(nh * 8, 2 * h1 * 8), BF16)
        rg = np.arange(n2 // 8, dtype=np.float64)
        tw = np.exp(-2j * np.pi * 8.0 * rg[None, :] * k1[:, None] / n)
        self.tw = jnp.asarray(np.concatenate([tw.real, tw.imag], axis=0), F32)
        q = np.arange(n2, dtype=np.float64)
        f2 = np.exp(-2j * np.pi * np.outer(q, q) / n2)
        fr, fi = f2.real, f2.imag
        self.gb = jnp.asarray(np.block([[fr, -fi], [fi, fr]]), BF16)
        self.gbi = jnp.asarray(np.block([[fr, fi], [-fi, fr]]), BF16)


def _fft_a_kernel(tw_ref, x_ref, fa_ref, a_ref, *, h1, nh, rows):
    rb = pl.program_id(2)
    cols = x_ref.shape[-1]

    def stage(rg):
        xg = x_ref[0, :, pl.ds(pl.multiple_of(rg * 8, 8), 8), :].reshape(nh * 8, cols).astype(BF16)
        return _dot(fa_ref[...], xg)

    def body(i, carry):
        p0 = stage(2 * i)
        p1 = stage(2 * i + 1)
        g0 = rb * (rows // 8) + 2 * i
        for k in range(h1):
            outs = []
            for p, g in ((p0, g0), (p1, g0 + 1)):
                pr = p[k * 8:(k + 1) * 8, :]
                pi = p[(h1 + k) * 8:(h1 + k + 1) * 8, :]
                tr = tw_ref[k, g]
                ti = tw_ref[h1 + k, g]
                outs.append((pr * tr - pi * ti, pr * ti + pi * tr))
            dst = pl.ds(pl.multiple_of(i * 16, 16), 16)
            a_ref[0, k, 0, dst, :] = jnp.concatenate([outs[0][0], outs[1][0]], axis=0).astype(BF16)
            a_ref[0, k, 1, dst, :] = jnp.concatenate([outs[0][1], outs[1][1]], axis=0).astype(BF16)
        return carry

    lax.fori_loop(0, rows // 16, body, 0)


def _fft_a(x4, plan, col_off=0, ncols=None):
    bq, nh, n2, width = x4.shape
    ncols = width if ncols is None else ncols
    cb0 = col_off // FFT_COLS
    rows = plan.rows
    kern = functools.partial(_fft_a_kernel, h1=plan.h1, nh=nh, rows=rows)
    return pl.pallas_call(
        kern,
        grid=(bq, ncols // FFT_COLS, n2 // rows),
        in_specs=[pl.BlockSpec(memory_space=pltpu.SMEM),
                  pl.BlockSpec((1, nh, rows, FFT_COLS), lambda b, c, r: (b, 0, r, cb0 + c)),
                  pl.BlockSpec(plan.fa.shape, lambda b, c, r: (0, 0))],
        out_specs=pl.BlockSpec((1, plan.h1, 2, rows, FFT_COLS), lambda b, c, r: (b, 0, 0, r, c)),
        out_shape=jax.ShapeDtypeStruct((bq, plan.h1, 2, n2, ncols), BF16),
        compiler_params=_cparams(("parallel", "parallel", "parallel"), 40),
        name="fft_outer_fwd",
    )(plan.tw, x4, plan.fa)


def _fft_mid_kernel(a_ref, kf_ref, gb_ref, gbi_ref, o_ref, *, n2):
    kb = a_ref.shape[1]
    cols = a_ref.shape[-1]
    for k in range(kb):
        y = _dot(gb_ref[...], a_ref[0, k].reshape(2 * n2, cols))
        yr, yi = y[0:n2, :], y[n2:2 * n2, :]
        kr, ki = kf_ref[0, k, 0], kf_ref[0, k, 1]
        z = jnp.concatenate([yr * kr - yi * ki, yr * ki + yi * kr], axis=0).astype(BF16)
        o_ref[0, k] = _dot(gbi_ref[...], z).reshape(2, n2, cols).astype(BF16)


def _fft_mid(a, kf, order, plan):
    bq, h1, _, n2, width = a.shape
    kern = functools.partial(_fft_mid_kernel, n2=n2)
    blk = (1, FFT_KB, 2, n2, FFT_COLS)
    return pl.pallas_call(
        kern,
        grid=(bq, width // FFT_COLS, h1 // FFT_KB),
        in_specs=[pl.BlockSpec(blk, lambda b, c, k: (b, k, 0, 0, c)),
                  pl.BlockSpec(blk, lambda b, c, k: (order, k, 0, 0, c)),
                  pl.BlockSpec(plan.gb.shape, lambda b, c, k: (0, 0)),
                  pl.BlockSpec(plan.gbi.shape, lambda b, c, k: (0, 0))],
        out_specs=pl.BlockSpec(blk, lambda b, c, k: (b, k, 0, 0, c)),
        out_shape=jax.ShapeDtypeStruct(a.shape, BF16),
        compiler_params=_cparams(("parallel", "parallel", "parallel"), 40),
        name="fft_inner_mul",
    )(a, kf, plan.gb, plan.gbi)


def _fft_c_kernel(tw_ref, b_ref, u_ref, g_ref, bias_ref, fc_ref, o_ref, *, h1, nh, rows):
    rb = pl.program_id(2)
    cols = o_ref.shape[-1]

    def body(i, carry):
        src = pl.ds(pl.multiple_of(i * 16, 16), 16)
        tiles = [[b_ref[0, k, p, src, :].astype(F32) for p in range(2)] for k in range(h1)]
        for half in range(2):
            g = rb * (rows // 8) + 2 * i + half
            re_rows, im_rows = [], []
            for k in range(h1):
                br = tiles[k][0][half * 8:(half + 1) * 8, :]
                bi = tiles[k][1][half * 8:(half + 1) * 8, :]
                tr = tw_ref[k, g]
                ti = tw_ref[h1 + k, g]
                re_rows.append(br * tr + bi * ti)
                im_rows.append(bi * tr - br * ti)
            s = jnp.concatenate(re_rows + im_rows, axis=0).astype(BF16)
            y = _dot(fc_ref[...], s).reshape(nh, 8, cols)
            dst = pl.ds(pl.multiple_of((2 * i + half) * 8, 8), 8)
            u = u_ref[0, :, dst, :]
            o_ref[0, :, dst, :] = (y + u * bias_ref[...]) * g_ref[0, :, dst, :]
        return carry

    lax.fori_loop(0, rows // 16, body, 0)


def _fft_c(bm, u4, u_off, g4, g_off, bias, plan):
    bq, h1, _, n2, width = bm.shape
    nh = plan.nh
    rows = plan.rows
    ub, gbk = u_off // FFT_COLS, g_off // FFT_COLS
    kern = functools.partial(_fft_c_kernel, h1=h1, nh=nh, rows=rows)
    xblk = (1, nh, rows, FFT_COLS)
    return pl.pallas_call(
        kern,
        grid=(bq, width // FFT_COLS, n2 // rows),
        in_specs=[pl.BlockSpec(memory_space=pltpu.SMEM),
                  pl.BlockSpec((1, h1, 2, rows, FFT_COLS), lambda b, c, r: (b, 0, 0, r, c)),
                  pl.BlockSpec(xblk, lambda b, c, r: (b, 0, r, ub + c)),
                  pl.BlockSpec(xblk, lambda b, c, r: (b, 0, r, gbk + c)),
                  pl.BlockSpec((1, FFT_COLS), lambda b, c, r: (0, c)),
                  pl.BlockSpec(plan.fc.shape, lambda b, c, r: (0, 0))],
        out_specs=pl.BlockSpec(xblk, lambda b, c, r: (b, 0, r, c)),
        out_shape=jax.ShapeDtypeStruct((bq, nh, n2, width), F32),
        compiler_params=_cparams(("parallel", "parallel", "parallel"), 48),
        name="fft_outer_inv",
    )(plan.tw, bm, u4, g4, bias, plan.fc)


def _hyfilt_gen_kernel(w1_ref, b1_ref, w2_ref, b2_ref, w3_ref, fr_ref, o_ref, ss_ref, *, seq):
    rblk = pl.program_id(0)
    tile, cols = o_ref.shape[1], o_ref.shape[2]
    irow = lax.broadcasted_iota(jnp.int32, (tile, V7X_LANES), 0) + rblk * tile
    row = irow.astype(F32)
    lane = lax.broadcasted_iota(jnp.int32, (tile, V7X_LANES), 1)
    t = row / (seq - 1.0)
    omega = (2.0 * math.pi / seq) * row
    band_step = (HY_BANDS - 1 - 1e-4) / (HY_BANDS - 1)
    is_cos = (lane >= 1) & (lane <= HY_BANDS)
    is_sin = (lane > HY_BANDS) & (lane <= 2 * HY_BANDS)
    bidx = jnp.where(is_cos, lane - 1, lane - 1 - HY_BANDS).astype(F32)
    ang = omega * (1e-4 + band_step * bidx)
    z = jnp.where(lane == 0, t, jnp.where(is_cos, jnp.cos(ang), jnp.where(is_sin, -jnp.sin(ang), 0.0)))
    fr = fr_ref[...]
    hid = jnp.sin(fr * (_dot_hi(z, w1_ref[...]) + b1_ref[...]))
    hid = jnp.sin(fr * (_dot_hi(hid, w2_ref[...]) + b2_ref[...]))
    filt = _dot_hi(hid, w3_ref[...])
    col = lax.broadcasted_iota(jnp.int32, (1, cols), 1)
    chan = (col % MIX_WIDTH).astype(F32)
    max_decay = math.log(HY_DECAY_TARGET) / HY_FAST_PCT
    min_decay = math.log(HY_DECAY_TARGET) / HY_SLOW_PCT
    delta = jnp.abs(min_decay + (max_decay - min_decay) / (MIX_WIDTH - 1) * chan)
    filt = filt * jnp.exp(-t[:, 0:1] * delta)
    is_bwd = (col // MIX_WIDTH) % 2 == 1
    rows_c = lax.broadcasted_iota(jnp.int32, (tile, cols), 0) + rblk * tile
    filt = jnp.where(is_bwd & (rows_c == seq - 1), 0.0, filt)
    o_ref[0] = filt

    @pl.when(rblk == 0)
    def _():
        ss_ref[...] = jnp.zeros_like(ss_ref)

    ss_ref[...] += jnp.sum(filt * filt, axis=0, keepdims=True)


def _hyfilt_gen(seq, w1p, b1, w2, b2, w3, freq):
    ncol = w3.shape[1]
    tile = min(seq, 512)
    const = lambda shape: pl.BlockSpec(shape, lambda r: (0,) * len(shape))
    kern = functools.partial(_hyfilt_gen_kernel, seq=seq)
    return pl.pallas_call(
        kern,
        grid=(seq // tile,),
        in_specs=[const(w1p.shape), const(b1.shape), const(w2.shape), const(b2.shape), const(w3.shape),
                  const(freq.shape)],
        out_specs=[pl.BlockSpec((1, tile, ncol), lambda r: (0, r, 0)),
                   pl.BlockSpec((1, ncol), lambda r: (0, 0))],
        out_shape=[jax.ShapeDtypeStruct((1, seq, ncol), F32), jax.ShapeDtypeStruct((1, ncol), F32)],
        compiler_params=_cparams(("arbitrary",), 40),
        name="hyena_filter_gen",
    )(w1p, b1, w2, b2, w3, freq)


def _hyfilt_spec_kernel(a0_ref, a1_ref, ss0_ref, ss1_ref, gb_ref, o_ref, *, n, n1, n2):
    kblk = pl.program_id(2)
    kb = a0_ref.shape[1]
    cols = a0_ref.shape[-1]
    scale = lax.rsqrt(ss0_ref[...] + ss1_ref[...] + EPS)
    k2 = lax.broadcasted_iota(jnp.int32, (n2, cols), 0).astype(F32)
    for k in range(kb):
        y0 = _dot(gb_ref[...], a0_ref[0, k].reshape(2 * n2, cols))
        y1 = _dot(gb_ref[...], a1_ref[0, k].reshape(2 * n2, cols))
        k1 = (kblk * kb + k).astype(F32)
        ang = (-2.0 * math.pi / n) * (k1 + n1 * k2)
        wr, wi = jnp.cos(ang), jnp.sin(ang)
        y1r, y1i = y1[0:n2, :], y1[n2:2 * n2, :]
        o_ref[0, k, 0] = (y0[0:n2, :] + (wr * y1r - wi * y1i)) * scale
        o_ref[0, k, 1] = (y0[n2:2 * n2, :] - (wr * y1i + wi * y1r)) * scale


def _hyfilt_spec(af, ss, plan):
    _, h1, _, n2, _ = af.shape
    ncb = MIX_WIDTH // FFT_COLS
    blk = (1, FFT_KB, 2, n2, FFT_COLS)
    kern = functools.partial(_hyfilt_spec_kernel, n=plan.n, n1=plan.n1, n2=n2)
    return pl.pallas_call(
        kern,
        grid=(HY_ORDER, ncb, h1 // FFT_KB),
        in_specs=[pl.BlockSpec(blk, lambda o, c, k: (0, k, 0, 0, o * 2 * ncb + c)),
                  pl.BlockSpec(blk, lambda o, c, k: (0, k, 0, 0, o * 2 * ncb + ncb + c)),
                  pl.BlockSpec((1, FFT_COLS), lambda o, c, k: (0, o * 2 * ncb + c)),
                  pl.BlockSpec((1, FFT_COLS), lambda o, c, k: (0, o * 2 * ncb + ncb + c)),
                  pl.BlockSpec(plan.gb.shape, lambda o, c, k: (0, 0))],
        out_specs=pl.BlockSpec(blk, lambda o, c, k: (o, k, 0, 0, c)),
        out_shape=jax.ShapeDtypeStruct((HY_ORDER, h1, 2, n2, MIX_WIDTH), F32),
        compiler_params=_cparams(("parallel", "parallel", "parallel"), 40),
        name="hyena_filter_spectrum",
    )(af, af, ss, ss, plan.gb)


def _hyena(hy_in, lw, plan):
    bsz, seq, _ = hy_in.shape
    zc = _hyconv(hy_in, lw["hy_conv_w"], lw["hy_conv_b"])
    filt, ss = _hyfilt_gen(seq, lw["hy_w1p"], lw["hy_b1"], lw["hy_w2"], lw["hy_b2"], lw["hy_w3"], lw["hy_freq"])
    af = _fft_a(filt.reshape(1, plan.nh, plan.n2, filt.shape[-1]), plan)
    kf = _hyfilt_spec(af, ss, plan)
    zc4 = zc.reshape(bsz, plan.nh, plan.n2, 3 * MIX_WIDTH)
    a = _fft_a(zc4, plan, col_off=0, ncols=MIX_WIDTH)
    bm = _fft_mid(a, kf, 0, plan)
    z1 = _fft_c(bm, zc4, 0, zc4, MIX_WIDTH, lw["hy_bias"][0:1], plan)
    a = _fft_a(z1, plan)
    bm = _fft_mid(a, kf, 1, plan)
    z2 = _fft_c(bm, z1, 0, zc4, 2 * MIX_WIDTH, lw["hy_bias"][1:2], plan)
    return z2.reshape(bsz, seq, MIX_WIDTH)


def _gla_kernel(*refs, reverse):
    if reverse:
        x_ref, of_ref, wla_ref, bla_ref, ng_ref, o_ref, st_ref = refs
    else:
        x_ref, wla_ref, bla_ref, o_ref, st_ref = refs
    c = pl.program_id(1)
    tile = GLA_TILE
    ck = GLA_CHUNK
    nck = tile // ck
    hw = V7X_LANES
    nh = GLA_HEADS

    @pl.when(c == 0)
    def _():
        st_ref[...] = jnp.zeros_like(st_ref)

    q = x_ref[0, :, 0:nh * hw] * (GLA_DK ** -0.5)
    k = x_ref[0, :, nh * hw:2 * nh * hw]
    v = x_ref[0, :, 2 * nh * hw:3 * nh * hw]
    lr = x_ref[0, :, 4 * nh * hw:4 * nh * hw + hw]
    zl = _dot(lr.astype(BF16), wla_ref[...]) + bla_ref[...]
    la = (jnp.minimum(zl, 0.0) - jnp.log(1.0 + jnp.exp(-jnp.abs(zl)))) / GLA_TAU

    row = lax.broadcasted_iota(jnp.int32, la.shape, 0) % ck
    bcum = la
    d = 1
    while d < ck:
        if reverse:
            bcum = bcum + jnp.where(row < ck - d, pltpu.roll(bcum, tile - d, 0), 0.0)
        else:
            bcum = bcum + jnp.where(row >= d, pltpu.roll(bcum, d, 0), 0.0)
        d *= 2
    b3 = bcum.reshape(nck, ck, nh * hw)
    blast = b3[:, 0:1, :] if reverse else b3[:, ck - 1:ck, :]
    q_e = (q * jnp.exp(bcum)).astype(BF16)
    k_e = (k * jnp.exp(-bcum)).astype(BF16)
    k_d = (k.reshape(nck, ck, nh * hw) * jnp.exp(blast - b3)).reshape(tile, nh * hw).astype(BF16)
    gch = jnp.exp(blast)
    vb = v.astype(BF16)

    ri = lax.broadcasted_iota(jnp.int32, (ck, ck), 0)
    ci = lax.broadcasted_iota(jnp.int32, (ck, ck), 1)
    mask = (ri <= ci) if reverse else (ri >= ci)
    order = range(nck - 1, -1, -1) if reverse else range(nck)
    outs = [None] * nck
    for n in order:
        rs = slice(n * ck, (n + 1) * ck)
        heads = []
        for h in range(nh):
            ls = slice(h * hw, (h + 1) * hw)
            qe, ke, kd, vh = q_e[rs, ls], k_e[rs, ls], k_d[rs, ls], vb[rs, ls]
            st = st_ref[h]
            sc = lax.dot_general(qe, ke, (((1,), (1,)), ((), ())), preferred_element_type=F32)
            sc = jnp.where(mask, sc, 0.0).astype(BF16)
            o = _dot(sc, vh) + lax.dot_general(qe, st.astype(BF16), (((1,), (1,)), ((), ())),
                                               preferred_element_type=F32)
            upd = lax.dot_general(vh, kd, (((0,), (0,)), ((), ())), preferred_element_type=F32)
            st_ref[h] = st * gch[n, :, ls] + upd
            heads.append(o)
        outs[n] = jnp.concatenate(heads, axis=1)
    o_dir = jnp.concatenate(outs, axis=0)
    if not reverse:
        o_ref[0] = o_dir
        return
    o = of_ref[0] + o_dir
    g = x_ref[0, :, 3 * nh * hw:4 * nh * hw]
    normed = []
    for h in range(nh):
        oh = o[:, h * hw:(h + 1) * hw]
        normed.append(oh * lax.rsqrt(jnp.mean(oh * oh, axis=-1, keepdims=True) + EPS))
    o = jnp.concatenate(normed, axis=1) * ng_ref[...]
    o_ref[0] = o * _silu(g)


def _gla(gla_in, wla, bla, norm_g):
    bsz, seq, width = gla_in.shape
    tile = GLA_TILE
    nc = seq // tile
    const = lambda shape: pl.BlockSpec(shape, lambda b, c: (0,) * len(shape))
    out_shape = jax.ShapeDtypeStruct((bsz, seq, MIX_WIDTH), F32)
    scratch = [pltpu.VMEM((GLA_HEADS, GLA_DV, V7X_LANES), F32)]
    o_f = pl.pallas_call(
        functools.partial(_gla_kernel, reverse=False),
        grid=(bsz, nc),
        in_specs=[pl.BlockSpec((1, tile, width), lambda b, c: (b, c, 0)), const(wla.shape[1:]), const(bla.shape[1:])],
        out_specs=pl.BlockSpec((1, tile, MIX_WIDTH), lambda b, c: (b, c, 0)),
        out_shape=out_shape,
        scratch_shapes=scratch,
        compiler_params=_cparams(("parallel", "arbitrary"), 40),
        name="gla_fwd",
    )(gla_in, wla[0], bla[0])
    return pl.pallas_call(
        functools.partial(_gla_kernel, reverse=True),
        grid=(bsz, nc),
        in_specs=[pl.BlockSpec((1, tile, width), lambda b, c: (b, nc - 1 - c, 0)),
                  pl.BlockSpec((1, tile, MIX_WIDTH), lambda b, c: (b, nc - 1 - c, 0)),
                  const(wla.shape[1:]), const(bla.shape[1:]), const(norm_g.shape)],
        out_specs=pl.BlockSpec((1, tile, MIX_WIDTH), lambda b, c: (b, nc - 1 - c, 0)),
        out_shape=out_shape,
        scratch_shapes=scratch,
        compiler_params=_cparams(("parallel", "arbitrary"), 40),
        name="gla_bwd",
    )(gla_in, o_f, wla[1], bla[1], norm_g)


def _s5_kernel(u_ref, t_ref, e_ref, o_ref_w, mu_ref, y_ref):
    ch = S5_CHUNK
    nrow = u_ref.shape[1] // ch
    half = S5_LANE_GROUPS * 2 * S5_STATE
    u = jnp.concatenate([u_ref[0, pl.ds(i, nrow, stride=ch), :] for i in range(ch)], axis=1).astype(BF16)
    y = _dot(u, t_ref[0])
    he = _dot(u, e_ref[0])
    row = lax.broadcasted_iota(jnp.int32, (nrow, half), 0)
    states = []
    for d in range(2):
        h = he[:, d * half:(d + 1) * half]
        step = 1
        s = 0
        while step < nrow:
            if d == 0:
                hs = jnp.where(row >= step, pltpu.roll(h, step, 0), 0.0)
            else:
                hs = jnp.where(row < nrow - step, pltpu.roll(h, nrow - step, 0), 0.0)
            h = h + hs * mu_ref[0, d, s, 0:1, :] + pltpu.roll(hs, half // 2, 1) * mu_ref[0, d, s, 1:2, :]
            step *= 2
            s += 1
        if d == 0:
            h = jnp.where(row >= 1, pltpu.roll(h, 1, 0), 0.0)
        else:
            h = jnp.where(row < nrow - 1, pltpu.roll(h, nrow - 1, 0), 0.0)
        states.append(h)
    hp = jnp.concatenate(states, axis=1).astype(BF16)
    y = y + _dot(hp, o_ref_w[0])
    for j in range(ch):
        y_ref[0, pl.ds(j, nrow, stride=ch), :] = y[:, j * V7X_LANES:(j + 1) * V7X_LANES]


def _s5(s5_in, tblk, eblk, oblk, mu):
    bsz, seq, width = s5_in.shape
    nb = width // V7X_LANES
    once = pl.Buffered(1)
    wspec = lambda arr: pl.BlockSpec((1,) + arr.shape[1:], lambda k, b: (k,) + (0,) * (arr.ndim - 1),
                                     pipeline_mode=once)
    xspec = pl.BlockSpec((1, seq, V7X_LANES), lambda k, b: (b, 0, k), pipeline_mode=once)
    return pl.pallas_call(
        _s5_kernel,
        grid=(nb, bsz),
        in_specs=[xspec, wspec(tblk), wspec(eblk), wspec(oblk), wspec(mu)],
        out_specs=pl.BlockSpec((1, seq, V7X_LANES), lambda k, b: (b, 0, k)),
        out_shape=jax.ShapeDtypeStruct(s5_in.shape, F32),
        compiler_params=_cparams(("arbitrary", "arbitrary"), 60),
        name="s5",
    )(s5_in, tblk, eblk, oblk, mu)


def _s5_tables(lam_re, lam_im, log_dt, b_re, b_im, c_re, c_im, nrow_max_log2):
    ch = S5_CHUNK
    hi = lax.Precision.HIGHEST
    cmul = lambda x, y: (x[0] * y[0] - x[1] * y[1], x[0] * y[1] + x[1] * y[0])
    lr_, li_ = lam_re.astype(F32), lam_im.astype(F32)
    dt = jnp.exp(log_dt.astype(F32))[..., None]
    ar, ai = lr_ * dt, li_ * dt

    def lam_pow(tau):
        t = jnp.asarray(tau, F32)
        t = t.reshape(t.shape + (1,) * 3)
        mag = jnp.exp(t * ar)
        return mag * jnp.cos(t * ai), mag * jnp.sin(t * ai)

    lb = lam_pow(jnp.ones((), F32))
    num = (lb[0] - 1.0, lb[1])
    den = lr_ * lr_ + li_ * li_
    ratio = ((num[0] * lr_ + num[1] * li_) / den, (num[1] * lr_ - num[0] * li_) / den)
    b_bar = cmul((ratio[0][..., None], ratio[1][..., None]), (b_re.astype(F32), b_im.astype(F32)))
    cc = (c_re.astype(F32), c_im.astype(F32))

    lp = lam_pow(jnp.arange(ch + 1, dtype=F32))
    cb = cmul((cc[0][..., None], cc[1][..., None]), (b_bar[0][:, :, None], b_bar[1][:, :, None]))
    kern = (jnp.einsum("dghpk,tdgp->tdghk", cb[0], lp[0][:ch], precision=hi)
            - jnp.einsum("dghpk,tdgp->tdghk", cb[1], lp[1][:ch], precision=hi))
    idx = jnp.arange(ch)
    lag = idx[None, :] - idx[:, None]
    kf = jnp.where((lag >= 0)[..., None, None, None], kern[jnp.clip(lag, 0, ch - 1), 0], 0.0)
    kb = jnp.where((lag <= 0)[..., None, None, None], kern[jnp.clip(-lag, 0, ch - 1), 1], 0.0)
    tmat = jnp.transpose(kf + kb, (2, 0, 4, 1, 3)).reshape(S5_GROUPS, ch * S5_GROUP, ch * S5_GROUP)

    def e_cols(powers, d):
        e = cmul((powers[0][..., None], powers[1][..., None]), (b_bar[0][d][None], b_bar[1][d][None]))
        return [jnp.transpose(x, (1, 0, 3, 2)).reshape(S5_GROUPS, ch * S5_GROUP, S5_STATE) for x in e]

    emat = jnp.concatenate(e_cols((lp[0][ch - 1 - idx, 0], lp[1][ch - 1 - idx, 0]), 0)
                           + e_cols((lp[0][idx, 1], lp[1][idx, 1]), 1), axis=-1)

    def o_rows(powers, d):
        o = cmul((cc[0][d][None], cc[1][d][None]), (powers[0][:, :, None, :], powers[1][:, :, None, :]))
        o = [jnp.transpose(x, (1, 3, 0, 2)).reshape(S5_GROUPS, S5_STATE, ch * S5_GROUP) for x in o]
        return [o[0], -o[1]]

    omat = jnp.concatenate(o_rows((lp[0][idx + 1, 0], lp[1][idx + 1, 0]), 0)
                           + o_rows((lp[0][ch - idx, 1], lp[1][ch - idx, 1]), 1), axis=1)
    mp = lam_pow(ch * (2.0 ** jnp.arange(nrow_max_log2, dtype=F32)))
    nb, gb, hh, pp = S5_GROUPS // S5_LANE_GROUPS, S5_LANE_GROUPS, S5_GROUP, S5_STATE
    eye = jnp.eye(gb, dtype=F32)
    big = ch * gb * hh
    tblk = jnp.einsum("kgihjm,gf->kighjfm", tmat.reshape(nb, gb, ch, hh, ch, hh), eye).reshape(nb, big, big)
    eblk = jnp.einsum("kgihdcp,gf->kighdcfp", emat.reshape(nb, gb, ch, hh, 2, 2, pp), eye)
    eblk = eblk.reshape(nb, big, 4 * gb * pp)
    oblk = jnp.einsum("kgdcpjm,gf->kdcgpjfm", omat.reshape(nb, gb, 2, 2, pp, ch, hh), eye)
    oblk = oblk.reshape(nb, 4 * gb * pp, big)
    mre = jnp.transpose(mp[0].reshape(-1, 2, nb, gb * pp), (2, 1, 0, 3))
    mim = jnp.transpose(mp[1].reshape(-1, 2, nb, gb * pp), (2, 1, 0, 3))
    mu = jnp.stack([jnp.concatenate([mre, mre], axis=-1), jnp.concatenate([-mim, mim], axis=-1)], axis=-2)
    return tblk.astype(BF16), eblk.astype(BF16), oblk.astype(BF16), mu.astype(F32)


def _merge_kernel(x_ref, ng_ref, yaf_ref, yab_ref, yb_ref, yc_ref, y5_ref, u5_ref, d5_ref, gluw_ref, glub_ref,
                  wgate_ref, bgate_ref, wbr_ref, wout_ref, o_ref):
    x = x_ref[...]
    h = _rms(x, ng_ref[...]).astype(BF16)
    y_d = _gelu_tanh(u5_ref[...] * d5_ref[...] + y5_ref[...])
    y_d = y_d * _sigmoid(_dot(y_d.astype(BF16), gluw_ref[...]) + glub_ref[...])
    branches = (yaf_ref[...] + yab_ref[...], yb_ref[...], yc_ref[...], y_d)
    merged = jnp.zeros(x.shape, F32)
    for i, y in enumerate(branches):
        gate = _sigmoid(_dot(h, wgate_ref[i]) + bgate_ref[i])
        merged = merged + gate * _dot(y.astype(BF16), wbr_ref[i])
    o_ref[...] = x + _dot(merged.astype(BF16), wout_ref[...])


def _merge(xf, lw, yaf, yab, yb, yc, y5, u5):
    t = xf.shape[0]
    tok = lambda w: pl.BlockSpec((TOK_TILE, w), lambda i: (i, 0))
    const = lambda arr: pl.BlockSpec(arr.shape, lambda i: (0,) * arr.ndim, pipeline_mode=pl.Buffered(1))
    weights = [lw["s5_d"], lw["s5_glu_w"], lw["s5_glu_b"], lw["w_gate"], lw["b_gate"], lw["w_branch"], lw["w_out"]]
    return pl.pallas_call(
        _merge_kernel,
        grid=(t // TOK_TILE,),
        in_specs=[tok(D_MODEL), const(lw["norm_mix_g"])] + [tok(MIX_WIDTH)] * 6 + [const(w) for w in weights],
        out_specs=tok(D_MODEL),
        out_shape=jax.ShapeDtypeStruct((t, D_MODEL), F32),
        compiler_params=_cparams(("parallel",), 48),
        name="merge",
    )(xf, lw["norm_mix_g"], yaf, yab, yb, yc, y5, u5, *weights)


def _moe_kernel(x_ref, ng_ref, wr_ref, br_ref, wg_ref, wu_ref, wd_ref, fg_ref, o_ref, *, final):
    x = x_ref[...]
    hf = _rms(x, ng_ref[...])
    logits = _dot_hi(hf, wr_ref[...]) + br_ref[...]
    lane = lax.broadcasted_iota(jnp.int32, logits.shape, 1).astype(F32)
    neg = -jnp.inf
    big = float(V7X_LANES)
    ng, ne = MOE_GROUPS, MOE_EXPERTS
    gl = jnp.where(lane < ng, logits, neg)
    gmax = jnp.max(gl, axis=1, keepdims=True)
    gidx = jnp.min(jnp.where(gl == gmax, lane, big), axis=1, keepdims=True)
    gprob = 1.0 / jnp.sum(jnp.exp(gl - gmax), axis=1, keepdims=True)
    lo = ng + ne * gidx
    sel = (lane >= lo) & (lane < lo + ne)
    m1 = jnp.max(jnp.where(sel, logits, neg), axis=1, keepdims=True)
    i1 = jnp.min(jnp.where(sel & (logits == m1), lane, big), axis=1, keepdims=True)
    sel2 = sel & (lane != i1)
    m2 = jnp.max(jnp.where(sel2, logits, neg), axis=1, keepdims=True)
    i2 = jnp.min(jnp.where(sel2 & (logits == m2), lane, big), axis=1, keepdims=True)
    e2 = jnp.exp(m2 - m1)
    w1 = gprob / (1.0 + e2)
    w2 = gprob * e2 / (1.0 + e2)
    ew = jnp.where(lane == i1, w1, jnp.where(lane == i2, w2, 0.0))
    h = hf.astype(BF16)
    acc = jnp.zeros(x.shape, F32)
    for idx in range(ng * ne):
        wcol = jnp.sum(jnp.where(lane == ng + idx, ew, 0.0), axis=1, keepdims=True)
        gate = _dot(h, wg_ref[idx])
        up = _dot(h, wu_ref[idx])
        act = _silu(gate) * up * wcol
        acc = acc + _dot(act.astype(BF16), wd_ref[idx])
    y = x + acc
    if final:
        y = _rms(y, fg_ref[...])
    o_ref[...] = y


def _moe(xf, lw, final_g, final):
    t = xf.shape[0]
    tok = pl.BlockSpec((TOK_TILE, D_MODEL), lambda i: (i, 0))
    const = lambda arr: pl.BlockSpec(arr.shape, lambda i: (0,) * arr.ndim, pipeline_mode=pl.Buffered(1))
    weights = [lw["norm_ffn_g"], lw["w_router"], lw["b_router"], lw["w_e_gate"], lw["w_e_up"], lw["w_e_down"],
               final_g]
    return pl.pallas_call(
        functools.partial(_moe_kernel, final=final),
        grid=(t // TOK_TILE,),
        in_specs=[tok] + [const(w) for w in weights],
        out_specs=tok,
        out_shape=jax.ShapeDtypeStruct((t, D_MODEL), F32),
        compiler_params=_cparams(("parallel",), 56),
        name="moe",
    )(xf, *weights)


def _block_diag(blocks):
    nb, bs, _ = blocks.shape
    eye = jnp.eye(nb, dtype=blocks.dtype)
    return jnp.einsum("nij,nm->nimj", blocks, eye).reshape(nb * bs, nb * bs)


def _pad_heads(w, axis=-1):
    shape = w.shape[:-1] + (GLA_HEADS, GLA_DK)
    w = w.reshape(shape)
    pad = [(0, 0)] * (w.ndim - 1) + [(0, V7X_LANES - GLA_DK)]
    return jnp.pad(w, pad).reshape(w.shape[:-2] + (GLA_HEADS * V7X_LANES,))


def _prep_layer(w, l, nrow_max_log2):
    f = lambda name: w[name][l]
    w_in = f("w_in")
    cuts = np.cumsum([MIX_WIDTH, MIX_WIDTH, 3 * MIX_WIDTH, GLA_HEADS * GLA_DK, GLA_HEADS * GLA_DK, MIX_WIDTH,
                      MIX_WIDTH, 2 * GLA_RANK]).tolist()
    xa, ga, hy, q, k, v, g, lr, s5 = jnp.split(w_in, cuts, axis=-1)
    lr = jnp.pad(lr, ((0, 0), (0, V7X_LANES - 2 * GLA_RANK)))
    w_pack = jnp.concatenate([xa, ga, hy, _pad_heads(q), _pad_heads(k), v, g, lr, s5], axis=-1).astype(BF16)
    assert w_pack.shape[1] == N_PACK
    lw = {"w_pack": w_pack, "norm_mix_g": f("norm_mix_g")[None]}
    lw["lru_conv_w"] = f("lru_conv_w")
    lw["lru_conv_b"] = f("lru_conv_b")[None]
    wa, wx = f("lru_wa"), f("lru_wx")
    lw["lru_wg"] = jnp.stack([jnp.concatenate([_block_diag(wa[d]), _block_diag(wx[d])], axis=1)
                              for d in range(2)]).astype(BF16)
    lw["lru_bg"] = jnp.concatenate([f("lru_ba"), f("lru_bx")], axis=-1)[:, None, :]
    lw["lru_lam"] = f("lru_lambda")[:, None, :]
    lw["hy_conv_w"] = f("hy_conv_w")
    lw["hy_conv_b"] = f("hy_conv_b")[None]
    lw["hy_w1p"] = jnp.pad(f("hy_w1"), ((0, V7X_LANES - HY_EMB), (0, 0)))
    lw["hy_b1"] = f("hy_b1")[None]
    lw["hy_w2"] = f("hy_w2")
    lw["hy_b2"] = f("hy_b2")[None]
    lw["hy_w3"] = f("hy_w3")
    lw["hy_freq"] = f("hy_freq")[None]
    lw["hy_bias"] = f("hy_bias")
    wg2 = _pad_heads(f("gla_wg2"))
    wla = jnp.zeros((2, V7X_LANES, GLA_HEADS * V7X_LANES), F32)
    wla = wla.at[0, 0:GLA_RANK].set(wg2[0]).at[1, GLA_RANK:2 * GLA_RANK].set(wg2[1])
    lw["gla_wla"] = wla.astype(BF16)
    lw["gla_bla"] = _pad_heads(f("gla_bg"))[:, None, :]
    lw["gla_norm_g"] = jnp.tile(f("gla_norm_g"), GLA_HEADS)[None]
    lw["s5_tables"] = _s5_tables(f("s5_lam_re"), f("s5_lam_im"), f("s5_log_dt"), f("s5_b_re"), f("s5_b_im"),
                                 f("s5_c_re"), f("s5_c_im"), nrow_max_log2)
    lw["s5_d"] = f("s5_d")[None]
    lw["s5_glu_w"] = f("s5_glu_w").astype(BF16)
    lw["s5_glu_b"] = f("s5_glu_b")[None]
    lw["w_gate"] = f("w_gate").astype(BF16)
    lw["b_gate"] = f("b_gate")[:, None, :]
    lw["w_branch"] = f("w_branch").astype(BF16)
    lw["w_out"] = f("w_out").astype(BF16)
    wr = jnp.concatenate([f("w_router_group"), jnp.transpose(f("w_router_expert"), (1, 0, 2)).reshape(D_MODEL, -1)],
                         axis=1)
    br = jnp.concatenate([f("b_router_group"), f("b_router_expert").reshape(-1)])
    nr = MOE_GROUPS + MOE_GROUPS * MOE_EXPERTS
    lw["w_router"] = jnp.pad(wr, ((0, 0), (0, V7X_LANES - nr)))
    lw["b_router"] = jnp.pad(br, (0, V7X_LANES - nr))[None]
    lw["norm_ffn_g"] = f("norm_ffn_g")[None]
    ne = MOE_GROUPS * MOE_EXPERTS
    lw["w_e_gate"] = f("w_e_gate").reshape(ne, D_MODEL, MOE_FF).astype(BF16)
    lw["w_e_up"] = f("w_e_up").reshape(ne, D_MODEL, MOE_FF).astype(BF16)
    lw["w_e_down"] = f("w_e_down").reshape(ne, MOE_FF, D_MODEL).astype(BF16)
    return lw


def _encoder(x, layers, final_g):
    bsz, seq, _ = x.shape
    plan = _FftPlan(seq)
    xf = x.reshape(bsz * seq, D_MODEL)
    s5_steps = int(math.log2(seq // S5_CHUNK))
    for l, lw in enumerate(layers):
        lru_in, hy_in, gla_in, s5_in = _inproj(xf, lw["norm_mix_g"], lw["w_pack"])
        shp = lambda a: a.reshape(bsz, seq, a.shape[-1])
        yaf, yab = _lru(shp(lru_in), lw["lru_conv_w"], lw["lru_conv_b"], lw["lru_wg"], lw["lru_bg"], lw["lru_lam"])
        yb = _hyena(shp(hy_in), lw, plan)
        yc = _gla(shp(gla_in), lw["gla_wla"], lw["gla_bla"], lw["gla_norm_g"])
        tm, em, om, mu = lw["s5_tables"]
        y5 = _s5(shp(s5_in), tm, em, om, mu[:, :, :s5_steps])
        flat = lambda a: a.reshape(bsz * seq, MIX_WIDTH)
        xf = _merge(xf, lw, flat(yaf), flat(yab), flat(yb), flat(yc), flat(y5), s5_in)
        xf = _moe(xf, lw, final_g, final=(l == len(layers) - 1))
    return xf.reshape(bsz, seq, D_MODEL)


def kernel(x_prompt, x_sample, norm_mix_g, w_in, lru_conv_w, lru_conv_b, lru_wa, lru_ba, lru_wx, lru_bx,
           lru_lambda, hy_conv_w, hy_conv_b, hy_w1, hy_b1, hy_w2, hy_b2, hy_w3, hy_freq, hy_bias,
           gla_wg2, gla_bg, gla_norm_g, s5_lam_re, s5_lam_im, s5_log_dt, s5_b_re, s5_b_im, s5_c_re, s5_c_im,
           s5_d, s5_glu_w, s5_glu_b, w_branch, w_gate, b_gate, w_out, norm_ffn_g, w_router_group,
           b_router_group, w_router_expert, b_router_expert, w_e_gate, w_e_up, w_e_down, final_norm_g):
    w = dict(norm_mix_g=norm_mix_g, w_in=w_in, lru_conv_w=lru_conv_w, lru_conv_b=lru_conv_b, lru_wa=lru_wa,
             lru_ba=lru_ba, lru_wx=lru_wx, lru_bx=lru_bx, lru_lambda=lru_lambda, hy_conv_w=hy_conv_w,
             hy_conv_b=hy_conv_b, hy_w1=hy_w1, hy_b1=hy_b1, hy_w2=hy_w2, hy_b2=hy_b2, hy_w3=hy_w3,
             hy_freq=hy_freq, hy_bias=hy_bias, gla_wg2=gla_wg2, gla_bg=gla_bg, gla_norm_g=gla_norm_g,
             s5_lam_re=s5_lam_re, s5_lam_im=s5_lam_im, s5_log_dt=s5_log_dt, s5_b_re=s5_b_re, s5_b_im=s5_b_im,
             s5_c_re=s5_c_re, s5_c_im=s5_c_im, s5_d=s5_d, s5_glu_w=s5_glu_w, s5_glu_b=s5_glu_b,
             w_branch=w_branch, w_gate=w_gate, b_gate=b_gate, w_out=w_out, norm_ffn_g=norm_ffn_g,
             w_router_group=w_router_group, b_router_group=b_router_group, w_router_expert=w_router_expert,
             b_router_expert=b_router_expert, w_e_gate=w_e_gate, w_e_up=w_e_up, w_e_down=w_e_down)
    max_seq = max(x_prompt.shape[1], x_sample.shape[1])
    nlog = int(math.log2(max_seq // S5_CHUNK))
    layers = [_prep_layer(w, l, nlog) for l in range(norm_mix_g.shape[0])]
    fg = final_norm_g[None]
    return (_encoder(x_prompt, layers, fg), _encoder(x_sample, layers, fg))
```

```python
import functools
import math

import numpy as np
import jax
import jax.numpy as jnp
from jax import lax
from jax.experimental import pallas as pl
from jax.experimental.pallas import tpu as pltpu

F32 = jnp.float32
BF16 = jnp.bfloat16

D_MODEL = 1024
DEPTH = 2
EPS = 1e-6
MIX_WIDTH = D_MODEL // 2
LRU_BLOCKS = 8
LRU_BLOCK = MIX_WIDTH // LRU_BLOCKS
LRU_CONV = 4
LRU_C = 8.0
HY_ORDER = 2
HY_CONV = 3
HY_EMB = 33
HY_BANDS = (HY_EMB - 1) // 2
HY_HIDDEN = 64
HY_DECAY_TARGET = 1e-2
HY_FAST_PCT = 0.3
HY_SLOW_PCT = 1.5
GLA_HEADS = 4
GLA_DK = MIX_WIDTH // 8
GLA_DV = MIX_WIDTH // GLA_HEADS
GLA_RANK = 16
GLA_TAU = 16.0
GLA_CHUNK = 64
S5_GROUP = 16
S5_GROUPS = MIX_WIDTH // S5_GROUP
S5_STATE = 64
MOE_GROUPS = 4
MOE_EXPERTS = 4
MOE_FF = D_MODEL // 4

V7X_LANES = 128
V7X_SUBLANES = 8
V7X_VMEM_BYTES = 64 * 2**20
MIB = 2**20

GLA_PACK = 4 * V7X_LANES * 2 + 512 + 512 + V7X_LANES
PK_LRU = (0, 1024)
PK_HY = (1024, 2560)
PK_GLA = (2560, 2560 + GLA_PACK)
PK_S5 = (PK_GLA[1], PK_GLA[1] + 512)
N_PACK = PK_S5[1]

TOK_TILE = 256
LRU_TILE = 256
CONV_TILE = 512
GLA_TILE = 256
S5_CHUNK = 16
S5_LANE_GROUPS = V7X_LANES // S5_GROUP
S5_SCAN_STEPS = 3
FFT_N1 = 64
FFT_ROWS = 128
FFT_COLS = 256
FFT_KB = 11


def _cparams(sem, vmem_mib):
    return pltpu.CompilerParams(dimension_semantics=sem, vmem_limit_bytes=int(vmem_mib * MIB))


def _rms(x, g):
    return x * lax.rsqrt(jnp.mean(x * x, axis=-1, keepdims=True) + EPS) * g


def _sigmoid(x):
    return 1.0 / (1.0 + jnp.exp(-x))


def _softplus(x):
    return jnp.maximum(x, 0.0) + jnp.log(1.0 + jnp.exp(-jnp.abs(x)))


def _gelu_tanh(x):
    return 0.5 * x * (1.0 + jnp.tanh(math.sqrt(2.0 / math.pi) * (x + 0.044715 * (x * x * x))))


def _silu(x):
    return x * _sigmoid(x)


def _dot(a, b):
    return jnp.dot(a, b, preferred_element_type=F32)


def _dot_hi(a, b):
    return jnp.dot(a, b, preferred_element_type=F32, precision=lax.Precision.HIGHEST)


def _inproj_kernel(x_ref, g_ref, w_ref, lru_ref, hy_ref, gla_ref, s5_ref):
    h = _rms(x_ref[...], g_ref[...]).astype(BF16)
    lru_ref[...] = _dot(h, w_ref[:, PK_LRU[0]:PK_LRU[1]])
    hy_ref[...] = _dot(h, w_ref[:, PK_HY[0]:PK_HY[1]])
    gla_ref[...] = _dot(h, w_ref[:, PK_GLA[0]:PK_GLA[1]])
    s5_ref[...] = _dot(h, w_ref[:, PK_S5[0]:PK_S5[1]])


def _inproj(xf, g, w_pack):
    t = xf.shape[0]
    widths = [PK_LRU[1] - PK_LRU[0], PK_HY[1] - PK_HY[0], PK_GLA[1] - PK_GLA[0], PK_S5[1] - PK_S5[0]]
    return pl.pallas_call(
        _inproj_kernel,
        grid=(t // TOK_TILE,),
        in_specs=[pl.BlockSpec((TOK_TILE, D_MODEL), lambda i: (i, 0)),
                  pl.BlockSpec((1, D_MODEL), lambda i: (0, 0)),
                  pl.BlockSpec((D_MODEL, N_PACK), lambda i: (0, 0))],
        out_specs=[pl.BlockSpec((TOK_TILE, w), lambda i: (i, 0)) for w in widths],
        out_shape=[jax.ShapeDtypeStruct((t, w), F32) for w in widths],
        compiler_params=_cparams(("parallel",), 48),
        name="inproj",
    )(xf, g, w_pack)


def _fill_ext(ext_ref, main, prev8, next8, first, last, tile):
    ext_ref[0:8, :] = jnp.where(first, 0.0, prev8)
    ext_ref[8:8 + tile, :] = main
    ext_ref[8 + tile:16 + tile, :] = jnp.where(last, 0.0, next8)


def _linear_scan_tile(a, b, carry, reverse):
    n = a.shape[0]
    sub = V7X_SUBLANES
    row = lax.broadcasted_iota(jnp.int32, a.shape, 0) % sub
    d = 1
    while d < sub:
        if reverse:
            a_s = pltpu.roll(a, n - d, 0)
            b_s = pltpu.roll(b, n - d, 0)
            valid = row < sub - d
        else:
            a_s = pltpu.roll(a, d, 0)
            b_s = pltpu.roll(b, d, 0)
            valid = row >= d
        b = jnp.where(valid, a * b_s + b, b)
        a = jnp.where(valid, a * a_s, a)
        d *= 2
    ngroup = n // sub
    out = [None] * ngroup
    for g in (range(ngroup - 1, -1, -1) if reverse else range(ngroup)):
        h = b[g * sub:(g + 1) * sub, :] + a[g * sub:(g + 1) * sub, :] * carry
        carry = h[0:1, :] if reverse else h[sub - 1:sub, :]
        out[g] = h
    return jnp.concatenate(out, axis=0), carry


def _lru_kernel(mf_ref, pf_ref, nf_ref, mb_ref, pb_ref, nb_ref, cw_ref, cb_ref, wg_ref, bg_ref, lam_ref,
                of_ref, ob_ref, extf_ref, extb_ref, carry_ref):
    c = pl.program_id(1)
    nc = pl.num_programs(1)
    tile = LRU_TILE

    @pl.when(c == 0)
    def _():
        carry_ref[...] = jnp.zeros_like(carry_ref)

    def one(m_ref, p_ref, n_ref, ext_ref, d, first, last, o_ref):
        x = m_ref[0, :, 0:MIX_WIDTH]
        ga = m_ref[0, :, MIX_WIDTH:2 * MIX_WIDTH]
        _fill_ext(ext_ref, x, p_ref[0], n_ref[0], first, last, tile)
        xc = cb_ref[...] + ext_ref[6:6 + tile, :] * cw_ref[0:1, :]
        xc = xc + ext_ref[7:7 + tile, :] * cw_ref[1:2, :]
        xc = xc + ext_ref[8:8 + tile, :] * cw_ref[2:3, :]
        xc = xc + ext_ref[9:9 + tile, :] * cw_ref[3:4, :]
        z = _dot(xc.astype(BF16), wg_ref[d]) + bg_ref[d]
        gate_r = _sigmoid(z[:, 0:MIX_WIDTH])
        gate_i = _sigmoid(z[:, MIX_WIDTH:2 * MIX_WIDTH])
        log_a = -LRU_C * gate_r * _softplus(-lam_ref[d])
        a = jnp.exp(log_a)
        b = jnp.sqrt(1.0 - a * a) * gate_i * xc
        h, last = _linear_scan_tile(a, b, carry_ref[d:d + 1, :], reverse=(d == 1))
        carry_ref[d:d + 1, :] = last
        o_ref[0] = h * _gelu_tanh(ga)

    one(mf_ref, pf_ref, nf_ref, extf_ref, 0, c == 0, c == nc - 1, of_ref)
    one(mb_ref, pb_ref, nb_ref, extb_ref, 1, c == nc - 1, c == 0, ob_ref)


def _lru(lru_in, cw, cb, wg, bg, lam):
    bsz, seq, _ = lru_in.shape
    tile = LRU_TILE
    nc = seq // tile
    r8 = tile // 8
    last8 = seq // 8 - 1

    def fwd(c):
        return c

    def bwd(c):
        return nc - 1 - c

    def specs(ch):
        return [pl.BlockSpec((1, tile, 2 * MIX_WIDTH), lambda b, c: (b, ch(c), 0)),
                pl.BlockSpec((1, 8, MIX_WIDTH), lambda b, c: (b, jnp.maximum(ch(c) * r8 - 1, 0), 0)),
                pl.BlockSpec((1, 8, MIX_WIDTH), lambda b, c: (b, jnp.minimum((ch(c) + 1) * r8, last8), 0))]

    const = lambda shape: pl.BlockSpec(shape, lambda b, c: (0,) * len(shape))
    return pl.pallas_call(
        _lru_kernel,
        grid=(bsz, nc),
        in_specs=specs(fwd) + specs(bwd) + [const(cw.shape), const(cb.shape), const(wg.shape), const(bg.shape),
                                            const(lam.shape)],
        out_specs=[pl.BlockSpec((1, tile, MIX_WIDTH), lambda b, c: (b, c, 0)),
                   pl.BlockSpec((1, tile, MIX_WIDTH), lambda b, c: (b, nc - 1 - c, 0))],
        out_shape=[jax.ShapeDtypeStruct((bsz, seq, MIX_WIDTH), F32)] * 2,
        scratch_shapes=[pltpu.VMEM((tile + 16, MIX_WIDTH), F32), pltpu.VMEM((tile + 16, MIX_WIDTH), F32),
                        pltpu.VMEM((8, MIX_WIDTH), F32)],
        compiler_params=_cparams(("parallel", "arbitrary"), 40),
        name="lru",
    )(lru_in, lru_in, lru_in, lru_in, lru_in, lru_in, cw, cb, wg, bg, lam)


def _hyconv_kernel(m_ref, p_ref, n_ref, cw_ref, cb_ref, o_ref, ext_ref):
    c = pl.program_id(1)
    nc = pl.num_programs(1)
    tile = CONV_TILE
    _fill_ext(ext_ref, m_ref[0], p_ref[0], n_ref[0], c == 0, c == nc - 1, tile)
    y = cb_ref[...] + ext_ref[7:7 + tile, :] * cw_ref[0:1, :]
    y = y + ext_ref[8:8 + tile, :] * cw_ref[1:2, :]
    y = y + ext_ref[9:9 + tile, :] * cw_ref[2:3, :]
    o_ref[0] = y


def _hyconv(hy_in, cw, cb):
    bsz, seq, width = hy_in.shape
    tile = CONV_TILE
    r8 = tile // 8
    last8 = seq // 8 - 1
    const = lambda shape: pl.BlockSpec(shape, lambda b, c: (0,) * len(shape))
    return pl.pallas_call(
        _hyconv_kernel,
        grid=(bsz, seq // tile),
        in_specs=[pl.BlockSpec((1, tile, width), lambda b, c: (b, c, 0)),
                  pl.BlockSpec((1, 8, width), lambda b, c: (b, jnp.maximum(c * r8 - 1, 0), 0)),
                  pl.BlockSpec((1, 8, width), lambda b, c: (b, jnp.minimum((c + 1) * r8, last8), 0)),
                  const(cw.shape), const(cb.shape)],
        out_specs=pl.BlockSpec((1, tile, width), lambda b, c: (b, c, 0)),
        out_shape=jax.ShapeDtypeStruct((bsz, seq, width), F32),
        scratch_shapes=[pltpu.VMEM((tile + 16, width), F32)],
        compiler_params=_cparams(("parallel", "parallel"), 40),
        name="hyconv",
    )(hy_in, hy_in, hy_in, cw, cb)


class _FftPlan:
    def __init__(self, seq):
        n = 2 * seq
        n1 = FFT_N1
        n2 = n // n1
        assert n1 * n2 == n and n2 % 16 == 0
        h1 = n1 // 2 + 1
        nh = n1 // 2
        self.n, self.n1, self.n2, self.h1, self.nh = n, n1, n2, h1, nh
        self.rows = min(FFT_ROWS, n2)
        assert n2 % self.rows == 0 and h1 % FFT_KB == 0
        k1 = np.arange(h1, dtype=np.float64)
        m1 = np.arange(nh, dtype=np.float64)
        r = np.arange(8, dtype=np.float64)
        ang = -2.0 * np.pi * (k1[:, None, None] * m1[None, None, :] / n1 + r[None, :, None] * k1[:, None, None] / n)
        e = np.exp(1j * ang)
        fa = np.zeros((h1, 8, nh, 8), np.complex128)
        for rr in range(8):
            fa[:, rr, :, rr] = e[:, rr, :]
        fa = fa.reshape(h1 * 8, nh * 8)
        self.fa = jnp.asarray(np.concatenate([fa.real, fa.imag], axis=0), BF16)
        ck = np.where((k1 == 0) | (k1 == n1 // 2), 1.0, 2.0)
        ec = np.conj(e) * ck[:, None, None] / n
        fc = np.zeros((nh, 8, 2, h1, 8), np.float64)
        for rr in range(8):
            fc[:, rr, 0, :, rr] = ec[:, rr, :].real.T
            fc[:, rr, 1, :, rr] = -ec[:, rr, :].imag.T
        self.fc = jnp.asarray(fc.reshape(nh * 8, 2 * h1 * 8), BF16)
        rg = np.arange(n2 // 8, dtype=np.float64)
        tw = np.exp(-2j * np.pi * 8.0 * rg[None, :] * k1[:, None] / n)
        self.tw = jnp.asarray(np.concatenate([tw.real, tw.imag], axis=0), F32)
        tw1 = np.exp(-2j * np.pi * k1 / n)
        self.tw1 = jnp.asarray(np.stack([tw1.real, tw1.imag]), F32)
        q = np.arange(n2, dtype=np.float64)
        f2 = np.exp(-2j * np.pi * np.outer(q, q) / n2)
        fr, fi = f2.real, f2.imag
        self.gb = jnp.asarray(np.block([[fr, -fi], [fi, fr]]), BF16)
        self.gbi = jnp.asarray(np.block([[fr, fi], [-fi, fr]]), BF16)


def _fft_a_kernel(tw_ref, x_ref, fa_ref, a_ref, *, h1, nh, rows):
    rb = pl.program_id(2)
    cols = x_ref.shape[-1]

    def stage(rg):
        xg = x_ref[0, :, pl.ds(pl.multiple_of(rg * 8, 8), 8), :].reshape(nh * 8, cols).astype(BF16)
        return _dot(fa_ref[...], xg)

    def body(i, carry):
        p0 = stage(2 * i)
        p1 = stage(2 * i + 1)
        g0 = rb * (rows // 8) + 2 * i
        for k in range(h1):
            outs = []
            for p, g in ((p0, g0), (p1, g0 + 1)):
                pr = p[k * 8:(k + 1) * 8, :]
                pi = p[(h1 + k) * 8:(h1 + k + 1) * 8, :]
                tr = tw_ref[k, g]
                ti = tw_ref[h1 + k, g]
                outs.append((pr * tr - pi * ti, pr * ti + pi * tr))
            dst = pl.ds(pl.multiple_of(i * 16, 16), 16)
            a_ref[0, k, 0, dst, :] = jnp.concatenate([outs[0][0], outs[1][0]], axis=0).astype(BF16)
            a_ref[0, k, 1, dst, :] = jnp.concatenate([outs[0][1], outs[1][1]], axis=0).astype(BF16)
        return carry

    lax.fori_loop(0, rows // 16, body, 0)


def _fft_a(x4, plan, col_off=0, ncols=None):
    bq, nh, n2, width = x4.shape
    ncols = width if ncols is None else ncols
    cb0 = col_off // FFT_COLS
    rows = plan.rows
    kern = functools.partial(_fft_a_kernel, h1=plan.h1, nh=nh, rows=rows)
    return pl.pallas_call(
        kern,
        grid=(bq, ncols // FFT_COLS, n2 // rows),
        in_specs=[pl.BlockSpec(memory_space=pltpu.SMEM),
                  pl.BlockSpec((1, nh, rows, FFT_COLS), lambda b, c, r: (b, 0, r, cb0 + c)),
                  pl.BlockSpec(plan.fa.shape, lambda b, c, r: (0, 0))],
        out_specs=pl.BlockSpec((1, plan.h1, 2, rows, FFT_COLS), lambda b, c, r: (b, 0, 0, r, c)),
        out_shape=jax.ShapeDtypeStruct((bq, plan.h1, 2, n2, ncols), BF16),
        compiler_params=_cparams(("parallel", "parallel", "parallel"), 40),
        name="fft_outer_fwd",
    )(plan.tw, x4, plan.fa)


def _fft_mid_kernel(a_ref, kf_ref, gb_ref, gbi_ref, o_ref, *, n2):
    kb = a_ref.shape[1]
    cols = a_ref.shape[-1]
    for k in range(kb):
        y = _dot(gb_ref[...], a_ref[0, k].reshape(2 * n2, cols))
        yr, yi = y[0:n2, :], y[n2:2 * n2, :]
        kr, ki = kf_ref[0, k, 0], kf_ref[0, k, 1]
        z = jnp.concatenate([yr * kr - yi * ki, yr * ki + yi * kr], axis=0).astype(BF16)
        o_ref[0, k] = _dot(gbi_ref[...], z).reshape(2, n2, cols).astype(BF16)


def _fft_mid(a, kf, order, plan):
    bq, h1, _, n2, width = a.shape
    kern = functools.partial(_fft_mid_kernel, n2=n2)
    blk = (1, FFT_KB, 2, n2, FFT_COLS)
    return pl.pallas_call(
        kern,
        grid=(bq, width // FFT_COLS, h1 // FFT_KB),
        in_specs=[pl.BlockSpec(blk, lambda b, c, k: (b, k, 0, 0, c)),
                  pl.BlockSpec(blk, lambda b, c, k: (order, k, 0, 0, c)),
                  pl.BlockSpec(plan.gb.shape, lambda b, c, k: (0, 0)),
                  pl.BlockSpec(plan.gbi.shape, lambda b, c, k: (0, 0))],
        out_specs=pl.BlockSpec(blk, lambda b, c, k: (b, k, 0, 0, c)),
        out_shape=jax.ShapeDtypeStruct(a.shape, BF16),
        compiler_params=_cparams(("parallel", "parallel", "parallel"), 40),
        name="fft_inner_mul",
    )(a, kf, plan.gb, plan.gbi)


def _fft_c_kernel(tw_ref, b_ref, u_ref, g_ref, bias_ref, fc_ref, o_ref, *, h1, nh, rows):
    rb = pl.program_id(2)
    cols = o_ref.shape[-1]

    def body(i, carry):
        src = pl.ds(pl.multiple_of(i * 16, 16), 16)
        tiles = [[b_ref[0, k, p, src, :].astype(F32) for p in range(2)] for k in range(h1)]
        for half in range(2):
            g = rb * (rows // 8) + 2 * i + half
            re_rows, im_rows = [], []
            for k in range(h1):
                br = tiles[k][0][half * 8:(half + 1) * 8, :]
                bi = tiles[k][1][half * 8:(half + 1) * 8, :]
                tr = tw_ref[k, g]
                ti = tw_ref[h1 + k, g]
                re_rows.append(br * tr + bi * ti)
                im_rows.append(bi * tr - br * ti)
            s = jnp.concatenate(re_rows + im_rows, axis=0).astype(BF16)
            y = _dot(fc_ref[...], s).reshape(nh, 8, cols)
            dst = pl.ds(pl.multiple_of((2 * i + half) * 8, 8), 8)
            u = u_ref[0, :, dst, :]
            o_ref[0, :, dst, :] = (y + u * bias_ref[...]) * g_ref[0, :, dst, :]
        return carry

    lax.fori_loop(0, rows // 16, body, 0)


def _fft_c(bm, u4, u_off, g4, g_off, bias, plan):
    bq, h1, _, n2, width = bm.shape
    nh = plan.nh
    rows = plan.rows
    ub, gbk = u_off // FFT_COLS, g_off // FFT_COLS
    kern = functools.partial(_fft_c_kernel, h1=h1, nh=nh, rows=rows)
    xblk = (1, nh, rows, FFT_COLS)
    return pl.pallas_call(
        kern,
        grid=(bq, width // FFT_COLS, n2 // rows),
        in_specs=[pl.BlockSpec(memory_space=pltpu.SMEM),
                  pl.BlockSpec((1, h1, 2, rows, FFT_COLS), lambda b, c, r: (b, 0, 0, r, c)),
                  pl.BlockSpec(xblk, lambda b, c, r: (b, 0, r, ub + c)),
                  pl.BlockSpec(xblk, lambda b, c, r: (b, 0, r, gbk + c)),
                  pl.BlockSpec((1, FFT_COLS), lambda b, c, r: (0, c)),
                  pl.BlockSpec(plan.fc.shape, lambda b, c, r: (0, 0))],
        out_specs=pl.BlockSpec(xblk, lambda b, c, r: (b, 0, r, c)),
        out_shape=jax.ShapeDtypeStruct((bq, nh, n2, width), F32),
        compiler_params=_cparams(("parallel", "parallel", "parallel"), 48),
        name="fft_outer_inv",
    )(plan.tw, bm, u4, g4, bias, plan.fc)


def _hyfilt_gen_kernel(w1_ref, b1_ref, w2_ref, b2_ref, w3_ref, fr_ref, o_ref, ss_ref, *, seq):
    rblk = pl.program_id(0)
    tile, cols = o_ref.shape[1], o_ref.shape[2]
    irow = lax.broadcasted_iota(jnp.int32, (tile, V7X_LANES), 0) + rblk * tile
    row = irow.astype(F32)
    lane = lax.broadcasted_iota(jnp.int32, (tile, V7X_LANES), 1)
    t = row / (seq - 1.0)
    omega = (2.0 * math.pi / seq) * row
    band_step = (HY_BANDS - 1 - 1e-4) / (HY_BANDS - 1)
    is_cos = (lane >= 1) & (lane <= HY_BANDS)
    is_sin = (lane > HY_BANDS) & (lane <= 2 * HY_BANDS)
    bidx = jnp.where(is_cos, lane - 1, lane - 1 - HY_BANDS).astype(F32)
    ang = omega * (1e-4 + band_step * bidx)
    z = jnp.where(lane == 0, t, jnp.where(is_cos, jnp.cos(ang), jnp.where(is_sin, -jnp.sin(ang), 0.0)))
    fr = fr_ref[...]
    hid = jnp.sin(fr * (_dot_hi(z, w1_ref[...]) + b1_ref[...]))
    hid = jnp.sin(fr * (_dot_hi(hid, w2_ref[...]) + b2_ref[...]))
    filt = _dot_hi(hid, w3_ref[...])
    col = lax.broadcasted_iota(jnp.int32, (1, cols), 1)
    chan = (col % MIX_WIDTH).astype(F32)
    max_decay = math.log(HY_DECAY_TARGET) / HY_FAST_PCT
    min_decay = math.log(HY_DECAY_TARGET) / HY_SLOW_PCT
    delta = jnp.abs(min_decay + (max_decay - min_decay) / (MIX_WIDTH - 1) * chan)
    filt = filt * jnp.exp(-t[:, 0:1] * delta)
    is_bwd = (col // MIX_WIDTH) % 2 == 1
    rows_c = lax.broadcasted_iota(jnp.int32, (tile, cols), 0) + rblk * tile
    filt = jnp.where(is_bwd & (rows_c == seq - 1), 0.0, filt)
    o_ref[0] = filt

    @pl.when(rblk == 0)
    def _():
        ss_ref[...] = jnp.zeros_like(ss_ref)

    ss_ref[...] += jnp.sum(filt * filt, axis=0, keepdims=True)


def _hyfilt_gen(seq, w1p, b1, w2, b2, w3, freq):
    ncol = w3.shape[1]
    tile = min(seq, 512)
    const = lambda shape: pl.BlockSpec(shape, lambda r: (0,) * len(shape))
    kern = functools.partial(_hyfilt_gen_kernel, seq=seq)
    return pl.pallas_call(
        kern,
        grid=(seq // tile,),
        in_specs=[const(w1p.shape), const(b1.shape), const(w2.shape), const(b2.shape), const(w3.shape),
                  const(freq.shape)],
        out_specs=[pl.BlockSpec((1, tile, ncol), lambda r: (0, r, 0)),
                   pl.BlockSpec((1, ncol), lambda r: (0, 0))],
        out_shape=[jax.ShapeDtypeStruct((1, seq, ncol), F32), jax.ShapeDtypeStruct((1, ncol), F32)],
        compiler_params=_cparams(("arbitrary",), 40),
        name="hyena_filter_gen",
    )(w1p, b1, w2, b2, w3, freq)


def _hyfilt_spec_kernel(tw1_ref, a0_ref, a1_ref, ss0_ref, ss1_ref, gb_ref, o_ref, *, n2):
    kblk = pl.program_id(2)
    kb = a0_ref.shape[1]
    cols = a0_ref.shape[-1]
    h1 = tw1_ref.shape[1]
    scale = lax.rsqrt(ss0_ref[...] + ss1_ref[...] + EPS)
    ang2 = (-2.0 * math.pi / n2) * lax.broadcasted_iota(jnp.int32, (n2, cols), 0).astype(F32)
    cr, ci = jnp.cos(ang2), jnp.sin(ang2)
    for k in range(kb):
        y0 = _dot(gb_ref[...], a0_ref[0, k].reshape(2 * n2, cols))
        y1 = _dot(gb_ref[...], a1_ref[0, k].reshape(2 * n2, cols))
        sr = tw1_ref[0, kblk * kb + k]
        si = tw1_ref[1, kblk * kb + k]
        wr, wi = cr * sr - ci * si, cr * si + ci * sr
        y1r, y1i = y1[0:n2, :], y1[n2:2 * n2, :]
        o_ref[0, k, 0] = (y0[0:n2, :] + (wr * y1r - wi * y1i)) * scale
        o_ref[0, k, 1] = (y0[n2:2 * n2, :] - (wr * y1i + wi * y1r)) * scale


def _hyfilt_spec(af, ss, plan):
    _, h1, _, n2, _ = af.shape
    ncb = MIX_WIDTH // FFT_COLS
    blk = (1, FFT_KB, 2, n2, FFT_COLS)
    kern = functools.partial(_hyfilt_spec_kernel, n2=n2)
    return pl.pallas_call(
        kern,
        grid=(HY_ORDER, ncb, h1 // FFT_KB),
        in_specs=[pl.BlockSpec(memory_space=pltpu.SMEM),
                  pl.BlockSpec(blk, lambda o, c, k: (0, k, 0, 0, o * 2 * ncb + c)),
                  pl.BlockSpec(blk, lambda o, c, k: (0, k, 0, 0, o * 2 * ncb + ncb + c)),
                  pl.BlockSpec((1, FFT_COLS), lambda o, c, k: (0, o * 2 * ncb + c)),
                  pl.BlockSpec((1, FFT_COLS), lambda o, c, k: (0, o * 2 * ncb + ncb + c)),
                  pl.BlockSpec(plan.gb.shape, lambda o, c, k: (0, 0))],
        out_specs=pl.BlockSpec(blk, lambda o, c, k: (o, k, 0, 0, c)),
        out_shape=jax.ShapeDtypeStruct((HY_ORDER, h1, 2, n2, MIX_WIDTH), F32),
        compiler_params=_cparams(("parallel", "parallel", "parallel"), 40),
        name="hyena_filter_spectrum",
    )(plan.tw1, af, af, ss, ss, plan.gb)


def _hyena(hy_in, lw, plan):
    bsz, seq, _ = hy_in.shape
    zc = _hyconv(hy_in, lw["hy_conv_w"], lw["hy_conv_b"])
    filt, ss = _hyfilt_gen(seq, lw["hy_w1p"], lw["hy_b1"], lw["hy_w2"], lw["hy_b2"], lw["hy_w3"], lw["hy_freq"])
    af = _fft_a(filt.reshape(1, plan.nh, plan.n2, filt.shape[-1]), plan)
    kf = _hyfilt_spec(af, ss, plan)
    zc4 = zc.reshape(bsz, plan.nh, plan.n2, 3 * MIX_WIDTH)
    a = _fft_a(zc4, plan, col_off=0, ncols=MIX_WIDTH)
    bm = _fft_mid(a, kf, 0, plan)
    z1 = _fft_c(bm, zc4, 0, zc4, MIX_WIDTH, lw["hy_bias"][0:1], plan)
    a = _fft_a(z1, plan)
    bm = _fft_mid(a, kf, 1, plan)
    z2 = _fft_c(bm, z1, 0, zc4, 2 * MIX_WIDTH, lw["hy_bias"][1:2], plan)
    return z2.reshape(bsz, seq, MIX_WIDTH)


def _gla_kernel(*refs, reverse):
    if reverse:
        x_ref, of_ref, wla_ref, bla_ref, ng_ref, o_ref, st_ref = refs
    else:
        x_ref, wla_ref, bla_ref, o_ref, st_ref = refs
    c = pl.program_id(1)
    tile = GLA_TILE
    ck = GLA_CHUNK
    nck = tile // ck
    hw = V7X_LANES
    nh = GLA_HEADS

    @pl.when(c == 0)
    def _():
        st_ref[...] = jnp.zeros_like(st_ref)

    q = x_ref[0, :, 0:nh * hw] * (GLA_DK ** -0.5)
    k = x_ref[0, :, nh * hw:2 * nh * hw]
    v = x_ref[0, :, 2 * nh * hw:3 * nh * hw]
    lr = x_ref[0, :, 4 * nh * hw:4 * nh * hw + hw]
    zl = _dot(lr.astype(BF16), wla_ref[...]) + bla_ref[...]
    la = (jnp.minimum(zl, 0.0) - jnp.log(1.0 + jnp.exp(-jnp.abs(zl)))) / GLA_TAU

    row = lax.broadcasted_iota(jnp.int32, la.shape, 0) % ck
    bcum = la
    d = 1
    while d < ck:
        if reverse:
            bcum = bcum + jnp.where(row < ck - d, pltpu.roll(bcum, tile - d, 0), 0.0)
        else:
            bcum = bcum + jnp.where(row >= d, pltpu.roll(bcum, d, 0), 0.0)
        d *= 2
    b3 = bcum.reshape(nck, ck, nh * hw)
    blast = b3[:, 0:1, :] if reverse else b3[:, ck - 1:ck, :]
    q_e = (q * jnp.exp(bcum)).astype(BF16)
    k_e = (k * jnp.exp(-bcum)).astype(BF16)
    k_d = (k.reshape(nck, ck, nh * hw) * jnp.exp(blast - b3)).reshape(tile, nh * hw).astype(BF16)
    gch = jnp.exp(blast)
    vb = v.astype(BF16)

    ri = lax.broadcasted_iota(jnp.int32, (ck, ck), 0)
    ci = lax.broadcasted_iota(jnp.int32, (ck, ck), 1)
    mask = (ri <= ci) if reverse else (ri >= ci)
    order = range(nck - 1, -1, -1) if reverse else range(nck)
    outs = [None] * nck
    for n in order:
        rs = slice(n * ck, (n + 1) * ck)
        heads = []
        for h in range(nh):
            ls = slice(h * hw, (h + 1) * hw)
            qe, ke, kd, vh = q_e[rs, ls], k_e[rs, ls], k_d[rs, ls], vb[rs, ls]
            st = st_ref[h]
            sc = lax.dot_general(qe, ke, (((1,), (1,)), ((), ())), preferred_element_type=F32)
            sc = jnp.where(mask, sc, 0.0).astype(BF16)
            o = _dot(sc, vh) + lax.dot_general(qe, st.astype(BF16), (((1,), (1,)), ((), ())),
                                               preferred_element_type=F32)
            upd = lax.dot_general(vh, kd, (((0,), (0,)), ((), ())), preferred_element_type=F32)
            st_ref[h] = st * gch[n, :, ls] + upd
            heads.append(o)
        outs[n] = jnp.concatenate(heads, axis=1)
    o_dir = jnp.concatenate(outs, axis=0)
    if not reverse:
        o_ref[0] = o_dir
        return
    o = of_ref[0] + o_dir
    g = x_ref[0, :, 3 * nh * hw:4 * nh * hw]
    normed = []
    for h in range(nh):
        oh = o[:, h * hw:(h + 1) * hw]
        normed.append(oh * lax.rsqrt(jnp.mean(oh * oh, axis=-1, keepdims=True) + EPS))
    o = jnp.concatenate(normed, axis=1) * ng_ref[...]
    o_ref[0] = o * _silu(g)


def _gla(gla_in, wla, bla, norm_g):
    bsz, seq, width = gla_in.shape
    tile = GLA_TILE
    nc = seq // tile
    const = lambda shape: pl.BlockSpec(shape, lambda b, c: (0,) * len(shape))
    out_shape = jax.ShapeDtypeStruct((bsz, seq, MIX_WIDTH), F32)
    scratch = [pltpu.VMEM((GLA_HEADS, GLA_DV, V7X_LANES), F32)]
    o_f = pl.pallas_call(
        functools.partial(_gla_kernel, reverse=False),
        grid=(bsz, nc),
        in_specs=[pl.BlockSpec((1, tile, width), lambda b, c: (b, c, 0)), const(wla.shape[1:]), const(bla.shape[1:])],
        out_specs=pl.BlockSpec((1, tile, MIX_WIDTH), lambda b, c: (b, c, 0)),
        out_shape=out_shape,
        scratch_shapes=scratch,
        compiler_params=_cparams(("parallel", "arbitrary"), 40),
        name="gla_fwd",
    )(gla_in, wla[0], bla[0])
    return pl.pallas_call(
        functools.partial(_gla_kernel, reverse=True),
        grid=(bsz, nc),
        in_specs=[pl.BlockSpec((1, tile, width), lambda b, c: (b, nc - 1 - c, 0)),
                  pl.BlockSpec((1, tile, MIX_WIDTH), lambda b, c: (b, nc - 1 - c, 0)),
                  const(wla.shape[1:]), const(bla.shape[1:]), const(norm_g.shape)],
        out_specs=pl.BlockSpec((1, tile, MIX_WIDTH), lambda b, c: (b, nc - 1 - c, 0)),
        out_shape=out_shape,
        scratch_shapes=scratch,
        compiler_params=_cparams(("parallel", "arbitrary"), 40),
        name="gla_bwd",
    )(gla_in, o_f, wla[1], bla[1], norm_g)


def _s5_kernel(u_ref, t_ref, e_ref, o_ref_w, mu_ref, mup_ref, y_ref):
    ch = S5_CHUNK
    sub = V7X_SUBLANES
    nrow = u_ref.shape[1] // ch
    half = S5_LANE_GROUPS * 2 * S5_STATE
    swap = lambda x: pltpu.roll(x, half // 2, 1)
    u = jnp.concatenate([u_ref[0, pl.ds(i, nrow, stride=ch), :] for i in range(ch)], axis=1).astype(BF16)
    y = _dot(u, t_ref[0])
    he = _dot(u, e_ref[0])
    row = lax.broadcasted_iota(jnp.int32, (nrow, half), 0)
    rsub = row % sub
    states = []
    for d in range(2):
        h = he[:, d * half:(d + 1) * half]
        for s in range(S5_SCAN_STEPS):
            step = 2 ** s
            if d == 0:
                hs = jnp.where(rsub >= step, pltpu.roll(h, step, 0), 0.0)
            else:
                hs = jnp.where(rsub < sub - step, pltpu.roll(h, nrow - step, 0), 0.0)
            h = h + hs * mu_ref[0, d, s, 0:1, :] + swap(hs) * mu_ref[0, d, s, 1:2, :]
        ngroup = nrow // sub
        carry = jnp.zeros((1, half), F32)
        out = [None] * ngroup
        for g in (range(ngroup) if d == 0 else range(ngroup - 1, -1, -1)):
            hg = h[g * sub:(g + 1) * sub, :] + carry * mup_ref[0, d, 0] + swap(carry) * mup_ref[0, d, 1]
            carry = hg[sub - 1:sub, :] if d == 0 else hg[0:1, :]
            out[g] = hg
        h = jnp.concatenate(out, axis=0)
        if d == 0:
            h = jnp.where(row >= 1, pltpu.roll(h, 1, 0), 0.0)
        else:
            h = jnp.where(row < nrow - 1, pltpu.roll(h, nrow - 1, 0), 0.0)
        states.append(h)
    hp = jnp.concatenate(states, axis=1).astype(BF16)
    y = y + lax.dot_general(hp, o_ref_w[0], (((1,), (1,)), ((), ())), preferred_element_type=F32)
    for j in range(ch):
        y_ref[0, pl.ds(j, nrow, stride=ch), :] = y[:, j * V7X_LANES:(j + 1) * V7X_LANES]


def _s5(s5_in, tblk, eblk, oblk, mu, mup):
    bsz, seq, width = s5_in.shape
    nb = width // V7X_LANES
    once = pl.Buffered(1)
    wspec = lambda arr: pl.BlockSpec((1,) + arr.shape[1:], lambda k, b: (k,) + (0,) * (arr.ndim - 1),
                                     pipeline_mode=once)
    xspec = pl.BlockSpec((1, seq, V7X_LANES), lambda k, b: (b, 0, k), pipeline_mode=once)
    return pl.pallas_call(
        _s5_kernel,
        grid=(nb, bsz),
        in_specs=[xspec, wspec(tblk), wspec(eblk), wspec(oblk), wspec(mu), wspec(mup)],
        out_specs=pl.BlockSpec((1, seq, V7X_LANES), lambda k, b: (b, 0, k)),
        out_shape=jax.ShapeDtypeStruct(s5_in.shape, F32),
        compiler_params=_cparams(("arbitrary", "arbitrary"), 60),
        name="s5",
    )(s5_in, tblk, eblk, oblk, mu, mup)


def _s5_tables(lam_re, lam_im, log_dt, b_re, b_im, c_re, c_im):
    ch = S5_CHUNK
    hi = lax.Precision.HIGHEST
    cmul = lambda x, y: (x[0] * y[0] - x[1] * y[1], x[0] * y[1] + x[1] * y[0])
    lr_, li_ = lam_re.astype(F32), lam_im.astype(F32)
    dt = jnp.exp(log_dt.astype(F32))[..., None]
    ar, ai = lr_ * dt, li_ * dt

    def lam_pow(tau):
        t = jnp.asarray(tau, F32)
        t = t.reshape(t.shape + (1,) * 3)
        mag = jnp.exp(t * ar)
        return mag * jnp.cos(t * ai), mag * jnp.sin(t * ai)

    lb = lam_pow(jnp.ones((), F32))
    num = (lb[0] - 1.0, lb[1])
    den = lr_ * lr_ + li_ * li_
    ratio = ((num[0] * lr_ + num[1] * li_) / den, (num[1] * lr_ - num[0] * li_) / den)
    b_bar = cmul((ratio[0][..., None], ratio[1][..., None]), (b_re.astype(F32), b_im.astype(F32)))
    cc = (c_re.astype(F32), c_im.astype(F32))

    nb, gb, hh, pp = S5_GROUPS // S5_LANE_GROUPS, S5_LANE_GROUPS, S5_GROUP, S5_STATE
    eye = jnp.eye(gb, dtype=F32)

    def b_base(x):
        x = jnp.transpose(x.reshape(2, nb, gb, pp, hh), (1, 2, 4, 0, 3))
        return (x[:, :, :, :, None, :] * eye[None, :, None, None, :, None]).reshape(nb, gb * hh, 2 * gb * pp)

    def c_base(x):
        x = jnp.transpose(x.reshape(2, nb, gb, hh, pp), (1, 2, 3, 0, 4))
        return (x[:, :, :, :, None, :] * eye[None, :, None, None, :, None]).reshape(nb, gb * hh, 2 * gb * pp)

    row = lambda x: jnp.transpose(x.reshape(2, nb, gb * pp), (1, 0, 2)).reshape(nb, 1, 2 * gb * pp)
    base = (b_base(b_bar[0]), b_base(b_bar[1]), c_base(cc[0]), c_base(cc[1]), row(ar), row(ai))
    dtab = _s5_lag_tables(*base)
    return _s5_block_tables(dtab, *base)


def _lam_pow(tau, ar, ai):
    mag = jnp.exp(tau * ar)
    return mag * jnp.cos(tau * ai), mag * jnp.sin(tau * ai)


def _s5_lag_kernel(btr_ref, bti_ref, ctr_ref, cti_ref, ar_ref, ai_ref, d_ref):
    ch = S5_CHUNK
    half = btr_ref.shape[-1] // 2
    tdot = lambda a, b: lax.dot_general(a, b, (((1,), (1,)), ((), ())), preferred_element_type=F32,
                                        precision=lax.Precision.HIGHEST)
    for d in range(2):
        ls = slice(d * half, (d + 1) * half)
        btr, bti, ctr, cti = btr_ref[0, :, ls], bti_ref[0, :, ls], ctr_ref[0, :, ls], cti_ref[0, :, ls]
        for lag in range(ch):
            lr, li = _lam_pow(float(lag), ar_ref[0, :, ls], ai_ref[0, :, ls])
            val = tdot(btr * lr - bti * li, ctr) - tdot(btr * li + bti * lr, cti)
            idx = ch - 1 + lag if d == 0 else ch - 1 - lag
            if d == 1 and lag == 0:
                d_ref[0, idx] = d_ref[0, idx] + val
            else:
                d_ref[0, idx] = val


def _s5_lag_tables(btr, bti, ctr, cti, ar, ai):
    nb = btr.shape[0]
    spec = lambda a: pl.BlockSpec((1,) + a.shape[1:], lambda k: (k, 0, 0))
    nlag = 2 * S5_CHUNK - 1
    return pl.pallas_call(
        _s5_lag_kernel,
        grid=(nb,),
        in_specs=[spec(a) for a in (btr, bti, ctr, cti, ar, ai)],
        out_specs=pl.BlockSpec((1, nlag, V7X_LANES, V7X_LANES), lambda k: (k, 0, 0, 0)),
        out_shape=jax.ShapeDtypeStruct((nb, nlag, V7X_LANES, V7X_LANES), F32),
        compiler_params=_cparams(("parallel",), 32),
        name="s5_lag_tables",
    )(btr, bti, ctr, cti, ar, ai)


def _s5_block_kernel(d_ref, btr_ref, bti_ref, ctr_ref, cti_ref, ar_ref, ai_ref, t_ref, e_ref, ot_ref, mu_ref,
                     mup_ref):
    ch = S5_CHUNK
    i = pl.program_id(1)
    half = btr_ref.shape[-1] // 2
    fi = i.astype(F32)
    for j in range(ch):
        t_ref[0, :, j * V7X_LANES:(j + 1) * V7X_LANES] = d_ref[0, j - i + ch - 1].astype(BF16)
    for d in range(2):
        ls = slice(d * half, (d + 1) * half)
        ar, ai = ar_ref[0, :, ls], ai_ref[0, :, ls]
        lr, li = _lam_pow(fi if d == 1 else (ch - 1.0) - fi, ar, ai)
        btr, bti = btr_ref[0, :, ls], bti_ref[0, :, ls]
        e_ref[0, :, 2 * d * half:(2 * d + 1) * half] = (btr * lr - bti * li).astype(BF16)
        e_ref[0, :, (2 * d + 1) * half:(2 * d + 2) * half] = (btr * li + bti * lr).astype(BF16)
        lr, li = _lam_pow(fi + 1.0 if d == 0 else ch - fi, ar, ai)
        ctr, cti = ctr_ref[0, :, ls], cti_ref[0, :, ls]
        ot_ref[0, :, 2 * d * half:(2 * d + 1) * half] = (ctr * lr - cti * li).astype(BF16)
        ot_ref[0, :, (2 * d + 1) * half:(2 * d + 2) * half] = (-(ctr * li + cti * lr)).astype(BF16)
        for s in range(S5_SCAN_STEPS):
            lr, li = _lam_pow(float(ch * 2 ** s), ar, ai)
            mu_ref[0, d, s, 0:1, :] = jnp.concatenate([lr, lr], axis=1)
            mu_ref[0, d, s, 1:2, :] = jnp.concatenate([-li, li], axis=1)
        for r in range(V7X_SUBLANES):
            lr, li = _lam_pow(float(ch * (r + 1 if d == 0 else V7X_SUBLANES - r)), ar, ai)
            mup_ref[0, d, 0, r:r + 1, :] = jnp.concatenate([lr, lr], axis=1)
            mup_ref[0, d, 1, r:r + 1, :] = jnp.concatenate([-li, li], axis=1)


def _s5_block_tables(dtab, btr, bti, ctr, cti, ar, ai):
    nb = btr.shape[0]
    ch = S5_CHUNK
    big = ch * V7X_LANES
    wide = 2 * btr.shape[-1]
    spec = lambda a: pl.BlockSpec((1,) + a.shape[1:], lambda k, i: (k,) + (0,) * (a.ndim - 1))
    tile = lambda w: pl.BlockSpec((1, V7X_LANES, w), lambda k, i: (k, i, 0))
    return pl.pallas_call(
        _s5_block_kernel,
        grid=(nb, ch),
        in_specs=[spec(a) for a in (dtab, btr, bti, ctr, cti, ar, ai)],
        out_specs=[tile(big), tile(wide), tile(wide),
                   pl.BlockSpec((1, 2, S5_SCAN_STEPS, 2, wide // 2), lambda k, i: (k, 0, 0, 0, 0)),
                   pl.BlockSpec((1, 2, 2, V7X_SUBLANES, wide // 2), lambda k, i: (k, 0, 0, 0, 0))],
        out_shape=[jax.ShapeDtypeStruct((nb, big, big), BF16), jax.ShapeDtypeStruct((nb, big, wide), BF16),
                   jax.ShapeDtypeStruct((nb, big, wide), BF16),
                   jax.ShapeDtypeStruct((nb, 2, S5_SCAN_STEPS, 2, wide // 2), F32),
                   jax.ShapeDtypeStruct((nb, 2, 2, V7X_SUBLANES, wide // 2), F32)],
        compiler_params=_cparams(("parallel", "arbitrary"), 32),
        name="s5_block_tables",
    )(dtab, btr, bti, ctr, cti, ar, ai)


def _merge_kernel(x_ref, ng_ref, yaf_ref, yab_ref, yb_ref, yc_ref, y5_ref, u5_ref, d5_ref, gluw_ref, glub_ref,
                  wgate_ref, bgate_ref, wbr_ref, wout_ref, o_ref):
    x = x_ref[...]
    h = _rms(x, ng_ref[...]).astype(BF16)
    y_d = _gelu_tanh(u5_ref[...] * d5_ref[...] + y5_ref[...])
    y_d = y_d * _sigmoid(_dot(y_d.astype(BF16), gluw_ref[...]) + glub_ref[...])
    branches = (yaf_ref[...] + yab_ref[...], yb_ref[...], yc_ref[...], y_d)
    merged = jnp.zeros(x.shape, F32)
    for i, y in enumerate(branches):
        gate = _sigmoid(_dot(h, wgate_ref[i]) + bgate_ref[i])
        merged = merged + gate * _dot(y.astype(BF16), wbr_ref[i])
    o_ref[...] = x + _dot(merged.astype(BF16), wout_ref[...])


def _merge(xf, lw, yaf, yab, yb, yc, y5, u5):
    t = xf.shape[0]
    tok = lambda w: pl.BlockSpec((TOK_TILE, w), lambda i: (i, 0))
    const = lambda arr: pl.BlockSpec(arr.shape, lambda i: (0,) * arr.ndim, pipeline_mode=pl.Buffered(1))
    weights = [lw["s5_d"], lw["s5_glu_w"], lw["s5_glu_b"], lw["w_gate"], lw["b_gate"], lw["w_branch"], lw["w_out"]]
    return pl.pallas_call(
        _merge_kernel,
        grid=(t // TOK_TILE,),
        in_specs=[tok(D_MODEL), const(lw["norm_mix_g"])] + [tok(MIX_WIDTH)] * 6 + [const(w) for w in weights],
        out_specs=tok(D_MODEL),
        out_shape=jax.ShapeDtypeStruct((t, D_MODEL), F32),
        compiler_params=_cparams(("parallel",), 48),
        name="merge",
    )(xf, lw["norm_mix_g"], yaf, yab, yb, yc, y5, u5, *weights)


def _moe_kernel(x_ref, ng_ref, wr_ref, br_ref, wg_ref, wu_ref, wd_ref, fg_ref, o_ref, *, final):
    x = x_ref[...]
    hf = _rms(x, ng_ref[...])
    logits = _dot_hi(hf, wr_ref[...]) + br_ref[...]
    lane = lax.broadcasted_iota(jnp.int32, logits.shape, 1).astype(F32)
    neg = -jnp.inf
    big = float(V7X_LANES)
    ng, ne = MOE_GROUPS, MOE_EXPERTS
    gl = jnp.where(lane < ng, logits, neg)
    gmax = jnp.max(gl, axis=1, keepdims=True)
    gidx = jnp.min(jnp.where(gl == gmax, lane, big), axis=1, keepdims=True)
    gprob = 1.0 / jnp.sum(jnp.exp(gl - gmax), axis=1, keepdims=True)
    lo = ng + ne * gidx
    sel = (lane >= lo) & (lane < lo + ne)
    m1 = jnp.max(jnp.where(sel, logits, neg), axis=1, keepdims=True)
    i1 = jnp.min(jnp.where(sel & (logits == m1), lane, big), axis=1, keepdims=True)
    sel2 = sel & (lane != i1)
    m2 = jnp.max(jnp.where(sel2, logits, neg), axis=1, keepdims=True)
    i2 = jnp.min(jnp.where(sel2 & (logits == m2), lane, big), axis=1, keepdims=True)
    e2 = jnp.exp(m2 - m1)
    w1 = gprob / (1.0 + e2)
    w2 = gprob * e2 / (1.0 + e2)
    ew = jnp.where(lane == i1, w1, jnp.where(lane == i2, w2, 0.0))
    h = hf.astype(BF16)
    acc = jnp.zeros(x.shape, F32)
    for idx in range(ng * ne):
        wcol = jnp.sum(jnp.where(lane == ng + idx, ew, 0.0), axis=1, keepdims=True)
        gate = _dot(h, wg_ref[idx])
        up = _dot(h, wu_ref[idx])
        act = _silu(gate) * up * wcol
        acc = acc + _dot(act.astype(BF16), wd_ref[idx])
    y = x + acc
    if final:
        y = _rms(y, fg_ref[...])
    o_ref[...] = y


def _moe(xf, lw, final_g, final):
    t = xf.shape[0]
    tok = pl.BlockSpec((TOK_TILE, D_MODEL), lambda i: (i, 0))
    const = lambda arr: pl.BlockSpec(arr.shape, lambda i: (0,) * arr.ndim, pipeline_mode=pl.Buffered(1))
    weights = [lw["norm_ffn_g"], lw["w_router"], lw["b_router"], lw["w_e_gate"], lw["w_e_up"], lw["w_e_down"],
               final_g]
    return pl.pallas_call(
        functools.partial(_moe_kernel, final=final),
        grid=(t // TOK_TILE,),
        in_specs=[tok] + [const(w) for w in weights],
        out_specs=tok,
        out_shape=jax.ShapeDtypeStruct((t, D_MODEL), F32),
        compiler_params=_cparams(("parallel",), 56),
        name="moe",
    )(xf, *weights)


def _block_diag(blocks):
    nb, bs, _ = blocks.shape
    eye = jnp.eye(nb, dtype=blocks.dtype)
    return jnp.einsum("nij,nm->nimj", blocks, eye).reshape(nb * bs, nb * bs)


def _pad_heads(w, axis=-1):
    shape = w.shape[:-1] + (GLA_HEADS, GLA_DK)
    w = w.reshape(shape)
    pad = [(0, 0)] * (w.ndim - 1) + [(0, V7X_LANES - GLA_DK)]
    return jnp.pad(w, pad).reshape(w.shape[:-2] + (GLA_HEADS * V7X_LANES,))


def _prep_layer(w, l):
    f = lambda name: w[name][l]
    w_in = f("w_in")
    cuts = np.cumsum([MIX_WIDTH, MIX_WIDTH, 3 * MIX_WIDTH, GLA_HEADS * GLA_DK, GLA_HEADS * GLA_DK, MIX_WIDTH,
                      MIX_WIDTH, 2 * GLA_RANK]).tolist()
    xa, ga, hy, q, k, v, g, lr, s5 = jnp.split(w_in, cuts, axis=-1)
    lr = jnp.pad(lr, ((0, 0), (0, V7X_LANES - 2 * GLA_RANK)))
    w_pack = jnp.concatenate([xa, ga, hy, _pad_heads(q), _pad_heads(k), v, g, lr, s5], axis=-1).astype(BF16)
    assert w_pack.shape[1] == N_PACK
    lw = {"w_pack": w_pack, "norm_mix_g": f("norm_mix_g")[None]}
    lw["lru_conv_w"] = f("lru_conv_w")
    lw["lru_conv_b"] = f("lru_conv_b")[None]
    wa, wx = f("lru_wa"), f("lru_wx")
    lw["lru_wg"] = jnp.stack([jnp.concatenate([_block_diag(wa[d]), _block_diag(wx[d])], axis=1)
                              for d in range(2)]).astype(BF16)
    lw["lru_bg"] = jnp.concatenate([f("lru_ba"), f("lru_bx")], axis=-1)[:, None, :]
    lw["lru_lam"] = f("lru_lambda")[:, None, :]
    lw["hy_conv_w"] = f("hy_conv_w")
    lw["hy_conv_b"] = f("hy_conv_b")[None]
    lw["hy_w1p"] = jnp.pad(f("hy_w1"), ((0, V7X_LANES - HY_EMB), (0, 0)))
    lw["hy_b1"] = f("hy_b1")[None]
    lw["hy_w2"] = f("hy_w2")
    lw["hy_b2"] = f("hy_b2")[None]
    lw["hy_w3"] = f("hy_w3")
    lw["hy_freq"] = f("hy_freq")[None]
    lw["hy_bias"] = f("hy_bias")
    wg2 = _pad_heads(f("gla_wg2"))
    wla = jnp.zeros((2, V7X_LANES, GLA_HEADS * V7X_LANES), F32)
    wla = wla.at[0, 0:GLA_RANK].set(wg2[0]).at[1, GLA_RANK:2 * GLA_RANK].set(wg2[1])
    lw["gla_wla"] = wla.astype(BF16)
    lw["gla_bla"] = _pad_heads(f("gla_bg"))[:, None, :]
    lw["gla_norm_g"] = jnp.tile(f("gla_norm_g"), GLA_HEADS)[None]
    lw["s5_tables"] = _s5_tables(f("s5_lam_re"), f("s5_lam_im"), f("s5_log_dt"), f("s5_b_re"), f("s5_b_im"),
                                 f("s5_c_re"), f("s5_c_im"))
    lw["s5_d"] = f("s5_d")[None]
    lw["s5_glu_w"] = f("s5_glu_w").astype(BF16)
    lw["s5_glu_b"] = f("s5_glu_b")[None]
    lw["w_gate"] = f("w_gate").astype(BF16)
    lw["b_gate"] = f("b_gate")[:, None, :]
    lw["w_branch"] = f("w_branch").astype(BF16)
    lw["w_out"] = f("w_out").astype(BF16)
    wr = jnp.concatenate([f("w_router_group"), jnp.transpose(f("w_router_expert"), (1, 0, 2)).reshape(D_MODEL, -1)],
                         axis=1)
    br = jnp.concatenate([f("b_router_group"), f("b_router_expert").reshape(-1)])
    nr = MOE_GROUPS + MOE_GROUPS * MOE_EXPERTS
    lw["w_router"] = jnp.pad(wr, ((0, 0), (0, V7X_LANES - nr)))
    lw["b_router"] = jnp.pad(br, (0, V7X_LANES - nr))[None]
    lw["norm_ffn_g"] = f("norm_ffn_g")[None]
    ne = MOE_GROUPS * MOE_EXPERTS
    lw["w_e_gate"] = f("w_e_gate").reshape(ne, D_MODEL, MOE_FF).astype(BF16)
    lw["w_e_up"] = f("w_e_up").reshape(ne, D_MODEL, MOE_FF).astype(BF16)
    lw["w_e_down"] = f("w_e_down").reshape(ne, MOE_FF, D_MODEL).astype(BF16)
    return lw


def _encoder(x, layers, final_g):
    bsz, seq, _ = x.shape
    plan = _FftPlan(seq)
    xf = x.reshape(bsz * seq, D_MODEL)
    for l, lw in enumerate(layers):
        lru_in, hy_in, gla_in, s5_in = _inproj(xf, lw["norm_mix_g"], lw["w_pack"])
        shp = lambda a: a.reshape(bsz, seq, a.shape[-1])
        yaf, yab = _lru(shp(lru_in), lw["lru_conv_w"], lw["lru_conv_b"], lw["lru_wg"], lw["lru_bg"], lw["lru_lam"])
        yb = _hyena(shp(hy_in), lw, plan)
        yc = _gla(shp(gla_in), lw["gla_wla"], lw["gla_bla"], lw["gla_norm_g"])
        y5 = _s5(shp(s5_in), *lw["s5_tables"])
        flat = lambda a: a.reshape(bsz * seq, MIX_WIDTH)
        xf = _merge(xf, lw, flat(yaf), flat(yab), flat(yb), flat(yc), flat(y5), s5_in)
        xf = _moe(xf, lw, final_g, final=(l == len(layers) - 1))
    return xf.reshape(bsz, seq, D_MODEL)


def kernel(x_prompt, x_sample, norm_mix_g, w_in, lru_conv_w, lru_conv_b, lru_wa, lru_ba, lru_wx, lru_bx,
           lru_lambda, hy_conv_w, hy_conv_b, hy_w1, hy_b1, hy_w2, hy_b2, hy_w3, hy_freq, hy_bias,
           gla_wg2, gla_bg, gla_norm_g, s5_lam_re, s5_lam_im, s5_log_dt, s5_b_re, s5_b_im, s5_c_re, s5_c_im,
           s5_d, s5_glu_w, s5_glu_b, w_branch, w_gate, b_gate, w_out, norm_ffn_g, w_router_group,
           b_router_group, w_router_expert, b_router_expert, w_e_gate, w_e_up, w_e_down, final_norm_g):
    w = dict(norm_mix_g=norm_mix_g, w_in=w_in, lru_conv_w=lru_conv_w, lru_conv_b=lru_conv_b, lru_wa=lru_wa,
             lru_ba=lru_ba, lru_wx=lru_wx, lru_bx=lru_bx, lru_lambda=lru_lambda, hy_conv_w=hy_conv_w,
             hy_conv_b=hy_conv_b, hy_w1=hy_w1, hy_b1=hy_b1, hy_w2=hy_w2, hy_b2=hy_b2, hy_w3=hy_w3,
             hy_freq=hy_freq, hy_bias=hy_bias, gla_wg2=gla_wg2, gla_bg=gla_bg, gla_norm_g=gla_norm_g,
             s5_lam_re=s5_lam_re, s5_lam_im=s5_lam_im, s5_log_dt=s5_log_dt, s5_b_re=s5_b_re, s5_b_im=s5_b_im,
             s5_c_re=s5_c_re, s5_c_im=s5_c_im, s5_d=s5_d, s5_glu_w=s5_glu_w, s5_glu_b=s5_glu_b,
             w_branch=w_branch, w_gate=w_gate, b_gate=b_gate, w_out=w_out, norm_ffn_g=norm_ffn_g,
             w_router_group=w_router_group, b_router_group=b_router_group, w_router_expert=w_router_expert,
             b_router_expert=b_router_expert, w_e_gate=w_e_gate, w_e_up=w_e_up, w_e_down=w_e_down)
    layers = [_prep_layer(w, l) for l in range(norm_mix_g.shape[0])]
    fg = final_norm_g[None]
    return (_encoder(x_prompt, layers, fg), _encoder(x_sample, layers, fg))
```

```python
import functools
import math

import numpy as np
import jax
import jax.numpy as jnp
from jax import lax
from jax.experimental import pallas as pl
from jax.experimental.pallas import tpu as pltpu

F32 = jnp.float32
BF16 = jnp.bfloat16

D_MODEL = 1024
DEPTH = 2
EPS = 1e-6
MIX_WIDTH = D_MODEL // 2
LRU_BLOCKS = 8
LRU_BLOCK = MIX_WIDTH // LRU_BLOCKS
LRU_CONV = 4
LRU_C = 8.0
HY_ORDER = 2
HY_CONV = 3
HY_EMB = 33
HY_BANDS = (HY_EMB - 1) // 2
HY_HIDDEN = 64
HY_DECAY_TARGET = 1e-2
HY_FAST_PCT = 0.3
HY_SLOW_PCT = 1.5
GLA_HEADS = 4
GLA_DK = MIX_WIDTH // 8
GLA_DV = MIX_WIDTH // GLA_HEADS
GLA_RANK = 16
GLA_TAU = 16.0
GLA_CHUNK = 64
S5_GROUP = 16
S5_GROUPS = MIX_WIDTH // S5_GROUP
S5_STATE = 64
MOE_GROUPS = 4
MOE_EXPERTS = 4
MOE_FF = D_MODEL // 4

V7X_LANES = 128
V7X_SUBLANES = 8
V7X_VMEM_BYTES = 64 * 2**20
MIB = 2**20

GLA_PACK = 4 * V7X_LANES * 2 + 512 + 512 + V7X_LANES
PK_LRU = (0, 1024)
PK_HY = (1024, 2560)
PK_GLA = (2560, 2560 + GLA_PACK)
PK_S5 = (PK_GLA[1], PK_GLA[1] + 512)
N_PACK = PK_S5[1]

TOK_TILE = 256
LRU_TILE = 256
CONV_TILE = 512
GLA_TILE = 256
GLA_BATCH = 2
S5_CHUNK = 16
S5_LANE_GROUPS = V7X_LANES // S5_GROUP
S5_SCAN_STEPS = 3
FFT_N1 = 64
FFT_ROWS = 128
FFT_COLS = 256
FFT_KB = 11


def _cparams(sem, vmem_mib):
    return pltpu.CompilerParams(dimension_semantics=sem, vmem_limit_bytes=int(vmem_mib * MIB))


def _rms(x, g):
    return x * lax.rsqrt(jnp.mean(x * x, axis=-1, keepdims=True) + EPS) * g


def _sigmoid(x):
    return 1.0 / (1.0 + jnp.exp(-x))


def _softplus(x):
    return jnp.maximum(x, 0.0) + jnp.log(1.0 + jnp.exp(-jnp.abs(x)))


def _gelu_tanh(x):
    return 0.5 * x * (1.0 + jnp.tanh(math.sqrt(2.0 / math.pi) * (x + 0.044715 * (x * x * x))))


def _silu(x):
    return x * _sigmoid(x)


def _dot(a, b):
    return jnp.dot(a, b, preferred_element_type=F32)


def _dot_hi(a, b):
    return jnp.dot(a, b, preferred_element_type=F32, precision=lax.Precision.HIGHEST)


def _inproj_kernel(x_ref, g_ref, w_ref, lru_ref, hy_ref, gla_ref, s5_ref):
    h = _rms(x_ref[...], g_ref[...]).astype(BF16)
    lru_ref[...] = _dot(h, w_ref[:, PK_LRU[0]:PK_LRU[1]])
    hy_ref[...] = _dot(h, w_ref[:, PK_HY[0]:PK_HY[1]])
    gla_ref[...] = _dot(h, w_ref[:, PK_GLA[0]:PK_GLA[1]])
    s5_ref[...] = _dot(h, w_ref[:, PK_S5[0]:PK_S5[1]])


def _inproj(xf, g, w_pack):
    t = xf.shape[0]
    widths = [PK_LRU[1] - PK_LRU[0], PK_HY[1] - PK_HY[0], PK_GLA[1] - PK_GLA[0], PK_S5[1] - PK_S5[0]]
    return pl.pallas_call(
        _inproj_kernel,
        grid=(t // TOK_TILE,),
        in_specs=[pl.BlockSpec((TOK_TILE, D_MODEL), lambda i: (i, 0)),
                  pl.BlockSpec((1, D_MODEL), lambda i: (0, 0)),
                  pl.BlockSpec((D_MODEL, N_PACK), lambda i: (0, 0))],
        out_specs=[pl.BlockSpec((TOK_TILE, w), lambda i: (i, 0)) for w in widths],
        out_shape=[jax.ShapeDtypeStruct((t, w), F32) for w in widths],
        compiler_params=_cparams(("parallel",), 48),
        name="inproj",
    )(xf, g, w_pack)


def _fill_ext(ext_ref, main, prev8, next8, first, last, tile):
    ext_ref[0:8, :] = jnp.where(first, 0.0, prev8)
    ext_ref[8:8 + tile, :] = main
    ext_ref[8 + tile:16 + tile, :] = jnp.where(last, 0.0, next8)


def _linear_scan_tile(a, b, carry, reverse):
    n = a.shape[0]
    sub = V7X_SUBLANES
    row = lax.broadcasted_iota(jnp.int32, a.shape, 0) % sub
    d = 1
    while d < sub:
        if reverse:
            a_s = pltpu.roll(a, n - d, 0)
            b_s = pltpu.roll(b, n - d, 0)
            valid = row < sub - d
        else:
            a_s = pltpu.roll(a, d, 0)
            b_s = pltpu.roll(b, d, 0)
            valid = row >= d
        b = jnp.where(valid, a * b_s + b, b)
        a = jnp.where(valid, a * a_s, a)
        d *= 2
    ngroup = n // sub
    out = [None] * ngroup
    for g in (range(ngroup - 1, -1, -1) if reverse else range(ngroup)):
        h = b[g * sub:(g + 1) * sub, :] + a[g * sub:(g + 1) * sub, :] * carry
        carry = h[0:1, :] if reverse else h[sub - 1:sub, :]
        out[g] = h
    return jnp.concatenate(out, axis=0), carry


def _lru_kernel(mf_ref, pf_ref, nf_ref, mb_ref, pb_ref, nb_ref, cw_ref, cb_ref, wg_ref, bg_ref, lam_ref,
                of_ref, ob_ref, extf_ref, extb_ref, carry_ref):
    c = pl.program_id(1)
    nc = pl.num_programs(1)
    tile = LRU_TILE

    @pl.when(c == 0)
    def _():
        carry_ref[...] = jnp.zeros_like(carry_ref)

    def one(m_ref, p_ref, n_ref, ext_ref, d, first, last, o_ref):
        x = m_ref[0, :, 0:MIX_WIDTH]
        ga = m_ref[0, :, MIX_WIDTH:2 * MIX_WIDTH]
        _fill_ext(ext_ref, x, p_ref[0], n_ref[0], first, last, tile)
        xc = cb_ref[...] + ext_ref[6:6 + tile, :] * cw_ref[0:1, :]
        xc = xc + ext_ref[7:7 + tile, :] * cw_ref[1:2, :]
        xc = xc + ext_ref[8:8 + tile, :] * cw_ref[2:3, :]
        xc = xc + ext_ref[9:9 + tile, :] * cw_ref[3:4, :]
        z = _dot(xc.astype(BF16), wg_ref[d]) + bg_ref[d]
        gate_r = _sigmoid(z[:, 0:MIX_WIDTH])
        gate_i = _sigmoid(z[:, MIX_WIDTH:2 * MIX_WIDTH])
        log_a = -LRU_C * gate_r * _softplus(-lam_ref[d])
        a = jnp.exp(log_a)
        t = 1.0 - a * a
        b = jnp.where(t > 0.0, t * lax.rsqrt(t), 0.0) * gate_i * xc
        h, last = _linear_scan_tile(a, b, carry_ref[d:d + 1, :], reverse=(d == 1))
        carry_ref[d:d + 1, :] = last
        o_ref[0] = h * _gelu_tanh(ga)

    one(mf_ref, pf_ref, nf_ref, extf_ref, 0, c == 0, c == nc - 1, of_ref)
    one(mb_ref, pb_ref, nb_ref, extb_ref, 1, c == nc - 1, c == 0, ob_ref)


def _lru(lru_in, cw, cb, wg, bg, lam):
    bsz, seq, _ = lru_in.shape
    tile = LRU_TILE
    nc = seq // tile
    r8 = tile // 8
    last8 = seq // 8 - 1

    def fwd(c):
        return c

    def bwd(c):
        return nc - 1 - c

    def specs(ch):
        return [pl.BlockSpec((1, tile, 2 * MIX_WIDTH), lambda b, c: (b, ch(c), 0)),
                pl.BlockSpec((1, 8, MIX_WIDTH), lambda b, c: (b, jnp.maximum(ch(c) * r8 - 1, 0), 0)),
                pl.BlockSpec((1, 8, MIX_WIDTH), lambda b, c: (b, jnp.minimum((ch(c) + 1) * r8, last8), 0))]

    const = lambda shape: pl.BlockSpec(shape, lambda b, c: (0,) * len(shape))
    return pl.pallas_call(
        _lru_kernel,
        grid=(bsz, nc),
        in_specs=specs(fwd) + specs(bwd) + [const(cw.shape), const(cb.shape), const(wg.shape), const(bg.shape),
                                            const(lam.shape)],
        out_specs=[pl.BlockSpec((1, tile, MIX_WIDTH), lambda b, c: (b, c, 0)),
                   pl.BlockSpec((1, tile, MIX_WIDTH), lambda b, c: (b, nc - 1 - c, 0))],
        out_shape=[jax.ShapeDtypeStruct((bsz, seq, MIX_WIDTH), F32)] * 2,
        scratch_shapes=[pltpu.VMEM((tile + 16, MIX_WIDTH), F32), pltpu.VMEM((tile + 16, MIX_WIDTH), F32),
                        pltpu.VMEM((8, MIX_WIDTH), F32)],
        compiler_params=_cparams(("parallel", "arbitrary"), 40),
        name="lru",
    )(lru_in, lru_in, lru_in, lru_in, lru_in, lru_in, cw, cb, wg, bg, lam)


def _hyconv_kernel(m_ref, p_ref, n_ref, cw_ref, cb_ref, o_ref, ext_ref):
    c = pl.program_id(1)
    nc = pl.num_programs(1)
    tile = CONV_TILE
    _fill_ext(ext_ref, m_ref[0], p_ref[0], n_ref[0], c == 0, c == nc - 1, tile)
    y = cb_ref[...] + ext_ref[7:7 + tile, :] * cw_ref[0:1, :]
    y = y + ext_ref[8:8 + tile, :] * cw_ref[1:2, :]
    y = y + ext_ref[9:9 + tile, :] * cw_ref[2:3, :]
    o_ref[0] = y


def _hyconv(hy_in, cw, cb):
    bsz, seq, width = hy_in.shape
    tile = CONV_TILE
    r8 = tile // 8
    last8 = seq // 8 - 1
    const = lambda shape: pl.BlockSpec(shape, lambda b, c: (0,) * len(shape))
    return pl.pallas_call(
        _hyconv_kernel,
        grid=(bsz, seq // tile),
        in_specs=[pl.BlockSpec((1, tile, width), lambda b, c: (b, c, 0)),
                  pl.BlockSpec((1, 8, width), lambda b, c: (b, jnp.maximum(c * r8 - 1, 0), 0)),
                  pl.BlockSpec((1, 8, width), lambda b, c: (b, jnp.minimum((c + 1) * r8, last8), 0)),
                  const(cw.shape), const(cb.shape)],
        out_specs=pl.BlockSpec((1, tile, width), lambda b, c: (b, c, 0)),
        out_shape=jax.ShapeDtypeStruct((bsz, seq, width), F32),
        scratch_shapes=[pltpu.VMEM((tile + 16, width), F32)],
        compiler_params=_cparams(("parallel", "parallel"), 40),
        name="hyconv",
    )(hy_in, hy_in, hy_in, cw, cb)


class _FftPlan:
    def __init__(self, seq):
        n = 2 * seq
        n1 = FFT_N1
        n2 = n // n1
        assert n1 * n2 == n and n2 % 16 == 0
        h1 = n1 // 2 + 1
        nh = n1 // 2
        self.n, self.n1, self.n2, self.h1, self.nh = n, n1, n2, h1, nh
        self.rows = min(FFT_ROWS, n2)
        assert n2 % self.rows == 0 and h1 % FFT_KB == 0
        k1 = np.arange(h1, dtype=np.float64)
        m1 = np.arange(nh, dtype=np.float64)
        r = np.arange(8, dtype=np.float64)
        ang = -2.0 * np.pi * (k1[:, None, None] * m1[None, None, :] / n1 + r[None, :, None] * k1[:, None, None] / n)
        e = np.exp(1j * ang)
        fa = np.zeros((h1, 8, nh, 8), np.complex128)
        for rr in range(8):
            fa[:, rr, :, rr] = e[:, rr, :]
        fa = fa.reshape(h1 * 8, nh * 8)
        self.fa = jnp.asarray(np.concatenate([fa.real, fa.imag], axis=0), BF16)
        ck = np.where((k1 == 0) | (k1 == n1 // 2), 1.0, 2.0)
        ec = np.conj(e) * ck[:, None, None] / n
        fc = np.zeros((nh, 8, 2, h1, 8), np.float64)
        for rr in range(8):
            fc[:, rr, 0, :, rr] = ec[:, rr, :].real.T
            fc[:, rr, 1, :, rr] = -ec[:, rr, :].imag.T
        self.fc = jnp.asarray(fc.reshape(nh * 8, 2 * h1 * 8), BF16)
        rg = np.arange(n2 // 8, dtype=np.float64)
        tw = np.exp(-2j * np.pi * 8.0 * rg[None, :] * k1[:, None] / n)
        self.tw = jnp.asarray(np.concatenate([tw.real, tw.imag], axis=0), F32)
        tw1 = np.exp(-2j * np.pi * k1 / n)
        self.tw1 = jnp.asarray(np.stack([tw1.real, tw1.imag]), F32)
        q = np.arange(n2, dtype=np.float64)
        f2 = np.exp(-2j * np.pi * np.outer(q, q) / n2)
        fr, fi = f2.real, f2.imag
        self.gb = jnp.asarray(np.block([[fr, -fi], [fi, fr]]), BF16)
        self.gbi = jnp.asarray(np.block([[fr, fi], [-fi, fr]]), BF16)


def _fft_a_kernel(tw_ref, x_ref, fa_ref, a_ref, *, h1, nh, rows):
    rb = pl.program_id(2)
    cols = x_ref.shape[-1]

    def stage(rg):
        xg = x_ref[0, :, pl.ds(pl.multiple_of(rg * 8, 8), 8), :].reshape(nh * 8, cols).astype(BF16)
        return _dot(fa_ref[...], xg)

    def body(i, carry):
        p0 = stage(2 * i)
        p1 = stage(2 * i + 1)
        g0 = rb * (rows // 8) + 2 * i
        for k in range(h1):
            outs = []
            for p, g in ((p0, g0), (p1, g0 + 1)):
                pr = p[k * 8:(k + 1) * 8, :]
                pi = p[(h1 + k) * 8:(h1 + k + 1) * 8, :]
                tr = tw_ref[k, g]
                ti = tw_ref[h1 + k, g]
                outs.append((pr * tr - pi * ti, pr * ti + pi * tr))
            dst = pl.ds(pl.multiple_of(i * 16, 16), 16)
            a_ref[0, k, 0, dst, :] = jnp.concatenate([outs[0][0], outs[1][0]], axis=0).astype(BF16)
            a_ref[0, k, 1, dst, :] = jnp.concatenate([outs[0][1], outs[1][1]], axis=0).astype(BF16)
        return carry

    lax.fori_loop(0, rows // 16, body, 0)


def _fft_a(x4, plan, col_off=0, ncols=None):
    bq, nh, n2, width = x4.shape
    ncols = width if ncols is None else ncols
    cb0 = col_off // FFT_COLS
    rows = plan.rows
    kern = functools.partial(_fft_a_kernel, h1=plan.h1, nh=nh, rows=rows)
    return pl.pallas_call(
        kern,
        grid=(bq, ncols // FFT_COLS, n2 // rows),
        in_specs=[pl.BlockSpec(memory_space=pltpu.SMEM),
                  pl.BlockSpec((1, nh, rows, FFT_COLS), lambda b, c, r: (b, 0, r, cb0 + c)),
                  pl.BlockSpec(plan.fa.shape, lambda b, c, r: (0, 0))],
        out_specs=pl.BlockSpec((1, plan.h1, 2, rows, FFT_COLS), lambda b, c, r: (b, 0, 0, r, c)),
        out_shape=jax.ShapeDtypeStruct((bq, plan.h1, 2, n2, ncols), BF16),
        compiler_params=_cparams(("parallel", "parallel", "parallel"), 40),
        name="fft_outer_fwd",
    )(plan.tw, x4, plan.fa)


def _fft_mid_kernel(a_ref, kf_ref, gb_ref, gbi_ref, o_ref, *, n2):
    kb = a_ref.shape[1]
    cols = a_ref.shape[-1]
    for k in range(kb):
        y = _dot(gb_ref[...], a_ref[0, k].reshape(2 * n2, cols))
        yr, yi = y[0:n2, :], y[n2:2 * n2, :]
        kr, ki = kf_ref[0, k, 0], kf_ref[0, k, 1]
        z = jnp.concatenate([yr * kr - yi * ki, yr * ki + yi * kr], axis=0).astype(BF16)
        o_ref[0, k] = _dot(gbi_ref[...], z).reshape(2, n2, cols).astype(BF16)


def _fft_mid(a, kf, order, plan):
    bq, h1, _, n2, width = a.shape
    kern = functools.partial(_fft_mid_kernel, n2=n2)
    blk = (1, FFT_KB, 2, n2, FFT_COLS)
    return pl.pallas_call(
        kern,
        grid=(bq, width // FFT_COLS, h1 // FFT_KB),
        in_specs=[pl.BlockSpec(blk, lambda b, c, k: (b, k, 0, 0, c)),
                  pl.BlockSpec(blk, lambda b, c, k: (order, k, 0, 0, c)),
                  pl.BlockSpec(plan.gb.shape, lambda b, c, k: (0, 0)),
                  pl.BlockSpec(plan.gbi.shape, lambda b, c, k: (0, 0))],
        out_specs=pl.BlockSpec(blk, lambda b, c, k: (b, k, 0, 0, c)),
        out_shape=jax.ShapeDtypeStruct(a.shape, BF16),
        compiler_params=_cparams(("parallel", "parallel", "parallel"), 40),
        name="fft_inner_mul",
    )(a, kf, plan.gb, plan.gbi)


def _fft_c_kernel(tw_ref, b_ref, u_ref, g_ref, bias_ref, fc_ref, o_ref, *, h1, nh, rows):
    rb = pl.program_id(2)
    cols = o_ref.shape[-1]

    def body(i, carry):
        src = pl.ds(pl.multiple_of(i * 16, 16), 16)
        tiles = [[b_ref[0, k, p, src, :].astype(F32) for p in range(2)] for k in range(h1)]
        for half in range(2):
            g = rb * (rows // 8) + 2 * i + half
            re_rows, im_rows = [], []
            for k in range(h1):
                br = tiles[k][0][half * 8:(half + 1) * 8, :]
                bi = tiles[k][1][half * 8:(half + 1) * 8, :]
                tr = tw_ref[k, g]
                ti = tw_ref[h1 + k, g]
                re_rows.append(br * tr + bi * ti)
                im_rows.append(bi * tr - br * ti)
            s = jnp.concatenate(re_rows + im_rows, axis=0).astype(BF16)
            y = _dot(fc_ref[...], s).reshape(nh, 8, cols)
            dst = pl.ds(pl.multiple_of((2 * i + half) * 8, 8), 8)
            u = u_ref[0, :, dst, :]
            o_ref[0, :, dst, :] = (y + u * bias_ref[...]) * g_ref[0, :, dst, :]
        return carry

    lax.fori_loop(0, rows // 16, body, 0)


def _fft_c(bm, u4, u_off, g4, g_off, bias, plan):
    bq, h1, _, n2, width = bm.shape
    nh = plan.nh
    rows = plan.rows
    ub, gbk = u_off // FFT_COLS, g_off // FFT_COLS
    kern = functools.partial(_fft_c_kernel, h1=h1, nh=nh, rows=rows)
    xblk = (1, nh, rows, FFT_COLS)
    return pl.pallas_call(
        kern,
        grid=(bq, width // FFT_COLS, n2 // rows),
        in_specs=[pl.BlockSpec(memory_space=pltpu.SMEM),
                  pl.BlockSpec((1, h1, 2, rows, FFT_COLS), lambda b, c, r: (b, 0, 0, r, c)),
                  pl.BlockSpec(xblk, lambda b, c, r: (b, 0, r, ub + c)),
                  pl.BlockSpec(xblk, lambda b, c, r: (b, 0, r, gbk + c)),
                  pl.BlockSpec((1, FFT_COLS), lambda b, c, r: (0, c)),
                  pl.BlockSpec(plan.fc.shape, lambda b, c, r: (0, 0))],
        out_specs=pl.BlockSpec(xblk, lambda b, c, r: (b, 0, r, c)),
        out_shape=jax.ShapeDtypeStruct((bq, nh, n2, width), F32),
        compiler_params=_cparams(("parallel", "parallel", "parallel"), 48),
        name="fft_outer_inv",
    )(plan.tw, bm, u4, g4, bias, plan.fc)


def _hyfilt_gen_kernel(w1_ref, b1_ref, w2_ref, b2_ref, w3_ref, fr_ref, o_ref, ss_ref, *, seq):
    rblk = pl.program_id(0)
    tile, cols = o_ref.shape[1], o_ref.shape[2]
    irow = lax.broadcasted_iota(jnp.int32, (tile, V7X_LANES), 0) + rblk * tile
    row = irow.astype(F32)
    lane = lax.broadcasted_iota(jnp.int32, (tile, V7X_LANES), 1)
    t = row / (seq - 1.0)
    omega = (2.0 * math.pi / seq) * row
    band_step = (HY_BANDS - 1 - 1e-4) / (HY_BANDS - 1)
    is_cos = (lane >= 1) & (lane <= HY_BANDS)
    is_sin = (lane > HY_BANDS) & (lane <= 2 * HY_BANDS)
    bidx = jnp.where(is_cos, lane - 1, lane - 1 - HY_BANDS).astype(F32)
    ang = omega * (1e-4 + band_step * bidx)
    trig = jnp.cos(ang + jnp.where(is_sin, 0.5 * math.pi, 0.0))
    z = jnp.where(lane == 0, t, jnp.where(is_cos | is_sin, trig, 0.0))
    fr = fr_ref[...]
    hid = jnp.sin(fr * (_dot_hi(z, w1_ref[...]) + b1_ref[...]))
    hid = jnp.sin(fr * (_dot_hi(hid, w2_ref[...]) + b2_ref[...]))
    filt = _dot_hi(hid, w3_ref[...])
    col = lax.broadcasted_iota(jnp.int32, (1, cols), 1)
    chan = (col % MIX_WIDTH).astype(F32)
    max_decay = math.log(HY_DECAY_TARGET) / HY_FAST_PCT
    min_decay = math.log(HY_DECAY_TARGET) / HY_SLOW_PCT
    delta = jnp.abs(min_decay + (max_decay - min_decay) / (MIX_WIDTH - 1) * chan)
    filt = filt * jnp.exp(-t[:, 0:1] * delta)
    is_bwd = (col // MIX_WIDTH) % 2 == 1
    rows_c = lax.broadcasted_iota(jnp.int32, (tile, cols), 0) + rblk * tile
    filt = jnp.where(is_bwd & (rows_c == seq - 1), 0.0, filt)
    o_ref[0] = filt

    @pl.when(rblk == 0)
    def _():
        ss_ref[...] = jnp.zeros_like(ss_ref)

    ss_ref[...] += jnp.sum(filt * filt, axis=0, keepdims=True)


def _hyfilt_gen(seq, w1p, b1, w2, b2, w3, freq):
    ncol = w3.shape[1]
    tile = min(seq, 512)
    const = lambda shape: pl.BlockSpec(shape, lambda r: (0,) * len(shape))
    kern = functools.partial(_hyfilt_gen_kernel, seq=seq)
    return pl.pallas_call(
        kern,
        grid=(seq // tile,),
        in_specs=[const(w1p.shape), const(b1.shape), const(w2.shape), const(b2.shape), const(w3.shape),
                  const(freq.shape)],
        out_specs=[pl.BlockSpec((1, tile, ncol), lambda r: (0, r, 0)),
                   pl.BlockSpec((1, ncol), lambda r: (0, 0))],
        out_shape=[jax.ShapeDtypeStruct((1, seq, ncol), F32), jax.ShapeDtypeStruct((1, ncol), F32)],
        compiler_params=_cparams(("arbitrary",), 40),
        name="hyena_filter_gen",
    )(w1p, b1, w2, b2, w3, freq)


def _hyfilt_spec_kernel(tw1_ref, a0_ref, a1_ref, ss0_ref, ss1_ref, gb_ref, o_ref, *, n2):
    kblk = pl.program_id(2)
    kb = a0_ref.shape[1]
    cols = a0_ref.shape[-1]
    h1 = tw1_ref.shape[1]
    scale = lax.rsqrt(ss0_ref[...] + ss1_ref[...] + EPS)
    ang2 = (-2.0 * math.pi / n2) * lax.broadcasted_iota(jnp.int32, (n2, cols), 0).astype(F32)
    cr, ci = jnp.cos(ang2), jnp.sin(ang2)
    for k in range(kb):
        y0 = _dot(gb_ref[...], a0_ref[0, k].reshape(2 * n2, cols))
        y1 = _dot(gb_ref[...], a1_ref[0, k].reshape(2 * n2, cols))
        sr = tw1_ref[0, kblk * kb + k]
        si = tw1_ref[1, kblk * kb + k]
        wr, wi = cr * sr - ci * si, cr * si + ci * sr
        y1r, y1i = y1[0:n2, :], y1[n2:2 * n2, :]
        o_ref[0, k, 0] = (y0[0:n2, :] + (wr * y1r - wi * y1i)) * scale
        o_ref[0, k, 1] = (y0[n2:2 * n2, :] - (wr * y1i + wi * y1r)) * scale


def _hyfilt_spec(af, ss, plan):
    _, h1, _, n2, _ = af.shape
    ncb = MIX_WIDTH // FFT_COLS
    blk = (1, FFT_KB, 2, n2, FFT_COLS)
    kern = functools.partial(_hyfilt_spec_kernel, n2=n2)
    return pl.pallas_call(
        kern,
        grid=(HY_ORDER, ncb, h1 // FFT_KB),
        in_specs=[pl.BlockSpec(memory_space=pltpu.SMEM),
                  pl.BlockSpec(blk, lambda o, c, k: (0, k, 0, 0, o * 2 * ncb + c)),
                  pl.BlockSpec(blk, lambda o, c, k: (0, k, 0, 0, o * 2 * ncb + ncb + c)),
                  pl.BlockSpec((1, FFT_COLS), lambda o, c, k: (0, o * 2 * ncb + c)),
                  pl.BlockSpec((1, FFT_COLS), lambda o, c, k: (0, o * 2 * ncb + ncb + c)),
                  pl.BlockSpec(plan.gb.shape, lambda o, c, k: (0, 0))],
        out_specs=pl.BlockSpec(blk, lambda o, c, k: (o, k, 0, 0, c)),
        out_shape=jax.ShapeDtypeStruct((HY_ORDER, h1, 2, n2, MIX_WIDTH), F32),
        compiler_params=_cparams(("parallel", "parallel", "parallel"), 40),
        name="hyena_filter_spectrum",
    )(plan.tw1, af, af, ss, ss, plan.gb)


def _hyena(hy_in, lw, plan):
    bsz, seq, _ = hy_in.shape
    zc = _hyconv(hy_in, lw["hy_conv_w"], lw["hy_conv_b"])
    filt, ss = _hyfilt_gen(seq, lw["hy_w1p"], lw["hy_b1"], lw["hy_w2"], lw["hy_b2"], lw["hy_w3"], lw["hy_freq"])
    af = _fft_a(filt.reshape(1, plan.nh, plan.n2, filt.shape[-1]), plan)
    kf = _hyfilt_spec(af, ss, plan)
    zc4 = zc.reshape(bsz, plan.nh, plan.n2, 3 * MIX_WIDTH)
    a = _fft_a(zc4, plan, col_off=0, ncols=MIX_WIDTH)
    bm = _fft_mid(a, kf, 0, plan)
    z1 = _fft_c(bm, zc4, 0, zc4, MIX_WIDTH, lw["hy_bias"][0:1], plan)
    a = _fft_a(z1, plan)
    bm = _fft_mid(a, kf, 1, plan)
    z2 = _fft_c(bm, z1, 0, zc4, 2 * MIX_WIDTH, lw["hy_bias"][1:2], plan)
    return z2.reshape(bsz, seq, MIX_WIDTH)


def _gla_kernel(*refs, reverse):
    if reverse:
        x_ref, of_ref, wla_ref, bla_ref, ng_ref, o_ref, st_ref = refs
    else:
        x_ref, wla_ref, bla_ref, o_ref, st_ref = refs
    c = pl.program_id(1)

    @pl.when(c == 0)
    def _():
        st_ref[...] = jnp.zeros_like(st_ref)

    for bb in range(x_ref.shape[0]):
        _gla_sequence(bb, x_ref, of_ref if reverse else None, wla_ref, bla_ref, ng_ref if reverse else None, o_ref,
                      st_ref, reverse)


def _gla_sequence(bb, x_ref, of_ref, wla_ref, bla_ref, ng_ref, o_ref, st_ref, reverse):
    tile = GLA_TILE
    ck = GLA_CHUNK
    nck = tile // ck
    hw = V7X_LANES
    nh = GLA_HEADS
    q = x_ref[bb, :, 0:nh * hw] * (GLA_DK ** -0.5)
    k = x_ref[bb, :, nh * hw:2 * nh * hw]
    v = x_ref[bb, :, 2 * nh * hw:3 * nh * hw]
    lr = x_ref[bb, :, 4 * nh * hw:4 * nh * hw + hw]
    zl = _dot(lr.astype(BF16), wla_ref[...]) + bla_ref[...]
    la = (jnp.minimum(zl, 0.0) - jnp.log(1.0 + jnp.exp(-jnp.abs(zl)))) / GLA_TAU

    row = lax.broadcasted_iota(jnp.int32, la.shape, 0) % ck
    bcum = la
    d = 1
    while d < ck:
        if reverse:
            bcum = bcum + jnp.where(row < ck - d, pltpu.roll(bcum, tile - d, 0), 0.0)
        else:
            bcum = bcum + jnp.where(row >= d, pltpu.roll(bcum, d, 0), 0.0)
        d *= 2
    b3 = bcum.reshape(nck, ck, nh * hw)
    blast = b3[:, 0:1, :] if reverse else b3[:, ck - 1:ck, :]
    q_e = (q * jnp.exp(bcum)).astype(BF16)
    k_e = (k * jnp.exp(-bcum)).astype(BF16)
    k_d = (k.reshape(nck, ck, nh * hw) * jnp.exp(blast - b3)).reshape(tile, nh * hw).astype(BF16)
    gch = jnp.exp(blast)
    vb = v.astype(BF16)

    ri = lax.broadcasted_iota(jnp.int32, (ck, ck), 0)
    ci = lax.broadcasted_iota(jnp.int32, (ck, ck), 1)
    mask = (ri <= ci) if reverse else (ri >= ci)
    order = range(nck - 1, -1, -1) if reverse else range(nck)
    outs = [None] * nck
    for n in order:
        rs = slice(n * ck, (n + 1) * ck)
        heads = []
        for h in range(nh):
            ls = slice(h * hw, (h + 1) * hw)
            qe, ke, kd, vh = q_e[rs, ls], k_e[rs, ls], k_d[rs, ls], vb[rs, ls]
            st = st_ref[bb, h]
            sc = lax.dot_general(qe, ke, (((1,), (1,)), ((), ())), preferred_element_type=F32)
            sc = jnp.where(mask, sc, 0.0).astype(BF16)
            o = _dot(sc, vh) + lax.dot_general(qe, st.astype(BF16), (((1,), (1,)), ((), ())),
                                               preferred_element_type=F32)
            upd = lax.dot_general(vh, kd, (((0,), (0,)), ((), ())), preferred_element_type=F32)
            st_ref[bb, h] = st * gch[n, :, ls] + upd
            heads.append(o)
        outs[n] = jnp.concatenate(heads, axis=1)
    o_dir = jnp.concatenate(outs, axis=0)
    if not reverse:
        o_ref[bb] = o_dir
        return
    o = of_ref[bb] + o_dir
    g = x_ref[bb, :, 3 * nh * hw:4 * nh * hw]
    normed = []
    for h in range(nh):
        oh = o[:, h * hw:(h + 1) * hw]
        normed.append(oh * lax.rsqrt(jnp.mean(oh * oh, axis=-1, keepdims=True) + EPS))
    o = jnp.concatenate(normed, axis=1) * ng_ref[...]
    o_ref[bb] = o * _silu(g)


def _gla(gla_in, wla, bla, norm_g):
    bsz, seq, width = gla_in.shape
    tile = GLA_TILE
    nc = seq // tile
    nb = min(GLA_BATCH, bsz)
    const = lambda shape: pl.BlockSpec(shape, lambda b, c: (0,) * len(shape))
    out_shape = jax.ShapeDtypeStruct((bsz, seq, MIX_WIDTH), F32)
    scratch = [pltpu.VMEM((nb, GLA_HEADS, GLA_DV, V7X_LANES), F32)]
    o_f = pl.pallas_call(
        functools.partial(_gla_kernel, reverse=False),
        grid=(bsz // nb, nc),
        in_specs=[pl.BlockSpec((nb, tile, width), lambda b, c: (b, c, 0)), const(wla.shape[1:]), const(bla.shape[1:])],
        out_specs=pl.BlockSpec((nb, tile, MIX_WIDTH), lambda b, c: (b, c, 0)),
        out_shape=out_shape,
        scratch_shapes=scratch,
        compiler_params=_cparams(("parallel", "arbitrary"), 48),
        name="gla_fwd",
    )(gla_in, wla[0], bla[0])
    return pl.pallas_call(
        functools.partial(_gla_kernel, reverse=True),
        grid=(bsz // nb, nc),
        in_specs=[pl.BlockSpec((nb, tile, width), lambda b, c: (b, nc - 1 - c, 0)),
                  pl.BlockSpec((nb, tile, MIX_WIDTH), lambda b, c: (b, nc - 1 - c, 0)),
                  const(wla.shape[1:]), const(bla.shape[1:]), const(norm_g.shape)],
        out_specs=pl.BlockSpec((nb, tile, MIX_WIDTH), lambda b, c: (b, nc - 1 - c, 0)),
        out_shape=out_shape,
        scratch_shapes=scratch,
        compiler_params=_cparams(("parallel", "arbitrary"), 48),
        name="gla_bwd",
    )(gla_in, o_f, wla[1], bla[1], norm_g)


def _s5_kernel(u_ref, t_ref, e_ref, o_ref_w, mu_ref, mup_ref, y_ref):
    ch = S5_CHUNK
    sub = V7X_SUBLANES
    nrow = u_ref.shape[1] // ch
    half = S5_LANE_GROUPS * 2 * S5_STATE
    swap = lambda x: pltpu.roll(x, half // 2, 1)
    u = jnp.concatenate([u_ref[0, pl.ds(i, nrow, stride=ch), :] for i in range(ch)], axis=1).astype(BF16)
    y = _dot(u, t_ref[0])
    he = _dot(u, e_ref[0])
    row = lax.broadcasted_iota(jnp.int32, (nrow, half), 0)
    rsub = row % sub
    states = []
    for d in range(2):
        h = he[:, d * half:(d + 1) * half]
        for s in range(S5_SCAN_STEPS):
            step = 2 ** s
            if d == 0:
                hs = jnp.where(rsub >= step, pltpu.roll(h, step, 0), 0.0)
            else:
                hs = jnp.where(rsub < sub - step, pltpu.roll(h, nrow - step, 0), 0.0)
            h = h + hs * mu_ref[0, d, s, 0:1, :] + swap(hs) * mu_ref[0, d, s, 1:2, :]
        ngroup = nrow // sub
        carry = jnp.zeros((1, half), F32)
        out = [None] * ngroup
        for g in (range(ngroup) if d == 0 else range(ngroup - 1, -1, -1)):
            hg = h[g * sub:(g + 1) * sub, :] + carry * mup_ref[0, d, 0] + swap(carry) * mup_ref[0, d, 1]
            carry = hg[sub - 1:sub, :] if d == 0 else hg[0:1, :]
            out[g] = hg
        h = jnp.concatenate(out, axis=0)
        if d == 0:
            h = jnp.where(row >= 1, pltpu.roll(h, 1, 0), 0.0)
        else:
            h = jnp.where(row < nrow - 1, pltpu.roll(h, nrow - 1, 0), 0.0)
        states.append(h)
    hp = jnp.concatenate(states, axis=1).astype(BF16)
    y = y + lax.dot_general(hp, o_ref_w[0], (((1,), (1,)), ((), ())), preferred_element_type=F32)
    for j in range(ch):
        y_ref[0, pl.ds(j, nrow, stride=ch), :] = y[:, j * V7X_LANES:(j + 1) * V7X_LANES]


def _s5(s5_in, tblk, eblk, oblk, mu, mup):
    bsz, seq, width = s5_in.shape
    nb = width // V7X_LANES
    once = pl.Buffered(1)
    wspec = lambda arr: pl.BlockSpec((1,) + arr.shape[1:], lambda k, b: (k,) + (0,) * (arr.ndim - 1),
                                     pipeline_mode=once)
    xspec = pl.BlockSpec((1, seq, V7X_LANES), lambda k, b: (b, 0, k), pipeline_mode=once)
    return pl.pallas_call(
        _s5_kernel,
        grid=(nb, bsz),
        in_specs=[xspec, wspec(tblk), wspec(eblk), wspec(oblk), wspec(mu), wspec(mup)],
        out_specs=pl.BlockSpec((1, seq, V7X_LANES), lambda k, b: (b, 0, k)),
        out_shape=jax.ShapeDtypeStruct(s5_in.shape, F32),
        compiler_params=_cparams(("arbitrary", "arbitrary"), 60),
        name="s5",
    )(s5_in, tblk, eblk, oblk, mu, mup)


def _s5_tables(lam_re, lam_im, log_dt, b_re, b_im, c_re, c_im):
    ch = S5_CHUNK
    hi = lax.Precision.HIGHEST
    cmul = lambda x, y: (x[0] * y[0] - x[1] * y[1], x[0] * y[1] + x[1] * y[0])
    lr_, li_ = lam_re.astype(F32), lam_im.astype(F32)
    dt = jnp.exp(log_dt.astype(F32))[..., None]
    ar, ai = lr_ * dt, li_ * dt

    def lam_pow(tau):
        t = jnp.asarray(tau, F32)
        t = t.reshape(t.shape + (1,) * 3)
        mag = jnp.exp(t * ar)
        return mag * jnp.cos(t * ai), mag * jnp.sin(t * ai)

    lb = lam_pow(jnp.ones((), F32))
    num = (lb[0] - 1.0, lb[1])
    den = lr_ * lr_ + li_ * li_
    ratio = ((num[0] * lr_ + num[1] * li_) / den, (num[1] * lr_ - num[0] * li_) / den)
    b_bar = cmul((ratio[0][..., None], ratio[1][..., None]), (b_re.astype(F32), b_im.astype(F32)))
    cc = (c_re.astype(F32), c_im.astype(F32))

    nb, gb, hh, pp = S5_GROUPS // S5_LANE_GROUPS, S5_LANE_GROUPS, S5_GROUP, S5_STATE
    eye = jnp.eye(gb, dtype=F32)

    def b_base(x):
        x = jnp.transpose(x.reshape(2, nb, gb, pp, hh), (1, 2, 4, 0, 3))
        return (x[:, :, :, :, None, :] * eye[None, :, None, None, :, None]).reshape(nb, gb * hh, 2 * gb * pp)

    def c_base(x):
        x = jnp.transpose(x.reshape(2, nb, gb, hh, pp), (1, 2, 3, 0, 4))
        return (x[:, :, :, :, None, :] * eye[None, :, None, None, :, None]).reshape(nb, gb * hh, 2 * gb * pp)

    row = lambda x: jnp.transpose(x.reshape(2, nb, gb * pp), (1, 0, 2)).reshape(nb, 1, 2 * gb * pp)
    base = (b_base(b_bar[0]), b_base(b_bar[1]), c_base(cc[0]), c_base(cc[1]), row(ar), row(ai))
    dtab = _s5_lag_tables(*base)
    return _s5_block_tables(dtab, *base)


def _lam_pow(tau, ar, ai):
    mag = jnp.exp(tau * ar)
    return mag * jnp.cos(tau * ai), mag * jnp.sin(tau * ai)


def _s5_lag_kernel(btr_ref, bti_ref, ctr_ref, cti_ref, ar_ref, ai_ref, d_ref):
    ch = S5_CHUNK
    half = btr_ref.shape[-1] // 2
    tdot = lambda a, b: lax.dot_general(a, b, (((1,), (1,)), ((), ())), preferred_element_type=F32,
                                        precision=lax.Precision.HIGHEST)
    for d in range(2):
        ls = slice(d * half, (d + 1) * half)
        btr, bti, ctr, cti = btr_ref[0, :, ls], bti_ref[0, :, ls], ctr_ref[0, :, ls], cti_ref[0, :, ls]
        for lag in range(ch):
            lr, li = _lam_pow(float(lag), ar_ref[0, :, ls], ai_ref[0, :, ls])
            val = tdot(btr * lr - bti * li, ctr) - tdot(btr * li + bti * lr, cti)
            idx = ch - 1 + lag if d == 0 else ch - 1 - lag
            if d == 1 and lag == 0:
                d_ref[0, idx] = d_ref[0, idx] + val
            else:
                d_ref[0, idx] = val


def _s5_lag_tables(btr, bti, ctr, cti, ar, ai):
    nb = btr.shape[0]
    spec = lambda a: pl.BlockSpec((1,) + a.shape[1:], lambda k: (k, 0, 0))
    nlag = 2 * S5_CHUNK - 1
    return pl.pallas_call(
        _s5_lag_kernel,
        grid=(nb,),
        in_specs=[spec(a) for a in (btr, bti, ctr, cti, ar, ai)],
        out_specs=pl.BlockSpec((1, nlag, V7X_LANES, V7X_LANES), lambda k: (k, 0, 0, 0)),
        out_shape=jax.ShapeDtypeStruct((nb, nlag, V7X_LANES, V7X_LANES), F32),
        compiler_params=_cparams(("parallel",), 32),
        name="s5_lag_tables",
    )(btr, bti, ctr, cti, ar, ai)


def _s5_block_kernel(d_ref, btr_ref, bti_ref, ctr_ref, cti_ref, ar_ref, ai_ref, t_ref, e_ref, ot_ref, mu_ref,
                     mup_ref):
    ch = S5_CHUNK
    i = pl.program_id(1)
    half = btr_ref.shape[-1] // 2
    fi = i.astype(F32)
    for j in range(ch):
        t_ref[0, :, j * V7X_LANES:(j + 1) * V7X_LANES] = d_ref[0, j - i + ch - 1].astype(BF16)
    for d in range(2):
        ls = slice(d * half, (d + 1) * half)
        ar, ai = ar_ref[0, :, ls], ai_ref[0, :, ls]
        lr, li = _lam_pow(fi if d == 1 else (ch - 1.0) - fi, ar, ai)
        btr, bti = btr_ref[0, :, ls], bti_ref[0, :, ls]
        e_ref[0, :, 2 * d * half:(2 * d + 1) * half] = (btr * lr - bti * li).astype(BF16)
        e_ref[0, :, (2 * d + 1) * half:(2 * d + 2) * half] = (btr * li + bti * lr).astype(BF16)
        lr, li = _lam_pow(fi + 1.0 if d == 0 else ch - fi, ar, ai)
        ctr, cti = ctr_ref[0, :, ls], cti_ref[0, :, ls]
        ot_ref[0, :, 2 * d * half:(2 * d + 1) * half] = (ctr * lr - cti * li).astype(BF16)
        ot_ref[0, :, (2 * d + 1) * half:(2 * d + 2) * half] = (-(ctr * li + cti * lr)).astype(BF16)

    @pl.when(i == 0)
    def _():
        for d in range(2):
            ar, ai = ar_ref[0, :, d * half:(d + 1) * half], ai_ref[0, :, d * half:(d + 1) * half]
            for s in range(S5_SCAN_STEPS):
                lr, li = _lam_pow(float(ch * 2 ** s), ar, ai)
                mu_ref[0, d, s, 0:1, :] = jnp.concatenate([lr, lr], axis=1)
                mu_ref[0, d, s, 1:2, :] = jnp.concatenate([-li, li], axis=1)
            for r in range(V7X_SUBLANES):
                lr, li = _lam_pow(float(ch * (r + 1 if d == 0 else V7X_SUBLANES - r)), ar, ai)
                mup_ref[0, d, 0, r:r + 1, :] = jnp.concatenate([lr, lr], axis=1)
                mup_ref[0, d, 1, r:r + 1, :] = jnp.concatenate([-li, li], axis=1)


def _s5_block_tables(dtab, btr, bti, ctr, cti, ar, ai):
    nb = btr.shape[0]
    ch = S5_CHUNK
    big = ch * V7X_LANES
    wide = 2 * btr.shape[-1]
    spec = lambda a: pl.BlockSpec((1,) + a.shape[1:], lambda k, i: (k,) + (0,) * (a.ndim - 1))
    tile = lambda w: pl.BlockSpec((1, V7X_LANES, w), lambda k, i: (k, i, 0))
    return pl.pallas_call(
        _s5_block_kernel,
        grid=(nb, ch),
        in_specs=[spec(a) for a in (dtab, btr, bti, ctr, cti, ar, ai)],
        out_specs=[tile(big), tile(wide), tile(wide),
                   pl.BlockSpec((1, 2, S5_SCAN_STEPS, 2, wide // 2), lambda k, i: (k, 0, 0, 0, 0)),
                   pl.BlockSpec((1, 2, 2, V7X_SUBLANES, wide // 2), lambda k, i: (k, 0, 0, 0, 0))],
        out_shape=[jax.ShapeDtypeStruct((nb, big, big), BF16), jax.ShapeDtypeStruct((nb, big, wide), BF16),
                   jax.ShapeDtypeStruct((nb, big, wide), BF16),
                   jax.ShapeDtypeStruct((nb, 2, S5_SCAN_STEPS, 2, wide // 2), F32),
                   jax.ShapeDtypeStruct((nb, 2, 2, V7X_SUBLANES, wide // 2), F32)],
        compiler_params=_cparams(("parallel", "arbitrary"), 32),
        name="s5_block_tables",
    )(dtab, btr, bti, ctr, cti, ar, ai)


def _merge_kernel(x_ref, ng_ref, yaf_ref, yab_ref, yb_ref, yc_ref, y5_ref, u5_ref, d5_ref, gluw_ref, glub_ref,
                  wgate_ref, bgate_ref, wbr_ref, wout_ref, o_ref):
    x = x_ref[...]
    h = _rms(x, ng_ref[...]).astype(BF16)
    y_d = _gelu_tanh(u5_ref[...] * d5_ref[...] + y5_ref[...])
    y_d = y_d * _sigmoid(_dot(y_d.astype(BF16), gluw_ref[...]) + glub_ref[...])
    branches = (yaf_ref[...] + yab_ref[...], yb_ref[...], yc_ref[...], y_d)
    merged = jnp.zeros(x.shape, F32)
    for i, y in enumerate(branches):
        gate = _sigmoid(_dot(h, wgate_ref[i]) + bgate_ref[i])
        merged = merged + gate * _dot(y.astype(BF16), wbr_ref[i])
    o_ref[...] = x + _dot(merged.astype(BF16), wout_ref[...])


def _merge(xf, lw, yaf, yab, yb, yc, y5, u5):
    t = xf.shape[0]
    tok = lambda w: pl.BlockSpec((TOK_TILE, w), lambda i: (i, 0))
    const = lambda arr: pl.BlockSpec(arr.shape, lambda i: (0,) * arr.ndim, pipeline_mode=pl.Buffered(1))
    weights = [lw["s5_d"], lw["s5_glu_w"], lw["s5_glu_b"], lw["w_gate"], lw["b_gate"], lw["w_branch"], lw["w_out"]]
    return pl.pallas_call(
        _merge_kernel,
        grid=(t // TOK_TILE,),
        in_specs=[tok(D_MODEL), const(lw["norm_mix_g"])] + [tok(MIX_WIDTH)] * 6 + [const(w) for w in weights],
        out_specs=tok(D_MODEL),
        out_shape=jax.ShapeDtypeStruct((t, D_MODEL), F32),
        compiler_params=_cparams(("parallel",), 48),
        name="merge",
    )(xf, lw["norm_mix_g"], yaf, yab, yb, yc, y5, u5, *weights)


def _router_logits(x, ng_ref, wr_ref, br_ref):
    hf = _rms(x, ng_ref[...])
    return hf, _dot_hi(hf, wr_ref[...]) + br_ref[...]


def _moe_route_kernel(x_ref, ng_ref, wr_ref, br_ref, g_ref):
    _, logits = _router_logits(x_ref[...], ng_ref, wr_ref, br_ref)
    lane = lax.broadcasted_iota(jnp.int32, logits.shape, 1).astype(F32)
    gl = jnp.where(lane < MOE_GROUPS, logits, -jnp.inf)
    gmax = jnp.max(gl, axis=1, keepdims=True)
    gidx = jnp.min(jnp.where(gl == gmax, lane, float(V7X_LANES)), axis=1, keepdims=True)
    g_ref[...] = gidx.astype(jnp.int32)


def _moe_route(xf, lw):
    t = xf.shape[0]
    const = lambda arr: pl.BlockSpec(arr.shape, lambda i: (0,) * arr.ndim)
    weights = [lw["norm_ffn_g"], lw["w_router"], lw["b_router"]]
    return pl.pallas_call(
        _moe_route_kernel,
        grid=(t // TOK_TILE,),
        in_specs=[pl.BlockSpec((TOK_TILE, D_MODEL), lambda i: (i, 0))] + [const(w) for w in weights],
        out_specs=pl.BlockSpec((TOK_TILE, 1), lambda i: (i, 0)),
        out_shape=jax.ShapeDtypeStruct((t, 1), jnp.int32),
        compiler_params=_cparams(("parallel",), 32),
        name="moe_route",
    )(xf, *weights)


def _moe_plan_kernel(pos_ref, src_ref):
    def clear(p, c):
        src_ref[p] = 0
        return c

    def place(t, c):
        src_ref[pos_ref[t]] = t
        return c

    lax.fori_loop(0, src_ref.shape[0], clear, 0)
    lax.fori_loop(0, pos_ref.shape[0], place, 0)


def _moe_plan(gidx, tile):
    t = gidx.shape[0]
    ntile = t // tile + MOE_GROUPS
    g = gidx.reshape(t)
    onehot = (g[:, None] == jnp.arange(MOE_GROUPS, dtype=jnp.int32)[None, :]).astype(jnp.int32)
    csum = jnp.cumsum(onehot, axis=0)
    rank = jnp.sum(onehot * (csum - 1), axis=1)
    count = csum[-1]
    gtiles = (count + tile - 1) // tile
    first = jnp.cumsum(gtiles) - gtiles
    pos = jnp.sum(onehot * first[None, :], axis=1) * tile + rank
    tid = jnp.arange(ntile, dtype=jnp.int32)
    tgroup = jnp.minimum(jnp.sum((tid[:, None] >= (first + gtiles)[None, :]).astype(jnp.int32), axis=1),
                         MOE_GROUPS - 1)
    oh_t = (tgroup[:, None] == jnp.arange(MOE_GROUPS, dtype=jnp.int32)[None, :]).astype(jnp.int32)
    tvalid = jnp.clip(jnp.sum(oh_t * count[None, :], axis=1) - (tid - jnp.sum(oh_t * first[None, :], axis=1)) * tile,
                      0, tile)
    src = pl.pallas_call(
        _moe_plan_kernel,
        in_specs=[pl.BlockSpec(memory_space=pltpu.SMEM)],
        out_specs=pl.BlockSpec(memory_space=pltpu.SMEM),
        out_shape=jax.ShapeDtypeStruct((ntile * tile,), jnp.int32),
        name="moe_plan",
    )(pos.astype(jnp.int32))
    return src, tgroup.astype(jnp.int32), tvalid.astype(jnp.int32)


def _moe_expert_kernel(src_ref, tg_ref, nv_ref, x_hbm, ng_ref, wr_ref, br_ref, w13_ref, w2_ref, fg_ref, o_hbm,
                       xbuf, ybuf, sem, *, final):
    i = pl.program_id(0)
    tile = xbuf.shape[0]
    base = i * tile
    nv = nv_ref[i]
    ng, ne, ff = MOE_GROUPS, MOE_EXPERTS, MOE_FF

    @pl.when(i == 0)
    def _():
        xbuf[...] = jnp.zeros_like(xbuf)

    def row_in(r):
        return pltpu.make_async_copy(x_hbm.at[pl.ds(src_ref[base + r], 1), :], xbuf.at[pl.ds(r, 1), :], sem.at[0])

    def row_out(r):
        return pltpu.make_async_copy(ybuf.at[pl.ds(r, 1), :], o_hbm.at[pl.ds(src_ref[base + r], 1), :], sem.at[1])

    def each_row(fn):
        def body(r, c):
            fn(r)
            return c
        lax.fori_loop(0, nv, body, 0)

    @pl.when(nv > 0)
    def _():
        each_row(lambda r: row_in(r).start())
        each_row(lambda r: row_in(r).wait())
        x = xbuf[...]
        hf, logits = _router_logits(x, ng_ref, wr_ref, br_ref)
        lane = lax.broadcasted_iota(jnp.int32, logits.shape, 1).astype(F32)
        neg = -jnp.inf
        big = float(V7X_LANES)
        grp = tg_ref[i].astype(F32)
        gl = jnp.where(lane < ng, logits, neg)
        gmax = jnp.max(gl, axis=1, keepdims=True)
        glog = jnp.sum(jnp.where(lane == grp, logits, 0.0), axis=1, keepdims=True)
        gprob = jnp.exp(glog - gmax) / jnp.sum(jnp.exp(gl - gmax), axis=1, keepdims=True)
        lo = ng + ne * grp
        sel = (lane >= lo) & (lane < lo + ne)
        m1 = jnp.max(jnp.where(sel, logits, neg), axis=1, keepdims=True)
        i1 = jnp.min(jnp.where(sel & (logits == m1), lane, big), axis=1, keepdims=True)
        sel2 = sel & (lane != i1)
        m2 = jnp.max(jnp.where(sel2, logits, neg), axis=1, keepdims=True)
        i2 = jnp.min(jnp.where(sel2 & (logits == m2), lane, big), axis=1, keepdims=True)
        e2 = jnp.exp(m2 - m1)
        w1 = gprob / (1.0 + e2)
        w2 = gprob * e2 / (1.0 + e2)
        gu = _dot(hf.astype(BF16), w13_ref[0])
        act = _silu(gu[:, 0:ne * ff]) * gu[:, ne * ff:2 * ne * ff]
        parts = []
        for e in range(ne):
            wcol = jnp.where(i1 == lo + e, w1, 0.0) + jnp.where(i2 == lo + e, w2, 0.0)
            parts.append(act[:, e * ff:(e + 1) * ff] * wcol)
        y = x + _dot(jnp.concatenate(parts, axis=1).astype(BF16), w2_ref[0])
        if final:
            y = _rms(y, fg_ref[...])
        ybuf[...] = y
        each_row(lambda r: row_out(r).start())
        each_row(lambda r: row_out(r).wait())


def _moe(xf, lw, final_g, final):
    t = xf.shape[0]
    tile = TOK_TILE
    src, tgroup, tvalid = _moe_plan(_moe_route(xf, lw), tile)
    ntile = tgroup.shape[0]
    const = lambda arr: pl.BlockSpec(arr.shape, lambda i, s, g, n: (0,) * arr.ndim)
    bygroup = lambda arr: pl.BlockSpec((1,) + arr.shape[1:], lambda i, s, g, n: (g[i],) + (0,) * (arr.ndim - 1))
    any_space = pl.BlockSpec(memory_space=pl.ANY)
    return pl.pallas_call(
        functools.partial(_moe_expert_kernel, final=final),
        grid_spec=pltpu.PrefetchScalarGridSpec(
            num_scalar_prefetch=3,
            grid=(ntile,),
            in_specs=[any_space, const(lw["norm_ffn_g"]), const(lw["w_router"]), const(lw["b_router"]),
                      bygroup(lw["w_e_13"]), bygroup(lw["w_e_2"]), const(final_g)],
            out_specs=any_space,
            scratch_shapes=[pltpu.VMEM((tile, D_MODEL), F32), pltpu.VMEM((tile, D_MODEL), F32),
                            pltpu.SemaphoreType.DMA((2,))],
        ),
        out_shape=jax.ShapeDtypeStruct((t, D_MODEL), F32),
        compiler_params=_cparams(("arbitrary",), 40),
        name="moe_experts",
    )(src, tgroup, tvalid, xf, lw["norm_ffn_g"], lw["w_router"], lw["b_router"], lw["w_e_13"], lw["w_e_2"], final_g)


def _block_diag(blocks):
    nb, bs, _ = blocks.shape
    eye = jnp.eye(nb, dtype=blocks.dtype)
    return jnp.einsum("nij,nm->nimj", blocks, eye).reshape(nb * bs, nb * bs)


def _pad_heads(w, axis=-1):
    shape = w.shape[:-1] + (GLA_HEADS, GLA_DK)
    w = w.reshape(shape)
    pad = [(0, 0)] * (w.ndim - 1) + [(0, V7X_LANES - GLA_DK)]
    return jnp.pad(w, pad).reshape(w.shape[:-2] + (GLA_HEADS * V7X_LANES,))


def _prep_layer(w, l):
    f = lambda name: w[name][l]
    w_in = f("w_in")
    cuts = np.cumsum([MIX_WIDTH, MIX_WIDTH, 3 * MIX_WIDTH, GLA_HEADS * GLA_DK, GLA_HEADS * GLA_DK, MIX_WIDTH,
                      MIX_WIDTH, 2 * GLA_RANK]).tolist()
    xa, ga, hy, q, k, v, g, lr, s5 = jnp.split(w_in, cuts, axis=-1)
    lr = jnp.pad(lr, ((0, 0), (0, V7X_LANES - 2 * GLA_RANK)))
    w_pack = jnp.concatenate([xa, ga, hy, _pad_heads(q), _pad_heads(k), v, g, lr, s5], axis=-1).astype(BF16)
    assert w_pack.shape[1] == N_PACK
    lw = {"w_pack": w_pack, "norm_mix_g": f("norm_mix_g")[None]}
    lw["lru_conv_w"] = f("lru_conv_w")
    lw["lru_conv_b"] = f("lru_conv_b")[None]
    wa, wx = f("lru_wa"), f("lru_wx")
    lw["lru_wg"] = jnp.stack([jnp.concatenate([_block_diag(wa[d]), _block_diag(wx[d])], axis=1)
                              for d in range(2)]).astype(BF16)
    lw["lru_bg"] = jnp.concatenate([f("lru_ba"), f("lru_bx")], axis=-1)[:, None, :]
    lw["lru_lam"] = f("lru_lambda")[:, None, :]
    lw["hy_conv_w"] = f("hy_conv_w")
    lw["hy_conv_b"] = f("hy_conv_b")[None]
    lw["hy_w1p"] = jnp.pad(f("hy_w1"), ((0, V7X_LANES - HY_EMB), (0, 0)))
    lw["hy_b1"] = f("hy_b1")[None]
    lw["hy_w2"] = f("hy_w2")
    lw["hy_b2"] = f("hy_b2")[None]
    lw["hy_w3"] = f("hy_w3")
    lw["hy_freq"] = f("hy_freq")[None]
    lw["hy_bias"] = f("hy_bias")
    wg2 = _pad_heads(f("gla_wg2"))
    wla = jnp.zeros((2, V7X_LANES, GLA_HEADS * V7X_LANES), F32)
    wla = wla.at[0, 0:GLA_RANK].set(wg2[0]).at[1, GLA_RANK:2 * GLA_RANK].set(wg2[1])
    lw["gla_wla"] = wla.astype(BF16)
    lw["gla_bla"] = _pad_heads(f("gla_bg"))[:, None, :]
    lw["gla_norm_g"] = jnp.tile(f("gla_norm_g"), GLA_HEADS)[None]
    lw["s5_tables"] = _s5_tables(f("s5_lam_re"), f("s5_lam_im"), f("s5_log_dt"), f("s5_b_re"), f("s5_b_im"),
                                 f("s5_c_re"), f("s5_c_im"))
    lw["s5_d"] = f("s5_d")[None]
    lw["s5_glu_w"] = f("s5_glu_w").astype(BF16)
    lw["s5_glu_b"] = f("s5_glu_b")[None]
    lw["w_gate"] = f("w_gate").astype(BF16)
    lw["b_gate"] = f("b_gate")[:, None, :]
    lw["w_branch"] = f("w_branch").astype(BF16)
    lw["w_out"] = f("w_out").astype(BF16)
    wr = jnp.concatenate([f("w_router_group"), jnp.transpose(f("w_router_expert"), (1, 0, 2)).reshape(D_MODEL, -1)],
                         axis=1)
    br = jnp.concatenate([f("b_router_group"), f("b_router_expert").reshape(-1)])
    nr = MOE_GROUPS + MOE_GROUPS * MOE_EXPERTS
    lw["w_router"] = jnp.pad(wr, ((0, 0), (0, V7X_LANES - nr)))
    lw["b_router"] = jnp.pad(br, (0, V7X_LANES - nr))[None]
    lw["norm_ffn_g"] = f("norm_ffn_g")[None]
    wide = lambda a: jnp.transpose(a, (0, 2, 1, 3)).reshape(MOE_GROUPS, D_MODEL, MOE_EXPERTS * MOE_FF)
    lw["w_e_13"] = jnp.concatenate([wide(f("w_e_gate")), wide(f("w_e_up"))], axis=-1).astype(BF16)
    lw["w_e_2"] = f("w_e_down").reshape(MOE_GROUPS, MOE_EXPERTS * MOE_FF, D_MODEL).astype(BF16)
    return lw


def _encoder(x, layers, final_g):
    bsz, seq, _ = x.shape
    plan = _FftPlan(seq)
    xf = x.reshape(bsz * seq, D_MODEL)
    for l, lw in enumerate(layers):
        lru_in, hy_in, gla_in, s5_in = _inproj(xf, lw["norm_mix_g"], lw["w_pack"])
        shp = lambda a: a.reshape(bsz, seq, a.shape[-1])
        yaf, yab = _lru(shp(lru_in), lw["lru_conv_w"], lw["lru_conv_b"], lw["lru_wg"], lw["lru_bg"], lw["lru_lam"])
        yb = _hyena(shp(hy_in), lw, plan)
        yc = _gla(shp(gla_in), lw["gla_wla"], lw["gla_bla"], lw["gla_norm_g"])
        y5 = _s5(shp(s5_in), *lw["s5_tables"])
        flat = lambda a: a.reshape(bsz * seq, MIX_WIDTH)
        xf = _merge(xf, lw, flat(yaf), flat(yab), flat(yb), flat(yc), flat(y5), s5_in)
        xf = _moe(xf, lw, final_g, final=(l == len(layers) - 1))
    return xf.reshape(bsz, seq, D_MODEL)


def kernel(x_prompt, x_sample, norm_mix_g, w_in, lru_conv_w, lru_conv_b, lru_wa, lru_ba, lru_wx, lru_bx,
           lru_lambda, hy_conv_w, hy_conv_b, hy_w1, hy_b1, hy_w2, hy_b2, hy_w3, hy_freq, hy_bias,
           gla_wg2, gla_bg, gla_norm_g, s5_lam_re, s5_lam_im, s5_log_dt, s5_b_re, s5_b_im, s5_c_re, s5_c_im,
           s5_d, s5_glu_w, s5_glu_b, w_branch, w_gate, b_gate, w_out, norm_ffn_g, w_router_group,
           b_router_group, w_router_expert, b_router_expert, w_e_gate, w_e_up, w_e_down, final_norm_g):
    w = dict(norm_mix_g=norm_mix_g, w_in=w_in, lru_conv_w=lru_conv_w, lru_conv_b=lru_conv_b, lru_wa=lru_wa,
             lru_ba=lru_ba, lru_wx=lru_wx, lru_bx=lru_bx, lru_lambda=lru_lambda, hy_conv_w=hy_conv_w,
             hy_conv_b=hy_conv_b, hy_w1=hy_w1, hy_b1=hy_b1, hy_w2=hy_w2, hy_b2=hy_b2, hy_w3=hy_w3,
             hy_freq=hy_freq, hy_bias=hy_bias, gla_wg2=gla_wg2, gla_bg=gla_bg, gla_norm_g=gla_norm_g,
             s5_lam_re=s5_lam_re, s5_lam_im=s5_lam_im, s5_log_dt=s5_log_dt, s5_b_re=s5_b_re, s5_b_im=s5_b_im,
             s5_c_re=s5_c_re, s5_c_im=s5_c_im, s5_d=s5_d, s5_glu_w=s5_glu_w, s5_glu_b=s5_glu_b,
             w_branch=w_branch, w_gate=w_gate, b_gate=b_gate, w_out=w_out, norm_ffn_g=norm_ffn_g,
             w_router_group=w_router_group, b_router_group=b_router_group, w_router_expert=w_router_expert,
             b_router_expert=b_router_expert, w_e_gate=w_e_gate, w_e_up=w_e_up, w_e_down=w_e_down)
    layers = [_prep_layer(w, l) for l in range(norm_mix_g.shape[0])]
    fg = final_norm_g[None]
    return (_encoder(x_prompt, layers, fg), _encoder(x_sample, layers, fg))
```

```python
import functools
import math

import numpy as np
import jax
import jax.numpy as jnp
from jax import lax
from jax.experimental import pallas as pl
from jax.experimental.pallas import tpu as pltpu

F32 = jnp.float32
BF16 = jnp.bfloat16

D_MODEL = 1024
DEPTH = 2
EPS = 1e-6
MIX_WIDTH = D_MODEL // 2
LRU_BLOCKS = 8
LRU_BLOCK = MIX_WIDTH // LRU_BLOCKS
LRU_CONV = 4
LRU_C = 8.0
HY_ORDER = 2
HY_CONV = 3
HY_EMB = 33
HY_BANDS = (HY_EMB - 1) // 2
HY_HIDDEN = 64
HY_DECAY_TARGET = 1e-2
HY_FAST_PCT = 0.3
HY_SLOW_PCT = 1.5
GLA_HEADS = 4
GLA_DK = MIX_WIDTH // 8
GLA_DV = MIX_WIDTH // GLA_HEADS
GLA_RANK = 16
GLA_TAU = 16.0
GLA_CHUNK = 64
S5_GROUP = 16
S5_GROUPS = MIX_WIDTH // S5_GROUP
S5_STATE = 64
MOE_GROUPS = 4
MOE_EXPERTS = 4
MOE_FF = D_MODEL // 4

V7X_LANES = 128
V7X_SUBLANES = 8
V7X_VMEM_BYTES = 64 * 2**20
MIB = 2**20

GLA_PACK = 4 * V7X_LANES * 2 + 512 + 512 + V7X_LANES
PK_LRU = (0, 1024)
PK_HY = (1024, 2560)
PK_GLA = (2560, 2560 + GLA_PACK)
PK_S5 = (PK_GLA[1], PK_GLA[1] + 512)
N_PACK = PK_S5[1]

TOK_TILE = 256
LRU_TILE = 256
CONV_TILE = 512
GLA_TILE = 256
GLA_BATCH = 2
S5_CHUNK = 16
S5_LANE_GROUPS = V7X_LANES // S5_GROUP
S5_SCAN_STEPS = 3
FFT_N1 = 64
FFT_ROWS = 128
FFT_COLS = 256
FFT_KB = 11


def _cparams(sem, vmem_mib):
    return pltpu.CompilerParams(dimension_semantics=sem, vmem_limit_bytes=int(vmem_mib * MIB))


def _rms(x, g):
    return x * lax.rsqrt(jnp.mean(x * x, axis=-1, keepdims=True) + EPS) * g


def _sigmoid(x):
    return 1.0 / (1.0 + jnp.exp(-x))


def _softplus(x):
    return jnp.maximum(x, 0.0) + jnp.log(1.0 + jnp.exp(-jnp.abs(x)))


def _gelu_tanh(x):
    return 0.5 * x * (1.0 + jnp.tanh(math.sqrt(2.0 / math.pi) * (x + 0.044715 * (x * x * x))))


def _silu(x):
    return x * _sigmoid(x)


def _dot(a, b):
    return jnp.dot(a, b, preferred_element_type=F32)


def _dot_hi(a, b):
    return jnp.dot(a, b, preferred_element_type=F32, precision=lax.Precision.HIGHEST)


def _inproj_kernel(x_ref, g_ref, w_ref, lru_ref, hy_ref, gla_ref, s5_ref):
    h = _rms(x_ref[...], g_ref[...]).astype(BF16)
    lru_ref[...] = _dot(h, w_ref[:, PK_LRU[0]:PK_LRU[1]])
    hy_ref[...] = _dot(h, w_ref[:, PK_HY[0]:PK_HY[1]])
    gla_ref[...] = _dot(h, w_ref[:, PK_GLA[0]:PK_GLA[1]])
    s5_ref[...] = _dot(h, w_ref[:, PK_S5[0]:PK_S5[1]])


def _inproj(xf, g, w_pack):
    t = xf.shape[0]
    widths = [PK_LRU[1] - PK_LRU[0], PK_HY[1] - PK_HY[0], PK_GLA[1] - PK_GLA[0], PK_S5[1] - PK_S5[0]]
    return pl.pallas_call(
        _inproj_kernel,
        grid=(t // TOK_TILE,),
        in_specs=[pl.BlockSpec((TOK_TILE, D_MODEL), lambda i: (i, 0)),
                  pl.BlockSpec((1, D_MODEL), lambda i: (0, 0)),
                  pl.BlockSpec((D_MODEL, N_PACK), lambda i: (0, 0))],
        out_specs=[pl.BlockSpec((TOK_TILE, w), lambda i: (i, 0)) for w in widths],
        out_shape=[jax.ShapeDtypeStruct((t, w), F32) for w in widths],
        compiler_params=_cparams(("parallel",), 48),
        name="inproj",
    )(xf, g, w_pack)


def _fill_ext(ext_ref, main, prev8, next8, first, last, tile):
    ext_ref[0:8, :] = jnp.where(first, 0.0, prev8)
    ext_ref[8:8 + tile, :] = main
    ext_ref[8 + tile:16 + tile, :] = jnp.where(last, 0.0, next8)


def _linear_scan_tile(a, b, carry, reverse):
    n = a.shape[0]
    sub = V7X_SUBLANES
    row = lax.broadcasted_iota(jnp.int32, a.shape, 0) % sub
    d = 1
    while d < sub:
        if reverse:
            a_s = pltpu.roll(a, n - d, 0)
            b_s = pltpu.roll(b, n - d, 0)
            valid = row < sub - d
        else:
            a_s = pltpu.roll(a, d, 0)
            b_s = pltpu.roll(b, d, 0)
            valid = row >= d
        b = jnp.where(valid, a * b_s + b, b)
        a = jnp.where(valid, a * a_s, a)
        d *= 2
    ngroup = n // sub
    out = [None] * ngroup
    for g in (range(ngroup - 1, -1, -1) if reverse else range(ngroup)):
        h = b[g * sub:(g + 1) * sub, :] + a[g * sub:(g + 1) * sub, :] * carry
        carry = h[0:1, :] if reverse else h[sub - 1:sub, :]
        out[g] = h
    return jnp.concatenate(out, axis=0), carry


def _lru_kernel(mf_ref, pf_ref, nf_ref, mb_ref, pb_ref, nb_ref, cw_ref, cb_ref, wg_ref, bg_ref, lam_ref,
                of_ref, ob_ref, extf_ref, extb_ref, carry_ref):
    c = pl.program_id(1)
    nc = pl.num_programs(1)
    tile = LRU_TILE

    @pl.when(c == 0)
    def _():
        carry_ref[...] = jnp.zeros_like(carry_ref)

    def one(m_ref, p_ref, n_ref, ext_ref, d, first, last, o_ref):
        x = m_ref[0, :, 0:MIX_WIDTH]
        ga = m_ref[0, :, MIX_WIDTH:2 * MIX_WIDTH]
        _fill_ext(ext_ref, x, p_ref[0], n_ref[0], first, last, tile)
        xc = cb_ref[...] + ext_ref[6:6 + tile, :] * cw_ref[0:1, :]
        xc = xc + ext_ref[7:7 + tile, :] * cw_ref[1:2, :]
        xc = xc + ext_ref[8:8 + tile, :] * cw_ref[2:3, :]
        xc = xc + ext_ref[9:9 + tile, :] * cw_ref[3:4, :]
        z = _dot(xc.astype(BF16), wg_ref[d]) + bg_ref[d]
        gate_r = _sigmoid(z[:, 0:MIX_WIDTH])
        gate_i = _sigmoid(z[:, MIX_WIDTH:2 * MIX_WIDTH])
        log_a = -LRU_C * gate_r * _softplus(-lam_ref[d])
        a = jnp.exp(log_a)
        t = 1.0 - a * a
        b = jnp.where(t > 0.0, t * lax.rsqrt(t), 0.0) * gate_i * xc
        h, last = _linear_scan_tile(a, b, carry_ref[d:d + 1, :], reverse=(d == 1))
        carry_ref[d:d + 1, :] = last
        o_ref[0] = h * _gelu_tanh(ga)

    one(mf_ref, pf_ref, nf_ref, extf_ref, 0, c == 0, c == nc - 1, of_ref)
    one(mb_ref, pb_ref, nb_ref, extb_ref, 1, c == nc - 1, c == 0, ob_ref)


def _lru(lru_in, cw, cb, wg, bg, lam):
    bsz, seq, _ = lru_in.shape
    tile = LRU_TILE
    nc = seq // tile
    r8 = tile // 8
    last8 = seq // 8 - 1

    def fwd(c):
        return c

    def bwd(c):
        return nc - 1 - c

    def specs(ch):
        return [pl.BlockSpec((1, tile, 2 * MIX_WIDTH), lambda b, c: (b, ch(c), 0)),
                pl.BlockSpec((1, 8, MIX_WIDTH), lambda b, c: (b, jnp.maximum(ch(c) * r8 - 1, 0), 0)),
                pl.BlockSpec((1, 8, MIX_WIDTH), lambda b, c: (b, jnp.minimum((ch(c) + 1) * r8, last8), 0))]

    const = lambda shape: pl.BlockSpec(shape, lambda b, c: (0,) * len(shape))
    return pl.pallas_call(
        _lru_kernel,
        grid=(bsz, nc),
        in_specs=specs(fwd) + specs(bwd) + [const(cw.shape), const(cb.shape), const(wg.shape), const(bg.shape),
                                            const(lam.shape)],
        out_specs=[pl.BlockSpec((1, tile, MIX_WIDTH), lambda b, c: (b, c, 0)),
                   pl.BlockSpec((1, tile, MIX_WIDTH), lambda b, c: (b, nc - 1 - c, 0))],
        out_shape=[jax.ShapeDtypeStruct((bsz, seq, MIX_WIDTH), F32)] * 2,
        scratch_shapes=[pltpu.VMEM((tile + 16, MIX_WIDTH), F32), pltpu.VMEM((tile + 16, MIX_WIDTH), F32),
                        pltpu.VMEM((8, MIX_WIDTH), F32)],
        compiler_params=_cparams(("parallel", "arbitrary"), 40),
        name="lru",
    )(lru_in, lru_in, lru_in, lru_in, lru_in, lru_in, cw, cb, wg, bg, lam)


def _hyconv_kernel(m_ref, p_ref, n_ref, cw_ref, cb_ref, o_ref, ext_ref):
    c = pl.program_id(1)
    nc = pl.num_programs(1)
    tile = CONV_TILE
    _fill_ext(ext_ref, m_ref[0], p_ref[0], n_ref[0], c == 0, c == nc - 1, tile)
    y = cb_ref[...] + ext_ref[7:7 + tile, :] * cw_ref[0:1, :]
    y = y + ext_ref[8:8 + tile, :] * cw_ref[1:2, :]
    y = y + ext_ref[9:9 + tile, :] * cw_ref[2:3, :]
    o_ref[0] = y


def _hyconv(hy_in, cw, cb):
    bsz, seq, width = hy_in.shape
    tile = CONV_TILE
    r8 = tile // 8
    last8 = seq // 8 - 1
    const = lambda shape: pl.BlockSpec(shape, lambda b, c: (0,) * len(shape))
    return pl.pallas_call(
        _hyconv_kernel,
        grid=(bsz, seq // tile),
        in_specs=[pl.BlockSpec((1, tile, width), lambda b, c: (b, c, 0)),
                  pl.BlockSpec((1, 8, width), lambda b, c: (b, jnp.maximum(c * r8 - 1, 0), 0)),
                  pl.BlockSpec((1, 8, width), lambda b, c: (b, jnp.minimum((c + 1) * r8, last8), 0)),
                  const(cw.shape), const(cb.shape)],
        out_specs=pl.BlockSpec((1, tile, width), lambda b, c: (b, c, 0)),
        out_shape=jax.ShapeDtypeStruct((bsz, seq, width), F32),
        scratch_shapes=[pltpu.VMEM((tile + 16, width), F32)],
        compiler_params=_cparams(("parallel", "parallel"), 40),
        name="hyconv",
    )(hy_in, hy_in, hy_in, cw, cb)


class _FftPlan:
    def __init__(self, seq):
        n = 2 * seq
        n1 = FFT_N1
        n2 = n // n1
        assert n1 * n2 == n and n2 % 16 == 0
        h1 = n1 // 2 + 1
        nh = n1 // 2
        self.n, self.n1, self.n2, self.h1, self.nh = n, n1, n2, h1, nh
        self.rows = min(FFT_ROWS, n2)
        assert n2 % self.rows == 0 and h1 % FFT_KB == 0
        k1 = np.arange(h1, dtype=np.float64)
        m1 = np.arange(nh, dtype=np.float64)
        r = np.arange(8, dtype=np.float64)
        ang = -2.0 * np.pi * (k1[:, None, None] * m1[None, None, :] / n1 + r[None, :, None] * k1[:, None, None] / n)
        e = np.exp(1j * ang)
        fa = np.zeros((h1, 8, nh, 8), np.complex128)
        for rr in range(8):
            fa[:, rr, :, rr] = e[:, rr, :]
        fa = fa.reshape(h1 * 8, nh * 8)
        self.fa = jnp.asarray(np.concatenate([fa.real, fa.imag], axis=0), BF16)
        ck = np.where((k1 == 0) | (k1 == n1 // 2), 1.0, 2.0)
        ec = np.conj(e) * ck[:, None, None] / n
        fc = np.zeros((nh, 8, 2, h1, 8), np.float64)
        for rr in range(8):
            fc[:, rr, 0, :, rr] = ec[:, rr, :].real.T
            fc[:, rr, 1, :, rr] = -ec[:, rr, :].imag.T
        self.fc = jnp.asarray(fc.reshape(nh * 8, 2 * h1 * 8), BF16)
        rg = np.arange(n2 // 8, dtype=np.float64)
        tw = np.exp(-2j * np.pi * 8.0 * rg[None, :] * k1[:, None] / n)
        self.tw = jnp.asarray(np.concatenate([tw.real, tw.imag], axis=0), F32)
        tw1 = np.exp(-2j * np.pi * k1 / n)
        self.tw1 = jnp.asarray(np.stack([tw1.real, tw1.imag]), F32)
        q = np.arange(n2, dtype=np.float64)
        f2 = np.exp(-2j * np.pi * np.outer(q, q) / n2)
        fr, fi = f2.real, f2.imag
        self.gb = jnp.asarray(np.block([[fr, -fi], [fi, fr]]), BF16)
        self.gbi = jnp.asarray(np.block([[fr, fi], [-fi, fr]]), BF16)


def _fft_a_kernel(tw_ref, x_ref, fa_ref, a_ref, *, h1, nh, rows):
    rb = pl.program_id(2)
    cols = x_ref.shape[-1]

    def stage(rg):
        xg = x_ref[0, :, pl.ds(pl.multiple_of(rg * 8, 8), 8), :].reshape(nh * 8, cols).astype(BF16)
        return _dot(fa_ref[...], xg)

    def body(i, carry):
        p0 = stage(2 * i)
        p1 = stage(2 * i + 1)
        g0 = rb * (rows // 8) + 2 * i
        for k in range(h1):
            outs = []
            for p, g in ((p0, g0), (p1, g0 + 1)):
                pr = p[k * 8:(k + 1) * 8, :]
                pi = p[(h1 + k) * 8:(h1 + k + 1) * 8, :]
                tr = tw_ref[k, g]
                ti = tw_ref[h1 + k, g]
                outs.append((pr * tr - pi * ti, pr * ti + pi * tr))
            dst = pl.ds(pl.multiple_of(i * 16, 16), 16)
            a_ref[0, k, 0, dst, :] = jnp.concatenate([outs[0][0], outs[1][0]], axis=0).astype(BF16)
            a_ref[0, k, 1, dst, :] = jnp.concatenate([outs[0][1], outs[1][1]], axis=0).astype(BF16)
        return carry

    lax.fori_loop(0, rows // 16, body, 0)


def _fft_a(x4, plan, col_off=0, ncols=None):
    bq, nh, n2, width = x4.shape
    ncols = width if ncols is None else ncols
    cb0 = col_off // FFT_COLS
    rows = plan.rows
    kern = functools.partial(_fft_a_kernel, h1=plan.h1, nh=nh, rows=rows)
    return pl.pallas_call(
        kern,
        grid=(bq, ncols // FFT_COLS, n2 // rows),
        in_specs=[pl.BlockSpec(memory_space=pltpu.SMEM),
                  pl.BlockSpec((1, nh, rows, FFT_COLS), lambda b, c, r: (b, 0, r, cb0 + c)),
                  pl.BlockSpec(plan.fa.shape, lambda b, c, r: (0, 0))],
        out_specs=pl.BlockSpec((1, plan.h1, 2, rows, FFT_COLS), lambda b, c, r: (b, 0, 0, r, c)),
        out_shape=jax.ShapeDtypeStruct((bq, plan.h1, 2, n2, ncols), BF16),
        compiler_params=_cparams(("parallel", "parallel", "parallel"), 40),
        name="fft_outer_fwd",
    )(plan.tw, x4, plan.fa)


def _fft_mid_kernel(a_ref, kf_ref, gb_ref, gbi_ref, o_ref, *, n2):
    kb = a_ref.shape[1]
    cols = a_ref.shape[-1]
    for k in range(kb):
        y = _dot(gb_ref[...], a_ref[0, k].reshape(2 * n2, cols))
        yr, yi = y[0:n2, :], y[n2:2 * n2, :]
        kr, ki = kf_ref[0, k, 0], kf_ref[0, k, 1]
        z = jnp.concatenate([yr * kr - yi * ki, yr * ki + yi * kr], axis=0).astype(BF16)
        o_ref[0, k] = _dot(gbi_ref[...], z).reshape(2, n2, cols).astype(BF16)


def _fft_mid(a, kf, order, plan):
    bq, h1, _, n2, width = a.shape
    kern = functools.partial(_fft_mid_kernel, n2=n2)
    blk = (1, FFT_KB, 2, n2, FFT_COLS)
    return pl.pallas_call(
        kern,
        grid=(bq, width // FFT_COLS, h1 // FFT_KB),
        in_specs=[pl.BlockSpec(blk, lambda b, c, k: (b, k, 0, 0, c)),
                  pl.BlockSpec(blk, lambda b, c, k: (order, k, 0, 0, c)),
                  pl.BlockSpec(plan.gb.shape, lambda b, c, k: (0, 0)),
                  pl.BlockSpec(plan.gbi.shape, lambda b, c, k: (0, 0))],
        out_specs=pl.BlockSpec(blk, lambda b, c, k: (b, k, 0, 0, c)),
        out_shape=jax.ShapeDtypeStruct(a.shape, BF16),
        compiler_params=_cparams(("parallel", "parallel", "parallel"), 40),
        name="fft_inner_mul",
    )(a, kf, plan.gb, plan.gbi)


def _fft_c_kernel(tw_ref, b_ref, u_ref, g_ref, bias_ref, fc_ref, o_ref, *, h1, nh, rows):
    rb = pl.program_id(2)
    cols = o_ref.shape[-1]

    def body(i, carry):
        src = pl.ds(pl.multiple_of(i * 16, 16), 16)
        tiles = [[b_ref[0, k, p, src, :].astype(F32) for p in range(2)] for k in range(h1)]
        for half in range(2):
            g = rb * (rows // 8) + 2 * i + half
            re_rows, im_rows = [], []
            for k in range(h1):
                br = tiles[k][0][half * 8:(half + 1) * 8, :]
                bi = tiles[k][1][half * 8:(half + 1) * 8, :]
                tr = tw_ref[k, g]
                ti = tw_ref[h1 + k, g]
                re_rows.append(br * tr + bi * ti)
                im_rows.append(bi * tr - br * ti)
            s = jnp.concatenate(re_rows + im_rows, axis=0).astype(BF16)
            y = _dot(fc_ref[...], s).reshape(nh, 8, cols)
            dst = pl.ds(pl.multiple_of((2 * i + half) * 8, 8), 8)
            u = u_ref[0, :, dst, :]
            o_ref[0, :, dst, :] = (y + u * bias_ref[...]) * g_ref[0, :, dst, :]
        return carry

    lax.fori_loop(0, rows // 16, body, 0)


def _fft_c(bm, u4, u_off, g4, g_off, bias, plan):
    bq, h1, _, n2, width = bm.shape
    nh = plan.nh
    rows = plan.rows
    ub, gbk = u_off // FFT_COLS, g_off // FFT_COLS
    kern = functools.partial(_fft_c_kernel, h1=h1, nh=nh, rows=rows)
    xblk = (1, nh, rows, FFT_COLS)
    return pl.pallas_call(
        kern,
        grid=(bq, width // FFT_COLS, n2 // rows),
        in_specs=[pl.BlockSpec(memory_space=pltpu.SMEM),
                  pl.BlockSpec((1, h1, 2, rows, FFT_COLS), lambda b, c, r: (b, 0, 0, r, c)),
                  pl.BlockSpec(xblk, lambda b, c, r: (b, 0, r, ub + c)),
                  pl.BlockSpec(xblk, lambda b, c, r: (b, 0, r, gbk + c)),
                  pl.BlockSpec((1, FFT_COLS), lambda b, c, r: (0, c)),
                  pl.BlockSpec(plan.fc.shape, lambda b, c, r: (0, 0))],
        out_specs=pl.BlockSpec(xblk, lambda b, c, r: (b, 0, r, c)),
        out_shape=jax.ShapeDtypeStruct((bq, nh, n2, width), F32),
        compiler_params=_cparams(("parallel", "parallel", "parallel"), 48),
        name="fft_outer_inv",
    )(plan.tw, bm, u4, g4, bias, plan.fc)


def _hyfilt_gen_kernel(w1_ref, b1_ref, w2_ref, b2_ref, w3_ref, fr_ref, o_ref, ss_ref, *, seq):
    rblk = pl.program_id(0)
    tile, cols = o_ref.shape[1], o_ref.shape[2]
    irow = lax.broadcasted_iota(jnp.int32, (tile, V7X_LANES), 0) + rblk * tile
    row = irow.astype(F32)
    lane = lax.broadcasted_iota(jnp.int32, (tile, V7X_LANES), 1)
    t = row / (seq - 1.0)
    omega = (2.0 * math.pi / seq) * row
    band_step = (HY_BANDS - 1 - 1e-4) / (HY_BANDS - 1)
    is_cos = (lane >= 1) & (lane <= HY_BANDS)
    is_sin = (lane > HY_BANDS) & (lane <= 2 * HY_BANDS)
    bidx = jnp.where(is_cos, lane - 1, lane - 1 - HY_BANDS).astype(F32)
    ang = omega * (1e-4 + band_step * bidx)
    trig = jnp.cos(ang + jnp.where(is_sin, 0.5 * math.pi, 0.0))
    z = jnp.where(lane == 0, t, jnp.where(is_cos | is_sin, trig, 0.0))
    fr = fr_ref[...]
    hid = jnp.sin(fr * (_dot_hi(z, w1_ref[...]) + b1_ref[...]))
    hid = jnp.sin(fr * (_dot_hi(hid, w2_ref[...]) + b2_ref[...]))
    filt = _dot_hi(hid, w3_ref[...])
    col = lax.broadcasted_iota(jnp.int32, (1, cols), 1)
    chan = (col % MIX_WIDTH).astype(F32)
    max_decay = math.log(HY_DECAY_TARGET) / HY_FAST_PCT
    min_decay = math.log(HY_DECAY_TARGET) / HY_SLOW_PCT
    delta = jnp.abs(min_decay + (max_decay - min_decay) / (MIX_WIDTH - 1) * chan)
    filt = filt * jnp.exp(-t[:, 0:1] * delta)
    is_bwd = (col // MIX_WIDTH) % 2 == 1
    rows_c = lax.broadcasted_iota(jnp.int32, (tile, cols), 0) + rblk * tile
    filt = jnp.where(is_bwd & (rows_c == seq - 1), 0.0, filt)
    o_ref[0] = filt

    @pl.when(rblk == 0)
    def _():
        ss_ref[...] = jnp.zeros_like(ss_ref)

    ss_ref[...] += jnp.sum(filt * filt, axis=0, keepdims=True)


def _hyfilt_gen(seq, w1p, b1, w2, b2, w3, freq):
    ncol = w3.shape[1]
    tile = min(seq, 512)
    const = lambda shape: pl.BlockSpec(shape, lambda r: (0,) * len(shape))
    kern = functools.partial(_hyfilt_gen_kernel, seq=seq)
    return pl.pallas_call(
        kern,
        grid=(seq // tile,),
        in_specs=[const(w1p.shape), const(b1.shape), const(w2.shape), const(b2.shape), const(w3.shape),
                  const(freq.shape)],
        out_specs=[pl.BlockSpec((1, tile, ncol), lambda r: (0, r, 0)),
                   pl.BlockSpec((1, ncol), lambda r: (0, 0))],
        out_shape=[jax.ShapeDtypeStruct((1, seq, ncol), F32), jax.ShapeDtypeStruct((1, ncol), F32)],
        compiler_params=_cparams(("arbitrary",), 40),
        name="hyena_filter_gen",
    )(w1p, b1, w2, b2, w3, freq)


def _hyfilt_spec_kernel(tw1_ref, a0_ref, a1_ref, ss0_ref, ss1_ref, gb_ref, o_ref, *, n2):
    kblk = pl.program_id(2)
    kb = a0_ref.shape[1]
    cols = a0_ref.shape[-1]
    h1 = tw1_ref.shape[1]
    scale = lax.rsqrt(ss0_ref[...] + ss1_ref[...] + EPS)
    ang2 = (-2.0 * math.pi / n2) * lax.broadcasted_iota(jnp.int32, (n2, cols), 0).astype(F32)
    cr, ci = jnp.cos(ang2), jnp.sin(ang2)
    for k in range(kb):
        y0 = _dot(gb_ref[...], a0_ref[0, k].reshape(2 * n2, cols))
        y1 = _dot(gb_ref[...], a1_ref[0, k].reshape(2 * n2, cols))
        sr = tw1_ref[0, kblk * kb + k]
        si = tw1_ref[1, kblk * kb + k]
        wr, wi = cr * sr - ci * si, cr * si + ci * sr
        y1r, y1i = y1[0:n2, :], y1[n2:2 * n2, :]
        o_ref[0, k, 0] = (y0[0:n2, :] + (wr * y1r - wi * y1i)) * scale
        o_ref[0, k, 1] = (y0[n2:2 * n2, :] - (wr * y1i + wi * y1r)) * scale


def _hyfilt_spec(af, ss, plan):
    _, h1, _, n2, _ = af.shape
    ncb = MIX_WIDTH // FFT_COLS
    blk = (1, FFT_KB, 2, n2, FFT_COLS)
    kern = functools.partial(_hyfilt_spec_kernel, n2=n2)
    return pl.pallas_call(
        kern,
        grid=(HY_ORDER, ncb, h1 // FFT_KB),
        in_specs=[pl.BlockSpec(memory_space=pltpu.SMEM),
                  pl.BlockSpec(blk, lambda o, c, k: (0, k, 0, 0, o * 2 * ncb + c)),
                  pl.BlockSpec(blk, lambda o, c, k: (0, k, 0, 0, o * 2 * ncb + ncb + c)),
                  pl.BlockSpec((1, FFT_COLS), lambda o, c, k: (0, o * 2 * ncb + c)),
                  pl.BlockSpec((1, FFT_COLS), lambda o, c, k: (0, o * 2 * ncb + ncb + c)),
                  pl.BlockSpec(plan.gb.shape, lambda o, c, k: (0, 0))],
        out_specs=pl.BlockSpec(blk, lambda o, c, k: (o, k, 0, 0, c)),
        out_shape=jax.ShapeDtypeStruct((HY_ORDER, h1, 2, n2, MIX_WIDTH), F32),
        compiler_params=_cparams(("parallel", "parallel", "parallel"), 40),
        name="hyena_filter_spectrum",
    )(plan.tw1, af, af, ss, ss, plan.gb)


def _hyena(hy_in, lw, plan):
    bsz, seq, _ = hy_in.shape
    zc = _hyconv(hy_in, lw["hy_conv_w"], lw["hy_conv_b"])
    filt, ss = _hyfilt_gen(seq, lw["hy_w1p"], lw["hy_b1"], lw["hy_w2"], lw["hy_b2"], lw["hy_w3"], lw["hy_freq"])
    af = _fft_a(filt.reshape(1, plan.nh, plan.n2, filt.shape[-1]), plan)
    kf = _hyfilt_spec(af, ss, plan)
    zc4 = zc.reshape(bsz, plan.nh, plan.n2, 3 * MIX_WIDTH)
    a = _fft_a(zc4, plan, col_off=0, ncols=MIX_WIDTH)
    bm = _fft_mid(a, kf, 0, plan)
    z1 = _fft_c(bm, zc4, 0, zc4, MIX_WIDTH, lw["hy_bias"][0:1], plan)
    a = _fft_a(z1, plan)
    bm = _fft_mid(a, kf, 1, plan)
    z2 = _fft_c(bm, z1, 0, zc4, 2 * MIX_WIDTH, lw["hy_bias"][1:2], plan)
    return z2.reshape(bsz, seq, MIX_WIDTH)


def _gla_kernel(*refs, reverse):
    if reverse:
        x_ref, of_ref, wla_ref, bla_ref, ng_ref, o_ref, st_ref = refs
    else:
        x_ref, wla_ref, bla_ref, o_ref, st_ref = refs
    c = pl.program_id(1)

    @pl.when(c == 0)
    def _():
        st_ref[...] = jnp.zeros_like(st_ref)

    for bb in range(x_ref.shape[0]):
        _gla_sequence(bb, x_ref, of_ref if reverse else None, wla_ref, bla_ref, ng_ref if reverse else None, o_ref,
                      st_ref, reverse)


def _gla_sequence(bb, x_ref, of_ref, wla_ref, bla_ref, ng_ref, o_ref, st_ref, reverse):
    tile = GLA_TILE
    ck = GLA_CHUNK
    nck = tile // ck
    hw = V7X_LANES
    nh = GLA_HEADS
    q = x_ref[bb, :, 0:nh * hw] * (GLA_DK ** -0.5)
    k = x_ref[bb, :, nh * hw:2 * nh * hw]
    v = x_ref[bb, :, 2 * nh * hw:3 * nh * hw]
    lr = x_ref[bb, :, 4 * nh * hw:4 * nh * hw + hw]
    zl = _dot(lr.astype(BF16), wla_ref[...]) + bla_ref[...]
    la = (jnp.minimum(zl, 0.0) - jnp.log(1.0 + jnp.exp(-jnp.abs(zl)))) / GLA_TAU

    row = lax.broadcasted_iota(jnp.int32, la.shape, 0) % ck
    bcum = la
    d = 1
    while d < ck:
        if reverse:
            bcum = bcum + jnp.where(row < ck - d, pltpu.roll(bcum, tile - d, 0), 0.0)
        else:
            bcum = bcum + jnp.where(row >= d, pltpu.roll(bcum, d, 0), 0.0)
        d *= 2
    b3 = bcum.reshape(nck, ck, nh * hw)
    blast = b3[:, 0:1, :] if reverse else b3[:, ck - 1:ck, :]
    q_e = (q * jnp.exp(bcum)).astype(BF16)
    k_e = (k * jnp.exp(-bcum)).astype(BF16)
    k_d = (k.reshape(nck, ck, nh * hw) * jnp.exp(blast - b3)).reshape(tile, nh * hw).astype(BF16)
    gch = jnp.exp(blast)
    vb = v.astype(BF16)

    ri = lax.broadcasted_iota(jnp.int32, (ck, ck), 0)
    ci = lax.broadcasted_iota(jnp.int32, (ck, ck), 1)
    mask = (ri <= ci) if reverse else (ri >= ci)
    order = range(nck - 1, -1, -1) if reverse else range(nck)
    outs = [None] * nck
    for n in order:
        rs = slice(n * ck, (n + 1) * ck)
        heads = []
        for h in range(nh):
            ls = slice(h * hw, (h + 1) * hw)
            qe, ke, kd, vh = q_e[rs, ls], k_e[rs, ls], k_d[rs, ls], vb[rs, ls]
            st = st_ref[bb, h]
            sc = lax.dot_general(qe, ke, (((1,), (1,)), ((), ())), preferred_element_type=F32)
            sc = jnp.where(mask, sc, 0.0).astype(BF16)
            o = _dot(sc, vh) + lax.dot_general(qe, st.astype(BF16), (((1,), (1,)), ((), ())),
                                               preferred_element_type=F32)
            upd = lax.dot_general(vh, kd, (((0,), (0,)), ((), ())), preferred_element_type=F32)
            st_ref[bb, h] = st * gch[n, :, ls] + upd
            heads.append(o)
        outs[n] = jnp.concatenate(heads, axis=1)
    o_dir = jnp.concatenate(outs, axis=0)
    if not reverse:
        o_ref[bb] = o_dir
        return
    o = of_ref[bb] + o_dir
    g = x_ref[bb, :, 3 * nh * hw:4 * nh * hw]
    normed = []
    for h in range(nh):
        oh = o[:, h * hw:(h + 1) * hw]
        normed.append(oh * lax.rsqrt(jnp.mean(oh * oh, axis=-1, keepdims=True) + EPS))
    o = jnp.concatenate(normed, axis=1) * ng_ref[...]
    o_ref[bb] = o * _silu(g)


def _gla(gla_in, wla, bla, norm_g):
    bsz, seq, width = gla_in.shape
    tile = GLA_TILE
    nc = seq // tile
    nb = min(GLA_BATCH, bsz)
    const = lambda shape: pl.BlockSpec(shape, lambda b, c: (0,) * len(shape))
    out_shape = jax.ShapeDtypeStruct((bsz, seq, MIX_WIDTH), F32)
    scratch = [pltpu.VMEM((nb, GLA_HEADS, GLA_DV, V7X_LANES), F32)]
    o_f = pl.pallas_call(
        functools.partial(_gla_kernel, reverse=False),
        grid=(bsz // nb, nc),
        in_specs=[pl.BlockSpec((nb, tile, width), lambda b, c: (b, c, 0)), const(wla.shape[1:]), const(bla.shape[1:])],
        out_specs=pl.BlockSpec((nb, tile, MIX_WIDTH), lambda b, c: (b, c, 0)),
        out_shape=out_shape,
        scratch_shapes=scratch,
        compiler_params=_cparams(("parallel", "arbitrary"), 48),
        name="gla_fwd",
    )(gla_in, wla[0], bla[0])
    return pl.pallas_call(
        functools.partial(_gla_kernel, reverse=True),
        grid=(bsz // nb, nc),
        in_specs=[pl.BlockSpec((nb, tile, width), lambda b, c: (b, nc - 1 - c, 0)),
                  pl.BlockSpec((nb, tile, MIX_WIDTH), lambda b, c: (b, nc - 1 - c, 0)),
                  const(wla.shape[1:]), const(bla.shape[1:]), const(norm_g.shape)],
        out_specs=pl.BlockSpec((nb, tile, MIX_WIDTH), lambda b, c: (b, nc - 1 - c, 0)),
        out_shape=out_shape,
        scratch_shapes=scratch,
        compiler_params=_cparams(("parallel", "arbitrary"), 48),
        name="gla_bwd",
    )(gla_in, o_f, wla[1], bla[1], norm_g)


def _s5_kernel(u_ref, t_ref, e_ref, o_ref_w, mu_ref, mup_ref, y_ref):
    ch = S5_CHUNK
    sub = V7X_SUBLANES
    nrow = u_ref.shape[1] // ch
    half = S5_LANE_GROUPS * 2 * S5_STATE
    swap = lambda x: pltpu.roll(x, half // 2, 1)
    u = jnp.concatenate([u_ref[0, pl.ds(i, nrow, stride=ch), :] for i in range(ch)], axis=1).astype(BF16)
    y = _dot(u, t_ref[0])
    he = _dot(u, e_ref[0])
    row = lax.broadcasted_iota(jnp.int32, (nrow, half), 0)
    rsub = row % sub
    states = []
    for d in range(2):
        h = he[:, d * half:(d + 1) * half]
        for s in range(S5_SCAN_STEPS):
            step = 2 ** s
            if d == 0:
                hs = jnp.where(rsub >= step, pltpu.roll(h, step, 0), 0.0)
            else:
                hs = jnp.where(rsub < sub - step, pltpu.roll(h, nrow - step, 0), 0.0)
            h = h + hs * mu_ref[0, d, s, 0:1, :] + swap(hs) * mu_ref[0, d, s, 1:2, :]
        ngroup = nrow // sub
        carry = jnp.zeros((1, half), F32)
        out = [None] * ngroup
        for g in (range(ngroup) if d == 0 else range(ngroup - 1, -1, -1)):
            hg = h[g * sub:(g + 1) * sub, :] + carry * mup_ref[0, d, 0] + swap(carry) * mup_ref[0, d, 1]
            carry = hg[sub - 1:sub, :] if d == 0 else hg[0:1, :]
            out[g] = hg
        h = jnp.concatenate(out, axis=0)
        if d == 0:
            h = jnp.where(row >= 1, pltpu.roll(h, 1, 0), 0.0)
        else:
            h = jnp.where(row < nrow - 1, pltpu.roll(h, nrow - 1, 0), 0.0)
        states.append(h)
    hp = jnp.concatenate(states, axis=1).astype(BF16)
    y = y + lax.dot_general(hp, o_ref_w[0], (((1,), (1,)), ((), ())), preferred_element_type=F32)
    for j in range(ch):
        y_ref[0, pl.ds(j, nrow, stride=ch), :] = y[:, j * V7X_LANES:(j + 1) * V7X_LANES]


def _s5(s5_in, tblk, eblk, oblk, mu, mup):
    bsz, seq, width = s5_in.shape
    nb = width // V7X_LANES
    once = pl.Buffered(1)
    wspec = lambda arr: pl.BlockSpec((1,) + arr.shape[1:], lambda k, b: (k,) + (0,) * (arr.ndim - 1),
                                     pipeline_mode=once)
    xspec = pl.BlockSpec((1, seq, V7X_LANES), lambda k, b: (b, 0, k), pipeline_mode=once)
    return pl.pallas_call(
        _s5_kernel,
        grid=(nb, bsz),
        in_specs=[xspec, wspec(tblk), wspec(eblk), wspec(oblk), wspec(mu), wspec(mup)],
        out_specs=pl.BlockSpec((1, seq, V7X_LANES), lambda k, b: (b, 0, k)),
        out_shape=jax.ShapeDtypeStruct(s5_in.shape, F32),
        compiler_params=_cparams(("arbitrary", "arbitrary"), 60),
        name="s5",
    )(s5_in, tblk, eblk, oblk, mu, mup)


def _s5_tables(lam_re, lam_im, log_dt, b_re, b_im, c_re, c_im):
    ch = S5_CHUNK
    hi = lax.Precision.HIGHEST
    cmul = lambda x, y: (x[0] * y[0] - x[1] * y[1], x[0] * y[1] + x[1] * y[0])
    lr_, li_ = lam_re.astype(F32), lam_im.astype(F32)
    dt = jnp.exp(log_dt.astype(F32))[..., None]
    ar, ai = lr_ * dt, li_ * dt

    def lam_pow(tau):
        t = jnp.asarray(tau, F32)
        t = t.reshape(t.shape + (1,) * 3)
        mag = jnp.exp(t * ar)
        return mag * jnp.cos(t * ai), mag * jnp.sin(t * ai)

    lb = lam_pow(jnp.ones((), F32))
    num = (lb[0] - 1.0, lb[1])
    den = lr_ * lr_ + li_ * li_
    ratio = ((num[0] * lr_ + num[1] * li_) / den, (num[1] * lr_ - num[0] * li_) / den)
    b_bar = cmul((ratio[0][..., None], ratio[1][..., None]), (b_re.astype(F32), b_im.astype(F32)))
    cc = (c_re.astype(F32), c_im.astype(F32))

    nb, gb, hh, pp = S5_GROUPS // S5_LANE_GROUPS, S5_LANE_GROUPS, S5_GROUP, S5_STATE
    eye = jnp.eye(gb, dtype=F32)

    def b_base(x):
        x = jnp.transpose(x.reshape(2, nb, gb, pp, hh), (1, 2, 4, 0, 3))
        return (x[:, :, :, :, None, :] * eye[None, :, None, None, :, None]).reshape(nb, gb * hh, 2 * gb * pp)

    def c_base(x):
        x = jnp.transpose(x.reshape(2, nb, gb, hh, pp), (1, 2, 3, 0, 4))
        return (x[:, :, :, :, None, :] * eye[None, :, None, None, :, None]).reshape(nb, gb * hh, 2 * gb * pp)

    row = lambda x: jnp.transpose(x.reshape(2, nb, gb * pp), (1, 0, 2)).reshape(nb, 1, 2 * gb * pp)
    base = (b_base(b_bar[0]), b_base(b_bar[1]), c_base(cc[0]), c_base(cc[1]), row(ar), row(ai))
    dtab = _s5_lag_tables(*base)
    return _s5_block_tables(dtab, *base)


def _lam_pow(tau, ar, ai):
    mag = jnp.exp(tau * ar)
    return mag * jnp.cos(tau * ai), mag * jnp.sin(tau * ai)


def _s5_lag_kernel(btr_ref, bti_ref, ctr_ref, cti_ref, ar_ref, ai_ref, d_ref):
    ch = S5_CHUNK
    half = btr_ref.shape[-1] // 2
    tdot = lambda a, b: lax.dot_general(a, b, (((1,), (1,)), ((), ())), preferred_element_type=F32,
                                        precision=lax.Precision.HIGHEST)
    for d in range(2):
        ls = slice(d * half, (d + 1) * half)
        btr, bti, ctr, cti = btr_ref[0, :, ls], bti_ref[0, :, ls], ctr_ref[0, :, ls], cti_ref[0, :, ls]
        for lag in range(ch):
            lr, li = _lam_pow(float(lag), ar_ref[0, :, ls], ai_ref[0, :, ls])
            val = tdot(btr * lr - bti * li, ctr) - tdot(btr * li + bti * lr, cti)
            idx = ch - 1 + lag if d == 0 else ch - 1 - lag
            if d == 1 and lag == 0:
                d_ref[0, idx] = d_ref[0, idx] + val
            else:
                d_ref[0, idx] = val


def _s5_lag_tables(btr, bti, ctr, cti, ar, ai):
    nb = btr.shape[0]
    spec = lambda a: pl.BlockSpec((1,) + a.shape[1:], lambda k: (k, 0, 0))
    nlag = 2 * S5_CHUNK - 1
    return pl.pallas_call(
        _s5_lag_kernel,
        grid=(nb,),
        in_specs=[spec(a) for a in (btr, bti, ctr, cti, ar, ai)],
        out_specs=pl.BlockSpec((1, nlag, V7X_LANES, V7X_LANES), lambda k: (k, 0, 0, 0)),
        out_shape=jax.ShapeDtypeStruct((nb, nlag, V7X_LANES, V7X_LANES), F32),
        compiler_params=_cparams(("parallel",), 32),
        name="s5_lag_tables",
    )(btr, bti, ctr, cti, ar, ai)


def _s5_block_kernel(d_ref, btr_ref, bti_ref, ctr_ref, cti_ref, ar_ref, ai_ref, t_ref, e_ref, ot_ref, mu_ref,
                     mup_ref):
    ch = S5_CHUNK
    i = pl.program_id(1)
    half = btr_ref.shape[-1] // 2
    fi = i.astype(F32)
    for j in range(ch):
        t_ref[0, :, j * V7X_LANES:(j + 1) * V7X_LANES] = d_ref[0, j - i + ch - 1].astype(BF16)
    for d in range(2):
        ls = slice(d * half, (d + 1) * half)
        ar, ai = ar_ref[0, :, ls], ai_ref[0, :, ls]
        lr, li = _lam_pow(fi if d == 1 else (ch - 1.0) - fi, ar, ai)
        btr, bti = btr_ref[0, :, ls], bti_ref[0, :, ls]
        e_ref[0, :, 2 * d * half:(2 * d + 1) * half] = (btr * lr - bti * li).astype(BF16)
        e_ref[0, :, (2 * d + 1) * half:(2 * d + 2) * half] = (btr * li + bti * lr).astype(BF16)
        lr, li = _lam_pow(fi + 1.0 if d == 0 else ch - fi, ar, ai)
        ctr, cti = ctr_ref[0, :, ls], cti_ref[0, :, ls]
        ot_ref[0, :, 2 * d * half:(2 * d + 1) * half] = (ctr * lr - cti * li).astype(BF16)
        ot_ref[0, :, (2 * d + 1) * half:(2 * d + 2) * half] = (-(ctr * li + cti * lr)).astype(BF16)

    @pl.when(i == 0)
    def _():
        for d in range(2):
            ar, ai = ar_ref[0, :, d * half:(d + 1) * half], ai_ref[0, :, d * half:(d + 1) * half]
            for s in range(S5_SCAN_STEPS):
                lr, li = _lam_pow(float(ch * 2 ** s), ar, ai)
                mu_ref[0, d, s, 0:1, :] = jnp.concatenate([lr, lr], axis=1)
                mu_ref[0, d, s, 1:2, :] = jnp.concatenate([-li, li], axis=1)
            for r in range(V7X_SUBLANES):
                lr, li = _lam_pow(float(ch * (r + 1 if d == 0 else V7X_SUBLANES - r)), ar, ai)
                mup_ref[0, d, 0, r:r + 1, :] = jnp.concatenate([lr, lr], axis=1)
                mup_ref[0, d, 1, r:r + 1, :] = jnp.concatenate([-li, li], axis=1)


def _s5_block_tables(dtab, btr, bti, ctr, cti, ar, ai):
    nb = btr.shape[0]
    ch = S5_CHUNK
    big = ch * V7X_LANES
    wide = 2 * btr.shape[-1]
    spec = lambda a: pl.BlockSpec((1,) + a.shape[1:], lambda k, i: (k,) + (0,) * (a.ndim - 1))
    tile = lambda w: pl.BlockSpec((1, V7X_LANES, w), lambda k, i: (k, i, 0))
    return pl.pallas_call(
        _s5_block_kernel,
        grid=(nb, ch),
        in_specs=[spec(a) for a in (dtab, btr, bti, ctr, cti, ar, ai)],
        out_specs=[tile(big), tile(wide), tile(wide),
                   pl.BlockSpec((1, 2, S5_SCAN_STEPS, 2, wide // 2), lambda k, i: (k, 0, 0, 0, 0)),
                   pl.BlockSpec((1, 2, 2, V7X_SUBLANES, wide // 2), lambda k, i: (k, 0, 0, 0, 0))],
        out_shape=[jax.ShapeDtypeStruct((nb, big, big), BF16), jax.ShapeDtypeStruct((nb, big, wide), BF16),
                   jax.ShapeDtypeStruct((nb, big, wide), BF16),
                   jax.ShapeDtypeStruct((nb, 2, S5_SCAN_STEPS, 2, wide // 2), F32),
                   jax.ShapeDtypeStruct((nb, 2, 2, V7X_SUBLANES, wide // 2), F32)],
        compiler_params=_cparams(("parallel", "arbitrary"), 32),
        name="s5_block_tables",
    )(dtab, btr, bti, ctr, cti, ar, ai)


def _merge_kernel(x_ref, ng_ref, yaf_ref, yab_ref, yb_ref, yc_ref, y5_ref, u5_ref, d5_ref, gluw_ref, glub_ref,
                  wgate_ref, bgate_ref, wbr_ref, wout_ref, fng_ref, wr_ref, br_ref, o_ref, g_ref):
    x = x_ref[...]
    h = _rms(x, ng_ref[...]).astype(BF16)
    y_d = _gelu_tanh(u5_ref[...] * d5_ref[...] + y5_ref[...])
    y_d = y_d * _sigmoid(_dot(y_d.astype(BF16), gluw_ref[...]) + glub_ref[...])
    branches = (yaf_ref[...] + yab_ref[...], yb_ref[...], yc_ref[...], y_d)
    merged = jnp.zeros(x.shape, F32)
    for i, y in enumerate(branches):
        gate = _sigmoid(_dot(h, wgate_ref[i]) + bgate_ref[i])
        merged = merged + gate * _dot(y.astype(BF16), wbr_ref[i])
    x_new = x + _dot(merged.astype(BF16), wout_ref[...])
    o_ref[...] = x_new
    g_ref[...] = _top_group(_router_logits(x_new, fng_ref, wr_ref, br_ref)[1])


def _merge(xf, lw, yaf, yab, yb, yc, y5, u5):
    t = xf.shape[0]
    tok = lambda w: pl.BlockSpec((TOK_TILE, w), lambda i: (i, 0))
    const = lambda arr: pl.BlockSpec(arr.shape, lambda i: (0,) * arr.ndim, pipeline_mode=pl.Buffered(1))
    weights = [lw["s5_d"], lw["s5_glu_w"], lw["s5_glu_b"], lw["w_gate"], lw["b_gate"], lw["w_branch"], lw["w_out"],
               lw["norm_ffn_g"], lw["w_router"], lw["b_router"]]
    return pl.pallas_call(
        _merge_kernel,
        grid=(t // TOK_TILE,),
        in_specs=[tok(D_MODEL), const(lw["norm_mix_g"])] + [tok(MIX_WIDTH)] * 6 + [const(w) for w in weights],
        out_specs=[tok(D_MODEL), tok(1)],
        out_shape=[jax.ShapeDtypeStruct((t, D_MODEL), F32), jax.ShapeDtypeStruct((t, 1), jnp.int32)],
        compiler_params=_cparams(("parallel",), 48),
        name="merge",
    )(xf, lw["norm_mix_g"], yaf, yab, yb, yc, y5, u5, *weights)


def _router_logits(x, ng_ref, wr_ref, br_ref):
    hf = _rms(x, ng_ref[...])
    return hf, _dot_hi(hf, wr_ref[...]) + br_ref[...]


def _top_group(logits):
    lane = lax.broadcasted_iota(jnp.int32, logits.shape, 1).astype(F32)
    gl = jnp.where(lane < MOE_GROUPS, logits, -jnp.inf)
    gmax = jnp.max(gl, axis=1, keepdims=True)
    return jnp.min(jnp.where(gl == gmax, lane, float(V7X_LANES)), axis=1, keepdims=True).astype(jnp.int32)


def _moe_plan_kernel(pos_ref, src_ref):
    def clear(p, c):
        src_ref[p] = 0
        return c

    def place(t, c):
        src_ref[pos_ref[t]] = t
        return c

    lax.fori_loop(0, src_ref.shape[0], clear, 0, unroll=16)
    lax.fori_loop(0, pos_ref.shape[0], place, 0, unroll=16)


def _moe_plan(gidx, tile):
    t = gidx.shape[0]
    ntile = t // tile + MOE_GROUPS
    g = gidx.reshape(t)
    onehot = (g[:, None] == jnp.arange(MOE_GROUPS, dtype=jnp.int32)[None, :]).astype(jnp.int32)
    csum = jnp.cumsum(onehot, axis=0)
    rank = jnp.sum(onehot * (csum - 1), axis=1)
    count = csum[-1]
    gtiles = (count + tile - 1) // tile
    first = jnp.cumsum(gtiles) - gtiles
    pos = jnp.sum(onehot * first[None, :], axis=1) * tile + rank
    tid = jnp.arange(ntile, dtype=jnp.int32)
    tgroup = jnp.minimum(jnp.sum((tid[:, None] >= (first + gtiles)[None, :]).astype(jnp.int32), axis=1),
                         MOE_GROUPS - 1)
    oh_t = (tgroup[:, None] == jnp.arange(MOE_GROUPS, dtype=jnp.int32)[None, :]).astype(jnp.int32)
    tvalid = jnp.clip(jnp.sum(oh_t * count[None, :], axis=1) - (tid - jnp.sum(oh_t * first[None, :], axis=1)) * tile,
                      0, tile)
    src = pl.pallas_call(
        _moe_plan_kernel,
        in_specs=[pl.BlockSpec(memory_space=pltpu.SMEM)],
        out_specs=pl.BlockSpec(memory_space=pltpu.SMEM),
        out_shape=jax.ShapeDtypeStruct((ntile * tile,), jnp.int32),
        name="moe_plan",
    )(pos.astype(jnp.int32))
    return src, tgroup.astype(jnp.int32), tvalid.astype(jnp.int32)


def _moe_expert_kernel(src_ref, tg_ref, nv_ref, x_hbm, ng_ref, wr_ref, br_ref, w13_ref, w2_ref, fg_ref, o_hbm,
                       xbuf, ybuf, sem, *, final):
    i = pl.program_id(0)
    nt = pl.num_programs(0)
    tile = xbuf.shape[1]
    slot = i % 2
    nv = nv_ref[i]
    ng, ne, ff = MOE_GROUPS, MOE_EXPERTS, MOE_FF

    def rows(j, wait, row, whole):
        count = nv_ref[j]

        @pl.when(count == tile)
        def _():
            if wait:
                whole().wait()
            else:
                def body(r, c):
                    row(j * tile + r, r).start()
                    return c
                lax.fori_loop(0, tile, body, 0, unroll=8)

        @pl.when((count > 0) & (count < tile))
        def _():
            def body(r, c):
                cp = row(j * tile + r, r)
                cp.wait() if wait else cp.start()
                return c
            lax.fori_loop(0, count, body, 0)

    def gather(j, s, wait):
        rows(j, wait,
             lambda p, r: pltpu.make_async_copy(x_hbm.at[pl.ds(src_ref[p], 1), :], xbuf.at[s, pl.ds(r, 1), :],
                                                sem.at[0, s]),
             lambda: pltpu.make_async_copy(x_hbm.at[pl.ds(0, tile), :], xbuf.at[s], sem.at[0, s]))

    def scatter(j, s, wait):
        rows(j, wait,
             lambda p, r: pltpu.make_async_copy(ybuf.at[s, pl.ds(r, 1), :], o_hbm.at[pl.ds(src_ref[p], 1), :],
                                                sem.at[1, s]),
             lambda: pltpu.make_async_copy(ybuf.at[s], o_hbm.at[pl.ds(0, tile), :], sem.at[1, s]))

    @pl.when(i == 0)
    def _():
        xbuf[...] = jnp.zeros_like(xbuf)
        gather(0, 0, False)

    @pl.when(i + 1 < nt)
    def _():
        gather(i + 1, 1 - slot, False)

    @pl.when(i >= 2)
    def _():
        scatter(i - 2, slot, True)

    gather(i, slot, True)

    @pl.when(nv > 0)
    def _():
        x = xbuf[slot]
        hf, logits = _router_logits(x, ng_ref, wr_ref, br_ref)
        lane = lax.broadcasted_iota(jnp.int32, logits.shape, 1).astype(F32)
        neg = -jnp.inf
        big = float(V7X_LANES)
        grp = tg_ref[i].astype(F32)
        gl = jnp.where(lane < ng, logits, neg)
        gmax = jnp.max(gl, axis=1, keepdims=True)
        glog = jnp.sum(jnp.where(lane == grp, logits, 0.0), axis=1, keepdims=True)
        gprob = jnp.exp(glog - gmax) / jnp.sum(jnp.exp(gl - gmax), axis=1, keepdims=True)
        lo = ng + ne * grp
        sel = (lane >= lo) & (lane < lo + ne)
        m1 = jnp.max(jnp.where(sel, logits, neg), axis=1, keepdims=True)
        i1 = jnp.min(jnp.where(sel & (logits == m1), lane, big), axis=1, keepdims=True)
        sel2 = sel & (lane != i1)
        m2 = jnp.max(jnp.where(sel2, logits, neg), axis=1, keepdims=True)
        i2 = jnp.min(jnp.where(sel2 & (logits == m2), lane, big), axis=1, keepdims=True)
        e2 = jnp.exp(m2 - m1)
        w1 = gprob / (1.0 + e2)
        w2 = gprob * e2 / (1.0 + e2)
        gu = _dot(hf.astype(BF16), w13_ref[0])
        act = _silu(gu[:, 0:ne * ff]) * gu[:, ne * ff:2 * ne * ff]
        parts = []
        for e in range(ne):
            wcol = jnp.where(i1 == lo + e, w1, 0.0) + jnp.where(i2 == lo + e, w2, 0.0)
            parts.append(act[:, e * ff:(e + 1) * ff] * wcol)
        y = x + _dot(jnp.concatenate(parts, axis=1).astype(BF16), w2_ref[0])
        if final:
            y = _rms(y, fg_ref[...])
        ybuf[slot] = y
        scatter(i, slot, False)

    @pl.when(i == nt - 1)
    def _():
        @pl.when(i >= 1)
        def _():
            scatter(i - 1, 1 - slot, True)
        scatter(i, slot, True)


def _moe(xf, gidx, lw, final_g, final):
    t = xf.shape[0]
    tile = TOK_TILE
    src, tgroup, tvalid = _moe_plan(gidx, tile)
    ntile = tgroup.shape[0]
    const = lambda arr: pl.BlockSpec(arr.shape, lambda i, s, g, n: (0,) * arr.ndim)
    bygroup = lambda arr: pl.BlockSpec((1,) + arr.shape[1:], lambda i, s, g, n: (g[i],) + (0,) * (arr.ndim - 1))
    any_space = pl.BlockSpec(memory_space=pl.ANY)
    return pl.pallas_call(
        functools.partial(_moe_expert_kernel, final=final),
        grid_spec=pltpu.PrefetchScalarGridSpec(
            num_scalar_prefetch=3,
            grid=(ntile,),
            in_specs=[any_space, const(lw["norm_ffn_g"]), const(lw["w_router"]), const(lw["b_router"]),
                      bygroup(lw["w_e_13"]), bygroup(lw["w_e_2"]), const(final_g)],
            out_specs=any_space,
            scratch_shapes=[pltpu.VMEM((2, tile, D_MODEL), F32), pltpu.VMEM((2, tile, D_MODEL), F32),
                            pltpu.SemaphoreType.DMA((2, 2))],
        ),
        out_shape=jax.ShapeDtypeStruct((t, D_MODEL), F32),
        compiler_params=_cparams(("arbitrary",), 40),
        name="moe_experts",
    )(src, tgroup, tvalid, xf, lw["norm_ffn_g"], lw["w_router"], lw["b_router"], lw["w_e_13"], lw["w_e_2"], final_g)


def _block_diag(blocks):
    nb, bs, _ = blocks.shape
    eye = jnp.eye(nb, dtype=blocks.dtype)
    return jnp.einsum("nij,nm->nimj", blocks, eye).reshape(nb * bs, nb * bs)


def _pad_heads(w, axis=-1):
    shape = w.shape[:-1] + (GLA_HEADS, GLA_DK)
    w = w.reshape(shape)
    pad = [(0, 0)] * (w.ndim - 1) + [(0, V7X_LANES - GLA_DK)]
    return jnp.pad(w, pad).reshape(w.shape[:-2] + (GLA_HEADS * V7X_LANES,))


def _prep_layer(w, l):
    f = lambda name: w[name][l]
    w_in = f("w_in")
    cuts = np.cumsum([MIX_WIDTH, MIX_WIDTH, 3 * MIX_WIDTH, GLA_HEADS * GLA_DK, GLA_HEADS * GLA_DK, MIX_WIDTH,
                      MIX_WIDTH, 2 * GLA_RANK]).tolist()
    xa, ga, hy, q, k, v, g, lr, s5 = jnp.split(w_in, cuts, axis=-1)
    lr = jnp.pad(lr, ((0, 0), (0, V7X_LANES - 2 * GLA_RANK)))
    w_pack = jnp.concatenate([xa, ga, hy, _pad_heads(q), _pad_heads(k), v, g, lr, s5], axis=-1).astype(BF16)
    assert w_pack.shape[1] == N_PACK
    lw = {"w_pack": w_pack, "norm_mix_g": f("norm_mix_g")[None]}
    lw["lru_conv_w"] = f("lru_conv_w")
    lw["lru_conv_b"] = f("lru_conv_b")[None]
    wa, wx = f("lru_wa"), f("lru_wx")
    lw["lru_wg"] = jnp.stack([jnp.concatenate([_block_diag(wa[d]), _block_diag(wx[d])], axis=1)
                              for d in range(2)]).astype(BF16)
    lw["lru_bg"] = jnp.concatenate([f("lru_ba"), f("lru_bx")], axis=-1)[:, None, :]
    lw["lru_lam"] = f("lru_lambda")[:, None, :]
    lw["hy_conv_w"] = f("hy_conv_w")
    lw["hy_conv_b"] = f("hy_conv_b")[None]
    lw["hy_w1p"] = jnp.pad(f("hy_w1"), ((0, V7X_LANES - HY_EMB), (0, 0)))
    lw["hy_b1"] = f("hy_b1")[None]
    lw["hy_w2"] = f("hy_w2")
    lw["hy_b2"] = f("hy_b2")[None]
    lw["hy_w3"] = f("hy_w3")
    lw["hy_freq"] = f("hy_freq")[None]
    lw["hy_bias"] = f("hy_bias")
    wg2 = _pad_heads(f("gla_wg2"))
    wla = jnp.zeros((2, V7X_LANES, GLA_HEADS * V7X_LANES), F32)
    wla = wla.at[0, 0:GLA_RANK].set(wg2[0]).at[1, GLA_RANK:2 * GLA_RANK].set(wg2[1])
    lw["gla_wla"] = wla.astype(BF16)
    lw["gla_bla"] = _pad_heads(f("gla_bg"))[:, None, :]
    lw["gla_norm_g"] = jnp.tile(f("gla_norm_g"), GLA_HEADS)[None]
    lw["s5_tables"] = _s5_tables(f("s5_lam_re"), f("s5_lam_im"), f("s5_log_dt"), f("s5_b_re"), f("s5_b_im"),
                                 f("s5_c_re"), f("s5_c_im"))
    lw["s5_d"] = f("s5_d")[None]
    lw["s5_glu_w"] = f("s5_glu_w").astype(BF16)
    lw["s5_glu_b"] = f("s5_glu_b")[None]
    lw["w_gate"] = f("w_gate").astype(BF16)
    lw["b_gate"] = f("b_gate")[:, None, :]
    lw["w_branch"] = f("w_branch").astype(BF16)
    lw["w_out"] = f("w_out").astype(BF16)
    wr = jnp.concatenate([f("w_router_group"), jnp.transpose(f("w_router_expert"), (1, 0, 2)).reshape(D_MODEL, -1)],
                         axis=1)
    br = jnp.concatenate([f("b_router_group"), f("b_router_expert").reshape(-1)])
    nr = MOE_GROUPS + MOE_GROUPS * MOE_EXPERTS
    lw["w_router"] = jnp.pad(wr, ((0, 0), (0, V7X_LANES - nr)))
    lw["b_router"] = jnp.pad(br, (0, V7X_LANES - nr))[None]
    lw["norm_ffn_g"] = f("norm_ffn_g")[None]
    wide = lambda a: jnp.transpose(a, (0, 2, 1, 3)).reshape(MOE_GROUPS, D_MODEL, MOE_EXPERTS * MOE_FF)
    lw["w_e_13"] = jnp.concatenate([wide(f("w_e_gate")), wide(f("w_e_up"))], axis=-1).astype(BF16)
    lw["w_e_2"] = f("w_e_down").reshape(MOE_GROUPS, MOE_EXPERTS * MOE_FF, D_MODEL).astype(BF16)
    return lw


def _encoder(x, layers, final_g):
    bsz, seq, _ = x.shape
    plan = _FftPlan(seq)
    xf = x.reshape(bsz * seq, D_MODEL)
    for l, lw in enumerate(layers):
        lru_in, hy_in, gla_in, s5_in = _inproj(xf, lw["norm_mix_g"], lw["w_pack"])
        shp = lambda a: a.reshape(bsz, seq, a.shape[-1])
        yaf, yab = _lru(shp(lru_in), lw["lru_conv_w"], lw["lru_conv_b"], lw["lru_wg"], lw["lru_bg"], lw["lru_lam"])
        yb = _hyena(shp(hy_in), lw, plan)
        yc = _gla(shp(gla_in), lw["gla_wla"], lw["gla_bla"], lw["gla_norm_g"])
        y5 = _s5(shp(s5_in), *lw["s5_tables"])
        flat = lambda a: a.reshape(bsz * seq, MIX_WIDTH)
        xf, gidx = _merge(xf, lw, flat(yaf), flat(yab), flat(yb), flat(yc), flat(y5), s5_in)
        xf = _moe(xf, gidx, lw, final_g, final=(l == len(layers) - 1))
    return xf.reshape(bsz, seq, D_MODEL)


def kernel(x_prompt, x_sample, norm_mix_g, w_in, lru_conv_w, lru_conv_b, lru_wa, lru_ba, lru_wx, lru_bx,
           lru_lambda, hy_conv_w, hy_conv_b, hy_w1, hy_b1, hy_w2, hy_b2, hy_w3, hy_freq, hy_bias,
           gla_wg2, gla_bg, gla_norm_g, s5_lam_re, s5_lam_im, s5_log_dt, s5_b_re, s5_b_im, s5_c_re, s5_c_im,
           s5_d, s5_glu_w, s5_glu_b, w_branch, w_gate, b_gate, w_out, norm_ffn_g, w_router_group,
           b_router_group, w_router_expert, b_router_expert, w_e_gate, w_e_up, w_e_down, final_norm_g):
    w = dict(norm_mix_g=norm_mix_g, w_in=w_in, lru_conv_w=lru_conv_w, lru_conv_b=lru_conv_b, lru_wa=lru_wa,
             lru_ba=lru_ba, lru_wx=lru_wx, lru_bx=lru_bx, lru_lambda=lru_lambda, hy_conv_w=hy_conv_w,
             hy_conv_b=hy_conv_b, hy_w1=hy_w1, hy_b1=hy_b1, hy_w2=hy_w2, hy_b2=hy_b2, hy_w3=hy_w3,
             hy_freq=hy_freq, hy_bias=hy_bias, gla_wg2=gla_wg2, gla_bg=gla_bg, gla_norm_g=gla_norm_g,
             s5_lam_re=s5_lam_re, s5_lam_im=s5_lam_im, s5_log_dt=s5_log_dt, s5_b_re=s5_b_re, s5_b_im=s5_b_im,
             s5_c_re=s5_c_re, s5_c_im=s5_c_im, s5_d=s5_d, s5_glu_w=s5_glu_w, s5_glu_b=s5_glu_b,
             w_branch=w_branch, w_gate=w_gate, b_gate=b_gate, w_out=w_out, norm_ffn_g=norm_ffn_g,
             w_router_group=w_router_group, b_router_group=b_router_group, w_router_expert=w_router_expert,
             b_router_expert=b_router_expert, w_e_gate=w_e_gate, w_e_up=w_e_up, w_e_down=w_e_down)
    layers = [_prep_layer(w, l) for l in range(norm_mix_g.shape[0])]
    fg = final_norm_g[None]
    return (_encoder(x_prompt, layers, fg), _encoder(x_sample, layers, fg))
```

```python
import functools
import math

import numpy as np
import jax
import jax.numpy as jnp
from jax import lax
from jax.experimental import pallas as pl
from jax.experimental.pallas import tpu as pltpu

F32 = jnp.float32
BF16 = jnp.bfloat16

D_MODEL = 1024
DEPTH = 2
EPS = 1e-6
MIX_WIDTH = D_MODEL // 2
LRU_BLOCKS = 8
LRU_BLOCK = MIX_WIDTH // LRU_BLOCKS
LRU_CONV = 4
LRU_C = 8.0
HY_ORDER = 2
HY_CONV = 3
HY_EMB = 33
HY_BANDS = (HY_EMB - 1) // 2
HY_HIDDEN = 64
HY_DECAY_TARGET = 1e-2
HY_FAST_PCT = 0.3
HY_SLOW_PCT = 1.5
GLA_HEADS = 4
GLA_DK = MIX_WIDTH // 8
GLA_DV = MIX_WIDTH // GLA_HEADS
GLA_RANK = 16
GLA_TAU = 16.0
GLA_CHUNK = 64
S5_GROUP = 16
S5_GROUPS = MIX_WIDTH // S5_GROUP
S5_STATE = 64
MOE_GROUPS = 4
MOE_EXPERTS = 4
MOE_FF = D_MODEL // 4

V7X_LANES = 128
V7X_SUBLANES = 8
V7X_VMEM_BYTES = 64 * 2**20
MIB = 2**20

GLA_PACK = 4 * V7X_LANES * 2 + 512 + 512 + V7X_LANES
PK_LRU = (0, 1024)
PK_HY = (1024, 2560)
PK_GLA = (2560, 2560 + GLA_PACK)
PK_S5 = (PK_GLA[1], PK_GLA[1] + 512)
N_PACK = PK_S5[1]

TOK_TILE = 256
LRU_TILE = 256
GLA_TILE = 256
GLA_BATCH = 2
S5_CHUNK = 16
S5_LANE_GROUPS = V7X_LANES // S5_GROUP
S5_SCAN_STEPS = 3
FFT_N1 = 64
FFT_ROWS = 128
FFT_COLS = 256
FFT_KB = 11


def _cparams(sem, vmem_mib):
    return pltpu.CompilerParams(dimension_semantics=sem, vmem_limit_bytes=int(vmem_mib * MIB))


def _rms(x, g):
    return x * lax.rsqrt(jnp.mean(x * x, axis=-1, keepdims=True) + EPS) * g


def _sigmoid(x):
    return 1.0 / (1.0 + jnp.exp(-x))


def _softplus(x):
    return jnp.maximum(x, 0.0) + jnp.log(1.0 + jnp.exp(-jnp.abs(x)))


def _gelu_tanh(x):
    return 0.5 * x * (1.0 + jnp.tanh(math.sqrt(2.0 / math.pi) * (x + 0.044715 * (x * x * x))))


def _silu(x):
    return x * _sigmoid(x)


def _dot(a, b):
    return jnp.dot(a, b, preferred_element_type=F32)


def _split3(w):
    hi = w.astype(BF16)
    lo = (w - hi.astype(F32)).astype(BF16)
    return jnp.concatenate([hi, lo, hi], axis=0)


def _dot3(a, w3):
    hi = a.astype(BF16)
    lo = (a - hi.astype(F32)).astype(BF16)
    return _dot(jnp.concatenate([hi, hi, lo], axis=1), w3)


def _inproj_kernel(x_ref, xp_ref, xn_ref, g_ref, w_ref, cw_ref, cb_ref, lru_ref, hy_ref, gla_ref, s5_ref, ext_ref, *,
                   tiles_per_seq):
    i = pl.program_id(0)
    tile = TOK_TILE
    h = _rms(x_ref[...], g_ref[...]).astype(BF16)
    lru_ref[...] = _dot(h, w_ref[:, PK_LRU[0]:PK_LRU[1]])
    gla_ref[...] = _dot(h, w_ref[:, PK_GLA[0]:PK_GLA[1]])
    s5_ref[...] = _dot(h, w_ref[:, PK_S5[0]:PK_S5[1]])
    w_hy = w_ref[:, PK_HY[0]:PK_HY[1]]
    edge = lambda ref: _dot(_rms(ref[...], g_ref[...]).astype(BF16), w_hy)
    _fill_ext(ext_ref, _dot(h, w_hy), edge(xp_ref), edge(xn_ref), i % tiles_per_seq == 0,
              i % tiles_per_seq == tiles_per_seq - 1, tile)
    y = cb_ref[...] + ext_ref[7:7 + tile, :] * cw_ref[0:1, :]
    y = y + ext_ref[8:8 + tile, :] * cw_ref[1:2, :]
    y = y + ext_ref[9:9 + tile, :] * cw_ref[2:3, :]
    hy_ref[...] = y


def _inproj(xf, g, w_pack, hy_cw, hy_cb, seq):
    t = xf.shape[0]
    tile = TOK_TILE
    r8 = tile // 8
    last8 = t // 8 - 1
    widths = [PK_LRU[1] - PK_LRU[0], PK_HY[1] - PK_HY[0], PK_GLA[1] - PK_GLA[0], PK_S5[1] - PK_S5[0]]
    const = lambda arr: pl.BlockSpec(arr.shape, lambda i: (0,) * arr.ndim)
    return pl.pallas_call(
        functools.partial(_inproj_kernel, tiles_per_seq=seq // tile),
        grid=(t // tile,),
        in_specs=[pl.BlockSpec((tile, D_MODEL), lambda i: (i, 0)),
                  pl.BlockSpec((8, D_MODEL), lambda i: (jnp.maximum(i * r8 - 1, 0), 0)),
                  pl.BlockSpec((8, D_MODEL), lambda i: (jnp.minimum((i + 1) * r8, last8), 0)),
                  const(g), const(w_pack), const(hy_cw), const(hy_cb)],
        out_specs=[pl.BlockSpec((tile, w), lambda i: (i, 0)) for w in widths],
        out_shape=[jax.ShapeDtypeStruct((t, w), F32) for w in widths],
        scratch_shapes=[pltpu.VMEM((tile + 16, widths[1]), F32)],
        compiler_params=_cparams(("parallel",), 52),
        name="inproj",
    )(xf, xf, xf, g, w_pack, hy_cw, hy_cb)


def _fill_ext(ext_ref, main, prev8, next8, first, last, tile):
    ext_ref[0:8, :] = jnp.where(first, 0.0, prev8)
    ext_ref[8:8 + tile, :] = main
    ext_ref[8 + tile:16 + tile, :] = jnp.where(last, 0.0, next8)


def _linear_scan_tile(a, b, carry, reverse):
    n = a.shape[0]
    sub = V7X_SUBLANES
    row = lax.broadcasted_iota(jnp.int32, a.shape, 0) % sub
    d = 1
    while d < sub:
        if reverse:
            a_s = pltpu.roll(a, n - d, 0)
            b_s = pltpu.roll(b, n - d, 0)
            valid = row < sub - d
        else:
            a_s = pltpu.roll(a, d, 0)
            b_s = pltpu.roll(b, d, 0)
            valid = row >= d
        b = jnp.where(valid, a * b_s + b, b)
        a = jnp.where(valid, a * a_s, a)
        d *= 2
    ngroup = n // sub
    out = [None] * ngroup
    for g in (range(ngroup - 1, -1, -1) if reverse else range(ngroup)):
        h = b[g * sub:(g + 1) * sub, :] + a[g * sub:(g + 1) * sub, :] * carry
        carry = h[0:1, :] if reverse else h[sub - 1:sub, :]
        out[g] = h
    return jnp.concatenate(out, axis=0), carry


def _lru_kernel(mf_ref, pf_ref, nf_ref, mb_ref, pb_ref, nb_ref, cw_ref, cb_ref, wg_ref, bg_ref, lam_ref,
                of_ref, ob_ref, extf_ref, extb_ref, carry_ref):
    c = pl.program_id(1)
    nc = pl.num_programs(1)
    tile = LRU_TILE

    @pl.when(c == 0)
    def _():
        carry_ref[...] = jnp.zeros_like(carry_ref)

    def one(m_ref, p_ref, n_ref, ext_ref, d, first, last, o_ref):
        x = m_ref[0, :, 0:MIX_WIDTH]
        ga = m_ref[0, :, MIX_WIDTH:2 * MIX_WIDTH]
        _fill_ext(ext_ref, x, p_ref[0], n_ref[0], first, last, tile)
        xc = cb_ref[...] + ext_ref[6:6 + tile, :] * cw_ref[0:1, :]
        xc = xc + ext_ref[7:7 + tile, :] * cw_ref[1:2, :]
        xc = xc + ext_ref[8:8 + tile, :] * cw_ref[2:3, :]
        xc = xc + ext_ref[9:9 + tile, :] * cw_ref[3:4, :]
        z = _dot(xc.astype(BF16), wg_ref[d]) + bg_ref[d]
        gate_r = _sigmoid(z[:, 0:MIX_WIDTH])
        gate_i = _sigmoid(z[:, MIX_WIDTH:2 * MIX_WIDTH])
        log_a = -LRU_C * gate_r * _softplus(-lam_ref[d])
        a = jnp.exp(log_a)
        t = 1.0 - a * a
        b = jnp.where(t > 0.0, t * lax.rsqrt(t), 0.0) * gate_i * xc
        h, last = _linear_scan_tile(a, b, carry_ref[d:d + 1, :], reverse=(d == 1))
        carry_ref[d:d + 1, :] = last
        o_ref[0] = h * _gelu_tanh(ga)

    one(mf_ref, pf_ref, nf_ref, extf_ref, 0, c == 0, c == nc - 1, of_ref)
    one(mb_ref, pb_ref, nb_ref, extb_ref, 1, c == nc - 1, c == 0, ob_ref)


def _lru(lru_in, cw, cb, wg, bg, lam):
    bsz, seq, _ = lru_in.shape
    tile = LRU_TILE
    nc = seq // tile
    r8 = tile // 8
    last8 = seq // 8 - 1

    def fwd(c):
        return c

    def bwd(c):
        return nc - 1 - c

    def specs(ch):
        return [pl.BlockSpec((1, tile, 2 * MIX_WIDTH), lambda b, c: (b, ch(c), 0)),
                pl.BlockSpec((1, 8, MIX_WIDTH), lambda b, c: (b, jnp.maximum(ch(c) * r8 - 1, 0), 0)),
                pl.BlockSpec((1, 8, MIX_WIDTH), lambda b, c: (b, jnp.minimum((ch(c) + 1) * r8, last8), 0))]

    const = lambda shape: pl.BlockSpec(shape, lambda b, c: (0,) * len(shape))
    return pl.pallas_call(
        _lru_kernel,
        grid=(bsz, nc),
        in_specs=specs(fwd) + specs(bwd) + [const(cw.shape), const(cb.shape), const(wg.shape), const(bg.shape),
                                            const(lam.shape)],
        out_specs=[pl.BlockSpec((1, tile, MIX_WIDTH), lambda b, c: (b, c, 0)),
                   pl.BlockSpec((1, tile, MIX_WIDTH), lambda b, c: (b, nc - 1 - c, 0))],
        out_shape=[jax.ShapeDtypeStruct((bsz, seq, MIX_WIDTH), F32)] * 2,
        scratch_shapes=[pltpu.VMEM((tile + 16, MIX_WIDTH), F32), pltpu.VMEM((tile + 16, MIX_WIDTH), F32),
                        pltpu.VMEM((8, MIX_WIDTH), F32)],
        compiler_params=_cparams(("parallel", "arbitrary"), 40),
        name="lru",
    )(lru_in, lru_in, lru_in, lru_in, lru_in, lru_in, cw, cb, wg, bg, lam)


class _FftPlan:
    def __init__(self, seq):
        n = 2 * seq
        n1 = FFT_N1
        n2 = n // n1
        assert n1 * n2 == n and n2 % 16 == 0
        h1 = n1 // 2 + 1
        nh = n1 // 2
        self.n, self.n1, self.n2, self.h1, self.nh = n, n1, n2, h1, nh
        self.rows = min(FFT_ROWS, n2)
        assert n2 % self.rows == 0 and h1 % FFT_KB == 0
        k1 = np.arange(h1, dtype=np.float64)
        m1 = np.arange(nh, dtype=np.float64)
        r = np.arange(8, dtype=np.float64)
        ang = -2.0 * np.pi * (k1[:, None, None] * m1[None, None, :] / n1 + r[None, :, None] * k1[:, None, None] / n)
        e = np.exp(1j * ang)
        fa = np.zeros((h1, 8, nh, 8), np.complex128)
        for rr in range(8):
            fa[:, rr, :, rr] = e[:, rr, :]
        fa = fa.reshape(h1 * 8, nh * 8)
        self.fa = jnp.asarray(np.concatenate([fa.real, fa.imag], axis=0), BF16)
        ck = np.where((k1 == 0) | (k1 == n1 // 2), 1.0, 2.0)
        ec = np.conj(e) * ck[:, None, None] / n
        fc = np.zeros((nh, 8, 2, h1, 8), np.float64)
        for rr in range(8):
            fc[:, rr, 0, :, rr] = ec[:, rr, :].real.T
            fc[:, rr, 1, :, rr] = -ec[:, rr, :].imag.T
        self.fc = jnp.asarray(fc.reshape(nh * 8, 2 * h1 * 8), BF16)
        rg = np.arange(n2 // 8, dtype=np.float64)
        tw = np.exp(-2j * np.pi * 8.0 * rg[None, :] * k1[:, None] / n)
        self.tw = jnp.asarray(np.concatenate([tw.real, tw.imag], axis=0), F32)
        tw1 = np.exp(-2j * np.pi * k1 / n)
        self.tw1 = jnp.asarray(np.stack([tw1.real, tw1.imag]), F32)
        q = np.arange(n2, dtype=np.float64)
        f2 = np.exp(-2j * np.pi * np.outer(q, q) / n2)
        fr, fi = f2.real, f2.imag
        self.gb = jnp.asarray(np.block([[fr, -fi], [fi, fr]]), BF16)
        self.gbi = jnp.asarray(np.block([[fr, fi], [-fi, fr]]), BF16)


def _fft_a_kernel(tw_ref, x_ref, fa_ref, a_ref, *, h1, nh, rows):
    rb = pl.program_id(2)
    cols = x_ref.shape[-1]

    def stage(rg):
        xg = x_ref[0, :, pl.ds(pl.multiple_of(rg * 8, 8), 8), :].reshape(nh * 8, cols).astype(BF16)
        return _dot(fa_ref[...], xg)

    def body(i, carry):
        p0 = stage(2 * i)
        p1 = stage(2 * i + 1)
        g0 = rb * (rows // 8) + 2 * i
        for k in range(h1):
            outs = []
            for p, g in ((p0, g0), (p1, g0 + 1)):
                pr = p[k * 8:(k + 1) * 8, :]
                pi = p[(h1 + k) * 8:(h1 + k + 1) * 8, :]
                tr = tw_ref[k, g]
                ti = tw_ref[h1 + k, g]
                outs.append((pr * tr - pi * ti, pr * ti + pi * tr))
            dst = pl.ds(pl.multiple_of(i * 16, 16), 16)
            a_ref[0, k, 0, dst, :] = jnp.concatenate([outs[0][0], outs[1][0]], axis=0).astype(BF16)
            a_ref[0, k, 1, dst, :] = jnp.concatenate([outs[0][1], outs[1][1]], axis=0).astype(BF16)
        return carry

    lax.fori_loop(0, rows // 16, body, 0)


def _fft_a(x4, plan, col_off=0, ncols=None):
    bq, nh, n2, width = x4.shape
    ncols = width if ncols is None else ncols
    cb0 = col_off // FFT_COLS
    rows = plan.rows
    kern = functools.partial(_fft_a_kernel, h1=plan.h1, nh=nh, rows=rows)
    return pl.pallas_call(
        kern,
        grid=(bq, ncols // FFT_COLS, n2 // rows),
        in_specs=[pl.BlockSpec(memory_space=pltpu.SMEM),
                  pl.BlockSpec((1, nh, rows, FFT_COLS), lambda b, c, r: (b, 0, r, cb0 + c)),
                  pl.BlockSpec(plan.fa.shape, lambda b, c, r: (0, 0))],
        out_specs=pl.BlockSpec((1, plan.h1, 2, rows, FFT_COLS), lambda b, c, r: (b, 0, 0, r, c)),
        out_shape=jax.ShapeDtypeStruct((bq, plan.h1, 2, n2, ncols), BF16),
        compiler_params=_cparams(("parallel", "parallel", "parallel"), 40),
        name="fft_outer_fwd",
    )(plan.tw, x4, plan.fa)


def _fft_mid_kernel(a_ref, kf_ref, gb_ref, gbi_ref, o_ref, *, n2):
    kb = a_ref.shape[1]
    cols = a_ref.shape[-1]
    for k in range(kb):
        y = _dot(gb_ref[...], a_ref[0, k].reshape(2 * n2, cols))
        yr, yi = y[0:n2, :], y[n2:2 * n2, :]
        kr, ki = kf_ref[0, k, 0], kf_ref[0, k, 1]
        z = jnp.concatenate([yr * kr - yi * ki, yr * ki + yi * kr], axis=0).astype(BF16)
        o_ref[0, k] = _dot(gbi_ref[...], z).reshape(2, n2, cols).astype(BF16)


def _fft_mid(a, kf, order, plan):
    bq, h1, _, n2, width = a.shape
    kern = functools.partial(_fft_mid_kernel, n2=n2)
    blk = (1, FFT_KB, 2, n2, FFT_COLS)
    return pl.pallas_call(
        kern,
        grid=(bq, width // FFT_COLS, h1 // FFT_KB),
        in_specs=[pl.BlockSpec(blk, lambda b, c, k: (b, k, 0, 0, c)),
                  pl.BlockSpec(blk, lambda b, c, k: (order, k, 0, 0, c)),
                  pl.BlockSpec(plan.gb.shape, lambda b, c, k: (0, 0)),
                  pl.BlockSpec(plan.gbi.shape, lambda b, c, k: (0, 0))],
        out_specs=pl.BlockSpec(blk, lambda b, c, k: (b, k, 0, 0, c)),
        out_shape=jax.ShapeDtypeStruct(a.shape, BF16),
        compiler_params=_cparams(("parallel", "parallel", "parallel"), 40),
        name="fft_inner_mul",
    )(a, kf, plan.gb, plan.gbi)


def _fft_c_kernel(tw_ref, b_ref, u_ref, g_ref, bias_ref, fc_ref, o_ref, *, h1, nh, rows):
    rb = pl.program_id(2)
    cols = o_ref.shape[-1]

    def body(i, carry):
        src = pl.ds(pl.multiple_of(i * 16, 16), 16)
        tiles = [[b_ref[0, k, p, src, :].astype(F32) for p in range(2)] for k in range(h1)]
        for half in range(2):
            g = rb * (rows // 8) + 2 * i + half
            re_rows, im_rows = [], []
            for k in range(h1):
                br = tiles[k][0][half * 8:(half + 1) * 8, :]
                bi = tiles[k][1][half * 8:(half + 1) * 8, :]
                tr = tw_ref[k, g]
                ti = tw_ref[h1 + k, g]
                re_rows.append(br * tr + bi * ti)
                im_rows.append(bi * tr - br * ti)
            s = jnp.concatenate(re_rows + im_rows, axis=0).astype(BF16)
            y = _dot(fc_ref[...], s).reshape(nh, 8, cols)
            dst = pl.ds(pl.multiple_of((2 * i + half) * 8, 8), 8)
            u = u_ref[0, :, dst, :]
            o_ref[0, :, dst, :] = (y + u * bias_ref[...]) * g_ref[0, :, dst, :]
        return carry

    lax.fori_loop(0, rows // 16, body, 0)


def _fft_c(bm, u4, u_off, g4, g_off, bias, plan):
    bq, h1, _, n2, width = bm.shape
    nh = plan.nh
    rows = plan.rows
    ub, gbk = u_off // FFT_COLS, g_off // FFT_COLS
    kern = functools.partial(_fft_c_kernel, h1=h1, nh=nh, rows=rows)
    xblk = (1, nh, rows, FFT_COLS)
    return pl.pallas_call(
        kern,
        grid=(bq, width // FFT_COLS, n2 // rows),
        in_specs=[pl.BlockSpec(memory_space=pltpu.SMEM),
                  pl.BlockSpec((1, h1, 2, rows, FFT_COLS), lambda b, c, r: (b, 0, 0, r, c)),
                  pl.BlockSpec(xblk, lambda b, c, r: (b, 0, r, ub + c)),
                  pl.BlockSpec(xblk, lambda b, c, r: (b, 0, r, gbk + c)),
                  pl.BlockSpec((1, FFT_COLS), lambda b, c, r: (0, c)),
                  pl.BlockSpec(plan.fc.shape, lambda b, c, r: (0, 0))],
        out_specs=pl.BlockSpec(xblk, lambda b, c, r: (b, 0, r, c)),
        out_shape=jax.ShapeDtypeStruct((bq, nh, n2, width), F32),
        compiler_params=_cparams(("parallel", "parallel", "parallel"), 48),
        name="fft_outer_inv",
    )(plan.tw, bm, u4, g4, bias, plan.fc)


def _hyfilt_gen_kernel(w1_ref, b1_ref, w2_ref, b2_ref, w3_ref, fr_ref, o_ref, ss_ref, *, seq):
    rblk = pl.program_id(0)
    tile, cols = o_ref.shape[1], o_ref.shape[2]
    irow = lax.broadcasted_iota(jnp.int32, (tile, V7X_LANES), 0) + rblk * tile
    row = irow.astype(F32)
    lane = lax.broadcasted_iota(jnp.int32, (tile, V7X_LANES), 1)
    t = row / (seq - 1.0)
    omega = (2.0 * math.pi / seq) * row
    band_step = (HY_BANDS - 1 - 1e-4) / (HY_BANDS - 1)
    is_cos = (lane >= 1) & (lane <= HY_BANDS)
    is_sin = (lane > HY_BANDS) & (lane <= 2 * HY_BANDS)
    bidx = jnp.where(is_cos, lane - 1, lane - 1 - HY_BANDS).astype(F32)
    ang = omega * (1e-4 + band_step * bidx)
    trig = jnp.cos(ang + jnp.where(is_sin, 0.5 * math.pi, 0.0))
    z = jnp.where(lane == 0, t, jnp.where(is_cos | is_sin, trig, 0.0))
    fr = fr_ref[...]
    hid = jnp.sin(fr * (_dot3(z, w1_ref[...]) + b1_ref[...]))
    hid = jnp.sin(fr * (_dot3(hid, w2_ref[...]) + b2_ref[...]))
    filt = _dot3(hid, w3_ref[...])
    col = lax.broadcasted_iota(jnp.int32, (1, cols), 1)
    chan = (col % MIX_WIDTH).astype(F32)
    max_decay = math.log(HY_DECAY_TARGET) / HY_FAST_PCT
    min_decay = math.log(HY_DECAY_TARGET) / HY_SLOW_PCT
    delta = jnp.abs(min_decay + (max_decay - min_decay) / (MIX_WIDTH - 1) * chan)
    filt = filt * jnp.exp(-t[:, 0:1] * delta)
    is_bwd = (col // MIX_WIDTH) % 2 == 1
    rows_c = lax.broadcasted_iota(jnp.int32, (tile, cols), 0) + rblk * tile
    filt = jnp.where(is_bwd & (rows_c == seq - 1), 0.0, filt)
    o_ref[0] = filt

    @pl.when(rblk == 0)
    def _():
        ss_ref[...] = jnp.zeros_like(ss_ref)

    ss_ref[...] += jnp.sum(filt * filt, axis=0, keepdims=True)


def _hyfilt_gen(seq, w1p, b1, w2, b2, w3, freq):
    ncol = w3.shape[1]
    tile = min(seq, 512)
    const = lambda shape: pl.BlockSpec(shape, lambda r: (0,) * len(shape))
    kern = functools.partial(_hyfilt_gen_kernel, seq=seq)
    return pl.pallas_call(
        kern,
        grid=(seq // tile,),
        in_specs=[const(w1p.shape), const(b1.shape), const(w2.shape), const(b2.shape), const(w3.shape),
                  const(freq.shape)],
        out_specs=[pl.BlockSpec((1, tile, ncol), lambda r: (0, r, 0)),
                   pl.BlockSpec((1, ncol), lambda r: (0, 0))],
        out_shape=[jax.ShapeDtypeStruct((1, seq, ncol), F32), jax.ShapeDtypeStruct((1, ncol), F32)],
        compiler_params=_cparams(("arbitrary",), 40),
        name="hyena_filter_gen",
    )(w1p, b1, w2, b2, w3, freq)


def _hyfilt_spec_kernel(tw1_ref, a0_ref, a1_ref, ss0_ref, ss1_ref, gb_ref, o_ref, *, n2):
    kblk = pl.program_id(2)
    kb = a0_ref.shape[1]
    cols = a0_ref.shape[-1]
    h1 = tw1_ref.shape[1]
    scale = lax.rsqrt(ss0_ref[...] + ss1_ref[...] + EPS)
    ang2 = (-2.0 * math.pi / n2) * lax.broadcasted_iota(jnp.int32, (n2, cols), 0).astype(F32)
    cr, ci = jnp.cos(ang2), jnp.sin(ang2)
    for k in range(kb):
        y0 = _dot(gb_ref[...], a0_ref[0, k].reshape(2 * n2, cols))
        y1 = _dot(gb_ref[...], a1_ref[0, k].reshape(2 * n2, cols))
        sr = tw1_ref[0, kblk * kb + k]
        si = tw1_ref[1, kblk * kb + k]
        wr, wi = cr * sr - ci * si, cr * si + ci * sr
        y1r, y1i = y1[0:n2, :], y1[n2:2 * n2, :]
        o_ref[0, k, 0] = (y0[0:n2, :] + (wr * y1r - wi * y1i)) * scale
        o_ref[0, k, 1] = (y0[n2:2 * n2, :] - (wr * y1i + wi * y1r)) * scale


def _hyfilt_spec(af, ss, plan):
    _, h1, _, n2, _ = af.shape
    ncb = MIX_WIDTH // FFT_COLS
    blk = (1, FFT_KB, 2, n2, FFT_COLS)
    kern = functools.partial(_hyfilt_spec_kernel, n2=n2)
    return pl.pallas_call(
        kern,
        grid=(HY_ORDER, ncb, h1 // FFT_KB),
        in_specs=[pl.BlockSpec(memory_space=pltpu.SMEM),
                  pl.BlockSpec(blk, lambda o, c, k: (0, k, 0, 0, o * 2 * ncb + c)),
                  pl.BlockSpec(blk, lambda o, c, k: (0, k, 0, 0, o * 2 * ncb + ncb + c)),
                  pl.BlockSpec((1, FFT_COLS), lambda o, c, k: (0, o * 2 * ncb + c)),
                  pl.BlockSpec((1, FFT_COLS), lambda o, c, k: (0, o * 2 * ncb + ncb + c)),
                  pl.BlockSpec(plan.gb.shape, lambda o, c, k: (0, 0))],
        out_specs=pl.BlockSpec(blk, lambda o, c, k: (o, k, 0, 0, c)),
        out_shape=jax.ShapeDtypeStruct((HY_ORDER, h1, 2, n2, MIX_WIDTH), F32),
        compiler_params=_cparams(("parallel", "parallel", "parallel"), 40),
        name="hyena_filter_spectrum",
    )(plan.tw1, af, af, ss, ss, plan.gb)


def _hyena(zc, lw, plan):
    bsz, seq, _ = zc.shape
    filt, ss = _hyfilt_gen(seq, lw["hy_w1p"], lw["hy_b1"], lw["hy_w2"], lw["hy_b2"], lw["hy_w3"], lw["hy_freq"])
    af = _fft_a(filt.reshape(1, plan.nh, plan.n2, filt.shape[-1]), plan)
    kf = _hyfilt_spec(af, ss, plan)
    zc4 = zc.reshape(bsz, plan.nh, plan.n2, 3 * MIX_WIDTH)
    a = _fft_a(zc4, plan, col_off=0, ncols=MIX_WIDTH)
    bm = _fft_mid(a, kf, 0, plan)
    z1 = _fft_c(bm, zc4, 0, zc4, MIX_WIDTH, lw["hy_bias"][0:1], plan)
    a = _fft_a(z1, plan)
    bm = _fft_mid(a, kf, 1, plan)
    z2 = _fft_c(bm, z1, 0, zc4, 2 * MIX_WIDTH, lw["hy_bias"][1:2], plan)
    return z2.reshape(bsz, seq, MIX_WIDTH)


def _gla_kernel(*refs, reverse):
    if reverse:
        x_ref, of_ref, wla_ref, bla_ref, ng_ref, o_ref, st_ref = refs
    else:
        x_ref, wla_ref, bla_ref, o_ref, st_ref = refs
    c = pl.program_id(1)

    @pl.when(c == 0)
    def _():
        st_ref[...] = jnp.zeros_like(st_ref)

    for bb in range(x_ref.shape[0]):
        _gla_sequence(bb, x_ref, of_ref if reverse else None, wla_ref, bla_ref, ng_ref if reverse else None, o_ref,
                      st_ref, reverse)


def _gla_sequence(bb, x_ref, of_ref, wla_ref, bla_ref, ng_ref, o_ref, st_ref, reverse):
    tile = GLA_TILE
    ck = GLA_CHUNK
    nck = tile // ck
    hw = V7X_LANES
    nh = GLA_HEADS
    q = x_ref[bb, :, 0:nh * hw] * (GLA_DK ** -0.5)
    k = x_ref[bb, :, nh * hw:2 * nh * hw]
    v = x_ref[bb, :, 2 * nh * hw:3 * nh * hw]
    lr = x_ref[bb, :, 4 * nh * hw:4 * nh * hw + hw]
    zl = _dot(lr.astype(BF16), wla_ref[...]) + bla_ref[...]
    la = (jnp.minimum(zl, 0.0) - jnp.log(1.0 + jnp.exp(-jnp.abs(zl)))) / GLA_TAU

    row = lax.broadcasted_iota(jnp.int32, la.shape, 0) % ck
    bcum = la
    d = 1
    while d < ck:
        if reverse:
            bcum = bcum + jnp.where(row < ck - d, pltpu.roll(bcum, tile - d, 0), 0.0)
        else:
            bcum = bcum + jnp.where(row >= d, pltpu.roll(bcum, d, 0), 0.0)
        d *= 2
    b3 = bcum.reshape(nck, ck, nh * hw)
    blast = b3[:, 0:1, :] if reverse else b3[:, ck - 1:ck, :]
    q_e = (q * jnp.exp(bcum)).astype(BF16)
    k_e = (k * jnp.exp(-bcum)).astype(BF16)
    k_d = (k.reshape(nck, ck, nh * hw) * jnp.exp(blast - b3)).reshape(tile, nh * hw).astype(BF16)
    gch = jnp.exp(blast)
    vb = v.astype(BF16)

    ri = lax.broadcasted_iota(jnp.int32, (ck, ck), 0)
    ci = lax.broadcasted_iota(jnp.int32, (ck, ck), 1)
    mask = (ri <= ci) if reverse else (ri >= ci)
    order = range(nck - 1, -1, -1) if reverse else range(nck)
    outs = [None] * nck
    for n in order:
        rs = slice(n * ck, (n + 1) * ck)
        heads = []
        for h in range(nh):
            ls = slice(h * hw, (h + 1) * hw)
            qe, ke, kd, vh = q_e[rs, ls], k_e[rs, ls], k_d[rs, ls], vb[rs, ls]
            st = st_ref[bb, h]
            sc = lax.dot_general(qe, ke, (((1,), (1,)), ((), ())), preferred_element_type=F32)
            sc = jnp.where(mask, sc, 0.0).astype(BF16)
            o = _dot(sc, vh) + lax.dot_general(qe, st.astype(BF16), (((1,), (1,)), ((), ())),
                                               preferred_element_type=F32)
            upd = lax.dot_general(vh, kd, (((0,), (0,)), ((), ())), preferred_element_type=F32)
            st_ref[bb, h] = st * gch[n, :, ls] + upd
            heads.append(o)
        outs[n] = jnp.concatenate(heads, axis=1)
    o_dir = jnp.concatenate(outs, axis=0)
    if not reverse:
        o_ref[bb] = o_dir
        return
    o = of_ref[bb] + o_dir
    g = x_ref[bb, :, 3 * nh * hw:4 * nh * hw]
    normed = []
    for h in range(nh):
        oh = o[:, h * hw:(h + 1) * hw]
        normed.append(oh * lax.rsqrt(jnp.mean(oh * oh, axis=-1, keepdims=True) + EPS))
    o = jnp.concatenate(normed, axis=1) * ng_ref[...]
    o_ref[bb] = o * _silu(g)


def _gla(gla_in, wla, bla, norm_g):
    bsz, seq, width = gla_in.shape
    tile = GLA_TILE
    nc = seq // tile
    nb = min(GLA_BATCH, bsz)
    const = lambda shape: pl.BlockSpec(shape, lambda b, c: (0,) * len(shape))
    out_shape = jax.ShapeDtypeStruct((bsz, seq, MIX_WIDTH), F32)
    scratch = [pltpu.VMEM((nb, GLA_HEADS, GLA_DV, V7X_LANES), F32)]
    o_f = pl.pallas_call(
        functools.partial(_gla_kernel, reverse=False),
        grid=(bsz // nb, nc),
        in_specs=[pl.BlockSpec((nb, tile, width), lambda b, c: (b, c, 0)), const(wla.shape[1:]), const(bla.shape[1:])],
        out_specs=pl.BlockSpec((nb, tile, MIX_WIDTH), lambda b, c: (b, c, 0)),
        out_shape=out_shape,
        scratch_shapes=scratch,
        compiler_params=_cparams(("parallel", "arbitrary"), 48),
        name="gla_fwd",
    )(gla_in, wla[0], bla[0])
    return pl.pallas_call(
        functools.partial(_gla_kernel, reverse=True),
        grid=(bsz // nb, nc),
        in_specs=[pl.BlockSpec((nb, tile, width), lambda b, c: (b, nc - 1 - c, 0)),
                  pl.BlockSpec((nb, tile, MIX_WIDTH), lambda b, c: (b, nc - 1 - c, 0)),
                  const(wla.shape[1:]), const(bla.shape[1:]), const(norm_g.shape)],
        out_specs=pl.BlockSpec((nb, tile, MIX_WIDTH), lambda b, c: (b, nc - 1 - c, 0)),
        out_shape=out_shape,
        scratch_shapes=scratch,
        compiler_params=_cparams(("parallel", "arbitrary"), 48),
        name="gla_bwd",
    )(gla_in, o_f, wla[1], bla[1], norm_g)


def _s5_kernel(u_ref, t_ref, e_ref, o_ref_w, mu_ref, mup_ref, y_ref):
    ch = S5_CHUNK
    sub = V7X_SUBLANES
    nrow = u_ref.shape[1] // ch
    half = S5_LANE_GROUPS * 2 * S5_STATE
    swap = lambda x: pltpu.roll(x, half // 2, 1)
    u = jnp.concatenate([u_ref[0, pl.ds(i, nrow, stride=ch), :] for i in range(ch)], axis=1).astype(BF16)
    y = _dot(u, t_ref[0])
    he = _dot(u, e_ref[0])
    row = lax.broadcasted_iota(jnp.int32, (nrow, half), 0)
    rsub = row % sub
    states = []
    for d in range(2):
        h = he[:, d * half:(d + 1) * half]
        for s in range(S5_SCAN_STEPS):
            step = 2 ** s
            if d == 0:
                hs = jnp.where(rsub >= step, pltpu.roll(h, step, 0), 0.0)
            else:
                hs = jnp.where(rsub < sub - step, pltpu.roll(h, nrow - step, 0), 0.0)
            h = h + hs * mu_ref[0, d, s, 0:1, :] + swap(hs) * mu_ref[0, d, s, 1:2, :]
        ngroup = nrow // sub
        carry = jnp.zeros((1, half), F32)
        out = [None] * ngroup
        for g in (range(ngroup) if d == 0 else range(ngroup - 1, -1, -1)):
            hg = h[g * sub:(g + 1) * sub, :] + carry * mup_ref[0, d, 0] + swap(carry) * mup_ref[0, d, 1]
            carry = hg[sub - 1:sub, :] if d == 0 else hg[0:1, :]
            out[g] = hg
        h = jnp.concatenate(out, axis=0)
        if d == 0:
            h = jnp.where(row >= 1, pltpu.roll(h, 1, 0), 0.0)
        else:
            h = jnp.where(row < nrow - 1, pltpu.roll(h, nrow - 1, 0), 0.0)
        states.append(h)
    hp = jnp.concatenate(states, axis=1).astype(BF16)
    y = y + lax.dot_general(hp, o_ref_w[0], (((1,), (1,)), ((), ())), preferred_element_type=F32)
    for j in range(ch):
        y_ref[0, pl.ds(j, nrow, stride=ch), :] = y[:, j * V7X_LANES:(j + 1) * V7X_LANES]


def _s5(s5_in, tblk, eblk, oblk, mu, mup):
    bsz, seq, width = s5_in.shape
    nb = width // V7X_LANES
    once = pl.Buffered(1)
    wspec = lambda arr: pl.BlockSpec((1,) + arr.shape[1:], lambda k, b: (k,) + (0,) * (arr.ndim - 1),
                                     pipeline_mode=once)
    xspec = pl.BlockSpec((1, seq, V7X_LANES), lambda k, b: (b, 0, k), pipeline_mode=once)
    return pl.pallas_call(
        _s5_kernel,
        grid=(nb, bsz),
        in_specs=[xspec, wspec(tblk), wspec(eblk), wspec(oblk), wspec(mu), wspec(mup)],
        out_specs=pl.BlockSpec((1, seq, V7X_LANES), lambda k, b: (b, 0, k)),
        out_shape=jax.ShapeDtypeStruct(s5_in.shape, F32),
        compiler_params=_cparams(("arbitrary", "arbitrary"), 60),
        name="s5",
    )(s5_in, tblk, eblk, oblk, mu, mup)


def _s5_tables(lam_re, lam_im, log_dt, b_re, b_im, c_re, c_im):
    ch = S5_CHUNK
    hi = lax.Precision.HIGHEST
    cmul = lambda x, y: (x[0] * y[0] - x[1] * y[1], x[0] * y[1] + x[1] * y[0])
    lr_, li_ = lam_re.astype(F32), lam_im.astype(F32)
    dt = jnp.exp(log_dt.astype(F32))[..., None]
    ar, ai = lr_ * dt, li_ * dt

    def lam_pow(tau):
        t = jnp.asarray(tau, F32)
        t = t.reshape(t.shape + (1,) * 3)
        mag = jnp.exp(t * ar)
        return mag * jnp.cos(t * ai), mag * jnp.sin(t * ai)

    lb = lam_pow(jnp.ones((), F32))
    num = (lb[0] - 1.0, lb[1])
    den = lr_ * lr_ + li_ * li_
    ratio = ((num[0] * lr_ + num[1] * li_) / den, (num[1] * lr_ - num[0] * li_) / den)
    b_bar = cmul((ratio[0][..., None], ratio[1][..., None]), (b_re.astype(F32), b_im.astype(F32)))
    cc = (c_re.astype(F32), c_im.astype(F32))

    nb, gb, hh, pp = S5_GROUPS // S5_LANE_GROUPS, S5_LANE_GROUPS, S5_GROUP, S5_STATE
    eye = jnp.eye(gb, dtype=F32)

    def b_base(x):
        x = jnp.transpose(x.reshape(2, nb, gb, pp, hh), (1, 2, 4, 0, 3))
        return (x[:, :, :, :, None, :] * eye[None, :, None, None, :, None]).reshape(nb, gb * hh, 2 * gb * pp)

    def c_base(x):
        x = jnp.transpose(x.reshape(2, nb, gb, hh, pp), (1, 2, 3, 0, 4))
        return (x[:, :, :, :, None, :] * eye[None, :, None, None, :, None]).reshape(nb, gb * hh, 2 * gb * pp)

    row = lambda x: jnp.transpose(x.reshape(2, nb, gb * pp), (1, 0, 2)).reshape(nb, 1, 2 * gb * pp)
    base = (b_base(b_bar[0]), b_base(b_bar[1]), c_base(cc[0]), c_base(cc[1]), row(ar), row(ai))
    dtab = _s5_lag_tables(*base)
    return _s5_block_tables(dtab, *base)


def _lam_pow(tau, ar, ai):
    mag = jnp.exp(tau * ar)
    return mag * jnp.cos(tau * ai), mag * jnp.sin(tau * ai)


def _s5_lag_kernel(btr_ref, bti_ref, ctr_ref, cti_ref, ar_ref, ai_ref, d_ref):
    ch = S5_CHUNK
    half = btr_ref.shape[-1] // 2
    tdot = lambda a, b: lax.dot_general(a, b, (((1,), (1,)), ((), ())), preferred_element_type=F32,
                                        precision=lax.Precision.HIGHEST)
    for d in range(2):
        ls = slice(d * half, (d + 1) * half)
        btr, bti, ctr, cti = btr_ref[0, :, ls], bti_ref[0, :, ls], ctr_ref[0, :, ls], cti_ref[0, :, ls]
        for lag in range(ch):
            lr, li = _lam_pow(float(lag), ar_ref[0, :, ls], ai_ref[0, :, ls])
            val = tdot(btr * lr - bti * li, ctr) - tdot(btr * li + bti * lr, cti)
            idx = ch - 1 + lag if d == 0 else ch - 1 - lag
            if d == 1 and lag == 0:
                d_ref[0, idx] = d_ref[0, idx] + val
            else:
                d_ref[0, idx] = val


def _s5_lag_tables(btr, bti, ctr, cti, ar, ai):
    nb = btr.shape[0]
    spec = lambda a: pl.BlockSpec((1,) + a.shape[1:], lambda k: (k, 0, 0))
    nlag = 2 * S5_CHUNK - 1
    return pl.pallas_call(
        _s5_lag_kernel,
        grid=(nb,),
        in_specs=[spec(a) for a in (btr, bti, ctr, cti, ar, ai)],
        out_specs=pl.BlockSpec((1, nlag, V7X_LANES, V7X_LANES), lambda k: (k, 0, 0, 0)),
        out_shape=jax.ShapeDtypeStruct((nb, nlag, V7X_LANES, V7X_LANES), F32),
        compiler_params=_cparams(("parallel",), 32),
        name="s5_lag_tables",
    )(btr, bti, ctr, cti, ar, ai)


def _s5_block_kernel(d_ref, btr_ref, bti_ref, ctr_ref, cti_ref, ar_ref, ai_ref, t_ref, e_ref, ot_ref, mu_ref,
                     mup_ref):
    ch = S5_CHUNK
    i = pl.program_id(1)
    half = btr_ref.shape[-1] // 2
    fi = i.astype(F32)
    for j in range(ch):
        t_ref[0, :, j * V7X_LANES:(j + 1) * V7X_LANES] = d_ref[0, j - i + ch - 1].astype(BF16)
    for d in range(2):
        ls = slice(d * half, (d + 1) * half)
        ar, ai = ar_ref[0, :, ls], ai_ref[0, :, ls]
        lr, li = _lam_pow(fi if d == 1 else (ch - 1.0) - fi, ar, ai)
        btr, bti = btr_ref[0, :, ls], bti_ref[0, :, ls]
        e_ref[0, :, 2 * d * half:(2 * d + 1) * half] = (btr * lr - bti * li).astype(BF16)
        e_ref[0, :, (2 * d + 1) * half:(2 * d + 2) * half] = (btr * li + bti * lr).astype(BF16)
        lr, li = _lam_pow(fi + 1.0 if d == 0 else ch - fi, ar, ai)
        ctr, cti = ctr_ref[0, :, ls], cti_ref[0, :, ls]
        ot_ref[0, :, 2 * d * half:(2 * d + 1) * half] = (ctr * lr - cti * li).astype(BF16)
        ot_ref[0, :, (2 * d + 1) * half:(2 * d + 2) * half] = (-(ctr * li + cti * lr)).astype(BF16)

    @pl.when(i == 0)
    def _():
        for d in range(2):
            ar, ai = ar_ref[0, :, d * half:(d + 1) * half], ai_ref[0, :, d * half:(d + 1) * half]
            for s in range(S5_SCAN_STEPS):
                lr, li = _lam_pow(float(ch * 2 ** s), ar, ai)
                mu_ref[0, d, s, 0:1, :] = jnp.concatenate([lr, lr], axis=1)
                mu_ref[0, d, s, 1:2, :] = jnp.concatenate([-li, li], axis=1)
            for r in range(V7X_SUBLANES):
                lr, li = _lam_pow(float(ch * (r + 1 if d == 0 else V7X_SUBLANES - r)), ar, ai)
                mup_ref[0, d, 0, r:r + 1, :] = jnp.concatenate([lr, lr], axis=1)
                mup_ref[0, d, 1, r:r + 1, :] = jnp.concatenate([-li, li], axis=1)


def _s5_block_tables(dtab, btr, bti, ctr, cti, ar, ai):
    nb = btr.shape[0]
    ch = S5_CHUNK
    big = ch * V7X_LANES
    wide = 2 * btr.shape[-1]
    spec = lambda a: pl.BlockSpec((1,) + a.shape[1:], lambda k, i: (k,) + (0,) * (a.ndim - 1))
    tile = lambda w: pl.BlockSpec((1, V7X_LANES, w), lambda k, i: (k, i, 0))
    return pl.pallas_call(
        _s5_block_kernel,
        grid=(nb, ch),
        in_specs=[spec(a) for a in (dtab, btr, bti, ctr, cti, ar, ai)],
        out_specs=[tile(big), tile(wide), tile(wide),
                   pl.BlockSpec((1, 2, S5_SCAN_STEPS, 2, wide // 2), lambda k, i: (k, 0, 0, 0, 0)),
                   pl.BlockSpec((1, 2, 2, V7X_SUBLANES, wide // 2), lambda k, i: (k, 0, 0, 0, 0))],
        out_shape=[jax.ShapeDtypeStruct((nb, big, big), BF16), jax.ShapeDtypeStruct((nb, big, wide), BF16),
                   jax.ShapeDtypeStruct((nb, big, wide), BF16),
                   jax.ShapeDtypeStruct((nb, 2, S5_SCAN_STEPS, 2, wide // 2), F32),
                   jax.ShapeDtypeStruct((nb, 2, 2, V7X_SUBLANES, wide // 2), F32)],
        compiler_params=_cparams(("parallel", "arbitrary"), 32),
        name="s5_block_tables",
    )(dtab, btr, bti, ctr, cti, ar, ai)


def _merge_kernel(x_ref, ng_ref, yaf_ref, yab_ref, yb_ref, yc_ref, y5_ref, u5_ref, d5_ref, gluw_ref, glub_ref,
                  wgate_ref, bgate_ref, wbr_ref, wout_ref, fng_ref, wr_ref, br_ref, o_ref, g_ref):
    x = x_ref[...]
    h = _rms(x, ng_ref[...]).astype(BF16)
    y_d = _gelu_tanh(u5_ref[...] * d5_ref[...] + y5_ref[...])
    y_d = y_d * _sigmoid(_dot(y_d.astype(BF16), gluw_ref[...]) + glub_ref[...])
    branches = (yaf_ref[...] + yab_ref[...], yb_ref[...], yc_ref[...], y_d)
    merged = jnp.zeros(x.shape, F32)
    for i, y in enumerate(branches):
        gate = _sigmoid(_dot(h, wgate_ref[i]) + bgate_ref[i])
        merged = merged + gate * _dot(y.astype(BF16), wbr_ref[i])
    x_new = x + _dot(merged.astype(BF16), wout_ref[...])
    o_ref[...] = x_new
    g_ref[...] = _top_group(_router_logits(x_new, fng_ref, wr_ref, br_ref)[1])


def _merge(xf, lw, yaf, yab, yb, yc, y5, u5):
    t = xf.shape[0]
    tok = lambda w: pl.BlockSpec((TOK_TILE, w), lambda i: (i, 0))
    const = lambda arr: pl.BlockSpec(arr.shape, lambda i: (0,) * arr.ndim, pipeline_mode=pl.Buffered(1))
    weights = [lw["s5_d"], lw["s5_glu_w"], lw["s5_glu_b"], lw["w_gate"], lw["b_gate"], lw["w_branch"], lw["w_out"],
               lw["norm_ffn_g"], lw["w_router"], lw["b_router"]]
    return pl.pallas_call(
        _merge_kernel,
        grid=(t // TOK_TILE,),
        in_specs=[tok(D_MODEL), const(lw["norm_mix_g"])] + [tok(MIX_WIDTH)] * 6 + [const(w) for w in weights],
        out_specs=[tok(D_MODEL), tok(1)],
        out_shape=[jax.ShapeDtypeStruct((t, D_MODEL), F32), jax.ShapeDtypeStruct((t, 1), jnp.int32)],
        compiler_params=_cparams(("parallel",), 48),
        name="merge",
    )(xf, lw["norm_mix_g"], yaf, yab, yb, yc, y5, u5, *weights)


def _router_logits(x, ng_ref, wr_ref, br_ref):
    hf = _rms(x, ng_ref[...])
    return hf, _dot3(hf, wr_ref[...]) + br_ref[...]


def _top_group(logits):
    lane = lax.broadcasted_iota(jnp.int32, logits.shape, 1).astype(F32)
    gl = jnp.where(lane < MOE_GROUPS, logits, -jnp.inf)
    gmax = jnp.max(gl, axis=1, keepdims=True)
    return jnp.min(jnp.where(gl == gmax, lane, float(V7X_LANES)), axis=1, keepdims=True).astype(jnp.int32)


def _moe_plan_kernel(pos_ref, src_ref):
    def clear(p, c):
        src_ref[p] = 0
        return c

    def place(t, c):
        src_ref[pos_ref[t]] = t
        return c

    lax.fori_loop(0, src_ref.shape[0], clear, 0, unroll=16)
    lax.fori_loop(0, pos_ref.shape[0], place, 0, unroll=16)


def _moe_plan(gidx, tile):
    t = gidx.shape[0]
    ntile = t // tile + MOE_GROUPS
    g = gidx.reshape(t)
    onehot = (g[:, None] == jnp.arange(MOE_GROUPS, dtype=jnp.int32)[None, :]).astype(jnp.int32)
    csum = jnp.cumsum(onehot, axis=0)
    rank = jnp.sum(onehot * (csum - 1), axis=1)
    count = csum[-1]
    gtiles = (count + tile - 1) // tile
    first = jnp.cumsum(gtiles) - gtiles
    pos = jnp.sum(onehot * first[None, :], axis=1) * tile + rank
    tid = jnp.arange(ntile, dtype=jnp.int32)
    tgroup = jnp.minimum(jnp.sum((tid[:, None] >= (first + gtiles)[None, :]).astype(jnp.int32), axis=1),
                         MOE_GROUPS - 1)
    oh_t = (tgroup[:, None] == jnp.arange(MOE_GROUPS, dtype=jnp.int32)[None, :]).astype(jnp.int32)
    tvalid = jnp.clip(jnp.sum(oh_t * count[None, :], axis=1) - (tid - jnp.sum(oh_t * first[None, :], axis=1)) * tile,
                      0, tile)
    src = pl.pallas_call(
        _moe_plan_kernel,
        in_specs=[pl.BlockSpec(memory_space=pltpu.SMEM)],
        out_specs=pl.BlockSpec(memory_space=pltpu.SMEM),
        out_shape=jax.ShapeDtypeStruct((ntile * tile,), jnp.int32),
        name="moe_plan",
    )(pos.astype(jnp.int32))
    return src, tgroup.astype(jnp.int32), tvalid.astype(jnp.int32)


def _moe_expert_kernel(src_ref, tg_ref, nv_ref, x_hbm, ng_ref, wr_ref, br_ref, w13_ref, w2_ref, fg_ref, o_hbm,
                       xbuf, ybuf, sem, *, final):
    i = pl.program_id(0)
    nt = pl.num_programs(0)
    tile = xbuf.shape[1]
    slot = i % 2
    nv = nv_ref[i]
    ng, ne, ff = MOE_GROUPS, MOE_EXPERTS, MOE_FF

    def rows(j, wait, row, whole):
        count = nv_ref[j]

        @pl.when(count == tile)
        def _():
            if wait:
                whole().wait()
            else:
                def body(r, c):
                    row(j * tile + r, r).start()
                    return c
                lax.fori_loop(0, tile, body, 0, unroll=8)

        @pl.when((count > 0) & (count < tile))
        def _():
            def body(r, c):
                cp = row(j * tile + r, r)
                cp.wait() if wait else cp.start()
                return c
            lax.fori_loop(0, count, body, 0)

    def gather(j, s, wait):
        rows(j, wait,
             lambda p, r: pltpu.make_async_copy(x_hbm.at[pl.ds(src_ref[p], 1), :], xbuf.at[s, pl.ds(r, 1), :],
                                                sem.at[0, s]),
             lambda: pltpu.make_async_copy(x_hbm.at[pl.ds(0, tile), :], xbuf.at[s], sem.at[0, s]))

    def scatter(j, s, wait):
        rows(j, wait,
             lambda p, r: pltpu.make_async_copy(ybuf.at[s, pl.ds(r, 1), :], o_hbm.at[pl.ds(src_ref[p], 1), :],
                                                sem.at[1, s]),
             lambda: pltpu.make_async_copy(ybuf.at[s], o_hbm.at[pl.ds(0, tile), :], sem.at[1, s]))

    @pl.when(i == 0)
    def _():
        xbuf[...] = jnp.zeros_like(xbuf)
        gather(0, 0, False)

    @pl.when(i + 1 < nt)
    def _():
        gather(i + 1, 1 - slot, False)

    @pl.when(i >= 2)
    def _():
        scatter(i - 2, slot, True)

    gather(i, slot, True)

    @pl.when(nv > 0)
    def _():
        x = xbuf[slot]
        hf, logits = _router_logits(x, ng_ref, wr_ref, br_ref)
        lane = lax.broadcasted_iota(jnp.int32, logits.shape, 1).astype(F32)
        neg = -jnp.inf
        big = float(V7X_LANES)
        grp = tg_ref[i].astype(F32)
        gl = jnp.where(lane < ng, logits, neg)
        gmax = jnp.max(gl, axis=1, keepdims=True)
        glog = jnp.sum(jnp.where(lane == grp, logits, 0.0), axis=1, keepdims=True)
        gprob = jnp.exp(glog - gmax) / jnp.sum(jnp.exp(gl - gmax), axis=1, keepdims=True)
        lo = ng + ne * grp
        sel = (lane >= lo) & (lane < lo + ne)
        m1 = jnp.max(jnp.where(sel, logits, neg), axis=1, keepdims=True)
        i1 = jnp.min(jnp.where(sel & (logits == m1), lane, big), axis=1, keepdims=True)
        sel2 = sel & (lane != i1)
        m2 = jnp.max(jnp.where(sel2, logits, neg), axis=1, keepdims=True)
        i2 = jnp.min(jnp.where(sel2 & (logits == m2), lane, big), axis=1, keepdims=True)
        e2 = jnp.exp(m2 - m1)
        w1 = gprob / (1.0 + e2)
        w2 = gprob * e2 / (1.0 + e2)
        gu = _dot(hf.astype(BF16), w13_ref[0])
        act = _silu(gu[:, 0:ne * ff]) * gu[:, ne * ff:2 * ne * ff]
        parts = []
        for e in range(ne):
            wcol = jnp.where(i1 == lo + e, w1, 0.0) + jnp.where(i2 == lo + e, w2, 0.0)
            parts.append(act[:, e * ff:(e + 1) * ff] * wcol)
        y = x + _dot(jnp.concatenate(parts, axis=1).astype(BF16), w2_ref[0])
        if final:
            y = _rms(y, fg_ref[...])
        ybuf[slot] = y
        scatter(i, slot, False)

    @pl.when(i == nt - 1)
    def _():
        @pl.when(i >= 1)
        def _():
            scatter(i - 1, 1 - slot, True)
        scatter(i, slot, True)


def _moe(xf, gidx, lw, final_g, final):
    t = xf.shape[0]
    tile = TOK_TILE
    src, tgroup, tvalid = _moe_plan(gidx, tile)
    ntile = tgroup.shape[0]
    const = lambda arr: pl.BlockSpec(arr.shape, lambda i, s, g, n: (0,) * arr.ndim)
    bygroup = lambda arr: pl.BlockSpec((1,) + arr.shape[1:], lambda i, s, g, n: (g[i],) + (0,) * (arr.ndim - 1))
    any_space = pl.BlockSpec(memory_space=pl.ANY)
    return pl.pallas_call(
        functools.partial(_moe_expert_kernel, final=final),
        grid_spec=pltpu.PrefetchScalarGridSpec(
            num_scalar_prefetch=3,
            grid=(ntile,),
            in_specs=[any_space, const(lw["norm_ffn_g"]), const(lw["w_router"]), const(lw["b_router"]),
                      bygroup(lw["w_e_13"]), bygroup(lw["w_e_2"]), const(final_g)],
            out_specs=any_space,
            scratch_shapes=[pltpu.VMEM((2, tile, D_MODEL), F32), pltpu.VMEM((2, tile, D_MODEL), F32),
                            pltpu.SemaphoreType.DMA((2, 2))],
        ),
        out_shape=jax.ShapeDtypeStruct((t, D_MODEL), F32),
        compiler_params=_cparams(("arbitrary",), 40),
        name="moe_experts",
    )(src, tgroup, tvalid, xf, lw["norm_ffn_g"], lw["w_router"], lw["b_router"], lw["w_e_13"], lw["w_e_2"], final_g)


def _block_diag(blocks):
    nb, bs, _ = blocks.shape
    eye = jnp.eye(nb, dtype=blocks.dtype)
    return jnp.einsum("nij,nm->nimj", blocks, eye).reshape(nb * bs, nb * bs)


def _pad_heads(w, axis=-1):
    shape = w.shape[:-1] + (GLA_HEADS, GLA_DK)
    w = w.reshape(shape)
    pad = [(0, 0)] * (w.ndim - 1) + [(0, V7X_LANES - GLA_DK)]
    return jnp.pad(w, pad).reshape(w.shape[:-2] + (GLA_HEADS * V7X_LANES,))


def _prep_layer(w, l):
    f = lambda name: w[name][l]
    w_in = f("w_in")
    cuts = np.cumsum([MIX_WIDTH, MIX_WIDTH, 3 * MIX_WIDTH, GLA_HEADS * GLA_DK, GLA_HEADS * GLA_DK, MIX_WIDTH,
                      MIX_WIDTH, 2 * GLA_RANK]).tolist()
    xa, ga, hy, q, k, v, g, lr, s5 = jnp.split(w_in, cuts, axis=-1)
    lr = jnp.pad(lr, ((0, 0), (0, V7X_LANES - 2 * GLA_RANK)))
    w_pack = jnp.concatenate([xa, ga, hy, _pad_heads(q), _pad_heads(k), v, g, lr, s5], axis=-1).astype(BF16)
    assert w_pack.shape[1] == N_PACK
    lw = {"w_pack": w_pack, "norm_mix_g": f("norm_mix_g")[None]}
    lw["lru_conv_w"] = f("lru_conv_w")
    lw["lru_conv_b"] = f("lru_conv_b")[None]
    wa, wx = f("lru_wa"), f("lru_wx")
    lw["lru_wg"] = jnp.stack([jnp.concatenate([_block_diag(wa[d]), _block_diag(wx[d])], axis=1)
                              for d in range(2)]).astype(BF16)
    lw["lru_bg"] = jnp.concatenate([f("lru_ba"), f("lru_bx")], axis=-1)[:, None, :]
    lw["lru_lam"] = f("lru_lambda")[:, None, :]
    lw["hy_conv_w"] = f("hy_conv_w")
    lw["hy_conv_b"] = f("hy_conv_b")[None]
    lw["hy_w1p"] = _split3(jnp.pad(f("hy_w1"), ((0, V7X_LANES - HY_EMB), (0, 0))))
    lw["hy_b1"] = f("hy_b1")[None]
    lw["hy_w2"] = _split3(f("hy_w2"))
    lw["hy_b2"] = f("hy_b2")[None]
    lw["hy_w3"] = _split3(f("hy_w3"))
    lw["hy_freq"] = f("hy_freq")[None]
    lw["hy_bias"] = f("hy_bias")
    wg2 = _pad_heads(f("gla_wg2"))
    wla = jnp.zeros((2, V7X_LANES, GLA_HEADS * V7X_LANES), F32)
    wla = wla.at[0, 0:GLA_RANK].set(wg2[0]).at[1, GLA_RANK:2 * GLA_RANK].set(wg2[1])
    lw["gla_wla"] = wla.astype(BF16)
    lw["gla_bla"] = _pad_heads(f("gla_bg"))[:, None, :]
    lw["gla_norm_g"] = jnp.tile(f("gla_norm_g"), GLA_HEADS)[None]
    lw["s5_tables"] = _s5_tables(f("s5_lam_re"), f("s5_lam_im"), f("s5_log_dt"), f("s5_b_re"), f("s5_b_im"),
                                 f("s5_c_re"), f("s5_c_im"))
    lw["s5_d"] = f("s5_d")[None]
    lw["s5_glu_w"] = f("s5_glu_w").astype(BF16)
    lw["s5_glu_b"] = f("s5_glu_b")[None]
    lw["w_gate"] = f("w_gate").astype(BF16)
    lw["b_gate"] = f("b_gate")[:, None, :]
    lw["w_branch"] = f("w_branch").astype(BF16)
    lw["w_out"] = f("w_out").astype(BF16)
    wr = jnp.concatenate([f("w_router_group"), jnp.transpose(f("w_router_expert"), (1, 0, 2)).reshape(D_MODEL, -1)],
                         axis=1)
    br = jnp.concatenate([f("b_router_group"), f("b_router_expert").reshape(-1)])
    nr = MOE_GROUPS + MOE_GROUPS * MOE_EXPERTS
    lw["w_router"] = _split3(jnp.pad(wr, ((0, 0), (0, V7X_LANES - nr))))
    lw["b_router"] = jnp.pad(br, (0, V7X_LANES - nr))[None]
    lw["norm_ffn_g"] = f("norm_ffn_g")[None]
    wide = lambda a: jnp.transpose(a, (0, 2, 1, 3)).reshape(MOE_GROUPS, D_MODEL, MOE_EXPERTS * MOE_FF)
    lw["w_e_13"] = jnp.concatenate([wide(f("w_e_gate")), wide(f("w_e_up"))], axis=-1).astype(BF16)
    lw["w_e_2"] = f("w_e_down").reshape(MOE_GROUPS, MOE_EXPERTS * MOE_FF, D_MODEL).astype(BF16)
    return lw


def _encoder(x, layers, final_g):
    bsz, seq, _ = x.shape
    plan = _FftPlan(seq)
    xf = x.reshape(bsz * seq, D_MODEL)
    for l, lw in enumerate(layers):
        lru_in, hy_in, gla_in, s5_in = _inproj(xf, lw["norm_mix_g"], lw["w_pack"], lw["hy_conv_w"],
                                               lw["hy_conv_b"], seq)
        shp = lambda a: a.reshape(bsz, seq, a.shape[-1])
        yaf, yab = _lru(shp(lru_in), lw["lru_conv_w"], lw["lru_conv_b"], lw["lru_wg"], lw["lru_bg"], lw["lru_lam"])
        yb = _hyena(shp(hy_in), lw, plan)
        yc = _gla(shp(gla_in), lw["gla_wla"], lw["gla_bla"], lw["gla_norm_g"])
        y5 = _s5(shp(s5_in), *lw["s5_tables"])
        flat = lambda a: a.reshape(bsz * seq, MIX_WIDTH)
        xf, gidx = _merge(xf, lw, flat(yaf), flat(yab), flat(yb), flat(yc), flat(y5), s5_in)
        xf = _moe(xf, gidx, lw, final_g, final=(l == len(layers) - 1))
    return xf.reshape(bsz, seq, D_MODEL)


def kernel(x_prompt, x_sample, norm_mix_g, w_in, lru_conv_w, lru_conv_b, lru_wa, lru_ba, lru_wx, lru_bx,
           lru_lambda, hy_conv_w, hy_conv_b, hy_w1, hy_b1, hy_w2, hy_b2, hy_w3, hy_freq, hy_bias,
           gla_wg2, gla_bg, gla_norm_g, s5_lam_re, s5_lam_im, s5_log_dt, s5_b_re, s5_b_im, s5_c_re, s5_c_im,
           s5_d, s5_glu_w, s5_glu_b, w_branch, w_gate, b_gate, w_out, norm_ffn_g, w_router_group,
           b_router_group, w_router_expert, b_router_expert, w_e_gate, w_e_up, w_e_down, final_norm_g):
    w = dict(norm_mix_g=norm_mix_g, w_in=w_in, lru_conv_w=lru_conv_w, lru_conv_b=lru_conv_b, lru_wa=lru_wa,
             lru_ba=lru_ba, lru_wx=lru_wx, lru_bx=lru_bx, lru_lambda=lru_lambda, hy_conv_w=hy_conv_w,
             hy_conv_b=hy_conv_b, hy_w1=hy_w1, hy_b1=hy_b1, hy_w2=hy_w2, hy_b2=hy_b2, hy_w3=hy_w3,
             hy_freq=hy_freq, hy_bias=hy_bias, gla_wg2=gla_wg2, gla_bg=gla_bg, gla_norm_g=gla_norm_g,
             s5_lam_re=s5_lam_re, s5_lam_im=s5_lam_im, s5_log_dt=s5_log_dt, s5_b_re=s5_b_re, s5_b_im=s5_b_im,
             s5_c_re=s5_c_re, s5_c_im=s5_c_im, s5_d=s5_d, s5_glu_w=s5_glu_w, s5_glu_b=s5_glu_b,
             w_branch=w_branch, w_gate=w_gate, b_gate=b_gate, w_out=w_out, norm_ffn_g=norm_ffn_g,
             w_router_group=w_router_group, b_router_group=b_router_group, w_router_expert=w_router_expert,
             b_router_expert=b_router_expert, w_e_gate=w_e_gate, w_e_up=w_e_up, w_e_down=w_e_down)
    layers = [_prep_layer(w, l) for l in range(norm_mix_g.shape[0])]
    fg = final_norm_g[None]
    return (_encoder(x_prompt, layers, fg), _encoder(x_sample, layers, fg))
```

```python
import functools
import math

import numpy as np
import jax
import jax.numpy as jnp
from jax import lax
from jax.experimental import pallas as pl
from jax.experimental.pallas import tpu as pltpu

F32 = jnp.float32
BF16 = jnp.bfloat16

D_MODEL = 1024
DEPTH = 2
EPS = 1e-6
MIX_WIDTH = D_MODEL // 2
LRU_BLOCKS = 8
LRU_BLOCK = MIX_WIDTH // LRU_BLOCKS
LRU_CONV = 4
LRU_C = 8.0
HY_ORDER = 2
HY_CONV = 3
HY_EMB = 33
HY_BANDS = (HY_EMB - 1) // 2
HY_HIDDEN = 64
HY_DECAY_TARGET = 1e-2
HY_FAST_PCT = 0.3
HY_SLOW_PCT = 1.5
GLA_HEADS = 4
GLA_DK = MIX_WIDTH // 8
GLA_DV = MIX_WIDTH // GLA_HEADS
GLA_RANK = 16
GLA_TAU = 16.0
GLA_CHUNK = 64
S5_GROUP = 16
S5_GROUPS = MIX_WIDTH // S5_GROUP
S5_STATE = 64
MOE_GROUPS = 4
MOE_EXPERTS = 4
MOE_FF = D_MODEL // 4

V7X_LANES = 128
V7X_SUBLANES = 8
V7X_VMEM_BYTES = 64 * 2**20
MIB = 2**20

GLA_PACK = 4 * V7X_LANES * 2 + 512 + 512 + V7X_LANES
PK_LRU = (0, 1024)
PK_HY = (1024, 2560)
PK_GLA = (2560, 2560 + GLA_PACK)
PK_S5 = (PK_GLA[1], PK_GLA[1] + 512)
N_PACK = PK_S5[1]

TOK_TILE = 512
MOE_TILE = 256
LRU_TILE = 256
GLA_TILE = 256
GLA_BATCH = 2
S5_CHUNK = 16
S5_LANE_GROUPS = V7X_LANES // S5_GROUP
S5_SCAN_STEPS = 3
FFT_N1 = 64
FFT_ROWS = 128
FFT_COLS = 256
FFT_KB = 11


def _cparams(sem, vmem_mib):
    return pltpu.CompilerParams(dimension_semantics=sem, vmem_limit_bytes=int(vmem_mib * MIB))


def _rms(x, g):
    return x * lax.rsqrt(jnp.mean(x * x, axis=-1, keepdims=True) + EPS) * g


def _sigmoid(x):
    return 1.0 / (1.0 + jnp.exp(-x))


def _softplus(x):
    return jnp.maximum(x, 0.0) + jnp.log(1.0 + jnp.exp(-jnp.abs(x)))


def _gelu_tanh(x):
    return 0.5 * x * (1.0 + jnp.tanh(math.sqrt(2.0 / math.pi) * (x + 0.044715 * (x * x * x))))


def _silu(x):
    return x * _sigmoid(x)


def _dot(a, b):
    return jnp.dot(a, b, preferred_element_type=F32)


def _split3(w):
    hi = w.astype(BF16)
    lo = (w - hi.astype(F32)).astype(BF16)
    return jnp.concatenate([hi, lo, hi], axis=0)


def _dot3(a, w3):
    hi = a.astype(BF16)
    lo = (a - hi.astype(F32)).astype(BF16)
    return _dot(jnp.concatenate([hi, hi, lo], axis=1), w3)


def _inproj_kernel(x_ref, xp_ref, xn_ref, g_ref, w_ref, cw_ref, cb_ref, lru_ref, hy_ref, gla_ref, s5_ref, ext_ref, *,
                   tiles_per_seq):
    i = pl.program_id(0)
    tile = TOK_TILE
    h = _rms(x_ref[...], g_ref[...]).astype(BF16)
    lru_ref[...] = _dot(h, w_ref[:, PK_LRU[0]:PK_LRU[1]])
    gla_ref[...] = _dot(h, w_ref[:, PK_GLA[0]:PK_GLA[1]])
    s5_ref[...] = _dot(h, w_ref[:, PK_S5[0]:PK_S5[1]])
    w_hy = w_ref[:, PK_HY[0]:PK_HY[1]]
    edge = lambda ref: _dot(_rms(ref[...], g_ref[...]).astype(BF16), w_hy)
    _fill_ext(ext_ref, _dot(h, w_hy), edge(xp_ref), edge(xn_ref), i % tiles_per_seq == 0,
              i % tiles_per_seq == tiles_per_seq - 1, tile)
    y = cb_ref[...] + ext_ref[7:7 + tile, :] * cw_ref[0:1, :]
    y = y + ext_ref[8:8 + tile, :] * cw_ref[1:2, :]
    y = y + ext_ref[9:9 + tile, :] * cw_ref[2:3, :]
    hy_ref[...] = y


def _inproj(xf, g, w_pack, hy_cw, hy_cb, seq):
    t = xf.shape[0]
    tile = TOK_TILE
    r8 = tile // 8
    last8 = t // 8 - 1
    widths = [PK_LRU[1] - PK_LRU[0], PK_HY[1] - PK_HY[0], PK_GLA[1] - PK_GLA[0], PK_S5[1] - PK_S5[0]]
    const = lambda arr: pl.BlockSpec(arr.shape, lambda i: (0,) * arr.ndim, pipeline_mode=pl.Buffered(1))
    return pl.pallas_call(
        functools.partial(_inproj_kernel, tiles_per_seq=seq // tile),
        grid=(t // tile,),
        in_specs=[pl.BlockSpec((tile, D_MODEL), lambda i: (i, 0)),
                  pl.BlockSpec((8, D_MODEL), lambda i: (jnp.maximum(i * r8 - 1, 0), 0)),
                  pl.BlockSpec((8, D_MODEL), lambda i: (jnp.minimum((i + 1) * r8, last8), 0)),
                  const(g), const(w_pack), const(hy_cw), const(hy_cb)],
        out_specs=[pl.BlockSpec((tile, w), lambda i: (i, 0)) for w in widths],
        out_shape=[jax.ShapeDtypeStruct((t, w), F32) for w in widths],
        scratch_shapes=[pltpu.VMEM((tile + 16, widths[1]), F32)],
        compiler_params=_cparams(("parallel",), 56),
        name="inproj",
    )(xf, xf, xf, g, w_pack, hy_cw, hy_cb)


def _fill_ext(ext_ref, main, prev8, next8, first, last, tile):
    ext_ref[0:8, :] = jnp.where(first, 0.0, prev8)
    ext_ref[8:8 + tile, :] = main
    ext_ref[8 + tile:16 + tile, :] = jnp.where(last, 0.0, next8)


def _linear_scan_tile(a, b, carry, reverse):
    n = a.shape[0]
    sub = V7X_SUBLANES
    row = lax.broadcasted_iota(jnp.int32, a.shape, 0) % sub
    d = 1
    while d < sub:
        if reverse:
            a_s = pltpu.roll(a, n - d, 0)
            b_s = pltpu.roll(b, n - d, 0)
            valid = row < sub - d
        else:
            a_s = pltpu.roll(a, d, 0)
            b_s = pltpu.roll(b, d, 0)
            valid = row >= d
        b = jnp.where(valid, a * b_s + b, b)
        a = jnp.where(valid, a * a_s, a)
        d *= 2
    ngroup = n // sub
    out = [None] * ngroup
    for g in (range(ngroup - 1, -1, -1) if reverse else range(ngroup)):
        h = b[g * sub:(g + 1) * sub, :] + a[g * sub:(g + 1) * sub, :] * carry
        carry = h[0:1, :] if reverse else h[sub - 1:sub, :]
        out[g] = h
    return jnp.concatenate(out, axis=0), carry


def _lru_kernel(mf_ref, pf_ref, nf_ref, mb_ref, pb_ref, nb_ref, cw_ref, cb_ref, wg_ref, bg_ref, lam_ref,
                of_ref, ob_ref, extf_ref, extb_ref, carry_ref):
    c = pl.program_id(1)
    nc = pl.num_programs(1)
    tile = LRU_TILE

    @pl.when(c == 0)
    def _():
        carry_ref[...] = jnp.zeros_like(carry_ref)

    def one(m_ref, p_ref, n_ref, ext_ref, d, first, last, o_ref):
        x = m_ref[0, :, 0:MIX_WIDTH]
        ga = m_ref[0, :, MIX_WIDTH:2 * MIX_WIDTH]
        _fill_ext(ext_ref, x, p_ref[0], n_ref[0], first, last, tile)
        xc = cb_ref[...] + ext_ref[6:6 + tile, :] * cw_ref[0:1, :]
        xc = xc + ext_ref[7:7 + tile, :] * cw_ref[1:2, :]
        xc = xc + ext_ref[8:8 + tile, :] * cw_ref[2:3, :]
        xc = xc + ext_ref[9:9 + tile, :] * cw_ref[3:4, :]
        z = _dot(xc.astype(BF16), wg_ref[d]) + bg_ref[d]
        gate_r = _sigmoid(z[:, 0:MIX_WIDTH])
        gate_i = _sigmoid(z[:, MIX_WIDTH:2 * MIX_WIDTH])
        log_a = -LRU_C * gate_r * _softplus(-lam_ref[d])
        a = jnp.exp(log_a)
        t = 1.0 - a * a
        b = jnp.where(t > 0.0, t * lax.rsqrt(t), 0.0) * gate_i * xc
        h, last = _linear_scan_tile(a, b, carry_ref[d:d + 1, :], reverse=(d == 1))
        carry_ref[d:d + 1, :] = last
        o_ref[0] = h * _gelu_tanh(ga)

    one(mf_ref, pf_ref, nf_ref, extf_ref, 0, c == 0, c == nc - 1, of_ref)
    one(mb_ref, pb_ref, nb_ref, extb_ref, 1, c == nc - 1, c == 0, ob_ref)


def _lru(lru_in, cw, cb, wg, bg, lam):
    bsz, seq, _ = lru_in.shape
    tile = LRU_TILE
    nc = seq // tile
    r8 = tile // 8
    last8 = seq // 8 - 1

    def fwd(c):
        return c

    def bwd(c):
        return nc - 1 - c

    def specs(ch):
        return [pl.BlockSpec((1, tile, 2 * MIX_WIDTH), lambda b, c: (b, ch(c), 0)),
                pl.BlockSpec((1, 8, MIX_WIDTH), lambda b, c: (b, jnp.maximum(ch(c) * r8 - 1, 0), 0)),
                pl.BlockSpec((1, 8, MIX_WIDTH), lambda b, c: (b, jnp.minimum((ch(c) + 1) * r8, last8), 0))]

    const = lambda shape: pl.BlockSpec(shape, lambda b, c: (0,) * len(shape))
    return pl.pallas_call(
        _lru_kernel,
        grid=(bsz, nc),
        in_specs=specs(fwd) + specs(bwd) + [const(cw.shape), const(cb.shape), const(wg.shape), const(bg.shape),
                                            const(lam.shape)],
        out_specs=[pl.BlockSpec((1, tile, MIX_WIDTH), lambda b, c: (b, c, 0)),
                   pl.BlockSpec((1, tile, MIX_WIDTH), lambda b, c: (b, nc - 1 - c, 0))],
        out_shape=[jax.ShapeDtypeStruct((bsz, seq, MIX_WIDTH), F32)] * 2,
        scratch_shapes=[pltpu.VMEM((tile + 16, MIX_WIDTH), F32), pltpu.VMEM((tile + 16, MIX_WIDTH), F32),
                        pltpu.VMEM((8, MIX_WIDTH), F32)],
        compiler_params=_cparams(("parallel", "arbitrary"), 40),
        name="lru",
    )(lru_in, lru_in, lru_in, lru_in, lru_in, lru_in, cw, cb, wg, bg, lam)


class _FftPlan:
    def __init__(self, seq):
        n = 2 * seq
        n1 = FFT_N1
        n2 = n // n1
        assert n1 * n2 == n and n2 % 16 == 0
        h1 = n1 // 2 + 1
        nh = n1 // 2
        self.n, self.n1, self.n2, self.h1, self.nh = n, n1, n2, h1, nh
        self.rows = min(FFT_ROWS, n2)
        assert n2 % self.rows == 0 and h1 % FFT_KB == 0
        k1 = np.arange(h1, dtype=np.float64)
        m1 = np.arange(nh, dtype=np.float64)
        r = np.arange(8, dtype=np.float64)
        ang = -2.0 * np.pi * (k1[:, None, None] * m1[None, None, :] / n1 + r[None, :, None] * k1[:, None, None] / n)
        e = np.exp(1j * ang)
        fa = np.zeros((h1, 8, nh, 8), np.complex128)
        for rr in range(8):
            fa[:, rr, :, rr] = e[:, rr, :]
        fa = fa.reshape(h1 * 8, nh * 8)
        self.fa = jnp.asarray(np.concatenate([fa.real, fa.imag], axis=0), BF16)
        ck = np.where((k1 == 0) | (k1 == n1 // 2), 1.0, 2.0)
        ec = np.conj(e) * ck[:, None, None] / n
        fc = np.zeros((nh, 8, 2, h1, 8), np.float64)
        for rr in range(8):
            fc[:, rr, 0, :, rr] = ec[:, rr, :].real.T
            fc[:, rr, 1, :, rr] = -ec[:, rr, :].imag.T
        self.fc = jnp.asarray(fc.reshape(nh * 8, 2 * h1 * 8), BF16)
        rg = np.arange(n2 // 8, dtype=np.float64)
        tw = np.exp(-2j * np.pi * 8.0 * rg[None, :] * k1[:, None] / n)
        self.tw = jnp.asarray(np.concatenate([tw.real, tw.imag], axis=0), F32)
        tw1 = np.exp(-2j * np.pi * k1 / n)
        self.tw1 = jnp.asarray(np.stack([tw1.real, tw1.imag]), F32)
        q = np.arange(n2, dtype=np.float64)
        f2 = np.exp(-2j * np.pi * np.outer(q, q) / n2)
        fr, fi = f2.real, f2.imag
        self.gb = jnp.asarray(np.block([[fr, -fi], [fi, fr]]), BF16)
        self.gbi = jnp.asarray(np.block([[fr, fi], [-fi, fr]]), BF16)


def _fft_a_kernel(tw_ref, x_ref, fa_ref, a_ref, *, h1, nh, rows):
    rb = pl.program_id(2)
    cols = x_ref.shape[-1]

    def stage(rg):
        xg = x_ref[0, :, pl.ds(pl.multiple_of(rg * 8, 8), 8), :].reshape(nh * 8, cols).astype(BF16)
        return _dot(fa_ref[...], xg)

    def body(i, carry):
        p0 = stage(2 * i)
        p1 = stage(2 * i + 1)
        g0 = rb * (rows // 8) + 2 * i
        for k in range(h1):
            outs = []
            for p, g in ((p0, g0), (p1, g0 + 1)):
                pr = p[k * 8:(k + 1) * 8, :]
                pi = p[(h1 + k) * 8:(h1 + k + 1) * 8, :]
                tr = tw_ref[k, g]
                ti = tw_ref[h1 + k, g]
                outs.append((pr * tr - pi * ti, pr * ti + pi * tr))
            dst = pl.ds(pl.multiple_of(i * 16, 16), 16)
            a_ref[0, k, 0, dst, :] = jnp.concatenate([outs[0][0], outs[1][0]], axis=0).astype(BF16)
            a_ref[0, k, 1, dst, :] = jnp.concatenate([outs[0][1], outs[1][1]], axis=0).astype(BF16)
        return carry

    lax.fori_loop(0, rows // 16, body, 0)


def _fft_a(x4, plan, col_off=0, ncols=None):
    bq, nh, n2, width = x4.shape
    ncols = width if ncols is None else ncols
    cb0 = col_off // FFT_COLS
    rows = plan.rows
    kern = functools.partial(_fft_a_kernel, h1=plan.h1, nh=nh, rows=rows)
    return pl.pallas_call(
        kern,
        grid=(bq, ncols // FFT_COLS, n2 // rows),
        in_specs=[pl.BlockSpec(memory_space=pltpu.SMEM),
                  pl.BlockSpec((1, nh, rows, FFT_COLS), lambda b, c, r: (b, 0, r, cb0 + c)),
                  pl.BlockSpec(plan.fa.shape, lambda b, c, r: (0, 0))],
        out_specs=pl.BlockSpec((1, plan.h1, 2, rows, FFT_COLS), lambda b, c, r: (b, 0, 0, r, c)),
        out_shape=jax.ShapeDtypeStruct((bq, plan.h1, 2, n2, ncols), BF16),
        compiler_params=_cparams(("parallel", "parallel", "parallel"), 40),
        name="fft_outer_fwd",
    )(plan.tw, x4, plan.fa)


def _fft_mid_kernel(a_ref, kf_ref, gb_ref, gbi_ref, o_ref, *, n2):
    kb = a_ref.shape[1]
    cols = a_ref.shape[-1]
    for k in range(kb):
        y = _dot(gb_ref[...], a_ref[0, k].reshape(2 * n2, cols))
        yr, yi = y[0:n2, :], y[n2:2 * n2, :]
        kr, ki = kf_ref[0, k, 0], kf_ref[0, k, 1]
        z = jnp.concatenate([yr * kr - yi * ki, yr * ki + yi * kr], axis=0).astype(BF16)
        o_ref[0, k] = _dot(gbi_ref[...], z).reshape(2, n2, cols).astype(BF16)


def _fft_mid(a, kf, order, plan):
    bq, h1, _, n2, width = a.shape
    kern = functools.partial(_fft_mid_kernel, n2=n2)
    blk = (1, FFT_KB, 2, n2, FFT_COLS)
    return pl.pallas_call(
        kern,
        grid=(bq, width // FFT_COLS, h1 // FFT_KB),
        in_specs=[pl.BlockSpec(blk, lambda b, c, k: (b, k, 0, 0, c)),
                  pl.BlockSpec(blk, lambda b, c, k: (order, k, 0, 0, c)),
                  pl.BlockSpec(plan.gb.shape, lambda b, c, k: (0, 0)),
                  pl.BlockSpec(plan.gbi.shape, lambda b, c, k: (0, 0))],
        out_specs=pl.BlockSpec(blk, lambda b, c, k: (b, k, 0, 0, c)),
        out_shape=jax.ShapeDtypeStruct(a.shape, BF16),
        compiler_params=_cparams(("parallel", "parallel", "parallel"), 40),
        name="fft_inner_mul",
    )(a, kf, plan.gb, plan.gbi)


def _fft_c_kernel(tw_ref, b_ref, u_ref, g_ref, bias_ref, fc_ref, o_ref, *, h1, nh, rows):
    rb = pl.program_id(2)
    cols = o_ref.shape[-1]

    def body(i, carry):
        src = pl.ds(pl.multiple_of(i * 16, 16), 16)
        tiles = [[b_ref[0, k, p, src, :].astype(F32) for p in range(2)] for k in range(h1)]
        for half in range(2):
            g = rb * (rows // 8) + 2 * i + half
            re_rows, im_rows = [], []
            for k in range(h1):
                br = tiles[k][0][half * 8:(half + 1) * 8, :]
                bi = tiles[k][1][half * 8:(half + 1) * 8, :]
                tr = tw_ref[k, g]
                ti = tw_ref[h1 + k, g]
                re_rows.append(br * tr + bi * ti)
                im_rows.append(bi * tr - br * ti)
            s = jnp.concatenate(re_rows + im_rows, axis=0).astype(BF16)
            y = _dot(fc_ref[...], s).reshape(nh, 8, cols)
            dst = pl.ds(pl.multiple_of((2 * i + half) * 8, 8), 8)
            u = u_ref[0, :, dst, :]
            o_ref[0, :, dst, :] = (y + u * bias_ref[...]) * g_ref[0, :, dst, :]
        return carry

    lax.fori_loop(0, rows // 16, body, 0)


def _fft_c(bm, u4, u_off, g4, g_off, bias, plan):
    bq, h1, _, n2, width = bm.shape
    nh = plan.nh
    rows = plan.rows
    ub, gbk = u_off // FFT_COLS, g_off // FFT_COLS
    kern = functools.partial(_fft_c_kernel, h1=h1, nh=nh, rows=rows)
    xblk = (1, nh, rows, FFT_COLS)
    return pl.pallas_call(
        kern,
        grid=(bq, width // FFT_COLS, n2 // rows),
        in_specs=[pl.BlockSpec(memory_space=pltpu.SMEM),
                  pl.BlockSpec((1, h1, 2, rows, FFT_COLS), lambda b, c, r: (b, 0, 0, r, c)),
                  pl.BlockSpec(xblk, lambda b, c, r: (b, 0, r, ub + c)),
                  pl.BlockSpec(xblk, lambda b, c, r: (b, 0, r, gbk + c)),
                  pl.BlockSpec((1, FFT_COLS), lambda b, c, r: (0, c)),
                  pl.BlockSpec(plan.fc.shape, lambda b, c, r: (0, 0))],
        out_specs=pl.BlockSpec(xblk, lambda b, c, r: (b, 0, r, c)),
        out_shape=jax.ShapeDtypeStruct((bq, nh, n2, width), F32),
        compiler_params=_cparams(("parallel", "parallel", "parallel"), 48),
        name="fft_outer_inv",
    )(plan.tw, bm, u4, g4, bias, plan.fc)


def _hyfilt_gen_kernel(w1_ref, b1_ref, w2_ref, b2_ref, w3_ref, fr_ref, o_ref, ss_ref, *, seq):
    rblk = pl.program_id(0)
    tile, cols = o_ref.shape[1], o_ref.shape[2]
    irow = lax.broadcasted_iota(jnp.int32, (tile, V7X_LANES), 0) + rblk * tile
    row = irow.astype(F32)
    lane = lax.broadcasted_iota(jnp.int32, (tile, V7X_LANES), 1)
    t = row / (seq - 1.0)
    omega = (2.0 * math.pi / seq) * row
    band_step = (HY_BANDS - 1 - 1e-4) / (HY_BANDS - 1)
    is_cos = (lane >= 1) & (lane <= HY_BANDS)
    is_sin = (lane > HY_BANDS) & (lane <= 2 * HY_BANDS)
    bidx = jnp.where(is_cos, lane - 1, lane - 1 - HY_BANDS).astype(F32)
    ang = omega * (1e-4 + band_step * bidx)
    trig = jnp.cos(ang + jnp.where(is_sin, 0.5 * math.pi, 0.0))
    z = jnp.where(lane == 0, t, jnp.where(is_cos | is_sin, trig, 0.0))
    fr = fr_ref[...]
    hid = jnp.sin(fr * (_dot3(z, w1_ref[...]) + b1_ref[...]))
    hid = jnp.sin(fr * (_dot3(hid, w2_ref[...]) + b2_ref[...]))
    filt = _dot3(hid, w3_ref[...])
    col = lax.broadcasted_iota(jnp.int32, (1, cols), 1)
    chan = (col % MIX_WIDTH).astype(F32)
    max_decay = math.log(HY_DECAY_TARGET) / HY_FAST_PCT
    min_decay = math.log(HY_DECAY_TARGET) / HY_SLOW_PCT
    delta = jnp.abs(min_decay + (max_decay - min_decay) / (MIX_WIDTH - 1) * chan)
    filt = filt * jnp.exp(-t[:, 0:1] * delta)
    is_bwd = (col // MIX_WIDTH) % 2 == 1
    rows_c = lax.broadcasted_iota(jnp.int32, (tile, cols), 0) + rblk * tile
    filt = jnp.where(is_bwd & (rows_c == seq - 1), 0.0, filt)
    o_ref[0] = filt

    @pl.when(rblk == 0)
    def _():
        ss_ref[...] = jnp.zeros_like(ss_ref)

    ss_ref[...] += jnp.sum(filt * filt, axis=0, keepdims=True)


def _hyfilt_gen(seq, w1p, b1, w2, b2, w3, freq):
    ncol = w3.shape[1]
    tile = min(seq, 512)
    const = lambda shape: pl.BlockSpec(shape, lambda r: (0,) * len(shape))
    kern = functools.partial(_hyfilt_gen_kernel, seq=seq)
    return pl.pallas_call(
        kern,
        grid=(seq // tile,),
        in_specs=[const(w1p.shape), const(b1.shape), const(w2.shape), const(b2.shape), const(w3.shape),
                  const(freq.shape)],
        out_specs=[pl.BlockSpec((1, tile, ncol), lambda r: (0, r, 0)),
                   pl.BlockSpec((1, ncol), lambda r: (0, 0))],
        out_shape=[jax.ShapeDtypeStruct((1, seq, ncol), F32), jax.ShapeDtypeStruct((1, ncol), F32)],
        compiler_params=_cparams(("arbitrary",), 40),
        name="hyena_filter_gen",
    )(w1p, b1, w2, b2, w3, freq)


def _hyfilt_spec_kernel(tw1_ref, a0_ref, a1_ref, ss0_ref, ss1_ref, gb_ref, o_ref, *, n2):
    kblk = pl.program_id(2)
    kb = a0_ref.shape[1]
    cols = a0_ref.shape[-1]
    h1 = tw1_ref.shape[1]
    scale = lax.rsqrt(ss0_ref[...] + ss1_ref[...] + EPS)
    ang2 = (-2.0 * math.pi / n2) * lax.broadcasted_iota(jnp.int32, (n2, cols), 0).astype(F32)
    cr, ci = jnp.cos(ang2), jnp.sin(ang2)
    for k in range(kb):
        y0 = _dot(gb_ref[...], a0_ref[0, k].reshape(2 * n2, cols))
        y1 = _dot(gb_ref[...], a1_ref[0, k].reshape(2 * n2, cols))
        sr = tw1_ref[0, kblk * kb + k]
        si = tw1_ref[1, kblk * kb + k]
        wr, wi = cr * sr - ci * si, cr * si + ci * sr
        y1r, y1i = y1[0:n2, :], y1[n2:2 * n2, :]
        o_ref[0, k, 0] = (y0[0:n2, :] + (wr * y1r - wi * y1i)) * scale
        o_ref[0, k, 1] = (y0[n2:2 * n2, :] - (wr * y1i + wi * y1r)) * scale


def _hyfilt_spec(af, ss, plan):
    _, h1, _, n2, _ = af.shape
    ncb = MIX_WIDTH // FFT_COLS
    blk = (1, FFT_KB, 2, n2, FFT_COLS)
    kern = functools.partial(_hyfilt_spec_kernel, n2=n2)
    return pl.pallas_call(
        kern,
        grid=(HY_ORDER, ncb, h1 // FFT_KB),
        in_specs=[pl.BlockSpec(memory_space=pltpu.SMEM),
                  pl.BlockSpec(blk, lambda o, c, k: (0, k, 0, 0, o * 2 * ncb + c)),
                  pl.BlockSpec(blk, lambda o, c, k: (0, k, 0, 0, o * 2 * ncb + ncb + c)),
                  pl.BlockSpec((1, FFT_COLS), lambda o, c, k: (0, o * 2 * ncb + c)),
                  pl.BlockSpec((1, FFT_COLS), lambda o, c, k: (0, o * 2 * ncb + ncb + c)),
                  pl.BlockSpec(plan.gb.shape, lambda o, c, k: (0, 0))],
        out_specs=pl.BlockSpec(blk, lambda o, c, k: (o, k, 0, 0, c)),
        out_shape=jax.ShapeDtypeStruct((HY_ORDER, h1, 2, n2, MIX_WIDTH), F32),
        compiler_params=_cparams(("parallel", "parallel", "parallel"), 40),
        name="hyena_filter_spectrum",
    )(plan.tw1, af, af, ss, ss, plan.gb)


def _hyena(zc, lw, plan):
    bsz, seq, _ = zc.shape
    filt, ss = _hyfilt_gen(seq, lw["hy_w1p"], lw["hy_b1"], lw["hy_w2"], lw["hy_b2"], lw["hy_w3"], lw["hy_freq"])
    af = _fft_a(filt.reshape(1, plan.nh, plan.n2, filt.shape[-1]), plan)
    kf = _hyfilt_spec(af, ss, plan)
    zc4 = zc.reshape(bsz, plan.nh, plan.n2, 3 * MIX_WIDTH)
    a = _fft_a(zc4, plan, col_off=0, ncols=MIX_WIDTH)
    bm = _fft_mid(a, kf, 0, plan)
    z1 = _fft_c(bm, zc4, 0, zc4, MIX_WIDTH, lw["hy_bias"][0:1], plan)
    a = _fft_a(z1, plan)
    bm = _fft_mid(a, kf, 1, plan)
    z2 = _fft_c(bm, z1, 0, zc4, 2 * MIX_WIDTH, lw["hy_bias"][1:2], plan)
    return z2.reshape(bsz, seq, MIX_WIDTH)


def _gla_kernel(*refs, reverse):
    if reverse:
        x_ref, of_ref, wla_ref, bla_ref, ng_ref, o_ref, st_ref = refs
    else:
        x_ref, wla_ref, bla_ref, o_ref, st_ref = refs
    c = pl.program_id(1)

    @pl.when(c == 0)
    def _():
        st_ref[...] = jnp.zeros_like(st_ref)

    for bb in range(x_ref.shape[0]):
        _gla_sequence(bb, x_ref, of_ref if reverse else None, wla_ref, bla_ref, ng_ref if reverse else None, o_ref,
                      st_ref, reverse)


def _gla_sequence(bb, x_ref, of_ref, wla_ref, bla_ref, ng_ref, o_ref, st_ref, reverse):
    tile = GLA_TILE
    ck = GLA_CHUNK
    nck = tile // ck
    hw = V7X_LANES
    nh = GLA_HEADS
    q = x_ref[bb, :, 0:nh * hw] * (GLA_DK ** -0.5)
    k = x_ref[bb, :, nh * hw:2 * nh * hw]
    v = x_ref[bb, :, 2 * nh * hw:3 * nh * hw]
    lr = x_ref[bb, :, 4 * nh * hw:4 * nh * hw + hw]
    zl = _dot(lr.astype(BF16), wla_ref[...]) + bla_ref[...]
    la = (jnp.minimum(zl, 0.0) - jnp.log(1.0 + jnp.exp(-jnp.abs(zl)))) / GLA_TAU

    row = lax.broadcasted_iota(jnp.int32, la.shape, 0) % ck
    bcum = la
    d = 1
    while d < ck:
        if reverse:
            bcum = bcum + jnp.where(row < ck - d, pltpu.roll(bcum, tile - d, 0), 0.0)
        else:
            bcum = bcum + jnp.where(row >= d, pltpu.roll(bcum, d, 0), 0.0)
        d *= 2
    b3 = bcum.reshape(nck, ck, nh * hw)
    blast = b3[:, 0:1, :] if reverse else b3[:, ck - 1:ck, :]
    q_e = (q * jnp.exp(bcum)).astype(BF16)
    k_e = (k * jnp.exp(-bcum)).astype(BF16)
    k_d = (k.reshape(nck, ck, nh * hw) * jnp.exp(blast - b3)).reshape(tile, nh * hw).astype(BF16)
    gch = jnp.exp(blast)
    vb = v.astype(BF16)

    ri = lax.broadcasted_iota(jnp.int32, (ck, ck), 0)
    ci = lax.broadcasted_iota(jnp.int32, (ck, ck), 1)
    mask = (ri <= ci) if reverse else (ri >= ci)
    order = range(nck - 1, -1, -1) if reverse else range(nck)
    outs = [None] * nck
    for n in order:
        rs = slice(n * ck, (n + 1) * ck)
        heads = []
        for h in range(nh):
            ls = slice(h * hw, (h + 1) * hw)
            qe, ke, kd, vh = q_e[rs, ls], k_e[rs, ls], k_d[rs, ls], vb[rs, ls]
            st = st_ref[bb, h]
            sc = lax.dot_general(qe, ke, (((1,), (1,)), ((), ())), preferred_element_type=F32)
            sc = jnp.where(mask, sc, 0.0).astype(BF16)
            o = _dot(sc, vh) + lax.dot_general(qe, st.astype(BF16), (((1,), (1,)), ((), ())),
                                               preferred_element_type=F32)
            upd = lax.dot_general(vh, kd, (((0,), (0,)), ((), ())), preferred_element_type=F32)
            st_ref[bb, h] = st * gch[n, :, ls] + upd
            heads.append(o)
        outs[n] = jnp.concatenate(heads, axis=1)
    o_dir = jnp.concatenate(outs, axis=0)
    if not reverse:
        o_ref[bb] = o_dir
        return
    o = of_ref[bb] + o_dir
    g = x_ref[bb, :, 3 * nh * hw:4 * nh * hw]
    normed = []
    for h in range(nh):
        oh = o[:, h * hw:(h + 1) * hw]
        normed.append(oh * lax.rsqrt(jnp.mean(oh * oh, axis=-1, keepdims=True) + EPS))
    o = jnp.concatenate(normed, axis=1) * ng_ref[...]
    o_ref[bb] = o * _silu(g)


def _gla(gla_in, wla, bla, norm_g):
    bsz, seq, width = gla_in.shape
    tile = GLA_TILE
    nc = seq // tile
    nb = min(GLA_BATCH, bsz)
    const = lambda shape: pl.BlockSpec(shape, lambda b, c: (0,) * len(shape))
    out_shape = jax.ShapeDtypeStruct((bsz, seq, MIX_WIDTH), F32)
    scratch = [pltpu.VMEM((nb, GLA_HEADS, GLA_DV, V7X_LANES), F32)]
    o_f = pl.pallas_call(
        functools.partial(_gla_kernel, reverse=False),
        grid=(bsz // nb, nc),
        in_specs=[pl.BlockSpec((nb, tile, width), lambda b, c: (b, c, 0)), const(wla.shape[1:]), const(bla.shape[1:])],
        out_specs=pl.BlockSpec((nb, tile, MIX_WIDTH), lambda b, c: (b, c, 0)),
        out_shape=out_shape,
        scratch_shapes=scratch,
        compiler_params=_cparams(("parallel", "arbitrary"), 48),
        name="gla_fwd",
    )(gla_in, wla[0], bla[0])
    return pl.pallas_call(
        functools.partial(_gla_kernel, reverse=True),
        grid=(bsz // nb, nc),
        in_specs=[pl.BlockSpec((nb, tile, width), lambda b, c: (b, nc - 1 - c, 0)),
                  pl.BlockSpec((nb, tile, MIX_WIDTH), lambda b, c: (b, nc - 1 - c, 0)),
                  const(wla.shape[1:]), const(bla.shape[1:]), const(norm_g.shape)],
        out_specs=pl.BlockSpec((nb, tile, MIX_WIDTH), lambda b, c: (b, nc - 1 - c, 0)),
        out_shape=out_shape,
        scratch_shapes=scratch,
        compiler_params=_cparams(("parallel", "arbitrary"), 48),
        name="gla_bwd",
    )(gla_in, o_f, wla[1], bla[1], norm_g)


def _s5_kernel(u_ref, t_ref, e_ref, o_ref_w, mu_ref, mup_ref, y_ref):
    ch = S5_CHUNK
    sub = V7X_SUBLANES
    nrow = u_ref.shape[1] // ch
    half = S5_LANE_GROUPS * 2 * S5_STATE
    swap = lambda x: pltpu.roll(x, half // 2, 1)
    u = jnp.concatenate([u_ref[0, pl.ds(i, nrow, stride=ch), :] for i in range(ch)], axis=1).astype(BF16)
    y = _dot(u, t_ref[0])
    he = _dot(u, e_ref[0])
    row = lax.broadcasted_iota(jnp.int32, (nrow, half), 0)
    rsub = row % sub
    states = []
    for d in range(2):
        h = he[:, d * half:(d + 1) * half]
        for s in range(S5_SCAN_STEPS):
            step = 2 ** s
            if d == 0:
                hs = jnp.where(rsub >= step, pltpu.roll(h, step, 0), 0.0)
            else:
                hs = jnp.where(rsub < sub - step, pltpu.roll(h, nrow - step, 0), 0.0)
            h = h + hs * mu_ref[0, d, s, 0:1, :] + swap(hs) * mu_ref[0, d, s, 1:2, :]
        ngroup = nrow // sub
        carry = jnp.zeros((1, half), F32)
        out = [None] * ngroup
        for g in (range(ngroup) if d == 0 else range(ngroup - 1, -1, -1)):
            hg = h[g * sub:(g + 1) * sub, :] + carry * mup_ref[0, d, 0] + swap(carry) * mup_ref[0, d, 1]
            carry = hg[sub - 1:sub, :] if d == 0 else hg[0:1, :]
            out[g] = hg
        h = jnp.concatenate(out, axis=0)
        if d == 0:
            h = jnp.where(row >= 1, pltpu.roll(h, 1, 0), 0.0)
        else:
            h = jnp.where(row < nrow - 1, pltpu.roll(h, nrow - 1, 0), 0.0)
        states.append(h)
    hp = jnp.concatenate(states, axis=1).astype(BF16)
    y = y + lax.dot_general(hp, o_ref_w[0], (((1,), (1,)), ((), ())), preferred_element_type=F32)
    for j in range(ch):
        y_ref[0, pl.ds(j, nrow, stride=ch), :] = y[:, j * V7X_LANES:(j + 1) * V7X_LANES]


def _s5(s5_in, tblk, eblk, oblk, mu, mup):
    bsz, seq, width = s5_in.shape
    nb = width // V7X_LANES
    once = pl.Buffered(1)
    wspec = lambda arr: pl.BlockSpec((1,) + arr.shape[1:], lambda k, b: (k,) + (0,) * (arr.ndim - 1),
                                     pipeline_mode=once)
    xspec = pl.BlockSpec((1, seq, V7X_LANES), lambda k, b: (b, 0, k))
    return pl.pallas_call(
        _s5_kernel,
        grid=(nb, bsz),
        in_specs=[xspec, wspec(tblk), wspec(eblk), wspec(oblk), wspec(mu), wspec(mup)],
        out_specs=pl.BlockSpec((1, seq, V7X_LANES), lambda k, b: (b, 0, k)),
        out_shape=jax.ShapeDtypeStruct(s5_in.shape, F32),
        compiler_params=_cparams(("arbitrary", "arbitrary"), 60),
        name="s5",
    )(s5_in, tblk, eblk, oblk, mu, mup)


def _s5_tables(lam_re, lam_im, log_dt, b_re, b_im, c_re, c_im):
    ch = S5_CHUNK
    hi = lax.Precision.HIGHEST
    cmul = lambda x, y: (x[0] * y[0] - x[1] * y[1], x[0] * y[1] + x[1] * y[0])
    lr_, li_ = lam_re.astype(F32), lam_im.astype(F32)
    dt = jnp.exp(log_dt.astype(F32))[..., None]
    ar, ai = lr_ * dt, li_ * dt

    def lam_pow(tau):
        t = jnp.asarray(tau, F32)
        t = t.reshape(t.shape + (1,) * 3)
        mag = jnp.exp(t * ar)
        return mag * jnp.cos(t * ai), mag * jnp.sin(t * ai)

    lb = lam_pow(jnp.ones((), F32))
    num = (lb[0] - 1.0, lb[1])
    den = lr_ * lr_ + li_ * li_
    ratio = ((num[0] * lr_ + num[1] * li_) / den, (num[1] * lr_ - num[0] * li_) / den)
    b_bar = cmul((ratio[0][..., None], ratio[1][..., None]), (b_re.astype(F32), b_im.astype(F32)))
    cc = (c_re.astype(F32), c_im.astype(F32))

    nb, gb, hh, pp = S5_GROUPS // S5_LANE_GROUPS, S5_LANE_GROUPS, S5_GROUP, S5_STATE
    eye = jnp.eye(gb, dtype=F32)

    def b_base(x):
        x = jnp.transpose(x.reshape(2, nb, gb, pp, hh), (1, 2, 4, 0, 3))
        return (x[:, :, :, :, None, :] * eye[None, :, None, None, :, None]).reshape(nb, gb * hh, 2 * gb * pp)

    def c_base(x):
        x = jnp.transpose(x.reshape(2, nb, gb, hh, pp), (1, 2, 3, 0, 4))
        return (x[:, :, :, :, None, :] * eye[None, :, None, None, :, None]).reshape(nb, gb * hh, 2 * gb * pp)

    row = lambda x: jnp.transpose(x.reshape(2, nb, gb * pp), (1, 0, 2)).reshape(nb, 1, 2 * gb * pp)
    base = (b_base(b_bar[0]), b_base(b_bar[1]), c_base(cc[0]), c_base(cc[1]), row(ar), row(ai))
    dtab = _s5_lag_tables(*base)
    return _s5_block_tables(dtab, *base)


def _lam_pow(tau, ar, ai):
    mag = jnp.exp(tau * ar)
    return mag * jnp.cos(tau * ai), mag * jnp.sin(tau * ai)


def _s5_lag_kernel(btr_ref, bti_ref, ctr_ref, cti_ref, ar_ref, ai_ref, d_ref):
    ch = S5_CHUNK
    half = btr_ref.shape[-1] // 2
    def split(x):
        hi = x.astype(BF16)
        return hi, (x - hi.astype(F32)).astype(BF16)

    def tdot(a, b_split):
        ah, al = split(a)
        bh, bl = b_split
        return lax.dot_general(jnp.concatenate([ah, ah, al], axis=1), jnp.concatenate([bh, bl, bh], axis=1),
                               (((1,), (1,)), ((), ())), preferred_element_type=F32)

    for d in range(2):
        ls = slice(d * half, (d + 1) * half)
        btr, bti = btr_ref[0, :, ls], bti_ref[0, :, ls]
        ctr, cti = split(ctr_ref[0, :, ls]), split(cti_ref[0, :, ls])
        for lag in range(ch):
            lr, li = _lam_pow(float(lag), ar_ref[0, :, ls], ai_ref[0, :, ls])
            val = tdot(btr * lr - bti * li, ctr) - tdot(btr * li + bti * lr, cti)
            idx = ch - 1 + lag if d == 0 else ch - 1 - lag
            if d == 1 and lag == 0:
                d_ref[0, idx] = d_ref[0, idx] + val
            else:
                d_ref[0, idx] = val


def _s5_lag_tables(btr, bti, ctr, cti, ar, ai):
    nb = btr.shape[0]
    spec = lambda a: pl.BlockSpec((1,) + a.shape[1:], lambda k: (k, 0, 0))
    nlag = 2 * S5_CHUNK - 1
    return pl.pallas_call(
        _s5_lag_kernel,
        grid=(nb,),
        in_specs=[spec(a) for a in (btr, bti, ctr, cti, ar, ai)],
        out_specs=pl.BlockSpec((1, nlag, V7X_LANES, V7X_LANES), lambda k: (k, 0, 0, 0)),
        out_shape=jax.ShapeDtypeStruct((nb, nlag, V7X_LANES, V7X_LANES), F32),
        compiler_params=_cparams(("parallel",), 32),
        name="s5_lag_tables",
    )(btr, bti, ctr, cti, ar, ai)


def _s5_block_kernel(d_ref, btr_ref, bti_ref, ctr_ref, cti_ref, ar_ref, ai_ref, t_ref, e_ref, ot_ref, mu_ref,
                     mup_ref):
    ch = S5_CHUNK
    i = pl.program_id(1)
    half = btr_ref.shape[-1] // 2
    fi = i.astype(F32)
    for j in range(ch):
        t_ref[0, :, j * V7X_LANES:(j + 1) * V7X_LANES] = d_ref[0, j - i + ch - 1].astype(BF16)
    for d in range(2):
        ls = slice(d * half, (d + 1) * half)
        ar, ai = ar_ref[0, :, ls], ai_ref[0, :, ls]
        lr, li = _lam_pow(fi if d == 1 else (ch - 1.0) - fi, ar, ai)
        btr, bti = btr_ref[0, :, ls], bti_ref[0, :, ls]
        e_ref[0, :, 2 * d * half:(2 * d + 1) * half] = (btr * lr - bti * li).astype(BF16)
        e_ref[0, :, (2 * d + 1) * half:(2 * d + 2) * half] = (btr * li + bti * lr).astype(BF16)
        lr, li = _lam_pow(fi + 1.0 if d == 0 else ch - fi, ar, ai)
        ctr, cti = ctr_ref[0, :, ls], cti_ref[0, :, ls]
        ot_ref[0, :, 2 * d * half:(2 * d + 1) * half] = (ctr * lr - cti * li).astype(BF16)
        ot_ref[0, :, (2 * d + 1) * half:(2 * d + 2) * half] = (-(ctr * li + cti * lr)).astype(BF16)

    @pl.when(i == 0)
    def _():
        for d in range(2):
            ar, ai = ar_ref[0, :, d * half:(d + 1) * half], ai_ref[0, :, d * half:(d + 1) * half]
            for s in range(S5_SCAN_STEPS):
                lr, li = _lam_pow(float(ch * 2 ** s), ar, ai)
                mu_ref[0, d, s, 0:1, :] = jnp.concatenate([lr, lr], axis=1)
                mu_ref[0, d, s, 1:2, :] = jnp.concatenate([-li, li], axis=1)
            for r in range(V7X_SUBLANES):
                lr, li = _lam_pow(float(ch * (r + 1 if d == 0 else V7X_SUBLANES - r)), ar, ai)
                mup_ref[0, d, 0, r:r + 1, :] = jnp.concatenate([lr, lr], axis=1)
                mup_ref[0, d, 1, r:r + 1, :] = jnp.concatenate([-li, li], axis=1)


def _s5_block_tables(dtab, btr, bti, ctr, cti, ar, ai):
    nb = btr.shape[0]
    ch = S5_CHUNK
    big = ch * V7X_LANES
    wide = 2 * btr.shape[-1]
    spec = lambda a: pl.BlockSpec((1,) + a.shape[1:], lambda k, i: (k,) + (0,) * (a.ndim - 1))
    tile = lambda w: pl.BlockSpec((1, V7X_LANES, w), lambda k, i: (k, i, 0))
    return pl.pallas_call(
        _s5_block_kernel,
        grid=(nb, ch),
        in_specs=[spec(a) for a in (dtab, btr, bti, ctr, cti, ar, ai)],
        out_specs=[tile(big), tile(wide), tile(wide),
                   pl.BlockSpec((1, 2, S5_SCAN_STEPS, 2, wide // 2), lambda k, i: (k, 0, 0, 0, 0)),
                   pl.BlockSpec((1, 2, 2, V7X_SUBLANES, wide // 2), lambda k, i: (k, 0, 0, 0, 0))],
        out_shape=[jax.ShapeDtypeStruct((nb, big, big), BF16), jax.ShapeDtypeStruct((nb, big, wide), BF16),
                   jax.ShapeDtypeStruct((nb, big, wide), BF16),
                   jax.ShapeDtypeStruct((nb, 2, S5_SCAN_STEPS, 2, wide // 2), F32),
                   jax.ShapeDtypeStruct((nb, 2, 2, V7X_SUBLANES, wide // 2), F32)],
        compiler_params=_cparams(("parallel", "arbitrary"), 32),
        name="s5_block_tables",
    )(dtab, btr, bti, ctr, cti, ar, ai)


def _merge_kernel(x_ref, ng_ref, yaf_ref, yab_ref, yb_ref, yc_ref, y5_ref, u5_ref, d5_ref, gluw_ref, glub_ref,
                  wgate_ref, bgate_ref, wbr_ref, wout_ref, fng_ref, wr_ref, br_ref, o_ref, g_ref):
    x = x_ref[...]
    h = _rms(x, ng_ref[...]).astype(BF16)
    y_d = _gelu_tanh(u5_ref[...] * d5_ref[...] + y5_ref[...])
    y_d = y_d * _sigmoid(_dot(y_d.astype(BF16), gluw_ref[...]) + glub_ref[...])
    branches = (yaf_ref[...] + yab_ref[...], yb_ref[...], yc_ref[...], y_d)
    merged = jnp.zeros(x.shape, F32)
    for i, y in enumerate(branches):
        gate = _sigmoid(_dot(h, wgate_ref[i]) + bgate_ref[i])
        merged = merged + gate * _dot(y.astype(BF16), wbr_ref[i])
    x_new = x + _dot(merged.astype(BF16), wout_ref[...])
    o_ref[...] = x_new
    g_ref[...] = _top_group(_router_logits(x_new, fng_ref, wr_ref, br_ref)[1])


def _merge(xf, lw, yaf, yab, yb, yc, y5, u5):
    t = xf.shape[0]
    tok = lambda w: pl.BlockSpec((TOK_TILE, w), lambda i: (i, 0))
    const = lambda arr: pl.BlockSpec(arr.shape, lambda i: (0,) * arr.ndim, pipeline_mode=pl.Buffered(1))
    weights = [lw["s5_d"], lw["s5_glu_w"], lw["s5_glu_b"], lw["w_gate"], lw["b_gate"], lw["w_branch"], lw["w_out"],
               lw["norm_ffn_g"], lw["w_router"], lw["b_router"]]
    return pl.pallas_call(
        _merge_kernel,
        grid=(t // TOK_TILE,),
        in_specs=[tok(D_MODEL), const(lw["norm_mix_g"])] + [tok(MIX_WIDTH)] * 6 + [const(w) for w in weights],
        out_specs=[tok(D_MODEL), tok(1)],
        out_shape=[jax.ShapeDtypeStruct((t, D_MODEL), F32), jax.ShapeDtypeStruct((t, 1), jnp.int32)],
        compiler_params=_cparams(("parallel",), 56),
        name="merge",
    )(xf, lw["norm_mix_g"], yaf, yab, yb, yc, y5, u5, *weights)


def _router_logits(x, ng_ref, wr_ref, br_ref):
    hf = _rms(x, ng_ref[...])
    return hf, _dot3(hf, wr_ref[...]) + br_ref[...]


def _top_group(logits):
    lane = lax.broadcasted_iota(jnp.int32, logits.shape, 1).astype(F32)
    gl = jnp.where(lane < MOE_GROUPS, logits, -jnp.inf)
    gmax = jnp.max(gl, axis=1, keepdims=True)
    return jnp.min(jnp.where(gl == gmax, lane, float(V7X_LANES)), axis=1, keepdims=True).astype(jnp.int32)


def _moe_plan_kernel(pos_ref, src_ref):
    def clear(p, c):
        src_ref[p] = 0
        return c

    def place(t, c):
        src_ref[pos_ref[t]] = t
        return c

    lax.fori_loop(0, src_ref.shape[0], clear, 0, unroll=16)
    lax.fori_loop(0, pos_ref.shape[0], place, 0, unroll=16)


def _moe_plan(gidx, tile):
    t = gidx.shape[0]
    ntile = t // tile + MOE_GROUPS
    g = gidx.reshape(t)
    onehot = (g[:, None] == jnp.arange(MOE_GROUPS, dtype=jnp.int32)[None, :]).astype(jnp.int32)
    csum = jnp.cumsum(onehot, axis=0)
    rank = jnp.sum(onehot * (csum - 1), axis=1)
    count = csum[-1]
    gtiles = (count + tile - 1) // tile
    first = jnp.cumsum(gtiles) - gtiles
    pos = jnp.sum(onehot * first[None, :], axis=1) * tile + rank
    tid = jnp.arange(ntile, dtype=jnp.int32)
    tgroup = jnp.minimum(jnp.sum((tid[:, None] >= (first + gtiles)[None, :]).astype(jnp.int32), axis=1),
                         MOE_GROUPS - 1)
    oh_t = (tgroup[:, None] == jnp.arange(MOE_GROUPS, dtype=jnp.int32)[None, :]).astype(jnp.int32)
    tvalid = jnp.clip(jnp.sum(oh_t * count[None, :], axis=1) - (tid - jnp.sum(oh_t * first[None, :], axis=1)) * tile,
                      0, tile)
    src = pl.pallas_call(
        _moe_plan_kernel,
        in_specs=[pl.BlockSpec(memory_space=pltpu.SMEM)],
        out_specs=pl.BlockSpec(memory_space=pltpu.SMEM),
        out_shape=jax.ShapeDtypeStruct((ntile * tile,), jnp.int32),
        name="moe_plan",
    )(pos.astype(jnp.int32))
    return src, tgroup.astype(jnp.int32), tvalid.astype(jnp.int32)


def _moe_expert_kernel(src_ref, tg_ref, nv_ref, x_hbm, ng_ref, wr_ref, br_ref, w13_ref, w2_ref, fg_ref, o_hbm,
                       xbuf, ybuf, sem, *, final):
    i = pl.program_id(0)
    nt = pl.num_programs(0)
    tile = xbuf.shape[1]
    slot = i % 2
    nv = nv_ref[i]
    ng, ne, ff = MOE_GROUPS, MOE_EXPERTS, MOE_FF

    def rows(j, wait, row, whole):
        count = nv_ref[j]

        @pl.when(count == tile)
        def _():
            if wait:
                whole().wait()
            else:
                def body(r, c):
                    row(j * tile + r, r).start()
                    return c
                lax.fori_loop(0, tile, body, 0, unroll=8)

        @pl.when((count > 0) & (count < tile))
        def _():
            def body(r, c):
                cp = row(j * tile + r, r)
                cp.wait() if wait else cp.start()
                return c
            lax.fori_loop(0, count, body, 0)

    def gather(j, s, wait):
        rows(j, wait,
             lambda p, r: pltpu.make_async_copy(x_hbm.at[pl.ds(src_ref[p], 1), :], xbuf.at[s, pl.ds(r, 1), :],
                                                sem.at[0, s]),
             lambda: pltpu.make_async_copy(x_hbm.at[pl.ds(0, tile), :], xbuf.at[s], sem.at[0, s]))

    def scatter(j, s, wait):
        rows(j, wait,
             lambda p, r: pltpu.make_async_copy(ybuf.at[s, pl.ds(r, 1), :], o_hbm.at[pl.ds(src_ref[p], 1), :],
                                                sem.at[1, s]),
             lambda: pltpu.make_async_copy(ybuf.at[s], o_hbm.at[pl.ds(0, tile), :], sem.at[1, s]))

    @pl.when(i == 0)
    def _():
        xbuf[...] = jnp.zeros_like(xbuf)
        gather(0, 0, False)

    @pl.when(i + 1 < nt)
    def _():
        gather(i + 1, 1 - slot, False)

    @pl.when(i >= 2)
    def _():
        scatter(i - 2, slot, True)

    gather(i, slot, True)

    @pl.when(nv > 0)
    def _():
        x = xbuf[slot]
        hf, logits = _router_logits(x, ng_ref, wr_ref, br_ref)
        lane = lax.broadcasted_iota(jnp.int32, logits.shape, 1).astype(F32)
        neg = -jnp.inf
        big = float(V7X_LANES)
        grp = tg_ref[i].astype(F32)
        gl = jnp.where(lane < ng, logits, neg)
        gmax = jnp.max(gl, axis=1, keepdims=True)
        glog = jnp.sum(jnp.where(lane == grp, logits, 0.0), axis=1, keepdims=True)
        gprob = jnp.exp(glog - gmax) / jnp.sum(jnp.exp(gl - gmax), axis=1, keepdims=True)
        lo = ng + ne * grp
        sel = (lane >= lo) & (lane < lo + ne)
        m1 = jnp.max(jnp.where(sel, logits, neg), axis=1, keepdims=True)
        i1 = jnp.min(jnp.where(sel & (logits == m1), lane, big), axis=1, keepdims=True)
        sel2 = sel & (lane != i1)
        m2 = jnp.max(jnp.where(sel2, logits, neg), axis=1, keepdims=True)
        i2 = jnp.min(jnp.where(sel2 & (logits == m2), lane, big), axis=1, keepdims=True)
        e2 = jnp.exp(m2 - m1)
        w1 = gprob / (1.0 + e2)
        w2 = gprob * e2 / (1.0 + e2)
        gu = _dot(hf.astype(BF16), w13_ref[0])
        act = _silu(gu[:, 0:ne * ff]) * gu[:, ne * ff:2 * ne * ff]
        parts = []
        for e in range(ne):
            wcol = jnp.where(i1 == lo + e, w1, 0.0) + jnp.where(i2 == lo + e, w2, 0.0)
            parts.append(act[:, e * ff:(e + 1) * ff] * wcol)
        y = x + _dot(jnp.concatenate(parts, axis=1).astype(BF16), w2_ref[0])
        if final:
            y = _rms(y, fg_ref[...])
        ybuf[slot] = y
        scatter(i, slot, False)

    @pl.when(i == nt - 1)
    def _():
        @pl.when(i >= 1)
        def _():
            scatter(i - 1, 1 - slot, True)
        scatter(i, slot, True)


def _moe(xf, gidx, lw, final_g, final):
    t = xf.shape[0]
    tile = MOE_TILE
    src, tgroup, tvalid = _moe_plan(gidx, tile)
    ntile = tgroup.shape[0]
    const = lambda arr: pl.BlockSpec(arr.shape, lambda i, s, g, n: (0,) * arr.ndim)
    bygroup = lambda arr: pl.BlockSpec((1,) + arr.shape[1:], lambda i, s, g, n: (g[i],) + (0,) * (arr.ndim - 1))
    any_space = pl.BlockSpec(memory_space=pl.ANY)
    return pl.pallas_call(
        functools.partial(_moe_expert_kernel, final=final),
        grid_spec=pltpu.PrefetchScalarGridSpec(
            num_scalar_prefetch=3,
            grid=(ntile,),
            in_specs=[any_space, const(lw["norm_ffn_g"]), const(lw["w_router"]), const(lw["b_router"]),
                      bygroup(lw["w_e_13"]), bygroup(lw["w_e_2"]), const(final_g)],
            out_specs=any_space,
            scratch_shapes=[pltpu.VMEM((2, tile, D_MODEL), F32), pltpu.VMEM((2, tile, D_MODEL), F32),
                            pltpu.SemaphoreType.DMA((2, 2))],
        ),
        out_shape=jax.ShapeDtypeStruct((t, D_MODEL), F32),
        compiler_params=_cparams(("arbitrary",), 40),
        name="moe_experts",
    )(src, tgroup, tvalid, xf, lw["norm_ffn_g"], lw["w_router"], lw["b_router"], lw["w_e_13"], lw["w_e_2"], final_g)


def _block_diag(blocks):
    nb, bs, _ = blocks.shape
    eye = jnp.eye(nb, dtype=blocks.dtype)
    return jnp.einsum("nij,nm->nimj", blocks, eye).reshape(nb * bs, nb * bs)


def _pad_heads(w, axis=-1):
    shape = w.shape[:-1] + (GLA_HEADS, GLA_DK)
    w = w.reshape(shape)
    pad = [(0, 0)] * (w.ndim - 1) + [(0, V7X_LANES - GLA_DK)]
    return jnp.pad(w, pad).reshape(w.shape[:-2] + (GLA_HEADS * V7X_LANES,))


def _prep_layer(w, l):
    f = lambda name: w[name][l]
    w_in = f("w_in")
    cuts = np.cumsum([MIX_WIDTH, MIX_WIDTH, 3 * MIX_WIDTH, GLA_HEADS * GLA_DK, GLA_HEADS * GLA_DK, MIX_WIDTH,
                      MIX_WIDTH, 2 * GLA_RANK]).tolist()
    xa, ga, hy, q, k, v, g, lr, s5 = jnp.split(w_in, cuts, axis=-1)
    lr = jnp.pad(lr, ((0, 0), (0, V7X_LANES - 2 * GLA_RANK)))
    w_pack = jnp.concatenate([xa, ga, hy, _pad_heads(q), _pad_heads(k), v, g, lr, s5], axis=-1).astype(BF16)
    assert w_pack.shape[1] == N_PACK
    lw = {"w_pack": w_pack, "norm_mix_g": f("norm_mix_g")[None]}
    lw["lru_conv_w"] = f("lru_conv_w")
    lw["lru_conv_b"] = f("lru_conv_b")[None]
    wa, wx = f("lru_wa"), f("lru_wx")
    lw["lru_wg"] = jnp.stack([jnp.concatenate([_block_diag(wa[d]), _block_diag(wx[d])], axis=1)
                              for d in range(2)]).astype(BF16)
    lw["lru_bg"] = jnp.concatenate([f("lru_ba"), f("lru_bx")], axis=-1)[:, None, :]
    lw["lru_lam"] = f("lru_lambda")[:, None, :]
    lw["hy_conv_w"] = f("hy_conv_w")
    lw["hy_conv_b"] = f("hy_conv_b")[None]
    lw["hy_w1p"] = _split3(jnp.pad(f("hy_w1"), ((0, V7X_LANES - HY_EMB), (0, 0))))
    lw["hy_b1"] = f("hy_b1")[None]
    lw["hy_w2"] = _split3(f("hy_w2"))
    lw["hy_b2"] = f("hy_b2")[None]
    lw["hy_w3"] = _split3(f("hy_w3"))
    lw["hy_freq"] = f("hy_freq")[None]
    lw["hy_bias"] = f("hy_bias")
    wg2 = _pad_heads(f("gla_wg2"))
    wla = jnp.zeros((2, V7X_LANES, GLA_HEADS * V7X_LANES), F32)
    wla = wla.at[0, 0:GLA_RANK].set(wg2[0]).at[1, GLA_RANK:2 * GLA_RANK].set(wg2[1])
    lw["gla_wla"] = wla.astype(BF16)
    lw["gla_bla"] = _pad_heads(f("gla_bg"))[:, None, :]
    lw["gla_norm_g"] = jnp.tile(f("gla_norm_g"), GLA_HEADS)[None]
    lw["s5_tables"] = _s5_tables(f("s5_lam_re"), f("s5_lam_im"), f("s5_log_dt"), f("s5_b_re"), f("s5_b_im"),
                                 f("s5_c_re"), f("s5_c_im"))
    lw["s5_d"] = f("s5_d")[None]
    lw["s5_glu_w"] = f("s5_glu_w").astype(BF16)
    lw["s5_glu_b"] = f("s5_glu_b")[None]
    lw["w_gate"] = f("w_gate").astype(BF16)
    lw["b_gate"] = f("b_gate")[:, None, :]
    lw["w_branch"] = f("w_branch").astype(BF16)
    lw["w_out"] = f("w_out").astype(BF16)
    wr = jnp.concatenate([f("w_router_group"), jnp.transpose(f("w_router_expert"), (1, 0, 2)).reshape(D_MODEL, -1)],
                         axis=1)
    br = jnp.concatenate([f("b_router_group"), f("b_router_expert").reshape(-1)])
    nr = MOE_GROUPS + MOE_GROUPS * MOE_EXPERTS
    lw["w_router"] = _split3(jnp.pad(wr, ((0, 0), (0, V7X_LANES - nr))))
    lw["b_router"] = jnp.pad(br, (0, V7X_LANES - nr))[None]
    lw["norm_ffn_g"] = f("norm_ffn_g")[None]
    wide = lambda a: jnp.transpose(a, (0, 2, 1, 3)).reshape(MOE_GROUPS, D_MODEL, MOE_EXPERTS * MOE_FF)
    lw["w_e_13"] = jnp.concatenate([wide(f("w_e_gate")), wide(f("w_e_up"))], axis=-1).astype(BF16)
    lw["w_e_2"] = f("w_e_down").reshape(MOE_GROUPS, MOE_EXPERTS * MOE_FF, D_MODEL).astype(BF16)
    return lw


def _encoder(x, layers, final_g):
    bsz, seq, _ = x.shape
    plan = _FftPlan(seq)
    xf = x.reshape(bsz * seq, D_MODEL)
    for l, lw in enumerate(layers):
        lru_in, hy_in, gla_in, s5_in = _inproj(xf, lw["norm_mix_g"], lw["w_pack"], lw["hy_conv_w"],
                                               lw["hy_conv_b"], seq)
        shp = lambda a: a.reshape(bsz, seq, a.shape[-1])
        yaf, yab = _lru(shp(lru_in), lw["lru_conv_w"], lw["lru_conv_b"], lw["lru_wg"], lw["lru_bg"], lw["lru_lam"])
        yb = _hyena(shp(hy_in), lw, plan)
        yc = _gla(shp(gla_in), lw["gla_wla"], lw["gla_bla"], lw["gla_norm_g"])
        y5 = _s5(shp(s5_in), *lw["s5_tables"])
        flat = lambda a: a.reshape(bsz * seq, MIX_WIDTH)
        xf, gidx = _merge(xf, lw, flat(yaf), flat(yab), flat(yb), flat(yc), flat(y5), s5_in)
        xf = _moe(xf, gidx, lw, final_g, final=(l == len(layers) - 1))
    return xf.reshape(bsz, seq, D_MODEL)


def kernel(x_prompt, x_sample, norm_mix_g, w_in, lru_conv_w, lru_conv_b, lru_wa, lru_ba, lru_wx, lru_bx,
           lru_lambda, hy_conv_w, hy_conv_b, hy_w1, hy_b1, hy_w2, hy_b2, hy_w3, hy_freq, hy_bias,
           gla_wg2, gla_bg, gla_norm_g, s5_lam_re, s5_lam_im, s5_log_dt, s5_b_re, s5_b_im, s5_c_re, s5_c_im,
           s5_d, s5_glu_w, s5_glu_b, w_branch, w_gate, b_gate, w_out, norm_ffn_g, w_router_group,
           b_router_group, w_router_expert, b_router_expert, w_e_gate, w_e_up, w_e_down, final_norm_g):
    w = dict(norm_mix_g=norm_mix_g, w_in=w_in, lru_conv_w=lru_conv_w, lru_conv_b=lru_conv_b, lru_wa=lru_wa,
             lru_ba=lru_ba, lru_wx=lru_wx, lru_bx=lru_bx, lru_lambda=lru_lambda, hy_conv_w=hy_conv_w,
             hy_conv_b=hy_conv_b, hy_w1=hy_w1, hy_b1=hy_b1, hy_w2=hy_w2, hy_b2=hy_b2, hy_w3=hy_w3,
             hy_freq=hy_freq, hy_bias=hy_bias, gla_wg2=gla_wg2, gla_bg=gla_bg, gla_norm_g=gla_norm_g,
             s5_lam_re=s5_lam_re, s5_lam_im=s5_lam_im, s5_log_dt=s5_log_dt, s5_b_re=s5_b_re, s5_b_im=s5_b_im,
             s5_c_re=s5_c_re, s5_c_im=s5_c_im, s5_d=s5_d, s5_glu_w=s5_glu_w, s5_glu_b=s5_glu_b,
             w_branch=w_branch, w_gate=w_gate, b_gate=b_gate, w_out=w_out, norm_ffn_g=norm_ffn_g,
             w_router_group=w_router_group, b_router_group=b_router_group, w_router_expert=w_router_expert,
             b_router_expert=b_router_expert, w_e_gate=w_e_gate, w_e_up=w_e_up, w_e_down=w_e_down)
    layers = [_prep_layer(w, l) for l in range(norm_mix_g.shape[0])]
    fg = final_norm_g[None]
    return (_encoder(x_prompt, layers, fg), _encoder(x_sample, layers, fg))
```

```python
import functools
import math

import numpy as np
import jax
import jax.numpy as jnp
from jax import lax
from jax.experimental import pallas as pl
from jax.experimental.pallas import tpu as pltpu

F32 = jnp.float32
BF16 = jnp.bfloat16

D_MODEL = 1024
DEPTH = 2
EPS = 1e-6
MIX_WIDTH = D_MODEL // 2
LRU_BLOCKS = 8
LRU_BLOCK = MIX_WIDTH // LRU_BLOCKS
LRU_CONV = 4
LRU_C = 8.0
HY_ORDER = 2
HY_CONV = 3
HY_EMB = 33
HY_BANDS = (HY_EMB - 1) // 2
HY_HIDDEN = 64
HY_DECAY_TARGET = 1e-2
HY_FAST_PCT = 0.3
HY_SLOW_PCT = 1.5
GLA_HEADS = 4
GLA_DK = MIX_WIDTH // 8
GLA_DV = MIX_WIDTH // GLA_HEADS
GLA_RANK = 16
GLA_TAU = 16.0
GLA_CHUNK = 64
S5_GROUP = 16
S5_GROUPS = MIX_WIDTH // S5_GROUP
S5_STATE = 64
MOE_GROUPS = 4
MOE_EXPERTS = 4
MOE_FF = D_MODEL // 4

V7X_LANES = 128
V7X_SUBLANES = 8
V7X_VMEM_BYTES = 64 * 2**20
MIB = 2**20

GLA_PACK = 4 * V7X_LANES * 2 + 512 + 512 + V7X_LANES
PK_LRU = (0, 1024)
PK_HY = (1024, 2560)
PK_GLA = (2560, 2560 + GLA_PACK)
PK_S5 = (PK_GLA[1], PK_GLA[1] + 512)
N_PACK = PK_S5[1]

TOK_TILE = 512
MOE_TILE = 256
MERGE_TILE = 256
LRU_TILE = 256
GLA_TILE = 512
GLA_BATCH = 2
S5_CHUNK = 16
S5_LANE_GROUPS = V7X_LANES // S5_GROUP
S5_SCAN_STEPS = 3
FFT_N1 = 64
FFT_ROWS = 128
FFT_COLS = 256
FFT_KB = 11


def _cparams(sem, vmem_mib):
    return pltpu.CompilerParams(dimension_semantics=sem, vmem_limit_bytes=int(vmem_mib * MIB))


def _rms(x, g):
    return x * lax.rsqrt(jnp.mean(x * x, axis=-1, keepdims=True) + EPS) * g


def _sigmoid(x):
    return 1.0 / (1.0 + jnp.exp(-x))


def _softplus(x):
    return jnp.maximum(x, 0.0) + jnp.log(1.0 + jnp.exp(-jnp.abs(x)))


def _gelu_tanh(x):
    return 0.5 * x * (1.0 + jnp.tanh(math.sqrt(2.0 / math.pi) * (x + 0.044715 * (x * x * x))))


def _silu(x):
    return x * _sigmoid(x)


def _dot(a, b):
    return jnp.dot(a, b, preferred_element_type=F32)


def _split3(w):
    hi = w.astype(BF16)
    lo = (w - hi.astype(F32)).astype(BF16)
    return jnp.concatenate([hi, lo, hi], axis=0)


def _dot3(a, w3):
    hi = a.astype(BF16)
    lo = (a - hi.astype(F32)).astype(BF16)
    return _dot(jnp.concatenate([hi, hi, lo], axis=1), w3)


def _inproj_kernel(x_ref, xp_ref, xn_ref, g_ref, w_ref, cw_ref, cb_ref, lru_ref, hy_ref, gla_ref, s5_ref, ext_ref, *,
                   tiles_per_seq):
    i = pl.program_id(0)
    tile = TOK_TILE
    h = _rms(x_ref[...], g_ref[...]).astype(BF16)
    lru_ref[...] = _dot(h, w_ref[:, PK_LRU[0]:PK_LRU[1]])
    gla_ref[...] = _dot(h, w_ref[:, PK_GLA[0]:PK_GLA[1]])
    s5_ref[...] = _dot(h, w_ref[:, PK_S5[0]:PK_S5[1]])
    w_hy = w_ref[:, PK_HY[0]:PK_HY[1]]
    edge = lambda ref: _dot(_rms(ref[...], g_ref[...]).astype(BF16), w_hy)
    _fill_ext(ext_ref, _dot(h, w_hy), edge(xp_ref), edge(xn_ref), i % tiles_per_seq == 0,
              i % tiles_per_seq == tiles_per_seq - 1, tile)
    y = cb_ref[...] + ext_ref[7:7 + tile, :] * cw_ref[0:1, :]
    y = y + ext_ref[8:8 + tile, :] * cw_ref[1:2, :]
    y = y + ext_ref[9:9 + tile, :] * cw_ref[2:3, :]
    hy_ref[...] = y


def _inproj(xf, g, w_pack, hy_cw, hy_cb, seq):
    t = xf.shape[0]
    tile = TOK_TILE
    r8 = tile // 8
    last8 = t // 8 - 1
    widths = [PK_LRU[1] - PK_LRU[0], PK_HY[1] - PK_HY[0], PK_GLA[1] - PK_GLA[0], PK_S5[1] - PK_S5[0]]
    const = lambda arr: pl.BlockSpec(arr.shape, lambda i: (0,) * arr.ndim, pipeline_mode=pl.Buffered(1))
    return pl.pallas_call(
        functools.partial(_inproj_kernel, tiles_per_seq=seq // tile),
        grid=(t // tile,),
        in_specs=[pl.BlockSpec((tile, D_MODEL), lambda i: (i, 0)),
                  pl.BlockSpec((8, D_MODEL), lambda i: (jnp.maximum(i * r8 - 1, 0), 0)),
                  pl.BlockSpec((8, D_MODEL), lambda i: (jnp.minimum((i + 1) * r8, last8), 0)),
                  const(g), const(w_pack), const(hy_cw), const(hy_cb)],
        out_specs=[pl.BlockSpec((tile, w), lambda i: (i, 0)) for w in widths],
        out_shape=[jax.ShapeDtypeStruct((t, w), F32) for w in widths],
        scratch_shapes=[pltpu.VMEM((tile + 16, widths[1]), F32)],
        compiler_params=_cparams(("parallel",), 56),
        name="inproj",
    )(xf, xf, xf, g, w_pack, hy_cw, hy_cb)


def _fill_ext(ext_ref, main, prev8, next8, first, last, tile):
    ext_ref[0:8, :] = jnp.where(first, 0.0, prev8)
    ext_ref[8:8 + tile, :] = main
    ext_ref[8 + tile:16 + tile, :] = jnp.where(last, 0.0, next8)


def _linear_scan_tile(a, b, carry, reverse):
    n = a.shape[0]
    sub = V7X_SUBLANES
    row = lax.broadcasted_iota(jnp.int32, a.shape, 0) % sub
    d = 1
    while d < sub:
        if reverse:
            a_s = pltpu.roll(a, n - d, 0)
            b_s = pltpu.roll(b, n - d, 0)
            valid = row < sub - d
        else:
            a_s = pltpu.roll(a, d, 0)
            b_s = pltpu.roll(b, d, 0)
            valid = row >= d
        b = jnp.where(valid, a * b_s + b, b)
        a = jnp.where(valid, a * a_s, a)
        d *= 2
    ngroup = n // sub
    out = [None] * ngroup
    for g in (range(ngroup - 1, -1, -1) if reverse else range(ngroup)):
        h = b[g * sub:(g + 1) * sub, :] + a[g * sub:(g + 1) * sub, :] * carry
        carry = h[0:1, :] if reverse else h[sub - 1:sub, :]
        out[g] = h
    return jnp.concatenate(out, axis=0), carry


def _lru_kernel(mf_ref, pf_ref, nf_ref, mb_ref, pb_ref, nb_ref, cw_ref, cb_ref, wg_ref, bg_ref, lam_ref,
                of_ref, ob_ref, extf_ref, extb_ref, carry_ref):
    c = pl.program_id(1)
    nc = pl.num_programs(1)
    tile = LRU_TILE

    @pl.when(c == 0)
    def _():
        carry_ref[...] = jnp.zeros_like(carry_ref)

    def one(m_ref, p_ref, n_ref, ext_ref, d, first, last, o_ref):
        x = m_ref[0, :, 0:MIX_WIDTH]
        ga = m_ref[0, :, MIX_WIDTH:2 * MIX_WIDTH]
        _fill_ext(ext_ref, x, p_ref[0], n_ref[0], first, last, tile)
        xc = cb_ref[...] + ext_ref[6:6 + tile, :] * cw_ref[0:1, :]
        xc = xc + ext_ref[7:7 + tile, :] * cw_ref[1:2, :]
        xc = xc + ext_ref[8:8 + tile, :] * cw_ref[2:3, :]
        xc = xc + ext_ref[9:9 + tile, :] * cw_ref[3:4, :]
        z = _dot(xc.astype(BF16), wg_ref[d]) + bg_ref[d]
        gate_r = _sigmoid(z[:, 0:MIX_WIDTH])
        gate_i = _sigmoid(z[:, MIX_WIDTH:2 * MIX_WIDTH])
        log_a = -LRU_C * gate_r * _softplus(-lam_ref[d])
        a = jnp.exp(log_a)
        t = 1.0 - a * a
        b = jnp.where(t > 0.0, t * lax.rsqrt(t), 0.0) * gate_i * xc
        h, last = _linear_scan_tile(a, b, carry_ref[d:d + 1, :], reverse=(d == 1))
        carry_ref[d:d + 1, :] = last
        o_ref[0] = h * _gelu_tanh(ga)

    one(mf_ref, pf_ref, nf_ref, extf_ref, 0, c == 0, c == nc - 1, of_ref)
    one(mb_ref, pb_ref, nb_ref, extb_ref, 1, c == nc - 1, c == 0, ob_ref)


def _lru(lru_in, cw, cb, wg, bg, lam):
    bsz, seq, _ = lru_in.shape
    tile = LRU_TILE
    nc = seq // tile
    r8 = tile // 8
    last8 = seq // 8 - 1

    def fwd(c):
        return c

    def bwd(c):
        return nc - 1 - c

    def specs(ch):
        return [pl.BlockSpec((1, tile, 2 * MIX_WIDTH), lambda b, c: (b, ch(c), 0)),
                pl.BlockSpec((1, 8, MIX_WIDTH), lambda b, c: (b, jnp.maximum(ch(c) * r8 - 1, 0), 0)),
                pl.BlockSpec((1, 8, MIX_WIDTH), lambda b, c: (b, jnp.minimum((ch(c) + 1) * r8, last8), 0))]

    const = lambda shape: pl.BlockSpec(shape, lambda b, c: (0,) * len(shape))
    return pl.pallas_call(
        _lru_kernel,
        grid=(bsz, nc),
        in_specs=specs(fwd) + specs(bwd) + [const(cw.shape), const(cb.shape), const(wg.shape), const(bg.shape),
                                            const(lam.shape)],
        out_specs=[pl.BlockSpec((1, tile, MIX_WIDTH), lambda b, c: (b, c, 0)),
                   pl.BlockSpec((1, tile, MIX_WIDTH), lambda b, c: (b, nc - 1 - c, 0))],
        out_shape=[jax.ShapeDtypeStruct((bsz, seq, MIX_WIDTH), F32)] * 2,
        scratch_shapes=[pltpu.VMEM((tile + 16, MIX_WIDTH), F32), pltpu.VMEM((tile + 16, MIX_WIDTH), F32),
                        pltpu.VMEM((8, MIX_WIDTH), F32)],
        compiler_params=_cparams(("parallel", "arbitrary"), 40),
        name="lru",
    )(lru_in, lru_in, lru_in, lru_in, lru_in, lru_in, cw, cb, wg, bg, lam)


class _FftPlan:
    def __init__(self, seq):
        n = 2 * seq
        n1 = FFT_N1
        n2 = n // n1
        assert n1 * n2 == n and n2 % 16 == 0
        h1 = n1 // 2 + 1
        nh = n1 // 2
        self.n, self.n1, self.n2, self.h1, self.nh = n, n1, n2, h1, nh
        self.rows = min(FFT_ROWS, n2)
        assert n2 % self.rows == 0 and h1 % FFT_KB == 0
        k1 = np.arange(h1, dtype=np.float64)
        m1 = np.arange(nh, dtype=np.float64)
        r = np.arange(8, dtype=np.float64)
        ang = -2.0 * np.pi * (k1[:, None, None] * m1[None, None, :] / n1 + r[None, :, None] * k1[:, None, None] / n)
        e = np.exp(1j * ang)
        fa = np.zeros((h1, 8, nh, 8), np.complex128)
        for rr in range(8):
            fa[:, rr, :, rr] = e[:, rr, :]
        fa = fa.reshape(h1 * 8, nh * 8)
        self.fa = jnp.asarray(np.concatenate([fa.real, fa.imag], axis=0), BF16)
        ck = np.where((k1 == 0) | (k1 == n1 // 2), 1.0, 2.0)
        ec = np.conj(e) * ck[:, None, None] / n
        fc = np.zeros((nh, 8, 2, h1, 8), np.float64)
        for rr in range(8):
            fc[:, rr, 0, :, rr] = ec[:, rr, :].real.T
            fc[:, rr, 1, :, rr] = -ec[:, rr, :].imag.T
        self.fc = jnp.asarray(fc.reshape(nh * 8, 2 * h1 * 8), BF16)
        rg = np.arange(n2 // 8, dtype=np.float64)
        tw = np.exp(-2j * np.pi * 8.0 * rg[None, :] * k1[:, None] / n)
        self.tw = jnp.asarray(np.concatenate([tw.real, tw.imag], axis=0), F32)
        tw1 = np.exp(-2j * np.pi * k1 / n)
        self.tw1 = jnp.asarray(np.stack([tw1.real, tw1.imag]), F32)
        q = np.arange(n2, dtype=np.float64)
        f2 = np.exp(-2j * np.pi * np.outer(q, q) / n2)
        fr, fi = f2.real, f2.imag
        self.gb = jnp.asarray(np.block([[fr, -fi], [fi, fr]]), BF16)
        self.gbi = jnp.asarray(np.block([[fr, fi], [-fi, fr]]), BF16)


def _fft_a_kernel(tw_ref, x_ref, fa_ref, a_ref, *, h1, nh, rows):
    rb = pl.program_id(2)
    cols = x_ref.shape[-1]

    def stage(rg):
        xg = x_ref[0, :, pl.ds(pl.multiple_of(rg * 8, 8), 8), :].reshape(nh * 8, cols).astype(BF16)
        return _dot(fa_ref[...], xg)

    def body(i, carry):
        p0 = stage(2 * i)
        p1 = stage(2 * i + 1)
        g0 = rb * (rows // 8) + 2 * i
        for k in range(h1):
            outs = []
            for p, g in ((p0, g0), (p1, g0 + 1)):
                pr = p[k * 8:(k + 1) * 8, :]
                pi = p[(h1 + k) * 8:(h1 + k + 1) * 8, :]
                tr = tw_ref[k, g]
                ti = tw_ref[h1 + k, g]
                outs.append((pr * tr - pi * ti, pr * ti + pi * tr))
            dst = pl.ds(pl.multiple_of(i * 16, 16), 16)
            a_ref[0, k, 0, dst, :] = jnp.concatenate([outs[0][0], outs[1][0]], axis=0).astype(BF16)
            a_ref[0, k, 1, dst, :] = jnp.concatenate([outs[0][1], outs[1][1]], axis=0).astype(BF16)
        return carry

    lax.fori_loop(0, rows // 16, body, 0)


def _fft_a(x4, plan, col_off=0, ncols=None):
    bq, nh, n2, width = x4.shape
    ncols = width if ncols is None else ncols
    cb0 = col_off // FFT_COLS
    rows = plan.rows
    kern = functools.partial(_fft_a_kernel, h1=plan.h1, nh=nh, rows=rows)
    return pl.pallas_call(
        kern,
        grid=(bq, ncols // FFT_COLS, n2 // rows),
        in_specs=[pl.BlockSpec(memory_space=pltpu.SMEM),
                  pl.BlockSpec((1, nh, rows, FFT_COLS), lambda b, c, r: (b, 0, r, cb0 + c)),
                  pl.BlockSpec(plan.fa.shape, lambda b, c, r: (0, 0))],
        out_specs=pl.BlockSpec((1, plan.h1, 2, rows, FFT_COLS), lambda b, c, r: (b, 0, 0, r, c)),
        out_shape=jax.ShapeDtypeStruct((bq, plan.h1, 2, n2, ncols), BF16),
        compiler_params=_cparams(("parallel", "parallel", "parallel"), 40),
        name="fft_outer_fwd",
    )(plan.tw, x4, plan.fa)


def _fft_mid_kernel(a_ref, kf_ref, gb_ref, gbi_ref, o_ref, *, n2):
    kb = a_ref.shape[1]
    cols = a_ref.shape[-1]
    for k in range(kb):
        y = _dot(gb_ref[...], a_ref[0, k].reshape(2 * n2, cols))
        yr, yi = y[0:n2, :], y[n2:2 * n2, :]
        kr, ki = kf_ref[0, k, 0], kf_ref[0, k, 1]
        z = jnp.concatenate([yr * kr - yi * ki, yr * ki + yi * kr], axis=0).astype(BF16)
        o_ref[0, k] = _dot(gbi_ref[...], z).reshape(2, n2, cols).astype(BF16)


def _fft_mid(a, kf, order, plan):
    bq, h1, _, n2, width = a.shape
    kern = functools.partial(_fft_mid_kernel, n2=n2)
    blk = (1, FFT_KB, 2, n2, FFT_COLS)
    return pl.pallas_call(
        kern,
        grid=(bq, width // FFT_COLS, h1 // FFT_KB),
        in_specs=[pl.BlockSpec(blk, lambda b, c, k: (b, k, 0, 0, c)),
                  pl.BlockSpec(blk, lambda b, c, k: (order, k, 0, 0, c)),
                  pl.BlockSpec(plan.gb.shape, lambda b, c, k: (0, 0)),
                  pl.BlockSpec(plan.gbi.shape, lambda b, c, k: (0, 0))],
        out_specs=pl.BlockSpec(blk, lambda b, c, k: (b, k, 0, 0, c)),
        out_shape=jax.ShapeDtypeStruct(a.shape, BF16),
        compiler_params=_cparams(("parallel", "parallel", "parallel"), 40),
        name="fft_inner_mul",
    )(a, kf, plan.gb, plan.gbi)


def _fft_c_kernel(tw_ref, b_ref, u_ref, g_ref, bias_ref, fc_ref, o_ref, *, h1, nh, rows):
    rb = pl.program_id(2)
    cols = o_ref.shape[-1]

    def body(i, carry):
        src = pl.ds(pl.multiple_of(i * 16, 16), 16)
        tiles = [[b_ref[0, k, p, src, :].astype(F32) for p in range(2)] for k in range(h1)]
        for half in range(2):
            g = rb * (rows // 8) + 2 * i + half
            re_rows, im_rows = [], []
            for k in range(h1):
                br = tiles[k][0][half * 8:(half + 1) * 8, :]
                bi = tiles[k][1][half * 8:(half + 1) * 8, :]
                tr = tw_ref[k, g]
                ti = tw_ref[h1 + k, g]
                re_rows.append(br * tr + bi * ti)
                im_rows.append(bi * tr - br * ti)
            s = jnp.concatenate(re_rows + im_rows, axis=0).astype(BF16)
            y = _dot(fc_ref[...], s).reshape(nh, 8, cols)
            dst = pl.ds(pl.multiple_of((2 * i + half) * 8, 8), 8)
            u = u_ref[0, :, dst, :]
            o_ref[0, :, dst, :] = (y + u * bias_ref[...]) * g_ref[0, :, dst, :]
        return carry

    lax.fori_loop(0, rows // 16, body, 0)


def _fft_c(bm, u4, u_off, g4, g_off, bias, plan):
    bq, h1, _, n2, width = bm.shape
    nh = plan.nh
    rows = plan.rows
    ub, gbk = u_off // FFT_COLS, g_off // FFT_COLS
    kern = functools.partial(_fft_c_kernel, h1=h1, nh=nh, rows=rows)
    xblk = (1, nh, rows, FFT_COLS)
    return pl.pallas_call(
        kern,
        grid=(bq, width // FFT_COLS, n2 // rows),
        in_specs=[pl.BlockSpec(memory_space=pltpu.SMEM),
                  pl.BlockSpec((1, h1, 2, rows, FFT_COLS), lambda b, c, r: (b, 0, 0, r, c)),
                  pl.BlockSpec(xblk, lambda b, c, r: (b, 0, r, ub + c)),
                  pl.BlockSpec(xblk, lambda b, c, r: (b, 0, r, gbk + c)),
                  pl.BlockSpec((1, FFT_COLS), lambda b, c, r: (0, c)),
                  pl.BlockSpec(plan.fc.shape, lambda b, c, r: (0, 0))],
        out_specs=pl.BlockSpec(xblk, lambda b, c, r: (b, 0, r, c)),
        out_shape=jax.ShapeDtypeStruct((bq, nh, n2, width), F32),
        compiler_params=_cparams(("parallel", "parallel", "parallel"), 48),
        name="fft_outer_inv",
    )(plan.tw, bm, u4, g4, bias, plan.fc)


def _hyfilt_gen_kernel(w1_ref, b1_ref, w2_ref, b2_ref, w3_ref, fr_ref, o_ref, ss_ref, *, seq):
    rblk = pl.program_id(0)
    tile, cols = o_ref.shape[1], o_ref.shape[2]
    irow = lax.broadcasted_iota(jnp.int32, (tile, V7X_LANES), 0) + rblk * tile
    row = irow.astype(F32)
    lane = lax.broadcasted_iota(jnp.int32, (tile, V7X_LANES), 1)
    t = row / (seq - 1.0)
    omega = (2.0 * math.pi / seq) * row
    band_step = (HY_BANDS - 1 - 1e-4) / (HY_BANDS - 1)
    is_cos = (lane >= 1) & (lane <= HY_BANDS)
    is_sin = (lane > HY_BANDS) & (lane <= 2 * HY_BANDS)
    bidx = jnp.where(is_cos, lane - 1, lane - 1 - HY_BANDS).astype(F32)
    ang = omega * (1e-4 + band_step * bidx)
    trig = jnp.cos(ang + jnp.where(is_sin, 0.5 * math.pi, 0.0))
    z = jnp.where(lane == 0, t, jnp.where(is_cos | is_sin, trig, 0.0))
    fr = fr_ref[...]
    hid = jnp.sin(fr * (_dot3(z, w1_ref[...]) + b1_ref[...]))
    hid = jnp.sin(fr * (_dot3(hid, w2_ref[...]) + b2_ref[...]))
    filt = _dot3(hid, w3_ref[...])
    col = lax.broadcasted_iota(jnp.int32, (1, cols), 1)
    chan = (col % MIX_WIDTH).astype(F32)
    max_decay = math.log(HY_DECAY_TARGET) / HY_FAST_PCT
    min_decay = math.log(HY_DECAY_TARGET) / HY_SLOW_PCT
    delta = jnp.abs(min_decay + (max_decay - min_decay) / (MIX_WIDTH - 1) * chan)
    filt = filt * jnp.exp(-t[:, 0:1] * delta)
    is_bwd = (col // MIX_WIDTH) % 2 == 1
    rows_c = lax.broadcasted_iota(jnp.int32, (tile, cols), 0) + rblk * tile
    filt = jnp.where(is_bwd & (rows_c == seq - 1), 0.0, filt)
    o_ref[0] = filt

    @pl.when(rblk == 0)
    def _():
        ss_ref[...] = jnp.zeros_like(ss_ref)

    ss_ref[...] += jnp.sum(filt * filt, axis=0, keepdims=True)


def _hyfilt_gen(seq, w1p, b1, w2, b2, w3, freq):
    ncol = w3.shape[1]
    tile = min(seq, 512)
    const = lambda shape: pl.BlockSpec(shape, lambda r: (0,) * len(shape))
    kern = functools.partial(_hyfilt_gen_kernel, seq=seq)
    return pl.pallas_call(
        kern,
        grid=(seq // tile,),
        in_specs=[const(w1p.shape), const(b1.shape), const(w2.shape), const(b2.shape), const(w3.shape),
                  const(freq.shape)],
        out_specs=[pl.BlockSpec((1, tile, ncol), lambda r: (0, r, 0)),
                   pl.BlockSpec((1, ncol), lambda r: (0, 0))],
        out_shape=[jax.ShapeDtypeStruct((1, seq, ncol), F32), jax.ShapeDtypeStruct((1, ncol), F32)],
        compiler_params=_cparams(("arbitrary",), 40),
        name="hyena_filter_gen",
    )(w1p, b1, w2, b2, w3, freq)


def _hyfilt_spec_kernel(tw1_ref, a0_ref, a1_ref, ss0_ref, ss1_ref, gb_ref, o_ref, *, n2):
    kblk = pl.program_id(2)
    kb = a0_ref.shape[1]
    cols = a0_ref.shape[-1]
    h1 = tw1_ref.shape[1]
    scale = lax.rsqrt(ss0_ref[...] + ss1_ref[...] + EPS)
    ang2 = (-2.0 * math.pi / n2) * lax.broadcasted_iota(jnp.int32, (n2, cols), 0).astype(F32)
    cr, ci = jnp.cos(ang2), jnp.sin(ang2)
    for k in range(kb):
        y0 = _dot(gb_ref[...], a0_ref[0, k].reshape(2 * n2, cols))
        y1 = _dot(gb_ref[...], a1_ref[0, k].reshape(2 * n2, cols))
        sr = tw1_ref[0, kblk * kb + k]
        si = tw1_ref[1, kblk * kb + k]
        wr, wi = cr * sr - ci * si, cr * si + ci * sr
        y1r, y1i = y1[0:n2, :], y1[n2:2 * n2, :]
        o_ref[0, k, 0] = (y0[0:n2, :] + (wr * y1r - wi * y1i)) * scale
        o_ref[0, k, 1] = (y0[n2:2 * n2, :] - (wr * y1i + wi * y1r)) * scale


def _hyfilt_spec(af, ss, plan):
    _, h1, _, n2, _ = af.shape
    ncb = MIX_WIDTH // FFT_COLS
    blk = (1, FFT_KB, 2, n2, FFT_COLS)
    kern = functools.partial(_hyfilt_spec_kernel, n2=n2)
    return pl.pallas_call(
        kern,
        grid=(HY_ORDER, ncb, h1 // FFT_KB),
        in_specs=[pl.BlockSpec(memory_space=pltpu.SMEM),
                  pl.BlockSpec(blk, lambda o, c, k: (0, k, 0, 0, o * 2 * ncb + c)),
                  pl.BlockSpec(blk, lambda o, c, k: (0, k, 0, 0, o * 2 * ncb + ncb + c)),
                  pl.BlockSpec((1, FFT_COLS), lambda o, c, k: (0, o * 2 * ncb + c)),
                  pl.BlockSpec((1, FFT_COLS), lambda o, c, k: (0, o * 2 * ncb + ncb + c)),
                  pl.BlockSpec(plan.gb.shape, lambda o, c, k: (0, 0))],
        out_specs=pl.BlockSpec(blk, lambda o, c, k: (o, k, 0, 0, c)),
        out_shape=jax.ShapeDtypeStruct((HY_ORDER, h1, 2, n2, MIX_WIDTH), F32),
        compiler_params=_cparams(("parallel", "parallel", "parallel"), 40),
        name="hyena_filter_spectrum",
    )(plan.tw1, af, af, ss, ss, plan.gb)


def _hyena(zc, lw, plan):
    bsz, seq, _ = zc.shape
    filt, ss = _hyfilt_gen(seq, lw["hy_w1p"], lw["hy_b1"], lw["hy_w2"], lw["hy_b2"], lw["hy_w3"], lw["hy_freq"])
    af = _fft_a(filt.reshape(1, plan.nh, plan.n2, filt.shape[-1]), plan)
    kf = _hyfilt_spec(af, ss, plan)
    zc4 = zc.reshape(bsz, plan.nh, plan.n2, 3 * MIX_WIDTH)
    a = _fft_a(zc4, plan, col_off=0, ncols=MIX_WIDTH)
    bm = _fft_mid(a, kf, 0, plan)
    z1 = _fft_c(bm, zc4, 0, zc4, MIX_WIDTH, lw["hy_bias"][0:1], plan)
    a = _fft_a(z1, plan)
    bm = _fft_mid(a, kf, 1, plan)
    z2 = _fft_c(bm, z1, 0, zc4, 2 * MIX_WIDTH, lw["hy_bias"][1:2], plan)
    return z2.reshape(bsz, seq, MIX_WIDTH)


def _gla_kernel(*refs, reverse):
    if reverse:
        x_ref, of_ref, wla_ref, bla_ref, ng_ref, o_ref, st_ref = refs
    else:
        x_ref, wla_ref, bla_ref, o_ref, st_ref = refs
    c = pl.program_id(1)

    @pl.when(c == 0)
    def _():
        st_ref[...] = jnp.zeros_like(st_ref)

    for bb in range(x_ref.shape[0]):
        _gla_sequence(bb, x_ref, of_ref if reverse else None, wla_ref, bla_ref, ng_ref if reverse else None, o_ref,
                      st_ref, reverse)


def _gla_sequence(bb, x_ref, of_ref, wla_ref, bla_ref, ng_ref, o_ref, st_ref, reverse):
    tile = GLA_TILE
    ck = GLA_CHUNK
    nck = tile // ck
    hw = V7X_LANES
    nh = GLA_HEADS
    q = x_ref[bb, :, 0:nh * hw] * (GLA_DK ** -0.5)
    k = x_ref[bb, :, nh * hw:2 * nh * hw]
    v = x_ref[bb, :, 2 * nh * hw:3 * nh * hw]
    lr = x_ref[bb, :, 4 * nh * hw:4 * nh * hw + hw]
    zl = _dot(lr.astype(BF16), wla_ref[...]) + bla_ref[...]
    la = (jnp.minimum(zl, 0.0) - jnp.log(1.0 + jnp.exp(-jnp.abs(zl)))) / GLA_TAU

    row = lax.broadcasted_iota(jnp.int32, la.shape, 0) % ck
    bcum = la
    d = 1
    while d < ck:
        if reverse:
            bcum = bcum + jnp.where(row < ck - d, pltpu.roll(bcum, tile - d, 0), 0.0)
        else:
            bcum = bcum + jnp.where(row >= d, pltpu.roll(bcum, d, 0), 0.0)
        d *= 2
    b3 = bcum.reshape(nck, ck, nh * hw)
    blast = b3[:, 0:1, :] if reverse else b3[:, ck - 1:ck, :]
    q_e = (q * jnp.exp(bcum)).astype(BF16)
    k_e = (k * jnp.exp(-bcum)).astype(BF16)
    k_d = (k.reshape(nck, ck, nh * hw) * jnp.exp(blast - b3)).reshape(tile, nh * hw).astype(BF16)
    gch = jnp.exp(blast)
    vb = v.astype(BF16)

    ri = lax.broadcasted_iota(jnp.int32, (ck, ck), 0)
    ci = lax.broadcasted_iota(jnp.int32, (ck, ck), 1)
    mask = (ri <= ci) if reverse else (ri >= ci)
    order = range(nck - 1, -1, -1) if reverse else range(nck)
    outs = [None] * nck
    for n in order:
        rs = slice(n * ck, (n + 1) * ck)
        heads = []
        for h in range(nh):
            ls = slice(h * hw, (h + 1) * hw)
            qe, ke, kd, vh = q_e[rs, ls], k_e[rs, ls], k_d[rs, ls], vb[rs, ls]
            st = st_ref[bb, h]
            sc = lax.dot_general(qe, ke, (((1,), (1,)), ((), ())), preferred_element_type=F32)
            sc = jnp.where(mask, sc, 0.0).astype(BF16)
            o = _dot(sc, vh) + lax.dot_general(qe, st.astype(BF16), (((1,), (1,)), ((), ())),
                                               preferred_element_type=F32)
            upd = lax.dot_general(vh, kd, (((0,), (0,)), ((), ())), preferred_element_type=F32)
            st_ref[bb, h] = st * gch[n, :, ls] + upd
            heads.append(o)
        outs[n] = jnp.concatenate(heads, axis=1)
    o_dir = jnp.concatenate(outs, axis=0)
    if not reverse:
        o_ref[bb] = o_dir
        return
    o = of_ref[bb] + o_dir
    g = x_ref[bb, :, 3 * nh * hw:4 * nh * hw]
    normed = []
    for h in range(nh):
        oh = o[:, h * hw:(h + 1) * hw]
        normed.append(oh * lax.rsqrt(jnp.mean(oh * oh, axis=-1, keepdims=True) + EPS))
    o = jnp.concatenate(normed, axis=1) * ng_ref[...]
    o_ref[bb] = o * _silu(g)


def _gla(gla_in, wla, bla, norm_g):
    bsz, seq, width = gla_in.shape
    tile = GLA_TILE
    nc = seq // tile
    nb = min(GLA_BATCH, bsz)
    const = lambda shape: pl.BlockSpec(shape, lambda b, c: (0,) * len(shape))
    out_shape = jax.ShapeDtypeStruct((bsz, seq, MIX_WIDTH), F32)
    scratch = [pltpu.VMEM((nb, GLA_HEADS, GLA_DV, V7X_LANES), F32)]
    o_f = pl.pallas_call(
        functools.partial(_gla_kernel, reverse=False),
        grid=(bsz // nb, nc),
        in_specs=[pl.BlockSpec((nb, tile, width), lambda b, c: (b, c, 0)), const(wla.shape[1:]), const(bla.shape[1:])],
        out_specs=pl.BlockSpec((nb, tile, MIX_WIDTH), lambda b, c: (b, c, 0)),
        out_shape=out_shape,
        scratch_shapes=scratch,
        compiler_params=_cparams(("parallel", "arbitrary"), 56),
        name="gla_fwd",
    )(gla_in, wla[0], bla[0])
    return pl.pallas_call(
        functools.partial(_gla_kernel, reverse=True),
        grid=(bsz // nb, nc),
        in_specs=[pl.BlockSpec((nb, tile, width), lambda b, c: (b, nc - 1 - c, 0)),
                  pl.BlockSpec((nb, tile, MIX_WIDTH), lambda b, c: (b, nc - 1 - c, 0)),
                  const(wla.shape[1:]), const(bla.shape[1:]), const(norm_g.shape)],
        out_specs=pl.BlockSpec((nb, tile, MIX_WIDTH), lambda b, c: (b, nc - 1 - c, 0)),
        out_shape=out_shape,
        scratch_shapes=scratch,
        compiler_params=_cparams(("parallel", "arbitrary"), 56),
        name="gla_bwd",
    )(gla_in, o_f, wla[1], bla[1], norm_g)


def _s5_kernel(u_ref, t_ref, e_ref, o_ref_w, mu_ref, mup_ref, y_ref):
    ch = S5_CHUNK
    sub = V7X_SUBLANES
    nrow = u_ref.shape[1] // ch
    half = S5_LANE_GROUPS * 2 * S5_STATE
    swap = lambda x: pltpu.roll(x, half // 2, 1)
    u = jnp.concatenate([u_ref[0, pl.ds(i, nrow, stride=ch), :] for i in range(ch)], axis=1).astype(BF16)
    y = _dot(u, t_ref[0])
    he = _dot(u, e_ref[0])
    row = lax.broadcasted_iota(jnp.int32, (nrow, half), 0)
    rsub = row % sub
    states = []
    for d in range(2):
        h = he[:, d * half:(d + 1) * half]
        for s in range(S5_SCAN_STEPS):
            step = 2 ** s
            if d == 0:
                hs = jnp.where(rsub >= step, pltpu.roll(h, step, 0), 0.0)
            else:
                hs = jnp.where(rsub < sub - step, pltpu.roll(h, nrow - step, 0), 0.0)
            h = h + hs * mu_ref[0, d, s, 0:1, :] + swap(hs) * mu_ref[0, d, s, 1:2, :]
        ngroup = nrow // sub
        carry = jnp.zeros((1, half), F32)
        out = [None] * ngroup
        for g in (range(ngroup) if d == 0 else range(ngroup - 1, -1, -1)):
            hg = h[g * sub:(g + 1) * sub, :] + carry * mup_ref[0, d, 0] + swap(carry) * mup_ref[0, d, 1]
            carry = hg[sub - 1:sub, :] if d == 0 else hg[0:1, :]
            out[g] = hg
        h = jnp.concatenate(out, axis=0)
        if d == 0:
            h = jnp.where(row >= 1, pltpu.roll(h, 1, 0), 0.0)
        else:
            h = jnp.where(row < nrow - 1, pltpu.roll(h, nrow - 1, 0), 0.0)
        states.append(h)
    hp = jnp.concatenate(states, axis=1).astype(BF16)
    y = y + lax.dot_general(hp, o_ref_w[0], (((1,), (1,)), ((), ())), preferred_element_type=F32)
    for j in range(ch):
        y_ref[0, pl.ds(j, nrow, stride=ch), :] = y[:, j * V7X_LANES:(j + 1) * V7X_LANES]


def _s5(s5_in, tblk, eblk, oblk, mu, mup):
    bsz, seq, width = s5_in.shape
    nb = width // V7X_LANES
    once = pl.Buffered(1)
    wspec = lambda arr: pl.BlockSpec((1,) + arr.shape[1:], lambda k, b: (k,) + (0,) * (arr.ndim - 1),
                                     pipeline_mode=once)
    xspec = pl.BlockSpec((1, seq, V7X_LANES), lambda k, b: (b, 0, k))
    return pl.pallas_call(
        _s5_kernel,
        grid=(nb, bsz),
        in_specs=[xspec, wspec(tblk), wspec(eblk), wspec(oblk), wspec(mu), wspec(mup)],
        out_specs=pl.BlockSpec((1, seq, V7X_LANES), lambda k, b: (b, 0, k)),
        out_shape=jax.ShapeDtypeStruct(s5_in.shape, F32),
        compiler_params=_cparams(("arbitrary", "arbitrary"), 60),
        name="s5",
    )(s5_in, tblk, eblk, oblk, mu, mup)


def _s5_tables(lam_re, lam_im, log_dt, b_re, b_im, c_re, c_im):
    ch = S5_CHUNK
    hi = lax.Precision.HIGHEST
    cmul = lambda x, y: (x[0] * y[0] - x[1] * y[1], x[0] * y[1] + x[1] * y[0])
    lr_, li_ = lam_re.astype(F32), lam_im.astype(F32)
    dt = jnp.exp(log_dt.astype(F32))[..., None]
    ar, ai = lr_ * dt, li_ * dt

    def lam_pow(tau):
        t = jnp.asarray(tau, F32)
        t = t.reshape(t.shape + (1,) * 3)
        mag = jnp.exp(t * ar)
        return mag * jnp.cos(t * ai), mag * jnp.sin(t * ai)

    lb = lam_pow(jnp.ones((), F32))
    num = (lb[0] - 1.0, lb[1])
    den = lr_ * lr_ + li_ * li_
    ratio = ((num[0] * lr_ + num[1] * li_) / den, (num[1] * lr_ - num[0] * li_) / den)
    b_bar = cmul((ratio[0][..., None], ratio[1][..., None]), (b_re.astype(F32), b_im.astype(F32)))
    cc = (c_re.astype(F32), c_im.astype(F32))

    nb, gb, hh, pp = S5_GROUPS // S5_LANE_GROUPS, S5_LANE_GROUPS, S5_GROUP, S5_STATE
    eye = jnp.eye(gb, dtype=F32)

    def b_base(x):
        x = jnp.transpose(x.reshape(2, nb, gb, pp, hh), (1, 2, 4, 0, 3))
        return (x[:, :, :, :, None, :] * eye[None, :, None, None, :, None]).reshape(nb, gb * hh, 2 * gb * pp)

    def c_base(x):
        x = jnp.transpose(x.reshape(2, nb, gb, hh, pp), (1, 2, 3, 0, 4))
        return (x[:, :, :, :, None, :] * eye[None, :, None, None, :, None]).reshape(nb, gb * hh, 2 * gb * pp)

    row = lambda x: jnp.transpose(x.reshape(2, nb, gb * pp), (1, 0, 2)).reshape(nb, 1, 2 * gb * pp)
    base = (b_base(b_bar[0]), b_base(b_bar[1]), c_base(cc[0]), c_base(cc[1]), row(ar), row(ai))
    dtab = _s5_lag_tables(*base)
    return _s5_block_tables(dtab, *base)


def _lam_pow(tau, ar, ai):
    mag = jnp.exp(tau * ar)
    return mag * jnp.cos(tau * ai), mag * jnp.sin(tau * ai)


def _s5_lag_kernel(btr_ref, bti_ref, ctr_ref, cti_ref, ar_ref, ai_ref, d_ref):
    ch = S5_CHUNK
    half = btr_ref.shape[-1] // 2
    def split(x):
        hi = x.astype(BF16)
        return hi, (x - hi.astype(F32)).astype(BF16)

    def tdot(a, b_split):
        ah, al = split(a)
        bh, bl = b_split
        return lax.dot_general(jnp.concatenate([ah, ah, al], axis=1), jnp.concatenate([bh, bl, bh], axis=1),
                               (((1,), (1,)), ((), ())), preferred_element_type=F32)

    for d in range(2):
        ls = slice(d * half, (d + 1) * half)
        btr, bti = btr_ref[0, :, ls], bti_ref[0, :, ls]
        ctr, cti = split(ctr_ref[0, :, ls]), split(cti_ref[0, :, ls])
        for lag in range(ch):
            lr, li = _lam_pow(float(lag), ar_ref[0, :, ls], ai_ref[0, :, ls])
            val = tdot(btr * lr - bti * li, ctr) - tdot(btr * li + bti * lr, cti)
            idx = ch - 1 + lag if d == 0 else ch - 1 - lag
            if d == 1 and lag == 0:
                d_ref[0, idx] = d_ref[0, idx] + val
            else:
                d_ref[0, idx] = val


def _s5_lag_tables(btr, bti, ctr, cti, ar, ai):
    nb = btr.shape[0]
    spec = lambda a: pl.BlockSpec((1,) + a.shape[1:], lambda k: (k, 0, 0))
    nlag = 2 * S5_CHUNK - 1
    return pl.pallas_call(
        _s5_lag_kernel,
        grid=(nb,),
        in_specs=[spec(a) for a in (btr, bti, ctr, cti, ar, ai)],
        out_specs=pl.BlockSpec((1, nlag, V7X_LANES, V7X_LANES), lambda k: (k, 0, 0, 0)),
        out_shape=jax.ShapeDtypeStruct((nb, nlag, V7X_LANES, V7X_LANES), F32),
        compiler_params=_cparams(("parallel",), 32),
        name="s5_lag_tables",
    )(btr, bti, ctr, cti, ar, ai)


def _s5_block_kernel(d_ref, btr_ref, bti_ref, ctr_ref, cti_ref, ar_ref, ai_ref, t_ref, e_ref, ot_ref, mu_ref,
                     mup_ref):
    ch = S5_CHUNK
    i = pl.program_id(1)
    half = btr_ref.shape[-1] // 2
    fi = i.astype(F32)
    for j in range(ch):
        t_ref[0, :, j * V7X_LANES:(j + 1) * V7X_LANES] = d_ref[0, j - i + ch - 1].astype(BF16)
    for d in range(2):
        ls = slice(d * half, (d + 1) * half)
        ar, ai = ar_ref[0, :, ls], ai_ref[0, :, ls]
        lr, li = _lam_pow(fi if d == 1 else (ch - 1.0) - fi, ar, ai)
        btr, bti = btr_ref[0, :, ls], bti_ref[0, :, ls]
        e_ref[0, :, 2 * d * half:(2 * d + 1) * half] = (btr * lr - bti * li).astype(BF16)
        e_ref[0, :, (2 * d + 1) * half:(2 * d + 2) * half] = (btr * li + bti * lr).astype(BF16)
        lr, li = _lam_pow(fi + 1.0 if d == 0 else ch - fi, ar, ai)
        ctr, cti = ctr_ref[0, :, ls], cti_ref[0, :, ls]
        ot_ref[0, :, 2 * d * half:(2 * d + 1) * half] = (ctr * lr - cti * li).astype(BF16)
        ot_ref[0, :, (2 * d + 1) * half:(2 * d + 2) * half] = (-(ctr * li + cti * lr)).astype(BF16)

    @pl.when(i == 0)
    def _():
        for d in range(2):
            ar, ai = ar_ref[0, :, d * half:(d + 1) * half], ai_ref[0, :, d * half:(d + 1) * half]
            for s in range(S5_SCAN_STEPS):
                lr, li = _lam_pow(float(ch * 2 ** s), ar, ai)
                mu_ref[0, d, s, 0:1, :] = jnp.concatenate([lr, lr], axis=1)
                mu_ref[0, d, s, 1:2, :] = jnp.concatenate([-li, li], axis=1)
            for r in range(V7X_SUBLANES):
                lr, li = _lam_pow(float(ch * (r + 1 if d == 0 else V7X_SUBLANES - r)), ar, ai)
                mup_ref[0, d, 0, r:r + 1, :] = jnp.concatenate([lr, lr], axis=1)
                mup_ref[0, d, 1, r:r + 1, :] = jnp.concatenate([-li, li], axis=1)


def _s5_block_tables(dtab, btr, bti, ctr, cti, ar, ai):
    nb = btr.shape[0]
    ch = S5_CHUNK
    big = ch * V7X_LANES
    wide = 2 * btr.shape[-1]
    spec = lambda a: pl.BlockSpec((1,) + a.shape[1:], lambda k, i: (k,) + (0,) * (a.ndim - 1))
    tile = lambda w: pl.BlockSpec((1, V7X_LANES, w), lambda k, i: (k, i, 0))
    return pl.pallas_call(
        _s5_block_kernel,
        grid=(nb, ch),
        in_specs=[spec(a) for a in (dtab, btr, bti, ctr, cti, ar, ai)],
        out_specs=[tile(big), tile(wide), tile(wide),
                   pl.BlockSpec((1, 2, S5_SCAN_STEPS, 2, wide // 2), lambda k, i: (k, 0, 0, 0, 0)),
                   pl.BlockSpec((1, 2, 2, V7X_SUBLANES, wide // 2), lambda k, i: (k, 0, 0, 0, 0))],
        out_shape=[jax.ShapeDtypeStruct((nb, big, big), BF16), jax.ShapeDtypeStruct((nb, big, wide), BF16),
                   jax.ShapeDtypeStruct((nb, big, wide), BF16),
                   jax.ShapeDtypeStruct((nb, 2, S5_SCAN_STEPS, 2, wide // 2), F32),
                   jax.ShapeDtypeStruct((nb, 2, 2, V7X_SUBLANES, wide // 2), F32)],
        compiler_params=_cparams(("parallel", "arbitrary"), 32),
        name="s5_block_tables",
    )(dtab, btr, bti, ctr, cti, ar, ai)


def _merge_kernel(x_ref, ng_ref, yaf_ref, yab_ref, yb_ref, yc_ref, y5_ref, u5_ref, d5_ref, gluw_ref, glub_ref,
                  wgate_ref, bgate_ref, wbr_ref, wout_ref, fng_ref, wr_ref, br_ref, o_ref, g_ref):
    x = x_ref[...]
    h = _rms(x, ng_ref[...]).astype(BF16)
    y_d = _gelu_tanh(u5_ref[...] * d5_ref[...] + y5_ref[...])
    y_d = y_d * _sigmoid(_dot(y_d.astype(BF16), gluw_ref[...]) + glub_ref[...])
    branches = (yaf_ref[...] + yab_ref[...], yb_ref[...], yc_ref[...], y_d)
    merged = jnp.zeros(x.shape, F32)
    for i, y in enumerate(branches):
        gate = _sigmoid(_dot(h, wgate_ref[i]) + bgate_ref[i])
        merged = merged + gate * _dot(y.astype(BF16), wbr_ref[i])
    x_new = x + _dot(merged.astype(BF16), wout_ref[...])
    o_ref[...] = x_new
    g_ref[...] = _top_group(_router_logits(x_new, fng_ref, wr_ref, br_ref)[1])


def _merge(xf, lw, yaf, yab, yb, yc, y5, u5):
    t = xf.shape[0]
    tok = lambda w: pl.BlockSpec((MERGE_TILE, w), lambda i: (i, 0))
    const = lambda arr: pl.BlockSpec(arr.shape, lambda i: (0,) * arr.ndim, pipeline_mode=pl.Buffered(1))
    weights = [lw["s5_d"], lw["s5_glu_w"], lw["s5_glu_b"], lw["w_gate"], lw["b_gate"], lw["w_branch"], lw["w_out"],
               lw["norm_ffn_g"], lw["w_router"], lw["b_router"]]
    return pl.pallas_call(
        _merge_kernel,
        grid=(t // MERGE_TILE,),
        in_specs=[tok(D_MODEL), const(lw["norm_mix_g"])] + [tok(MIX_WIDTH)] * 6 + [const(w) for w in weights],
        out_specs=[tok(D_MODEL), tok(1)],
        out_shape=[jax.ShapeDtypeStruct((t, D_MODEL), F32), jax.ShapeDtypeStruct((t, 1), jnp.int32)],
        compiler_params=_cparams(("parallel",), 56),
        name="merge",
    )(xf, lw["norm_mix_g"], yaf, yab, yb, yc, y5, u5, *weights)


def _router_logits(x, ng_ref, wr_ref, br_ref):
    hf = _rms(x, ng_ref[...])
    return hf, _dot3(hf, wr_ref[...]) + br_ref[...]


def _top_group(logits):
    lane = lax.broadcasted_iota(jnp.int32, logits.shape, 1).astype(F32)
    gl = jnp.where(lane < MOE_GROUPS, logits, -jnp.inf)
    gmax = jnp.max(gl, axis=1, keepdims=True)
    return jnp.min(jnp.where(gl == gmax, lane, float(V7X_LANES)), axis=1, keepdims=True).astype(jnp.int32)


def _moe_plan_kernel(pos_ref, src_ref):
    def clear(p, c):
        src_ref[p] = 0
        return c

    def place(t, c):
        src_ref[pos_ref[t]] = t
        return c

    lax.fori_loop(0, src_ref.shape[0], clear, 0, unroll=16)
    lax.fori_loop(0, pos_ref.shape[0], place, 0, unroll=16)


def _moe_plan(gidx, tile):
    t = gidx.shape[0]
    ntile = t // tile + MOE_GROUPS
    g = gidx.reshape(t)
    onehot = (g[:, None] == jnp.arange(MOE_GROUPS, dtype=jnp.int32)[None, :]).astype(jnp.int32)
    csum = jnp.cumsum(onehot, axis=0)
    rank = jnp.sum(onehot * (csum - 1), axis=1)
    count = csum[-1]
    gtiles = (count + tile - 1) // tile
    first = jnp.cumsum(gtiles) - gtiles
    pos = jnp.sum(onehot * first[None, :], axis=1) * tile + rank
    tid = jnp.arange(ntile, dtype=jnp.int32)
    tgroup = jnp.minimum(jnp.sum((tid[:, None] >= (first + gtiles)[None, :]).astype(jnp.int32), axis=1),
                         MOE_GROUPS - 1)
    oh_t = (tgroup[:, None] == jnp.arange(MOE_GROUPS, dtype=jnp.int32)[None, :]).astype(jnp.int32)
    tvalid = jnp.clip(jnp.sum(oh_t * count[None, :], axis=1) - (tid - jnp.sum(oh_t * first[None, :], axis=1)) * tile,
                      0, tile)
    src = pl.pallas_call(
        _moe_plan_kernel,
        in_specs=[pl.BlockSpec(memory_space=pltpu.SMEM)],
        out_specs=pl.BlockSpec(memory_space=pltpu.SMEM),
        out_shape=jax.ShapeDtypeStruct((ntile * tile,), jnp.int32),
        name="moe_plan",
    )(pos.astype(jnp.int32))
    return src, tgroup.astype(jnp.int32), tvalid.astype(jnp.int32)


def _moe_expert_kernel(src_ref, tg_ref, nv_ref, x_hbm, ng_ref, wr_ref, br_ref, w13_ref, w2_ref, fg_ref, o_hbm,
                       xbuf, ybuf, sem, *, final):
    i = pl.program_id(0)
    nt = pl.num_programs(0)
    tile = xbuf.shape[1]
    slot = i % 2
    nv = nv_ref[i]
    ng, ne, ff = MOE_GROUPS, MOE_EXPERTS, MOE_FF

    def rows(j, wait, row, whole):
        count = nv_ref[j]

        @pl.when(count == tile)
        def _():
            if wait:
                whole().wait()
            else:
                def body(q, c):
                    row(j * tile + 2 * q, 2 * q).start(priority=0)
                    row(j * tile + 2 * q + 1, 2 * q + 1).start(priority=1)
                    return c
                lax.fori_loop(0, tile // 2, body, 0, unroll=4)

        @pl.when((count > 0) & (count < tile))
        def _():
            def body(r, c):
                cp = row(j * tile + r, r)
                cp.wait() if wait else cp.start()
                return c
            lax.fori_loop(0, count, body, 0)

    def gather(j, s, wait):
        rows(j, wait,
             lambda p, r: pltpu.make_async_copy(x_hbm.at[pl.ds(src_ref[p], 1), :], xbuf.at[s, pl.ds(r, 1), :],
                                                sem.at[0, s]),
             lambda: pltpu.make_async_copy(x_hbm.at[pl.ds(0, tile), :], xbuf.at[s], sem.at[0, s]))

    def scatter(j, s, wait):
        rows(j, wait,
             lambda p, r: pltpu.make_async_copy(ybuf.at[s, pl.ds(r, 1), :], o_hbm.at[pl.ds(src_ref[p], 1), :],
                                                sem.at[1, s]),
             lambda: pltpu.make_async_copy(ybuf.at[s], o_hbm.at[pl.ds(0, tile), :], sem.at[1, s]))

    @pl.when(i == 0)
    def _():
        xbuf[...] = jnp.zeros_like(xbuf)
        gather(0, 0, False)

    @pl.when(i + 1 < nt)
    def _():
        gather(i + 1, 1 - slot, False)

    @pl.when(i >= 2)
    def _():
        scatter(i - 2, slot, True)

    gather(i, slot, True)

    @pl.when(nv > 0)
    def _():
        x = xbuf[slot]
        hf, logits = _router_logits(x, ng_ref, wr_ref, br_ref)
        lane = lax.broadcasted_iota(jnp.int32, logits.shape, 1).astype(F32)
        neg = -jnp.inf
        big = float(V7X_LANES)
        grp = tg_ref[i].astype(F32)
        gl = jnp.where(lane < ng, logits, neg)
        gmax = jnp.max(gl, axis=1, keepdims=True)
        glog = jnp.sum(jnp.where(lane == grp, logits, 0.0), axis=1, keepdims=True)
        gprob = jnp.exp(glog - gmax) / jnp.sum(jnp.exp(gl - gmax), axis=1, keepdims=True)
        lo = ng + ne * grp
        sel = (lane >= lo) & (lane < lo + ne)
        m1 = jnp.max(jnp.where(sel, logits, neg), axis=1, keepdims=True)
        i1 = jnp.min(jnp.where(sel & (logits == m1), lane, big), axis=1, keepdims=True)
        sel2 = sel & (lane != i1)
        m2 = jnp.max(jnp.where(sel2, logits, neg), axis=1, keepdims=True)
        i2 = jnp.min(jnp.where(sel2 & (logits == m2), lane, big), axis=1, keepdims=True)
        e2 = jnp.exp(m2 - m1)
        w1 = gprob / (1.0 + e2)
        w2 = gprob * e2 / (1.0 + e2)
        gu = _dot(hf.astype(BF16), w13_ref[0])
        act = _silu(gu[:, 0:ne * ff]) * gu[:, ne * ff:2 * ne * ff]
        parts = []
        for e in range(ne):
            wcol = jnp.where(i1 == lo + e, w1, 0.0) + jnp.where(i2 == lo + e, w2, 0.0)
            parts.append(act[:, e * ff:(e + 1) * ff] * wcol)
        y = x + _dot(jnp.concatenate(parts, axis=1).astype(BF16), w2_ref[0])
        if final:
            y = _rms(y, fg_ref[...])
        ybuf[slot] = y
        scatter(i, slot, False)

    @pl.when(i == nt - 1)
    def _():
        @pl.when(i >= 1)
        def _():
            scatter(i - 1, 1 - slot, True)
        scatter(i, slot, True)


def _moe(xf, gidx, lw, final_g, final):
    t = xf.shape[0]
    tile = MOE_TILE
    src, tgroup, tvalid = _moe_plan(gidx, tile)
    ntile = tgroup.shape[0]
    const = lambda arr: pl.BlockSpec(arr.shape, lambda i, s, g, n: (0,) * arr.ndim)
    bygroup = lambda arr: pl.BlockSpec((1,) + arr.shape[1:], lambda i, s, g, n: (g[i],) + (0,) * (arr.ndim - 1))
    any_space = pl.BlockSpec(memory_space=pl.ANY)
    return pl.pallas_call(
        functools.partial(_moe_expert_kernel, final=final),
        grid_spec=pltpu.PrefetchScalarGridSpec(
            num_scalar_prefetch=3,
            grid=(ntile,),
            in_specs=[any_space, const(lw["norm_ffn_g"]), const(lw["w_router"]), const(lw["b_router"]),
                      bygroup(lw["w_e_13"]), bygroup(lw["w_e_2"]), const(final_g)],
            out_specs=any_space,
            scratch_shapes=[pltpu.VMEM((2, tile, D_MODEL), F32), pltpu.VMEM((2, tile, D_MODEL), F32),
                            pltpu.SemaphoreType.DMA((2, 2))],
        ),
        out_shape=jax.ShapeDtypeStruct((t, D_MODEL), F32),
        compiler_params=_cparams(("arbitrary",), 40),
        name="moe_experts",
    )(src, tgroup, tvalid, xf, lw["norm_ffn_g"], lw["w_router"], lw["b_router"], lw["w_e_13"], lw["w_e_2"], final_g)


def _block_diag(blocks):
    nb, bs, _ = blocks.shape
    eye = jnp.eye(nb, dtype=blocks.dtype)
    return jnp.einsum("nij,nm->nimj", blocks, eye).reshape(nb * bs, nb * bs)


def _pad_heads(w, axis=-1):
    shape = w.shape[:-1] + (GLA_HEADS, GLA_DK)
    w = w.reshape(shape)
    pad = [(0, 0)] * (w.ndim - 1) + [(0, V7X_LANES - GLA_DK)]
    return jnp.pad(w, pad).reshape(w.shape[:-2] + (GLA_HEADS * V7X_LANES,))


def _prep_layer(w, l):
    f = lambda name: w[name][l]
    w_in = f("w_in")
    cuts = np.cumsum([MIX_WIDTH, MIX_WIDTH, 3 * MIX_WIDTH, GLA_HEADS * GLA_DK, GLA_HEADS * GLA_DK, MIX_WIDTH,
                      MIX_WIDTH, 2 * GLA_RANK]).tolist()
    xa, ga, hy, q, k, v, g, lr, s5 = jnp.split(w_in, cuts, axis=-1)
    lr = jnp.pad(lr, ((0, 0), (0, V7X_LANES - 2 * GLA_RANK)))
    w_pack = jnp.concatenate([xa, ga, hy, _pad_heads(q), _pad_heads(k), v, g, lr, s5], axis=-1).astype(BF16)
    assert w_pack.shape[1] == N_PACK
    lw = {"w_pack": w_pack, "norm_mix_g": f("norm_mix_g")[None]}
    lw["lru_conv_w"] = f("lru_conv_w")
    lw["lru_conv_b"] = f("lru_conv_b")[None]
    wa, wx = f("lru_wa"), f("lru_wx")
    lw["lru_wg"] = jnp.stack([jnp.concatenate([_block_diag(wa[d]), _block_diag(wx[d])], axis=1)
                              for d in range(2)]).astype(BF16)
    lw["lru_bg"] = jnp.concatenate([f("lru_ba"), f("lru_bx")], axis=-1)[:, None, :]
    lw["lru_lam"] = f("lru_lambda")[:, None, :]
    lw["hy_conv_w"] = f("hy_conv_w")
    lw["hy_conv_b"] = f("hy_conv_b")[None]
    lw["hy_w1p"] = _split3(jnp.pad(f("hy_w1"), ((0, V7X_LANES - HY_EMB), (0, 0))))
    lw["hy_b1"] = f("hy_b1")[None]
    lw["hy_w2"] = _split3(f("hy_w2"))
    lw["hy_b2"] = f("hy_b2")[None]
    lw["hy_w3"] = _split3(f("hy_w3"))
    lw["hy_freq"] = f("hy_freq")[None]
    lw["hy_bias"] = f("hy_bias")
    wg2 = _pad_heads(f("gla_wg2"))
    wla = jnp.zeros((2, V7X_LANES, GLA_HEADS * V7X_LANES), F32)
    wla = wla.at[0, 0:GLA_RANK].set(wg2[0]).at[1, GLA_RANK:2 * GLA_RANK].set(wg2[1])
    lw["gla_wla"] = wla.astype(BF16)
    lw["gla_bla"] = _pad_heads(f("gla_bg"))[:, None, :]
    lw["gla_norm_g"] = jnp.tile(f("gla_norm_g"), GLA_HEADS)[None]
    lw["s5_tables"] = _s5_tables(f("s5_lam_re"), f("s5_lam_im"), f("s5_log_dt"), f("s5_b_re"), f("s5_b_im"),
                                 f("s5_c_re"), f("s5_c_im"))
    lw["s5_d"] = f("s5_d")[None]
    lw["s5_glu_w"] = f("s5_glu_w").astype(BF16)
    lw["s5_glu_b"] = f("s5_glu_b")[None]
    lw["w_gate"] = f("w_gate").astype(BF16)
    lw["b_gate"] = f("b_gate")[:, None, :]
    lw["w_branch"] = f("w_branch").astype(BF16)
    lw["w_out"] = f("w_out").astype(BF16)
    wr = jnp.concatenate([f("w_router_group"), jnp.transpose(f("w_router_expert"), (1, 0, 2)).reshape(D_MODEL, -1)],
                         axis=1)
    br = jnp.concatenate([f("b_router_group"), f("b_router_expert").reshape(-1)])
    nr = MOE_GROUPS + MOE_GROUPS * MOE_EXPERTS
    lw["w_router"] = _split3(jnp.pad(wr, ((0, 0), (0, V7X_LANES - nr))))
    lw["b_router"] = jnp.pad(br, (0, V7X_LANES - nr))[None]
    lw["norm_ffn_g"] = f("norm_ffn_g")[None]
    wide = lambda a: jnp.transpose(a, (0, 2, 1, 3)).reshape(MOE_GROUPS, D_MODEL, MOE_EXPERTS * MOE_FF)
    lw["w_e_13"] = jnp.concatenate([wide(f("w_e_gate")), wide(f("w_e_up"))], axis=-1).astype(BF16)
    lw["w_e_2"] = f("w_e_down").reshape(MOE_GROUPS, MOE_EXPERTS * MOE_FF, D_MODEL).astype(BF16)
    return lw


def _encoder(x, layers, final_g):
    bsz, seq, _ = x.shape
    plan = _FftPlan(seq)
    xf = x.reshape(bsz * seq, D_MODEL)
    for l, lw in enumerate(layers):
        lru_in, hy_in, gla_in, s5_in = _inproj(xf, lw["norm_mix_g"], lw["w_pack"], lw["hy_conv_w"],
                                               lw["hy_conv_b"], seq)
        shp = lambda a: a.reshape(bsz, seq, a.shape[-1])
        yaf, yab = _lru(shp(lru_in), lw["lru_conv_w"], lw["lru_conv_b"], lw["lru_wg"], lw["lru_bg"], lw["lru_lam"])
        yb = _hyena(shp(hy_in), lw, plan)
        yc = _gla(shp(gla_in), lw["gla_wla"], lw["gla_bla"], lw["gla_norm_g"])
        y5 = _s5(shp(s5_in), *lw["s5_tables"])
        flat = lambda a: a.reshape(bsz * seq, MIX_WIDTH)
        xf, gidx = _merge(xf, lw, flat(yaf), flat(yab), flat(yb), flat(yc), flat(y5), s5_in)
        xf = _moe(xf, gidx, lw, final_g, final=(l == len(layers) - 1))
    return xf.reshape(bsz, seq, D_MODEL)


def kernel(x_prompt, x_sample, norm_mix_g, w_in, lru_conv_w, lru_conv_b, lru_wa, lru_ba, lru_wx, lru_bx,
           lru_lambda, hy_conv_w, hy_conv_b, hy_w1, hy_b1, hy_w2, hy_b2, hy_w3, hy_freq, hy_bias,
           gla_wg2, gla_bg, gla_norm_g, s5_lam_re, s5_lam_im, s5_log_dt, s5_b_re, s5_b_im, s5_c_re, s5_c_im,
           s5_d, s5_glu_w, s5_glu_b, w_branch, w_gate, b_gate, w_out, norm_ffn_g, w_router_group,
           b_router_group, w_router_expert, b_router_expert, w_e_gate, w_e_up, w_e_down, final_norm_g):
    w = dict(norm_mix_g=norm_mix_g, w_in=w_in, lru_conv_w=lru_conv_w, lru_conv_b=lru_conv_b, lru_wa=lru_wa,
             lru_ba=lru_ba, lru_wx=lru_wx, lru_bx=lru_bx, lru_lambda=lru_lambda, hy_conv_w=hy_conv_w,
             hy_conv_b=hy_conv_b, hy_w1=hy_w1, hy_b1=hy_b1, hy_w2=hy_w2, hy_b2=hy_b2, hy_w3=hy_w3,
             hy_freq=hy_freq, hy_bias=hy_bias, gla_wg2=gla_wg2, gla_bg=gla_bg, gla_norm_g=gla_norm_g,
             s5_lam_re=s5_lam_re, s5_lam_im=s5_lam_im, s5_log_dt=s5_log_dt, s5_b_re=s5_b_re, s5_b_im=s5_b_im,
             s5_c_re=s5_c_re, s5_c_im=s5_c_im, s5_d=s5_d, s5_glu_w=s5_glu_w, s5_glu_b=s5_glu_b,
             w_branch=w_branch, w_gate=w_gate, b_gate=b_gate, w_out=w_out, norm_ffn_g=norm_ffn_g,
             w_router_group=w_router_group, b_router_group=b_router_group, w_router_expert=w_router_expert,
             b_router_expert=b_router_expert, w_e_gate=w_e_gate, w_e_up=w_e_up, w_e_down=w_e_down)
    layers = [_prep_layer(w, l) for l in range(norm_mix_g.shape[0])]
    fg = final_norm_g[None]
    return (_encoder(x_prompt, layers, fg), _encoder(x_sample, layers, fg))
```

```python
import functools
import math

import numpy as np
import jax
import jax.numpy as jnp
from jax import lax
from jax.experimental import pallas as pl
from jax.experimental.pallas import tpu as pltpu

F32 = jnp.float32
BF16 = jnp.bfloat16

D_MODEL = 1024
DEPTH = 2
EPS = 1e-6
MIX_WIDTH = D_MODEL // 2
LRU_BLOCKS = 8
LRU_BLOCK = MIX_WIDTH // LRU_BLOCKS
LRU_CONV = 4
LRU_C = 8.0
HY_ORDER = 2
HY_CONV = 3
HY_EMB = 33
HY_BANDS = (HY_EMB - 1) // 2
HY_HIDDEN = 64
HY_DECAY_TARGET = 1e-2
HY_FAST_PCT = 0.3
HY_SLOW_PCT = 1.5
GLA_HEADS = 4
GLA_DK = MIX_WIDTH // 8
GLA_DV = MIX_WIDTH // GLA_HEADS
GLA_RANK = 16
GLA_TAU = 16.0
GLA_CHUNK = 64
S5_GROUP = 16
S5_GROUPS = MIX_WIDTH // S5_GROUP
S5_STATE = 64
MOE_GROUPS = 4
MOE_EXPERTS = 4
MOE_FF = D_MODEL // 4

V7X_LANES = 128
V7X_SUBLANES = 8
V7X_VMEM_BYTES = 64 * 2**20
MIB = 2**20

GLA_PACK = 4 * V7X_LANES * 2 + 512 + 512 + V7X_LANES
PK_LRU = (0, 1024)
PK_HY = (1024, 2560)
PK_GLA = (2560, 2560 + GLA_PACK)
PK_S5 = (PK_GLA[1], PK_GLA[1] + 512)
N_PACK = PK_S5[1]

TOK_TILE = 512
MOE_TILE = 256
MERGE_TILE = 256
LRU_TILE = 256
GLA_TILE = 512
GLA_BATCH = 2
S5_CHUNK = 16
S5_LANE_GROUPS = V7X_LANES // S5_GROUP
S5_SCAN_STEPS = 3
FFT_N1 = 64
FFT_ROWS = 128
FFT_COLS = 256
FFT_KB = 11


def _cparams(sem, vmem_mib):
    return pltpu.CompilerParams(dimension_semantics=sem, vmem_limit_bytes=int(vmem_mib * MIB))


def _rms(x, g):
    return x * lax.rsqrt(jnp.mean(x * x, axis=-1, keepdims=True) + EPS) * g


def _sigmoid(x):
    return 1.0 / (1.0 + jnp.exp(-x))


def _softplus(x):
    return jnp.maximum(x, 0.0) + jnp.log(1.0 + jnp.exp(-jnp.abs(x)))


def _gelu_tanh(x):
    return 0.5 * x * (1.0 + jnp.tanh(math.sqrt(2.0 / math.pi) * (x + 0.044715 * (x * x * x))))


def _silu(x):
    return x * _sigmoid(x)


def _dot(a, b):
    return jnp.dot(a, b, preferred_element_type=F32)


def _split3(w):
    hi = w.astype(BF16)
    lo = (w - hi.astype(F32)).astype(BF16)
    return jnp.concatenate([hi, lo, hi], axis=0)


def _dot3(a, w3):
    hi = a.astype(BF16)
    lo = (a - hi.astype(F32)).astype(BF16)
    return _dot(jnp.concatenate([hi, hi, lo], axis=1), w3)


def _inproj_kernel(x_ref, xp_ref, xn_ref, g_ref, w_ref, cw_ref, cb_ref, lru_ref, hy_ref, gla_ref, s5_ref, ext_ref, *,
                   tiles_per_seq):
    i = pl.program_id(0)
    tile = TOK_TILE
    h = _rms(x_ref[...], g_ref[...]).astype(BF16)
    lru_ref[...] = _dot(h, w_ref[:, PK_LRU[0]:PK_LRU[1]])
    gla_ref[...] = _dot(h, w_ref[:, PK_GLA[0]:PK_GLA[1]])
    s5_ref[...] = _dot(h, w_ref[:, PK_S5[0]:PK_S5[1]])
    w_hy = w_ref[:, PK_HY[0]:PK_HY[1]]
    edge = lambda ref: _dot(_rms(ref[...], g_ref[...]).astype(BF16), w_hy)
    _fill_ext(ext_ref, _dot(h, w_hy), edge(xp_ref), edge(xn_ref), i % tiles_per_seq == 0,
              i % tiles_per_seq == tiles_per_seq - 1, tile)
    y = cb_ref[...] + ext_ref[7:7 + tile, :] * cw_ref[0:1, :]
    y = y + ext_ref[8:8 + tile, :] * cw_ref[1:2, :]
    y = y + ext_ref[9:9 + tile, :] * cw_ref[2:3, :]
    hy_ref[...] = y


def _inproj(xf, g, w_pack, hy_cw, hy_cb, seq):
    t = xf.shape[0]
    tile = TOK_TILE
    r8 = tile // 8
    last8 = t // 8 - 1
    widths = [PK_LRU[1] - PK_LRU[0], PK_HY[1] - PK_HY[0], PK_GLA[1] - PK_GLA[0], PK_S5[1] - PK_S5[0]]
    const = lambda arr: pl.BlockSpec(arr.shape, lambda i: (0,) * arr.ndim, pipeline_mode=pl.Buffered(1))
    return pl.pallas_call(
        functools.partial(_inproj_kernel, tiles_per_seq=seq // tile),
        grid=(t // tile,),
        in_specs=[pl.BlockSpec((tile, D_MODEL), lambda i: (i, 0)),
                  pl.BlockSpec((8, D_MODEL), lambda i: (jnp.maximum(i * r8 - 1, 0), 0)),
                  pl.BlockSpec((8, D_MODEL), lambda i: (jnp.minimum((i + 1) * r8, last8), 0)),
                  const(g), const(w_pack), const(hy_cw), const(hy_cb)],
        out_specs=[pl.BlockSpec((tile, w), lambda i: (i, 0)) for w in widths],
        out_shape=[jax.ShapeDtypeStruct((t, w), F32) for w in widths],
        scratch_shapes=[pltpu.VMEM((tile + 16, widths[1]), F32)],
        compiler_params=_cparams(("parallel",), 56),
        name="inproj",
    )(xf, xf, xf, g, w_pack, hy_cw, hy_cb)


def _fill_ext(ext_ref, main, prev8, next8, first, last, tile):
    ext_ref[0:8, :] = jnp.where(first, 0.0, prev8)
    ext_ref[8:8 + tile, :] = main
    ext_ref[8 + tile:16 + tile, :] = jnp.where(last, 0.0, next8)


def _linear_scan_tile(a, b, carry, reverse):
    n = a.shape[0]
    sub = V7X_SUBLANES
    row = lax.broadcasted_iota(jnp.int32, a.shape, 0) % sub
    d = 1
    while d < sub:
        if reverse:
            a_s = pltpu.roll(a, n - d, 0)
            b_s = pltpu.roll(b, n - d, 0)
            valid = row < sub - d
        else:
            a_s = pltpu.roll(a, d, 0)
            b_s = pltpu.roll(b, d, 0)
            valid = row >= d
        b = jnp.where(valid, a * b_s + b, b)
        a = jnp.where(valid, a * a_s, a)
        d *= 2
    ngroup = n // sub
    out = [None] * ngroup
    for g in (range(ngroup - 1, -1, -1) if reverse else range(ngroup)):
        h = b[g * sub:(g + 1) * sub, :] + a[g * sub:(g + 1) * sub, :] * carry
        carry = h[0:1, :] if reverse else h[sub - 1:sub, :]
        out[g] = h
    return jnp.concatenate(out, axis=0), carry


def _lru_kernel(mf_ref, pf_ref, nf_ref, mb_ref, pb_ref, nb_ref, cw_ref, cb_ref, wg_ref, bg_ref, lam_ref,
                of_ref, ob_ref, extf_ref, extb_ref, carry_ref):
    c = pl.program_id(1)
    nc = pl.num_programs(1)
    tile = LRU_TILE

    @pl.when(c == 0)
    def _():
        carry_ref[...] = jnp.zeros_like(carry_ref)

    def one(m_ref, p_ref, n_ref, ext_ref, d, first, last, o_ref):
        x = m_ref[0, :, 0:MIX_WIDTH]
        ga = m_ref[0, :, MIX_WIDTH:2 * MIX_WIDTH]
        _fill_ext(ext_ref, x, p_ref[0], n_ref[0], first, last, tile)
        xc = cb_ref[...] + ext_ref[6:6 + tile, :] * cw_ref[0:1, :]
        xc = xc + ext_ref[7:7 + tile, :] * cw_ref[1:2, :]
        xc = xc + ext_ref[8:8 + tile, :] * cw_ref[2:3, :]
        xc = xc + ext_ref[9:9 + tile, :] * cw_ref[3:4, :]
        z = _dot(xc.astype(BF16), wg_ref[d]) + bg_ref[d]
        gate_r = _sigmoid(z[:, 0:MIX_WIDTH])
        gate_i = _sigmoid(z[:, MIX_WIDTH:2 * MIX_WIDTH])
        log_a = -LRU_C * gate_r * _softplus(-lam_ref[d])
        a = jnp.exp(log_a)
        t = 1.0 - a * a
        b = jnp.where(t > 0.0, t * lax.rsqrt(t), 0.0) * gate_i * xc
        h, last = _linear_scan_tile(a, b, carry_ref[d:d + 1, :], reverse=(d == 1))
        carry_ref[d:d + 1, :] = last
        o_ref[0] = h * _gelu_tanh(ga)

    one(mf_ref, pf_ref, nf_ref, extf_ref, 0, c == 0, c == nc - 1, of_ref)
    one(mb_ref, pb_ref, nb_ref, extb_ref, 1, c == nc - 1, c == 0, ob_ref)


def _lru(lru_in, cw, cb, wg, bg, lam):
    bsz, seq, _ = lru_in.shape
    tile = LRU_TILE
    nc = seq // tile
    r8 = tile // 8
    last8 = seq // 8 - 1

    def fwd(c):
        return c

    def bwd(c):
        return nc - 1 - c

    def specs(ch):
        return [pl.BlockSpec((1, tile, 2 * MIX_WIDTH), lambda b, c: (b, ch(c), 0)),
                pl.BlockSpec((1, 8, MIX_WIDTH), lambda b, c: (b, jnp.maximum(ch(c) * r8 - 1, 0), 0)),
                pl.BlockSpec((1, 8, MIX_WIDTH), lambda b, c: (b, jnp.minimum((ch(c) + 1) * r8, last8), 0))]

    const = lambda shape: pl.BlockSpec(shape, lambda b, c: (0,) * len(shape))
    return pl.pallas_call(
        _lru_kernel,
        grid=(bsz, nc),
        in_specs=specs(fwd) + specs(bwd) + [const(cw.shape), const(cb.shape), const(wg.shape), const(bg.shape),
                                            const(lam.shape)],
        out_specs=[pl.BlockSpec((1, tile, MIX_WIDTH), lambda b, c: (b, c, 0)),
                   pl.BlockSpec((1, tile, MIX_WIDTH), lambda b, c: (b, nc - 1 - c, 0))],
        out_shape=[jax.ShapeDtypeStruct((bsz, seq, MIX_WIDTH), F32)] * 2,
        scratch_shapes=[pltpu.VMEM((tile + 16, MIX_WIDTH), F32), pltpu.VMEM((tile + 16, MIX_WIDTH), F32),
                        pltpu.VMEM((8, MIX_WIDTH), F32)],
        compiler_params=_cparams(("parallel", "arbitrary"), 40),
        name="lru",
    )(lru_in, lru_in, lru_in, lru_in, lru_in, lru_in, cw, cb, wg, bg, lam)


class _FftPlan:
    def __init__(self, seq):
        n = 2 * seq
        n1 = FFT_N1
        n2 = n // n1
        assert n1 * n2 == n and n2 % 16 == 0
        h1 = n1 // 2 + 1
        nh = n1 // 2
        self.n, self.n1, self.n2, self.h1, self.nh = n, n1, n2, h1, nh
        self.rows = min(FFT_ROWS, n2)
        assert n2 % self.rows == 0 and h1 % FFT_KB == 0
        k1 = np.arange(h1, dtype=np.float64)
        m1 = np.arange(nh, dtype=np.float64)
        r = np.arange(8, dtype=np.float64)
        ang = -2.0 * np.pi * (k1[:, None, None] * m1[None, None, :] / n1 + r[None, :, None] * k1[:, None, None] / n)
        e = np.exp(1j * ang)
        fa = np.zeros((h1, 8, nh, 8), np.complex128)
        for rr in range(8):
            fa[:, rr, :, rr] = e[:, rr, :]
        fa = fa.reshape(h1 * 8, nh * 8)
        self.fa = jnp.asarray(np.concatenate([fa.real, fa.imag], axis=0), BF16)
        ck = np.where((k1 == 0) | (k1 == n1 // 2), 1.0, 2.0)
        ec = np.conj(e) * ck[:, None, None] / n
        fc = np.zeros((nh, 8, 2, h1, 8), np.float64)
        for rr in range(8):
            fc[:, rr, 0, :, rr] = ec[:, rr, :].real.T
            fc[:, rr, 1, :, rr] = -ec[:, rr, :].imag.T
        self.fc = jnp.asarray(fc.reshape(nh * 8, 2 * h1 * 8), BF16)
        rg = np.arange(n2 // 8, dtype=np.float64)
        tw = np.exp(-2j * np.pi * 8.0 * rg[None, :] * k1[:, None] / n)
        self.tw = jnp.asarray(np.concatenate([tw.real, tw.imag], axis=0), F32)
        tw1 = np.exp(-2j * np.pi * k1 / n)
        self.tw1 = jnp.asarray(np.stack([tw1.real, tw1.imag]), F32)
        q = np.arange(n2, dtype=np.float64)
        f2 = np.exp(-2j * np.pi * np.outer(q, q) / n2)
        fr, fi = f2.real, f2.imag
        self.gb = jnp.asarray(np.block([[fr, -fi], [fi, fr]]), BF16)
        self.gbi = jnp.asarray(np.block([[fr, fi], [-fi, fr]]), BF16)


def _fft_a_kernel(tw_ref, x_ref, fa_ref, a_ref, *, h1, nh, rows):
    rb = pl.program_id(2)
    cols = x_ref.shape[-1]

    def stage(rg):
        xg = x_ref[0, :, pl.ds(pl.multiple_of(rg * 8, 8), 8), :].reshape(nh * 8, cols).astype(BF16)
        return _dot(fa_ref[...], xg)

    def body(i, carry):
        p0 = stage(2 * i)
        p1 = stage(2 * i + 1)
        g0 = rb * (rows // 8) + 2 * i
        for k in range(h1):
            outs = []
            for p, g in ((p0, g0), (p1, g0 + 1)):
                pr = p[k * 8:(k + 1) * 8, :]
                pi = p[(h1 + k) * 8:(h1 + k + 1) * 8, :]
                tr = tw_ref[k, g]
                ti = tw_ref[h1 + k, g]
                outs.append((pr * tr - pi * ti, pr * ti + pi * tr))
            dst = pl.ds(pl.multiple_of(i * 16, 16), 16)
            a_ref[0, k, 0, dst, :] = jnp.concatenate([outs[0][0], outs[1][0]], axis=0).astype(BF16)
            a_ref[0, k, 1, dst, :] = jnp.concatenate([outs[0][1], outs[1][1]], axis=0).astype(BF16)
        return carry

    lax.fori_loop(0, rows // 16, body, 0)


def _fft_a(x4, plan, col_off=0, ncols=None):
    bq, nh, n2, width = x4.shape
    ncols = width if ncols is None else ncols
    cb0 = col_off // FFT_COLS
    rows = plan.rows
    kern = functools.partial(_fft_a_kernel, h1=plan.h1, nh=nh, rows=rows)
    return pl.pallas_call(
        kern,
        grid=(bq, ncols // FFT_COLS, n2 // rows),
        in_specs=[pl.BlockSpec(memory_space=pltpu.SMEM),
                  pl.BlockSpec((1, nh, rows, FFT_COLS), lambda b, c, r: (b, 0, r, cb0 + c)),
                  pl.BlockSpec(plan.fa.shape, lambda b, c, r: (0, 0))],
        out_specs=pl.BlockSpec((1, plan.h1, 2, rows, FFT_COLS), lambda b, c, r: (b, 0, 0, r, c)),
        out_shape=jax.ShapeDtypeStruct((bq, plan.h1, 2, n2, ncols), BF16),
        compiler_params=_cparams(("parallel", "parallel", "parallel"), 40),
        name="fft_outer_fwd",
    )(plan.tw, x4, plan.fa)


def _fft_mid_kernel(a_ref, kf_ref, gb_ref, gbi_ref, o_ref, *, n2):
    kb = a_ref.shape[1]
    cols = a_ref.shape[-1]
    for k in range(kb):
        y = _dot(gb_ref[...], a_ref[0, k].reshape(2 * n2, cols))
        yr, yi = y[0:n2, :], y[n2:2 * n2, :]
        kr, ki = kf_ref[0, k, 0], kf_ref[0, k, 1]
        z = jnp.concatenate([yr * kr - yi * ki, yr * ki + yi * kr], axis=0).astype(BF16)
        o_ref[0, k] = _dot(gbi_ref[...], z).reshape(2, n2, cols).astype(BF16)


def _fft_mid(a, kf, order, plan):
    bq, h1, _, n2, width = a.shape
    kern = functools.partial(_fft_mid_kernel, n2=n2)
    blk = (1, FFT_KB, 2, n2, FFT_COLS)
    return pl.pallas_call(
        kern,
        grid=(bq, width // FFT_COLS, h1 // FFT_KB),
        in_specs=[pl.BlockSpec(blk, lambda b, c, k: (b, k, 0, 0, c)),
                  pl.BlockSpec(blk, lambda b, c, k: (order, k, 0, 0, c)),
                  pl.BlockSpec(plan.gb.shape, lambda b, c, k: (0, 0)),
                  pl.BlockSpec(plan.gbi.shape, lambda b, c, k: (0, 0))],
        out_specs=pl.BlockSpec(blk, lambda b, c, k: (b, k, 0, 0, c)),
        out_shape=jax.ShapeDtypeStruct(a.shape, BF16),
        compiler_params=_cparams(("parallel", "parallel", "parallel"), 40),
        name="fft_inner_mul",
    )(a, kf, plan.gb, plan.gbi)


def _fft_c_kernel(tw_ref, b_ref, u_ref, g_ref, bias_ref, fc_ref, o_ref, *, h1, nh, rows):
    rb = pl.program_id(2)
    cols = o_ref.shape[-1]

    def body(i, carry):
        src = pl.ds(pl.multiple_of(i * 16, 16), 16)
        tiles = [[b_ref[0, k, p, src, :].astype(F32) for p in range(2)] for k in range(h1)]
        for half in range(2):
            g = rb * (rows // 8) + 2 * i + half
            re_rows, im_rows = [], []
            for k in range(h1):
                br = tiles[k][0][half * 8:(half + 1) * 8, :]
                bi = tiles[k][1][half * 8:(half + 1) * 8, :]
                tr = tw_ref[k, g]
                ti = tw_ref[h1 + k, g]
                re_rows.append(br * tr + bi * ti)
                im_rows.append(bi * tr - br * ti)
            s = jnp.concatenate(re_rows + im_rows, axis=0).astype(BF16)
            y = _dot(fc_ref[...], s).reshape(nh, 8, cols)
            dst = pl.ds(pl.multiple_of((2 * i + half) * 8, 8), 8)
            u = u_ref[0, :, dst, :]
            o_ref[0, :, dst, :] = (y + u * bias_ref[...]) * g_ref[0, :, dst, :]
        return carry

    lax.fori_loop(0, rows // 16, body, 0)


def _fft_c(bm, u4, u_off, g4, g_off, bias, plan):
    bq, h1, _, n2, width = bm.shape
    nh = plan.nh
    rows = plan.rows
    ub, gbk = u_off // FFT_COLS, g_off // FFT_COLS
    kern = functools.partial(_fft_c_kernel, h1=h1, nh=nh, rows=rows)
    xblk = (1, nh, rows, FFT_COLS)
    return pl.pallas_call(
        kern,
        grid=(bq, width // FFT_COLS, n2 // rows),
        in_specs=[pl.BlockSpec(memory_space=pltpu.SMEM),
                  pl.BlockSpec((1, h1, 2, rows, FFT_COLS), lambda b, c, r: (b, 0, 0, r, c)),
                  pl.BlockSpec(xblk, lambda b, c, r: (b, 0, r, ub + c)),
                  pl.BlockSpec(xblk, lambda b, c, r: (b, 0, r, gbk + c)),
                  pl.BlockSpec((1, FFT_COLS), lambda b, c, r: (0, c)),
                  pl.BlockSpec(plan.fc.shape, lambda b, c, r: (0, 0))],
        out_specs=pl.BlockSpec(xblk, lambda b, c, r: (b, 0, r, c)),
        out_shape=jax.ShapeDtypeStruct((bq, nh, n2, width), F32),
        compiler_params=_cparams(("parallel", "parallel", "parallel"), 48),
        name="fft_outer_inv",
    )(plan.tw, bm, u4, g4, bias, plan.fc)


def _hyfilt_gen_kernel(w1_ref, b1_ref, w2_ref, b2_ref, w3_ref, fr_ref, o_ref, ss_ref, *, seq):
    rblk = pl.program_id(0)
    tile, cols = o_ref.shape[1], o_ref.shape[2]
    irow = lax.broadcasted_iota(jnp.int32, (tile, V7X_LANES), 0) + rblk * tile
    row = irow.astype(F32)
    lane = lax.broadcasted_iota(jnp.int32, (tile, V7X_LANES), 1)
    t = row / (seq - 1.0)
    omega = (2.0 * math.pi / seq) * row
    band_step = (HY_BANDS - 1 - 1e-4) / (HY_BANDS - 1)
    is_cos = (lane >= 1) & (lane <= HY_BANDS)
    is_sin = (lane > HY_BANDS) & (lane <= 2 * HY_BANDS)
    bidx = jnp.where(is_cos, lane - 1, lane - 1 - HY_BANDS).astype(F32)
    ang = omega * (1e-4 + band_step * bidx)
    trig = jnp.cos(ang + jnp.where(is_sin, 0.5 * math.pi, 0.0))
    z = jnp.where(lane == 0, t, jnp.where(is_cos | is_sin, trig, 0.0))
    fr = fr_ref[...]
    hid = jnp.sin(fr * (_dot3(z, w1_ref[...]) + b1_ref[...]))
    hid = jnp.sin(fr * (_dot3(hid, w2_ref[...]) + b2_ref[...]))
    filt = _dot3(hid, w3_ref[...])
    col = lax.broadcasted_iota(jnp.int32, (1, cols), 1)
    chan = (col % MIX_WIDTH).astype(F32)
    max_decay = math.log(HY_DECAY_TARGET) / HY_FAST_PCT
    min_decay = math.log(HY_DECAY_TARGET) / HY_SLOW_PCT
    delta = jnp.abs(min_decay + (max_decay - min_decay) / (MIX_WIDTH - 1) * chan)
    filt = filt * jnp.exp(-t[:, 0:1] * delta)
    is_bwd = (col // MIX_WIDTH) % 2 == 1
    rows_c = lax.broadcasted_iota(jnp.int32, (tile, cols), 0) + rblk * tile
    filt = jnp.where(is_bwd & (rows_c == seq - 1), 0.0, filt)
    o_ref[0] = filt

    @pl.when(rblk == 0)
    def _():
        ss_ref[...] = jnp.zeros_like(ss_ref)

    ss_ref[...] += jnp.sum(filt * filt, axis=0, keepdims=True)


def _hyfilt_gen(seq, w1p, b1, w2, b2, w3, freq):
    ncol = w3.shape[1]
    tile = min(seq, 512)
    const = lambda shape: pl.BlockSpec(shape, lambda r: (0,) * len(shape))
    kern = functools.partial(_hyfilt_gen_kernel, seq=seq)
    return pl.pallas_call(
        kern,
        grid=(seq // tile,),
        in_specs=[const(w1p.shape), const(b1.shape), const(w2.shape), const(b2.shape), const(w3.shape),
                  const(freq.shape)],
        out_specs=[pl.BlockSpec((1, tile, ncol), lambda r: (0, r, 0)),
                   pl.BlockSpec((1, ncol), lambda r: (0, 0))],
        out_shape=[jax.ShapeDtypeStruct((1, seq, ncol), F32), jax.ShapeDtypeStruct((1, ncol), F32)],
        compiler_params=_cparams(("arbitrary",), 40),
        name="hyena_filter_gen",
    )(w1p, b1, w2, b2, w3, freq)


def _hyfilt_spec_kernel(tw1_ref, a0_ref, a1_ref, ss0_ref, ss1_ref, gb_ref, o_ref, *, n2):
    kblk = pl.program_id(2)
    kb = a0_ref.shape[1]
    cols = a0_ref.shape[-1]
    scale = lax.rsqrt(ss0_ref[...] + ss1_ref[...] + EPS)
    ang2 = (-2.0 * math.pi / n2) * lax.broadcasted_iota(jnp.int32, (n2, cols), 0).astype(F32)
    cr, ci = jnp.cos(ang2), jnp.sin(ang2)
    for k in range(kb):
        y0 = _dot(gb_ref[...], a0_ref[0, k].reshape(2 * n2, cols))
        y1 = _dot(gb_ref[...], a1_ref[0, k].reshape(2 * n2, cols))
        sr = tw1_ref[0, kblk * kb + k]
        si = tw1_ref[1, kblk * kb + k]
        wr, wi = cr * sr - ci * si, cr * si + ci * sr
        y1r, y1i = y1[0:n2, :], y1[n2:2 * n2, :]
        o_ref[0, k, 0] = (y0[0:n2, :] + (wr * y1r - wi * y1i)) * scale
        o_ref[0, k, 1] = (y0[n2:2 * n2, :] - (wr * y1i + wi * y1r)) * scale


def _hyfilt_spec(af, ss, plan):
    _, h1, _, n2, _ = af.shape
    ncb = MIX_WIDTH // FFT_COLS
    blk = (1, FFT_KB, 2, n2, FFT_COLS)
    kern = functools.partial(_hyfilt_spec_kernel, n2=n2)
    return pl.pallas_call(
        kern,
        grid=(HY_ORDER, ncb, h1 // FFT_KB),
        in_specs=[pl.BlockSpec(memory_space=pltpu.SMEM),
                  pl.BlockSpec(blk, lambda o, c, k: (0, k, 0, 0, o * 2 * ncb + c)),
                  pl.BlockSpec(blk, lambda o, c, k: (0, k, 0, 0, o * 2 * ncb + ncb + c)),
                  pl.BlockSpec((1, FFT_COLS), lambda o, c, k: (0, o * 2 * ncb + c)),
                  pl.BlockSpec((1, FFT_COLS), lambda o, c, k: (0, o * 2 * ncb + ncb + c)),
                  pl.BlockSpec(plan.gb.shape, lambda o, c, k: (0, 0))],
        out_specs=pl.BlockSpec(blk, lambda o, c, k: (o, k, 0, 0, c)),
        out_shape=jax.ShapeDtypeStruct((HY_ORDER, h1, 2, n2, MIX_WIDTH), F32),
        compiler_params=_cparams(("parallel", "parallel", "parallel"), 40),
        name="hyena_filter_spectrum",
    )(plan.tw1, af, af, ss, ss, plan.gb)


def _hyena(zc, lw, plan):
    bsz, seq, _ = zc.shape
    filt, ss = _hyfilt_gen(seq, lw["hy_w1p"], lw["hy_b1"], lw["hy_w2"], lw["hy_b2"], lw["hy_w3"], lw["hy_freq"])
    af = _fft_a(filt.reshape(1, plan.nh, plan.n2, filt.shape[-1]), plan)
    kf = _hyfilt_spec(af, ss, plan)
    zc4 = zc.reshape(bsz, plan.nh, plan.n2, 3 * MIX_WIDTH)
    a = _fft_a(zc4, plan, col_off=0, ncols=MIX_WIDTH)
    bm = _fft_mid(a, kf, 0, plan)
    z1 = _fft_c(bm, zc4, 0, zc4, MIX_WIDTH, lw["hy_bias"][0:1], plan)
    a = _fft_a(z1, plan)
    bm = _fft_mid(a, kf, 1, plan)
    z2 = _fft_c(bm, z1, 0, zc4, 2 * MIX_WIDTH, lw["hy_bias"][1:2], plan)
    return z2.reshape(bsz, seq, MIX_WIDTH)


def _gla_kernel(*refs, reverse):
    if reverse:
        x_ref, of_ref, wla_ref, bla_ref, ng_ref, o_ref, st_ref = refs
    else:
        x_ref, wla_ref, bla_ref, o_ref, st_ref = refs
    c = pl.program_id(1)

    @pl.when(c == 0)
    def _():
        st_ref[...] = jnp.zeros_like(st_ref)

    for bb in range(x_ref.shape[0]):
        _gla_sequence(bb, x_ref, of_ref if reverse else None, wla_ref, bla_ref, ng_ref if reverse else None, o_ref,
                      st_ref, reverse)


def _gla_sequence(bb, x_ref, of_ref, wla_ref, bla_ref, ng_ref, o_ref, st_ref, reverse):
    tile = GLA_TILE
    ck = GLA_CHUNK
    nck = tile // ck
    hw = V7X_LANES
    nh = GLA_HEADS
    q = x_ref[bb, :, 0:nh * hw] * (GLA_DK ** -0.5)
    k = x_ref[bb, :, nh * hw:2 * nh * hw]
    v = x_ref[bb, :, 2 * nh * hw:3 * nh * hw]
    lr = x_ref[bb, :, 4 * nh * hw:4 * nh * hw + hw]
    zl = _dot(lr.astype(BF16), wla_ref[...]) + bla_ref[...]
    la = (jnp.minimum(zl, 0.0) - jnp.log(1.0 + jnp.exp(-jnp.abs(zl)))) / GLA_TAU

    row = lax.broadcasted_iota(jnp.int32, la.shape, 0) % ck
    bcum = la
    d = 1
    while d < ck:
        if reverse:
            bcum = bcum + jnp.where(row < ck - d, pltpu.roll(bcum, tile - d, 0), 0.0)
        else:
            bcum = bcum + jnp.where(row >= d, pltpu.roll(bcum, d, 0), 0.0)
        d *= 2
    b3 = bcum.reshape(nck, ck, nh * hw)
    blast = b3[:, 0:1, :] if reverse else b3[:, ck - 1:ck, :]
    q_e = (q * jnp.exp(bcum)).astype(BF16)
    k_e = (k * jnp.exp(-bcum)).astype(BF16)
    k_d = (k.reshape(nck, ck, nh * hw) * jnp.exp(blast - b3)).reshape(tile, nh * hw).astype(BF16)
    gch = jnp.exp(blast)
    vb = v.astype(BF16)

    ri = lax.broadcasted_iota(jnp.int32, (ck, ck), 0)
    ci = lax.broadcasted_iota(jnp.int32, (ck, ck), 1)
    mask = (ri <= ci) if reverse else (ri >= ci)
    order = range(nck - 1, -1, -1) if reverse else range(nck)
    outs = [None] * nck
    for n in order:
        rs = slice(n * ck, (n + 1) * ck)
        heads = []
        for h in range(nh):
            ls = slice(h * hw, (h + 1) * hw)
            qe, ke, kd, vh = q_e[rs, ls], k_e[rs, ls], k_d[rs, ls], vb[rs, ls]
            st = st_ref[bb, h]
            sc = lax.dot_general(qe, ke, (((1,), (1,)), ((), ())), preferred_element_type=F32)
            sc = jnp.where(mask, sc, 0.0).astype(BF16)
            o = _dot(sc, vh) + lax.dot_general(qe, st.astype(BF16), (((1,), (1,)), ((), ())),
                                               preferred_element_type=F32)
            upd = lax.dot_general(vh, kd, (((0,), (0,)), ((), ())), preferred_element_type=F32)
            st_ref[bb, h] = st * gch[n, :, ls] + upd
            heads.append(o)
        outs[n] = jnp.concatenate(heads, axis=1)
    o_dir = jnp.concatenate(outs, axis=0)
    if not reverse:
        o_ref[bb] = o_dir
        return
    o = of_ref[bb] + o_dir
    g = x_ref[bb, :, 3 * nh * hw:4 * nh * hw]
    normed = []
    for h in range(nh):
        oh = o[:, h * hw:(h + 1) * hw]
        normed.append(oh * lax.rsqrt(jnp.mean(oh * oh, axis=-1, keepdims=True) + EPS))
    o = jnp.concatenate(normed, axis=1) * ng_ref[...]
    o_ref[bb] = o * _silu(g)


def _gla(gla_in, wla, bla, norm_g):
    bsz, seq, width = gla_in.shape
    tile = GLA_TILE
    nc = seq // tile
    nb = min(GLA_BATCH, bsz)
    const = lambda shape: pl.BlockSpec(shape, lambda b, c: (0,) * len(shape))
    out_shape = jax.ShapeDtypeStruct((bsz, seq, MIX_WIDTH), F32)
    scratch = [pltpu.VMEM((nb, GLA_HEADS, GLA_DV, V7X_LANES), F32)]
    o_f = pl.pallas_call(
        functools.partial(_gla_kernel, reverse=False),
        grid=(bsz // nb, nc),
        in_specs=[pl.BlockSpec((nb, tile, width), lambda b, c: (b, c, 0)), const(wla.shape[1:]), const(bla.shape[1:])],
        out_specs=pl.BlockSpec((nb, tile, MIX_WIDTH), lambda b, c: (b, c, 0)),
        out_shape=out_shape,
        scratch_shapes=scratch,
        compiler_params=_cparams(("parallel", "arbitrary"), 56),
        name="gla_fwd",
    )(gla_in, wla[0], bla[0])
    return pl.pallas_call(
        functools.partial(_gla_kernel, reverse=True),
        grid=(bsz // nb, nc),
        in_specs=[pl.BlockSpec((nb, tile, width), lambda b, c: (b, nc - 1 - c, 0)),
                  pl.BlockSpec((nb, tile, MIX_WIDTH), lambda b, c: (b, nc - 1 - c, 0)),
                  const(wla.shape[1:]), const(bla.shape[1:]), const(norm_g.shape)],
        out_specs=pl.BlockSpec((nb, tile, MIX_WIDTH), lambda b, c: (b, nc - 1 - c, 0)),
        out_shape=out_shape,
        scratch_shapes=scratch,
        compiler_params=_cparams(("parallel", "arbitrary"), 56),
        name="gla_bwd",
    )(gla_in, o_f, wla[1], bla[1], norm_g)


def _s5_kernel(u_ref, t_ref, e_ref, o_ref_w, mu_ref, mup_ref, y_ref):
    ch = S5_CHUNK
    sub = V7X_SUBLANES
    nrow = u_ref.shape[1] // ch
    half = S5_LANE_GROUPS * 2 * S5_STATE
    swap = lambda x: pltpu.roll(x, half // 2, 1)
    u = jnp.concatenate([u_ref[0, pl.ds(i, nrow, stride=ch), :] for i in range(ch)], axis=1).astype(BF16)
    y = _dot(u, t_ref[0])
    he = _dot(u, e_ref[0])
    row = lax.broadcasted_iota(jnp.int32, (nrow, half), 0)
    rsub = row % sub
    states = []
    for d in range(2):
        h = he[:, d * half:(d + 1) * half]
        for s in range(S5_SCAN_STEPS):
            step = 2 ** s
            if d == 0:
                hs = jnp.where(rsub >= step, pltpu.roll(h, step, 0), 0.0)
            else:
                hs = jnp.where(rsub < sub - step, pltpu.roll(h, nrow - step, 0), 0.0)
            h = h + hs * mu_ref[0, d, s, 0:1, :] + swap(hs) * mu_ref[0, d, s, 1:2, :]
        ngroup = nrow // sub
        carry = jnp.zeros((1, half), F32)
        out = [None] * ngroup
        for g in (range(ngroup) if d == 0 else range(ngroup - 1, -1, -1)):
            hg = h[g * sub:(g + 1) * sub, :] + carry * mup_ref[0, d, 0] + swap(carry) * mup_ref[0, d, 1]
            carry = hg[sub - 1:sub, :] if d == 0 else hg[0:1, :]
            out[g] = hg
        h = jnp.concatenate(out, axis=0)
        if d == 0:
            h = jnp.where(row >= 1, pltpu.roll(h, 1, 0), 0.0)
        else:
            h = jnp.where(row < nrow - 1, pltpu.roll(h, nrow - 1, 0), 0.0)
        states.append(h)
    hp = jnp.concatenate(states, axis=1).astype(BF16)
    y = y + lax.dot_general(hp, o_ref_w[0], (((1,), (1,)), ((), ())), preferred_element_type=F32)
    for j in range(ch):
        y_ref[0, pl.ds(j, nrow, stride=ch), :] = y[:, j * V7X_LANES:(j + 1) * V7X_LANES]


def _s5(s5_in, tblk, eblk, oblk, mu, mup):
    bsz, seq, width = s5_in.shape
    nb = width // V7X_LANES
    once = pl.Buffered(1)
    wspec = lambda arr: pl.BlockSpec((1,) + arr.shape[1:], lambda k, b: (k,) + (0,) * (arr.ndim - 1),
                                     pipeline_mode=once)
    xspec = pl.BlockSpec((1, seq, V7X_LANES), lambda k, b: (b, 0, k))
    return pl.pallas_call(
        _s5_kernel,
        grid=(nb, bsz),
        in_specs=[xspec, wspec(tblk), wspec(eblk), wspec(oblk), wspec(mu), wspec(mup)],
        out_specs=pl.BlockSpec((1, seq, V7X_LANES), lambda k, b: (b, 0, k)),
        out_shape=jax.ShapeDtypeStruct(s5_in.shape, F32),
        compiler_params=_cparams(("arbitrary", "arbitrary"), 60),
        name="s5",
    )(s5_in, tblk, eblk, oblk, mu, mup)


def _s5_tables(lam_re, lam_im, log_dt, b_re, b_im, c_re, c_im):
    ch = S5_CHUNK
    cmul = lambda x, y: (x[0] * y[0] - x[1] * y[1], x[0] * y[1] + x[1] * y[0])
    lr_, li_ = lam_re.astype(F32), lam_im.astype(F32)
    dt = jnp.exp(log_dt.astype(F32))[..., None]
    ar, ai = lr_ * dt, li_ * dt

    def lam_pow(tau):
        t = jnp.asarray(tau, F32)
        t = t.reshape(t.shape + (1,) * 3)
        mag = jnp.exp(t * ar)
        return mag * jnp.cos(t * ai), mag * jnp.sin(t * ai)

    lb = lam_pow(jnp.ones((), F32))
    num = (lb[0] - 1.0, lb[1])
    den = lr_ * lr_ + li_ * li_
    ratio = ((num[0] * lr_ + num[1] * li_) / den, (num[1] * lr_ - num[0] * li_) / den)
    b_bar = cmul((ratio[0][..., None], ratio[1][..., None]), (b_re.astype(F32), b_im.astype(F32)))
    cc = (c_re.astype(F32), c_im.astype(F32))

    nb, gb, hh, pp = S5_GROUPS // S5_LANE_GROUPS, S5_LANE_GROUPS, S5_GROUP, S5_STATE
    eye = jnp.eye(gb, dtype=F32)

    def b_base(x):
        x = jnp.transpose(x.reshape(2, nb, gb, pp, hh), (1, 2, 4, 0, 3))
        return (x[:, :, :, :, None, :] * eye[None, :, None, None, :, None]).reshape(nb, gb * hh, 2 * gb * pp)

    def c_base(x):
        x = jnp.transpose(x.reshape(2, nb, gb, hh, pp), (1, 2, 3, 0, 4))
        return (x[:, :, :, :, None, :] * eye[None, :, None, None, :, None]).reshape(nb, gb * hh, 2 * gb * pp)

    row = lambda x: jnp.transpose(x.reshape(2, nb, gb * pp), (1, 0, 2)).reshape(nb, 1, 2 * gb * pp)
    base = (b_base(b_bar[0]), b_base(b_bar[1]), c_base(cc[0]), c_base(cc[1]), row(ar), row(ai))
    dtab = _s5_lag_tables(*base)
    return _s5_block_tables(dtab, *base)


def _lam_pow(tau, ar, ai):
    mag = jnp.exp(tau * ar)
    return mag * jnp.cos(tau * ai), mag * jnp.sin(tau * ai)


def _s5_lag_kernel(btr_ref, bti_ref, ctr_ref, cti_ref, ar_ref, ai_ref, d_ref):
    ch = S5_CHUNK
    half = btr_ref.shape[-1] // 2
    def split(x):
        hi = x.astype(BF16)
        return hi, (x - hi.astype(F32)).astype(BF16)

    def tdot(a, b_split):
        ah, al = split(a)
        bh, bl = b_split
        return lax.dot_general(jnp.concatenate([ah, ah, al], axis=1), jnp.concatenate([bh, bl, bh], axis=1),
                               (((1,), (1,)), ((), ())), preferred_element_type=F32)

    for d in range(2):
        ls = slice(d * half, (d + 1) * half)
        btr, bti = btr_ref[0, :, ls], bti_ref[0, :, ls]
        ctr, cti = split(ctr_ref[0, :, ls]), split(cti_ref[0, :, ls])
        for lag in range(ch):
            lr, li = _lam_pow(float(lag), ar_ref[0, :, ls], ai_ref[0, :, ls])
            val = tdot(btr * lr - bti * li, ctr) - tdot(btr * li + bti * lr, cti)
            idx = ch - 1 + lag if d == 0 else ch - 1 - lag
            if d == 1 and lag == 0:
                d_ref[0, idx] = d_ref[0, idx] + val
            else:
                d_ref[0, idx] = val


def _s5_lag_tables(btr, bti, ctr, cti, ar, ai):
    nb = btr.shape[0]
    spec = lambda a: pl.BlockSpec((1,) + a.shape[1:], lambda k: (k, 0, 0))
    nlag = 2 * S5_CHUNK - 1
    return pl.pallas_call(
        _s5_lag_kernel,
        grid=(nb,),
        in_specs=[spec(a) for a in (btr, bti, ctr, cti, ar, ai)],
        out_specs=pl.BlockSpec((1, nlag, V7X_LANES, V7X_LANES), lambda k: (k, 0, 0, 0)),
        out_shape=jax.ShapeDtypeStruct((nb, nlag, V7X_LANES, V7X_LANES), F32),
        compiler_params=_cparams(("parallel",), 32),
        name="s5_lag_tables",
    )(btr, bti, ctr, cti, ar, ai)


def _s5_block_kernel(d_ref, btr_ref, bti_ref, ctr_ref, cti_ref, ar_ref, ai_ref, t_ref, e_ref, ot_ref, mu_ref,
                     mup_ref):
    ch = S5_CHUNK
    i = pl.program_id(1)
    half = btr_ref.shape[-1] // 2
    fi = i.astype(F32)
    for j in range(ch):
        t_ref[0, :, j * V7X_LANES:(j + 1) * V7X_LANES] = d_ref[0, j - i + ch - 1].astype(BF16)
    for d in range(2):
        ls = slice(d * half, (d + 1) * half)
        ar, ai = ar_ref[0, :, ls], ai_ref[0, :, ls]
        lr, li = _lam_pow(fi if d == 1 else (ch - 1.0) - fi, ar, ai)
        btr, bti = btr_ref[0, :, ls], bti_ref[0, :, ls]
        e_ref[0, :, 2 * d * half:(2 * d + 1) * half] = (btr * lr - bti * li).astype(BF16)
        e_ref[0, :, (2 * d + 1) * half:(2 * d + 2) * half] = (btr * li + bti * lr).astype(BF16)
        lr, li = _lam_pow(fi + 1.0 if d == 0 else ch - fi, ar, ai)
        ctr, cti = ctr_ref[0, :, ls], cti_ref[0, :, ls]
        ot_ref[0, :, 2 * d * half:(2 * d + 1) * half] = (ctr * lr - cti * li).astype(BF16)
        ot_ref[0, :, (2 * d + 1) * half:(2 * d + 2) * half] = (-(ctr * li + cti * lr)).astype(BF16)

    @pl.when(i == 0)
    def _():
        for d in range(2):
            ar, ai = ar_ref[0, :, d * half:(d + 1) * half], ai_ref[0, :, d * half:(d + 1) * half]
            for s in range(S5_SCAN_STEPS):
                lr, li = _lam_pow(float(ch * 2 ** s), ar, ai)
                mu_ref[0, d, s, 0:1, :] = jnp.concatenate([lr, lr], axis=1)
                mu_ref[0, d, s, 1:2, :] = jnp.concatenate([-li, li], axis=1)
            for r in range(V7X_SUBLANES):
                lr, li = _lam_pow(float(ch * (r + 1 if d == 0 else V7X_SUBLANES - r)), ar, ai)
                mup_ref[0, d, 0, r:r + 1, :] = jnp.concatenate([lr, lr], axis=1)
                mup_ref[0, d, 1, r:r + 1, :] = jnp.concatenate([-li, li], axis=1)


def _s5_block_tables(dtab, btr, bti, ctr, cti, ar, ai):
    nb = btr.shape[0]
    ch = S5_CHUNK
    big = ch * V7X_LANES
    wide = 2 * btr.shape[-1]
    spec = lambda a: pl.BlockSpec((1,) + a.shape[1:], lambda k, i: (k,) + (0,) * (a.ndim - 1))
    tile = lambda w: pl.BlockSpec((1, V7X_LANES, w), lambda k, i: (k, i, 0))
    return pl.pallas_call(
        _s5_block_kernel,
        grid=(nb, ch),
        in_specs=[spec(a) for a in (dtab, btr, bti, ctr, cti, ar, ai)],
        out_specs=[tile(big), tile(wide), tile(wide),
                   pl.BlockSpec((1, 2, S5_SCAN_STEPS, 2, wide // 2), lambda k, i: (k, 0, 0, 0, 0)),
                   pl.BlockSpec((1, 2, 2, V7X_SUBLANES, wide // 2), lambda k, i: (k, 0, 0, 0, 0))],
        out_shape=[jax.ShapeDtypeStruct((nb, big, big), BF16), jax.ShapeDtypeStruct((nb, big, wide), BF16),
                   jax.ShapeDtypeStruct((nb, big, wide), BF16),
                   jax.ShapeDtypeStruct((nb, 2, S5_SCAN_STEPS, 2, wide // 2), F32),
                   jax.ShapeDtypeStruct((nb, 2, 2, V7X_SUBLANES, wide // 2), F32)],
        compiler_params=_cparams(("parallel", "arbitrary"), 32),
        name="s5_block_tables",
    )(dtab, btr, bti, ctr, cti, ar, ai)


def _merge_kernel(x_ref, ng_ref, yaf_ref, yab_ref, yb_ref, yc_ref, y5_ref, u5_ref, d5_ref, gluw_ref, glub_ref,
                  wgate_ref, bgate_ref, wbr_ref, wout_ref, fng_ref, wr_ref, br_ref, o_ref, g_ref):
    x = x_ref[...]
    h = _rms(x, ng_ref[...]).astype(BF16)
    y_d = _gelu_tanh(u5_ref[...] * d5_ref[...] + y5_ref[...])
    y_d = y_d * _sigmoid(_dot(y_d.astype(BF16), gluw_ref[...]) + glub_ref[...])
    branches = (yaf_ref[...] + yab_ref[...], yb_ref[...], yc_ref[...], y_d)
    merged = jnp.zeros(x.shape, F32)
    for i, y in enumerate(branches):
        gate = _sigmoid(_dot(h, wgate_ref[i]) + bgate_ref[i])
        merged = merged + gate * _dot(y.astype(BF16), wbr_ref[i])
    x_new = x + _dot(merged.astype(BF16), wout_ref[...])
    o_ref[...] = x_new
    g_ref[...] = _top_group(_router_logits(x_new, fng_ref, wr_ref, br_ref)[1])


def _merge(xf, lw, yaf, yab, yb, yc, y5, u5):
    t = xf.shape[0]
    tok = lambda w: pl.BlockSpec((MERGE_TILE, w), lambda i: (i, 0))
    const = lambda arr: pl.BlockSpec(arr.shape, lambda i: (0,) * arr.ndim, pipeline_mode=pl.Buffered(1))
    weights = [lw["s5_d"], lw["s5_glu_w"], lw["s5_glu_b"], lw["w_gate"], lw["b_gate"], lw["w_branch"], lw["w_out"],
               lw["norm_ffn_g"], lw["w_router"], lw["b_router"]]
    return pl.pallas_call(
        _merge_kernel,
        grid=(t // MERGE_TILE,),
        in_specs=[tok(D_MODEL), const(lw["norm_mix_g"])] + [tok(MIX_WIDTH)] * 6 + [const(w) for w in weights],
        out_specs=[tok(D_MODEL), tok(1)],
        out_shape=[jax.ShapeDtypeStruct((t, D_MODEL), F32), jax.ShapeDtypeStruct((t, 1), jnp.int32)],
        compiler_params=_cparams(("parallel",), 56),
        name="merge",
    )(xf, lw["norm_mix_g"], yaf, yab, yb, yc, y5, u5, *weights)


def _router_logits(x, ng_ref, wr_ref, br_ref):
    hf = _rms(x, ng_ref[...])
    return hf, _dot3(hf, wr_ref[...]) + br_ref[...]


def _top_group(logits):
    lane = lax.broadcasted_iota(jnp.int32, logits.shape, 1).astype(F32)
    gl = jnp.where(lane < MOE_GROUPS, logits, -jnp.inf)
    gmax = jnp.max(gl, axis=1, keepdims=True)
    return jnp.min(jnp.where(gl == gmax, lane, float(V7X_LANES)), axis=1, keepdims=True).astype(jnp.int32)


def _moe_plan_kernel(pos_ref, src_ref):
    def clear(p, c):
        src_ref[p] = 0
        return c

    def place(t, c):
        src_ref[pos_ref[t]] = t
        return c

    lax.fori_loop(0, src_ref.shape[0], clear, 0, unroll=16)
    lax.fori_loop(0, pos_ref.shape[0], place, 0, unroll=16)


def _moe_plan(gidx, tile):
    t = gidx.shape[0]
    ntile = t // tile + MOE_GROUPS
    g = gidx.reshape(t)
    onehot = (g[:, None] == jnp.arange(MOE_GROUPS, dtype=jnp.int32)[None, :]).astype(jnp.int32)
    csum = jnp.cumsum(onehot, axis=0)
    rank = jnp.sum(onehot * (csum - 1), axis=1)
    count = csum[-1]
    gtiles = (count + tile - 1) // tile
    first = jnp.cumsum(gtiles) - gtiles
    pos = jnp.sum(onehot * first[None, :], axis=1) * tile + rank
    tid = jnp.arange(ntile, dtype=jnp.int32)
    tgroup = jnp.minimum(jnp.sum((tid[:, None] >= (first + gtiles)[None, :]).astype(jnp.int32), axis=1),
                         MOE_GROUPS - 1)
    oh_t = (tgroup[:, None] == jnp.arange(MOE_GROUPS, dtype=jnp.int32)[None, :]).astype(jnp.int32)
    tvalid = jnp.clip(jnp.sum(oh_t * count[None, :], axis=1) - (tid - jnp.sum(oh_t * first[None, :], axis=1)) * tile,
                      0, tile)
    src = pl.pallas_call(
        _moe_plan_kernel,
        in_specs=[pl.BlockSpec(memory_space=pltpu.SMEM)],
        out_specs=pl.BlockSpec(memory_space=pltpu.SMEM),
        out_shape=jax.ShapeDtypeStruct((ntile * tile,), jnp.int32),
        name="moe_plan",
    )(pos.astype(jnp.int32))
    return src, tgroup.astype(jnp.int32), tvalid.astype(jnp.int32)


def _moe_expert_kernel(src_ref, tg_ref, nv_ref, x_hbm, ng_ref, wr_ref, br_ref, w13_ref, w2_ref, fg_ref, o_hbm,
                       xbuf, ybuf, sem, *, final):
    i = pl.program_id(0)
    nt = pl.num_programs(0)
    tile = xbuf.shape[1]
    slot = i % 2
    nv = nv_ref[i]
    ng, ne, ff = MOE_GROUPS, MOE_EXPERTS, MOE_FF

    def rows(j, wait, row, whole):
        count = nv_ref[j]

        @pl.when(count == tile)
        def _():
            if wait:
                whole().wait()
            else:
                def body(r, c):
                    row(j * tile + r, r).start()
                    return c
                lax.fori_loop(0, tile, body, 0, unroll=16)

        @pl.when((count > 0) & (count < tile))
        def _():
            def body(r, c):
                cp = row(j * tile + r, r)
                cp.wait() if wait else cp.start()
                return c
            lax.fori_loop(0, count, body, 0)

    def gather(j, s, wait):
        rows(j, wait,
             lambda p, r: pltpu.make_async_copy(x_hbm.at[pl.ds(src_ref[p], 1), :], xbuf.at[s, pl.ds(r, 1), :],
                                                sem.at[0, s]),
             lambda: pltpu.make_async_copy(x_hbm.at[pl.ds(0, tile), :], xbuf.at[s], sem.at[0, s]))

    def scatter(j, s, wait):
        rows(j, wait,
             lambda p, r: pltpu.make_async_copy(ybuf.at[s, pl.ds(r, 1), :], o_hbm.at[pl.ds(src_ref[p], 1), :],
                                                sem.at[1, s]),
             lambda: pltpu.make_async_copy(ybuf.at[s], o_hbm.at[pl.ds(0, tile), :], sem.at[1, s]))

    @pl.when(i == 0)
    def _():
        xbuf[...] = jnp.zeros_like(xbuf)
        gather(0, 0, False)

    @pl.when(i + 1 < nt)
    def _():
        gather(i + 1, 1 - slot, False)

    @pl.when(i >= 2)
    def _():
        scatter(i - 2, slot, True)

    gather(i, slot, True)

    @pl.when(nv > 0)
    def _():
        x = xbuf[slot]
        hf, logits = _router_logits(x, ng_ref, wr_ref, br_ref)
        lane = lax.broadcasted_iota(jnp.int32, logits.shape, 1).astype(F32)
        neg = -jnp.inf
        big = float(V7X_LANES)
        grp = tg_ref[i].astype(F32)
        gl = jnp.where(lane < ng, logits, neg)
        gmax = jnp.max(gl, axis=1, keepdims=True)
        glog = jnp.sum(jnp.where(lane == grp, logits, 0.0), axis=1, keepdims=True)
        gprob = jnp.exp(glog - gmax) / jnp.sum(jnp.exp(gl - gmax), axis=1, keepdims=True)
        lo = ng + ne * grp
        sel = (lane >= lo) & (lane < lo + ne)
        m1 = jnp.max(jnp.where(sel, logits, neg), axis=1, keepdims=True)
        i1 = jnp.min(jnp.where(sel & (logits == m1), lane, big), axis=1, keepdims=True)
        sel2 = sel & (lane != i1)
        m2 = jnp.max(jnp.where(sel2, logits, neg), axis=1, keepdims=True)
        i2 = jnp.min(jnp.where(sel2 & (logits == m2), lane, big), axis=1, keepdims=True)
        e2 = jnp.exp(m2 - m1)
        w1 = gprob / (1.0 + e2)
        w2 = gprob * e2 / (1.0 + e2)
        gu = _dot(hf.astype(BF16), w13_ref[0])
        act = _silu(gu[:, 0:ne * ff]) * gu[:, ne * ff:2 * ne * ff]
        parts = []
        for e in range(ne):
            wcol = jnp.where(i1 == lo + e, w1, 0.0) + jnp.where(i2 == lo + e, w2, 0.0)
            parts.append(act[:, e * ff:(e + 1) * ff] * wcol)
        y = x + _dot(jnp.concatenate(parts, axis=1).astype(BF16), w2_ref[0])
        if final:
            y = _rms(y, fg_ref[...])
        ybuf[slot] = y
        scatter(i, slot, False)

    @pl.when(i == nt - 1)
    def _():
        @pl.when(i >= 1)
        def _():
            scatter(i - 1, 1 - slot, True)
        scatter(i, slot, True)


def _moe(xf, gidx, lw, final_g, final):
    t = xf.shape[0]
    tile = MOE_TILE
    src, tgroup, tvalid = _moe_plan(gidx, tile)
    ntile = tgroup.shape[0]
    const = lambda arr: pl.BlockSpec(arr.shape, lambda i, s, g, n: (0,) * arr.ndim)
    bygroup = lambda arr: pl.BlockSpec((1,) + arr.shape[1:], lambda i, s, g, n: (g[i],) + (0,) * (arr.ndim - 1))
    any_space = pl.BlockSpec(memory_space=pl.ANY)
    return pl.pallas_call(
        functools.partial(_moe_expert_kernel, final=final),
        grid_spec=pltpu.PrefetchScalarGridSpec(
            num_scalar_prefetch=3,
            grid=(ntile,),
            in_specs=[any_space, const(lw["norm_ffn_g"]), const(lw["w_router"]), const(lw["b_router"]),
                      bygroup(lw["w_e_13"]), bygroup(lw["w_e_2"]), const(final_g)],
            out_specs=any_space,
            scratch_shapes=[pltpu.VMEM((2, tile, D_MODEL), F32), pltpu.VMEM((2, tile, D_MODEL), F32),
                            pltpu.SemaphoreType.DMA((2, 2))],
        ),
        out_shape=jax.ShapeDtypeStruct((t, D_MODEL), F32),
        compiler_params=_cparams(("arbitrary",), 40),
        name="moe_experts",
    )(src, tgroup, tvalid, xf, lw["norm_ffn_g"], lw["w_router"], lw["b_router"], lw["w_e_13"], lw["w_e_2"], final_g)


def _block_diag(blocks):
    nb, bs, _ = blocks.shape
    eye = jnp.eye(nb, dtype=blocks.dtype)
    return jnp.einsum("nij,nm->nimj", blocks, eye).reshape(nb * bs, nb * bs)


def _pad_heads(w, axis=-1):
    shape = w.shape[:-1] + (GLA_HEADS, GLA_DK)
    w = w.reshape(shape)
    pad = [(0, 0)] * (w.ndim - 1) + [(0, V7X_LANES - GLA_DK)]
    return jnp.pad(w, pad).reshape(w.shape[:-2] + (GLA_HEADS * V7X_LANES,))


def _prep_layer(w, l):
    f = lambda name: w[name][l]
    w_in = f("w_in")
    cuts = np.cumsum([MIX_WIDTH, MIX_WIDTH, 3 * MIX_WIDTH, GLA_HEADS * GLA_DK, GLA_HEADS * GLA_DK, MIX_WIDTH,
                      MIX_WIDTH, 2 * GLA_RANK]).tolist()
    xa, ga, hy, q, k, v, g, lr, s5 = jnp.split(w_in, cuts, axis=-1)
    lr = jnp.pad(lr, ((0, 0), (0, V7X_LANES - 2 * GLA_RANK)))
    w_pack = jnp.concatenate([xa, ga, hy, _pad_heads(q), _pad_heads(k), v, g, lr, s5], axis=-1).astype(BF16)
    assert w_pack.shape[1] == N_PACK
    lw = {"w_pack": w_pack, "norm_mix_g": f("norm_mix_g")[None]}
    lw["lru_conv_w"] = f("lru_conv_w")
    lw["lru_conv_b"] = f("lru_conv_b")[None]
    wa, wx = f("lru_wa"), f("lru_wx")
    lw["lru_wg"] = jnp.stack([jnp.concatenate([_block_diag(wa[d]), _block_diag(wx[d])], axis=1)
                              for d in range(2)]).astype(BF16)
    lw["lru_bg"] = jnp.concatenate([f("lru_ba"), f("lru_bx")], axis=-1)[:, None, :]
    lw["lru_lam"] = f("lru_lambda")[:, None, :]
    lw["hy_conv_w"] = f("hy_conv_w")
    lw["hy_conv_b"] = f("hy_conv_b")[None]
    lw["hy_w1p"] = _split3(jnp.pad(f("hy_w1"), ((0, V7X_LANES - HY_EMB), (0, 0))))
    lw["hy_b1"] = f("hy_b1")[None]
    lw["hy_w2"] = _split3(f("hy_w2"))
    lw["hy_b2"] = f("hy_b2")[None]
    lw["hy_w3"] = _split3(f("hy_w3"))
    lw["hy_freq"] = f("hy_freq")[None]
    lw["hy_bias"] = f("hy_bias")
    wg2 = _pad_heads(f("gla_wg2"))
    wla = jnp.zeros((2, V7X_LANES, GLA_HEADS * V7X_LANES), F32)
    wla = wla.at[0, 0:GLA_RANK].set(wg2[0]).at[1, GLA_RANK:2 * GLA_RANK].set(wg2[1])
    lw["gla_wla"] = wla.astype(BF16)
    lw["gla_bla"] = _pad_heads(f("gla_bg"))[:, None, :]
    lw["gla_norm_g"] = jnp.tile(f("gla_norm_g"), GLA_HEADS)[None]
    lw["s5_tables"] = _s5_tables(f("s5_lam_re"), f("s5_lam_im"), f("s5_log_dt"), f("s5_b_re"), f("s5_b_im"),
                                 f("s5_c_re"), f("s5_c_im"))
    lw["s5_d"] = f("s5_d")[None]
    lw["s5_glu_w"] = f("s5_glu_w").astype(BF16)
    lw["s5_glu_b"] = f("s5_glu_b")[None]
    lw["w_gate"] = f("w_gate").astype(BF16)
    lw["b_gate"] = f("b_gate")[:, None, :]
    lw["w_branch"] = f("w_branch").astype(BF16)
    lw["w_out"] = f("w_out").astype(BF16)
    wr = jnp.concatenate([f("w_router_group"), jnp.transpose(f("w_router_expert"), (1, 0, 2)).reshape(D_MODEL, -1)],
                         axis=1)
    br = jnp.concatenate([f("b_router_group"), f("b_router_expert").reshape(-1)])
    nr = MOE_GROUPS + MOE_GROUPS * MOE_EXPERTS
    lw["w_router"] = _split3(jnp.pad(wr, ((0, 0), (0, V7X_LANES - nr))))
    lw["b_router"] = jnp.pad(br, (0, V7X_LANES - nr))[None]
    lw["norm_ffn_g"] = f("norm_ffn_g")[None]
    wide = lambda a: jnp.transpose(a, (0, 2, 1, 3)).reshape(MOE_GROUPS, D_MODEL, MOE_EXPERTS * MOE_FF)
    lw["w_e_13"] = jnp.concatenate([wide(f("w_e_gate")), wide(f("w_e_up"))], axis=-1).astype(BF16)
    lw["w_e_2"] = f("w_e_down").reshape(MOE_GROUPS, MOE_EXPERTS * MOE_FF, D_MODEL).astype(BF16)
    return lw


def _encoder(x, layers, final_g):
    bsz, seq, _ = x.shape
    plan = _FftPlan(seq)
    xf = x.reshape(bsz * seq, D_MODEL)
    for l, lw in enumerate(layers):
        lru_in, hy_in, gla_in, s5_in = _inproj(xf, lw["norm_mix_g"], lw["w_pack"], lw["hy_conv_w"],
                                               lw["hy_conv_b"], seq)
        shp = lambda a: a.reshape(bsz, seq, a.shape[-1])
        yaf, yab = _lru(shp(lru_in), lw["lru_conv_w"], lw["lru_conv_b"], lw["lru_wg"], lw["lru_bg"], lw["lru_lam"])
        yb = _hyena(shp(hy_in), lw, plan)
        yc = _gla(shp(gla_in), lw["gla_wla"], lw["gla_bla"], lw["gla_norm_g"])
        y5 = _s5(shp(s5_in), *lw["s5_tables"])
        flat = lambda a: a.reshape(bsz * seq, MIX_WIDTH)
        xf, gidx = _merge(xf, lw, flat(yaf), flat(yab), flat(yb), flat(yc), flat(y5), s5_in)
        xf = _moe(xf, gidx, lw, final_g, final=(l == len(layers) - 1))
    return xf.reshape(bsz, seq, D_MODEL)


def kernel(x_prompt, x_sample, norm_mix_g, w_in, lru_conv_w, lru_conv_b, lru_wa, lru_ba, lru_wx, lru_bx,
           lru_lambda, hy_conv_w, hy_conv_b, hy_w1, hy_b1, hy_w2, hy_b2, hy_w3, hy_freq, hy_bias,
           gla_wg2, gla_bg, gla_norm_g, s5_lam_re, s5_lam_im, s5_log_dt, s5_b_re, s5_b_im, s5_c_re, s5_c_im,
           s5_d, s5_glu_w, s5_glu_b, w_branch, w_gate, b_gate, w_out, norm_ffn_g, w_router_group,
           b_router_group, w_router_expert, b_router_expert, w_e_gate, w_e_up, w_e_down, final_norm_g):
    w = dict(norm_mix_g=norm_mix_g, w_in=w_in, lru_conv_w=lru_conv_w, lru_conv_b=lru_conv_b, lru_wa=lru_wa,
             lru_ba=lru_ba, lru_wx=lru_wx, lru_bx=lru_bx, lru_lambda=lru_lambda, hy_conv_w=hy_conv_w,
             hy_conv_b=hy_conv_b, hy_w1=hy_w1, hy_b1=hy_b1, hy_w2=hy_w2, hy_b2=hy_b2, hy_w3=hy_w3,
             hy_freq=hy_freq, hy_bias=hy_bias, gla_wg2=gla_wg2, gla_bg=gla_bg, gla_norm_g=gla_norm_g,
             s5_lam_re=s5_lam_re, s5_lam_im=s5_lam_im, s5_log_dt=s5_log_dt, s5_b_re=s5_b_re, s5_b_im=s5_b_im,
             s5_c_re=s5_c_re, s5_c_im=s5_c_im, s5_d=s5_d, s5_glu_w=s5_glu_w, s5_glu_b=s5_glu_b,
             w_branch=w_branch, w_gate=w_gate, b_gate=b_gate, w_out=w_out, norm_ffn_g=norm_ffn_g,
             w_router_group=w_router_group, b_router_group=b_router_group, w_router_expert=w_router_expert,
             b_router_expert=b_router_expert, w_e_gate=w_e_gate, w_e_up=w_e_up, w_e_down=w_e_down)
    layers = [_prep_layer(w, l) for l in range(norm_mix_g.shape[0])]
    fg = final_norm_g[None]
    return (_encoder(x_prompt, layers, fg), _encoder(x_sample, layers, fg))
```

```python
import functools
import math

import numpy as np
import jax
import jax.numpy as jnp
from jax import lax
from jax.experimental import pallas as pl
from jax.experimental.pallas import tpu as pltpu

F32 = jnp.float32
BF16 = jnp.bfloat16

D_MODEL = 1024
DEPTH = 2
EPS = 1e-6
MIX_WIDTH = D_MODEL // 2
LRU_BLOCKS = 8
LRU_BLOCK = MIX_WIDTH // LRU_BLOCKS
LRU_CONV = 4
LRU_C = 8.0
HY_ORDER = 2
HY_CONV = 3
HY_EMB = 33
HY_BANDS = (HY_EMB - 1) // 2
HY_HIDDEN = 64
HY_DECAY_TARGET = 1e-2
HY_FAST_PCT = 0.3
HY_SLOW_PCT = 1.5
GLA_HEADS = 4
GLA_DK = MIX_WIDTH // 8
GLA_DV = MIX_WIDTH // GLA_HEADS
GLA_RANK = 16
GLA_TAU = 16.0
GLA_CHUNK = 64
S5_GROUP = 16
S5_GROUPS = MIX_WIDTH // S5_GROUP
S5_STATE = 64
MOE_GROUPS = 4
MOE_EXPERTS = 4
MOE_FF = D_MODEL // 4

V7X_LANES = 128
V7X_SUBLANES = 8
V7X_VMEM_BYTES = 64 * 2**20
MIB = 2**20

GLA_PACK = 4 * V7X_LANES * 2 + 512 + 512 + V7X_LANES
PK_LRU = (0, 1024)
PK_HY = (1024, 2560)
PK_GLA = (2560, 2560 + GLA_PACK)
PK_S5 = (PK_GLA[1], PK_GLA[1] + 512)
N_PACK = PK_S5[1]

TOK_TILE = 512
MOE_TILE = 256
MERGE_TILE = 256
LRU_TILE = 256
GLA_TILE = 512
GLA_BATCH = 2
S5_CHUNK = 16
S5_LANE_GROUPS = V7X_LANES // S5_GROUP
S5_SCAN_STEPS = 3
FFT_N1 = 64
FFT_ROWS = 128
FFT_COLS = 256
FFT_KB = 11


def _cparams(sem, vmem_mib):
    return pltpu.CompilerParams(dimension_semantics=sem, vmem_limit_bytes=int(vmem_mib * MIB))


def _rms(x, g):
    return x * lax.rsqrt(jnp.mean(x * x, axis=-1, keepdims=True) + EPS) * g


def _sigmoid(x):
    return 1.0 / (1.0 + jnp.exp(-x))


def _softplus(x):
    return jnp.maximum(x, 0.0) + jnp.log(1.0 + jnp.exp(-jnp.abs(x)))


def _gelu_tanh(x):
    return 0.5 * x * (1.0 + jnp.tanh(math.sqrt(2.0 / math.pi) * (x + 0.044715 * (x * x * x))))


def _silu(x):
    return x * _sigmoid(x)


def _dot(a, b):
    return jnp.dot(a, b, preferred_element_type=F32)


def _split3(w):
    hi = w.astype(BF16)
    lo = (w - hi.astype(F32)).astype(BF16)
    return jnp.concatenate([hi, lo, hi], axis=0)


def _dot3(a, w3):
    hi = a.astype(BF16)
    lo = (a - hi.astype(F32)).astype(BF16)
    return _dot(jnp.concatenate([hi, hi, lo], axis=1), w3)


def _inproj_kernel(x_ref, xp_ref, xn_ref, g_ref, w_ref, cw_ref, cb_ref, lru_ref, hy_ref, gla_ref, s5_ref, ext_ref, *,
                   tiles_per_seq):
    i = pl.program_id(0)
    tile = TOK_TILE
    hf = _rms(x_ref[...], g_ref[...])
    h = hf.astype(BF16)
    lru_ref[...] = _dot(h, w_ref[:, PK_LRU[0]:PK_LRU[1]])
    gla_ref[...] = _dot(h, w_ref[:, PK_GLA[0]:PK_GLA[1]])
    s5_ref[...] = _dot(h, w_ref[:, PK_S5[0]:PK_S5[1]])
    stacked = jnp.concatenate([_rms(xp_ref[...], g_ref[...]), hf, _rms(xn_ref[...], g_ref[...])], axis=0)
    rows = _dot(stacked.astype(BF16), w_ref[:, PK_HY[0]:PK_HY[1]])
    _fill_ext(ext_ref, rows[8:8 + tile, :], rows[0:8, :], rows[8 + tile:16 + tile, :], i % tiles_per_seq == 0,
              i % tiles_per_seq == tiles_per_seq - 1, tile)
    y = cb_ref[...] + ext_ref[7:7 + tile, :] * cw_ref[0:1, :]
    y = y + ext_ref[8:8 + tile, :] * cw_ref[1:2, :]
    y = y + ext_ref[9:9 + tile, :] * cw_ref[2:3, :]
    hy_ref[...] = y


def _inproj(xf, g, w_pack, hy_cw, hy_cb, seq):
    t = xf.shape[0]
    tile = TOK_TILE
    r8 = tile // 8
    last8 = t // 8 - 1
    widths = [PK_LRU[1] - PK_LRU[0], PK_HY[1] - PK_HY[0], PK_GLA[1] - PK_GLA[0], PK_S5[1] - PK_S5[0]]
    const = lambda arr: pl.BlockSpec(arr.shape, lambda i: (0,) * arr.ndim, pipeline_mode=pl.Buffered(1))
    return pl.pallas_call(
        functools.partial(_inproj_kernel, tiles_per_seq=seq // tile),
        grid=(t // tile,),
        in_specs=[pl.BlockSpec((tile, D_MODEL), lambda i: (i, 0)),
                  pl.BlockSpec((8, D_MODEL), lambda i: (jnp.maximum(i * r8 - 1, 0), 0)),
                  pl.BlockSpec((8, D_MODEL), lambda i: (jnp.minimum((i + 1) * r8, last8), 0)),
                  const(g), const(w_pack), const(hy_cw), const(hy_cb)],
        out_specs=[pl.BlockSpec((tile, w), lambda i: (i, 0)) for w in widths],
        out_shape=[jax.ShapeDtypeStruct((t, w), F32) for w in widths],
        scratch_shapes=[pltpu.VMEM((tile + 16, widths[1]), F32)],
        compiler_params=_cparams(("parallel",), 56),
        name="inproj",
    )(xf, xf, xf, g, w_pack, hy_cw, hy_cb)


def _fill_ext(ext_ref, main, prev8, next8, first, last, tile):
    ext_ref[0:8, :] = jnp.where(first, 0.0, prev8)
    ext_ref[8:8 + tile, :] = main
    ext_ref[8 + tile:16 + tile, :] = jnp.where(last, 0.0, next8)


def _linear_scan_tile(a, b, carry, reverse):
    n = a.shape[0]
    sub = V7X_SUBLANES
    row = lax.broadcasted_iota(jnp.int32, a.shape, 0) % sub
    d = 1
    while d < sub:
        if reverse:
            a_s = pltpu.roll(a, n - d, 0)
            b_s = pltpu.roll(b, n - d, 0)
            valid = row < sub - d
        else:
            a_s = pltpu.roll(a, d, 0)
            b_s = pltpu.roll(b, d, 0)
            valid = row >= d
        b = jnp.where(valid, a * b_s + b, b)
        a = jnp.where(valid, a * a_s, a)
        d *= 2
    ngroup = n // sub
    out = [None] * ngroup
    for g in (range(ngroup - 1, -1, -1) if reverse else range(ngroup)):
        h = b[g * sub:(g + 1) * sub, :] + a[g * sub:(g + 1) * sub, :] * carry
        carry = h[0:1, :] if reverse else h[sub - 1:sub, :]
        out[g] = h
    return jnp.concatenate(out, axis=0), carry


def _lru_kernel(mf_ref, pf_ref, nf_ref, mb_ref, pb_ref, nb_ref, cw_ref, cb_ref, wg_ref, bg_ref, lam_ref,
                of_ref, ob_ref, extf_ref, extb_ref, carry_ref):
    c = pl.program_id(1)
    nc = pl.num_programs(1)
    tile = LRU_TILE

    @pl.when(c == 0)
    def _():
        carry_ref[...] = jnp.zeros_like(carry_ref)

    def one(m_ref, p_ref, n_ref, ext_ref, d, first, last, o_ref):
        x = m_ref[0, :, 0:MIX_WIDTH]
        ga = m_ref[0, :, MIX_WIDTH:2 * MIX_WIDTH]
        _fill_ext(ext_ref, x, p_ref[0], n_ref[0], first, last, tile)
        xc = cb_ref[...] + ext_ref[6:6 + tile, :] * cw_ref[0:1, :]
        xc = xc + ext_ref[7:7 + tile, :] * cw_ref[1:2, :]
        xc = xc + ext_ref[8:8 + tile, :] * cw_ref[2:3, :]
        xc = xc + ext_ref[9:9 + tile, :] * cw_ref[3:4, :]
        z = _dot(xc.astype(BF16), wg_ref[d]) + bg_ref[d]
        gate_r = _sigmoid(z[:, 0:MIX_WIDTH])
        gate_i = _sigmoid(z[:, MIX_WIDTH:2 * MIX_WIDTH])
        log_a = -LRU_C * gate_r * _softplus(-lam_ref[d])
        a = jnp.exp(log_a)
        t = 1.0 - a * a
        b = jnp.where(t > 0.0, t * lax.rsqrt(t), 0.0) * gate_i * xc
        h, last = _linear_scan_tile(a, b, carry_ref[d:d + 1, :], reverse=(d == 1))
        carry_ref[d:d + 1, :] = last
        o_ref[0] = h * _gelu_tanh(ga)

    one(mf_ref, pf_ref, nf_ref, extf_ref, 0, c == 0, c == nc - 1, of_ref)
    one(mb_ref, pb_ref, nb_ref, extb_ref, 1, c == nc - 1, c == 0, ob_ref)


def _lru(lru_in, cw, cb, wg, bg, lam):
    bsz, seq, _ = lru_in.shape
    tile = LRU_TILE
    nc = seq // tile
    r8 = tile // 8
    last8 = seq // 8 - 1

    def fwd(c):
        return c

    def bwd(c):
        return nc - 1 - c

    def specs(ch):
        return [pl.BlockSpec((1, tile, 2 * MIX_WIDTH), lambda b, c: (b, ch(c), 0)),
                pl.BlockSpec((1, 8, MIX_WIDTH), lambda b, c: (b, jnp.maximum(ch(c) * r8 - 1, 0), 0)),
                pl.BlockSpec((1, 8, MIX_WIDTH), lambda b, c: (b, jnp.minimum((ch(c) + 1) * r8, last8), 0))]

    const = lambda shape: pl.BlockSpec(shape, lambda b, c: (0,) * len(shape))
    return pl.pallas_call(
        _lru_kernel,
        grid=(bsz, nc),
        in_specs=specs(fwd) + specs(bwd) + [const(cw.shape), const(cb.shape), const(wg.shape), const(bg.shape),
                                            const(lam.shape)],
        out_specs=[pl.BlockSpec((1, tile, MIX_WIDTH), lambda b, c: (b, c, 0)),
                   pl.BlockSpec((1, tile, MIX_WIDTH), lambda b, c: (b, nc - 1 - c, 0))],
        out_shape=[jax.ShapeDtypeStruct((bsz, seq, MIX_WIDTH), F32)] * 2,
        scratch_shapes=[pltpu.VMEM((tile + 16, MIX_WIDTH), F32), pltpu.VMEM((tile + 16, MIX_WIDTH), F32),
                        pltpu.VMEM((8, MIX_WIDTH), F32)],
        compiler_params=_cparams(("parallel", "arbitrary"), 40),
        name="lru",
    )(lru_in, lru_in, lru_in, lru_in, lru_in, lru_in, cw, cb, wg, bg, lam)


class _FftPlan:
    def __init__(self, seq):
        n = 2 * seq
        n1 = FFT_N1
        n2 = n // n1
        assert n1 * n2 == n and n2 % 16 == 0
        h1 = n1 // 2 + 1
        nh = n1 // 2
        self.n, self.n1, self.n2, self.h1, self.nh = n, n1, n2, h1, nh
        self.rows = min(FFT_ROWS, n2)
        assert n2 % self.rows == 0 and h1 % FFT_KB == 0
        k1 = np.arange(h1, dtype=np.float64)
        m1 = np.arange(nh, dtype=np.float64)
        r = np.arange(8, dtype=np.float64)
        ang = -2.0 * np.pi * (k1[:, None, None] * m1[None, None, :] / n1 + r[None, :, None] * k1[:, None, None] / n)
        e = np.exp(1j * ang)
        fa = np.zeros((h1, 8, nh, 8), np.complex128)
        for rr in range(8):
            fa[:, rr, :, rr] = e[:, rr, :]
        fa = fa.reshape(h1 * 8, nh * 8)
        self.fa = jnp.asarray(np.concatenate([fa.real, fa.imag], axis=0), BF16)
        ck = np.where((k1 == 0) | (k1 == n1 // 2), 1.0, 2.0)
        ec = np.conj(e) * ck[:, None, None] / n
        fc = np.zeros((nh, 8, 2, h1, 8), np.float64)
        for rr in range(8):
            fc[:, rr, 0, :, rr] = ec[:, rr, :].real.T
            fc[:, rr, 1, :, rr] = -ec[:, rr, :].imag.T
        self.fc = jnp.asarray(fc.reshape(nh * 8, 2 * h1 * 8), BF16)
        rg = np.arange(n2 // 8, dtype=np.float64)
        tw = np.exp(-2j * np.pi * 8.0 * rg[None, :] * k1[:, None] / n)
        self.tw = jnp.asarray(np.concatenate([tw.real, tw.imag], axis=0), F32)
        tw1 = np.exp(-2j * np.pi * k1 / n)
        self.tw1 = jnp.asarray(np.stack([tw1.real, tw1.imag]), F32)
        q = np.arange(n2, dtype=np.float64)
        f2 = np.exp(-2j * np.pi * np.outer(q, q) / n2)
        fr, fi = f2.real, f2.imag
        self.gb = jnp.asarray(np.block([[fr, -fi], [fi, fr]]), BF16)
        self.gbi = jnp.asarray(np.block([[fr, fi], [-fi, fr]]), BF16)


def _fft_a_kernel(tw_ref, x_ref, fa_ref, a_ref, *, h1, nh, rows):
    rb = pl.program_id(2)
    cols = x_ref.shape[-1]

    def stage(rg):
        xg = x_ref[0, :, pl.ds(pl.multiple_of(rg * 8, 8), 8), :].reshape(nh * 8, cols).astype(BF16)
        return _dot(fa_ref[...], xg)

    def body(i, carry):
        p0 = stage(2 * i)
        p1 = stage(2 * i + 1)
        g0 = rb * (rows // 8) + 2 * i
        for k in range(h1):
            outs = []
            for p, g in ((p0, g0), (p1, g0 + 1)):
                pr = p[k * 8:(k + 1) * 8, :]
                pi = p[(h1 + k) * 8:(h1 + k + 1) * 8, :]
                tr = tw_ref[k, g]
                ti = tw_ref[h1 + k, g]
                outs.append((pr * tr - pi * ti, pr * ti + pi * tr))
            dst = pl.ds(pl.multiple_of(i * 16, 16), 16)
            a_ref[0, k, 0, dst, :] = jnp.concatenate([outs[0][0], outs[1][0]], axis=0).astype(BF16)
            a_ref[0, k, 1, dst, :] = jnp.concatenate([outs[0][1], outs[1][1]], axis=0).astype(BF16)
        return carry

    lax.fori_loop(0, rows // 16, body, 0)


def _fft_a(x4, plan, col_off=0, ncols=None):
    bq, nh, n2, width = x4.shape
    ncols = width if ncols is None else ncols
    cb0 = col_off // FFT_COLS
    rows = plan.rows
    kern = functools.partial(_fft_a_kernel, h1=plan.h1, nh=nh, rows=rows)
    return pl.pallas_call(
        kern,
        grid=(bq, ncols // FFT_COLS, n2 // rows),
        in_specs=[pl.BlockSpec(memory_space=pltpu.SMEM),
                  pl.BlockSpec((1, nh, rows, FFT_COLS), lambda b, c, r: (b, 0, r, cb0 + c)),
                  pl.BlockSpec(plan.fa.shape, lambda b, c, r: (0, 0))],
        out_specs=pl.BlockSpec((1, plan.h1, 2, rows, FFT_COLS), lambda b, c, r: (b, 0, 0, r, c)),
        out_shape=jax.ShapeDtypeStruct((bq, plan.h1, 2, n2, ncols), BF16),
        compiler_params=_cparams(("parallel", "parallel", "parallel"), 40),
        name="fft_outer_fwd",
    )(plan.tw, x4, plan.fa)


def _fft_mid_kernel(a_ref, kf_ref, gb_ref, gbi_ref, o_ref, *, n2):
    kb = a_ref.shape[1]
    cols = a_ref.shape[-1]
    for k in range(kb):
        y = _dot(gb_ref[...], a_ref[0, k].reshape(2 * n2, cols))
        yr, yi = y[0:n2, :], y[n2:2 * n2, :]
        kr, ki = kf_ref[0, k, 0], kf_ref[0, k, 1]
        z = jnp.concatenate([yr * kr - yi * ki, yr * ki + yi * kr], axis=0).astype(BF16)
        o_ref[0, k] = _dot(gbi_ref[...], z).reshape(2, n2, cols).astype(BF16)


def _fft_mid(a, kf, order, plan):
    bq, h1, _, n2, width = a.shape
    kern = functools.partial(_fft_mid_kernel, n2=n2)
    blk = (1, FFT_KB, 2, n2, FFT_COLS)
    return pl.pallas_call(
        kern,
        grid=(bq, width // FFT_COLS, h1 // FFT_KB),
        in_specs=[pl.BlockSpec(blk, lambda b, c, k: (b, k, 0, 0, c)),
                  pl.BlockSpec(blk, lambda b, c, k: (order, k, 0, 0, c)),
                  pl.BlockSpec(plan.gb.shape, lambda b, c, k: (0, 0)),
                  pl.BlockSpec(plan.gbi.shape, lambda b, c, k: (0, 0))],
        out_specs=pl.BlockSpec(blk, lambda b, c, k: (b, k, 0, 0, c)),
        out_shape=jax.ShapeDtypeStruct(a.shape, BF16),
        compiler_params=_cparams(("parallel", "parallel", "parallel"), 40),
        name="fft_inner_mul",
    )(a, kf, plan.gb, plan.gbi)


def _fft_c_kernel(tw_ref, b_ref, u_ref, g_ref, bias_ref, fc_ref, o_ref, *, h1, nh, rows):
    rb = pl.program_id(2)
    cols = o_ref.shape[-1]

    def body(i, carry):
        src = pl.ds(pl.multiple_of(i * 16, 16), 16)
        tiles = [[b_ref[0, k, p, src, :].astype(F32) for p in range(2)] for k in range(h1)]
        for half in range(2):
            g = rb * (rows // 8) + 2 * i + half
            re_rows, im_rows = [], []
            for k in range(h1):
                br = tiles[k][0][half * 8:(half + 1) * 8, :]
                bi = tiles[k][1][half * 8:(half + 1) * 8, :]
                tr = tw_ref[k, g]
                ti = tw_ref[h1 + k, g]
                re_rows.append(br * tr + bi * ti)
                im_rows.append(bi * tr - br * ti)
            s = jnp.concatenate(re_rows + im_rows, axis=0).astype(BF16)
            y = _dot(fc_ref[...], s).reshape(nh, 8, cols)
            dst = pl.ds(pl.multiple_of((2 * i + half) * 8, 8), 8)
            u = u_ref[0, :, dst, :]
            o_ref[0, :, dst, :] = (y + u * bias_ref[...]) * g_ref[0, :, dst, :]
        return carry

    lax.fori_loop(0, rows // 16, body, 0)


def _fft_c(bm, u4, u_off, g4, g_off, bias, plan):
    bq, h1, _, n2, width = bm.shape
    nh = plan.nh
    rows = plan.rows
    ub, gbk = u_off // FFT_COLS, g_off // FFT_COLS
    kern = functools.partial(_fft_c_kernel, h1=h1, nh=nh, rows=rows)
    xblk = (1, nh, rows, FFT_COLS)
    return pl.pallas_call(
        kern,
        grid=(bq, width // FFT_COLS, n2 // rows),
        in_specs=[pl.BlockSpec(memory_space=pltpu.SMEM),
                  pl.BlockSpec((1, h1, 2, rows, FFT_COLS), lambda b, c, r: (b, 0, 0, r, c)),
                  pl.BlockSpec(xblk, lambda b, c, r: (b, 0, r, ub + c)),
                  pl.BlockSpec(xblk, lambda b, c, r: (b, 0, r, gbk + c)),
                  pl.BlockSpec((1, FFT_COLS), lambda b, c, r: (0, c)),
                  pl.BlockSpec(plan.fc.shape, lambda b, c, r: (0, 0))],
        out_specs=pl.BlockSpec(xblk, lambda b, c, r: (b, 0, r, c)),
        out_shape=jax.ShapeDtypeStruct((bq, nh, n2, width), F32),
        compiler_params=_cparams(("parallel", "parallel", "parallel"), 48),
        name="fft_outer_inv",
    )(plan.tw, bm, u4, g4, bias, plan.fc)


def _hyfilt_gen_kernel(w1_ref, b1_ref, w2_ref, b2_ref, w3_ref, fr_ref, o_ref, ss_ref, *, seq):
    rblk = pl.program_id(0)
    tile, cols = o_ref.shape[1], o_ref.shape[2]
    irow = lax.broadcasted_iota(jnp.int32, (tile, V7X_LANES), 0) + rblk * tile
    row = irow.astype(F32)
    lane = lax.broadcasted_iota(jnp.int32, (tile, V7X_LANES), 1)
    t = row / (seq - 1.0)
    omega = (2.0 * math.pi / seq) * row
    band_step = (HY_BANDS - 1 - 1e-4) / (HY_BANDS - 1)
    is_cos = (lane >= 1) & (lane <= HY_BANDS)
    is_sin = (lane > HY_BANDS) & (lane <= 2 * HY_BANDS)
    bidx = jnp.where(is_cos, lane - 1, lane - 1 - HY_BANDS).astype(F32)
    ang = omega * (1e-4 + band_step * bidx)
    trig = jnp.cos(ang + jnp.where(is_sin, 0.5 * math.pi, 0.0))
    z = jnp.where(lane == 0, t, jnp.where(is_cos | is_sin, trig, 0.0))
    fr = fr_ref[...]
    hid = jnp.sin(fr * (_dot3(z, w1_ref[...]) + b1_ref[...]))
    hid = jnp.sin(fr * (_dot3(hid, w2_ref[...]) + b2_ref[...]))
    filt = _dot3(hid, w3_ref[...])
    col = lax.broadcasted_iota(jnp.int32, (1, cols), 1)
    chan = (col % MIX_WIDTH).astype(F32)
    max_decay = math.log(HY_DECAY_TARGET) / HY_FAST_PCT
    min_decay = math.log(HY_DECAY_TARGET) / HY_SLOW_PCT
    delta = jnp.abs(min_decay + (max_decay - min_decay) / (MIX_WIDTH - 1) * chan)
    filt = filt * jnp.exp(-t[:, 0:1] * delta)
    is_bwd = (col // MIX_WIDTH) % 2 == 1
    rows_c = lax.broadcasted_iota(jnp.int32, (tile, cols), 0) + rblk * tile
    filt = jnp.where(is_bwd & (rows_c == seq - 1), 0.0, filt)
    o_ref[0] = filt

    @pl.when(rblk == 0)
    def _():
        ss_ref[...] = jnp.zeros_like(ss_ref)

    ss_ref[...] += jnp.sum(filt * filt, axis=0, keepdims=True)


def _hyfilt_gen(seq, w1p, b1, w2, b2, w3, freq):
    ncol = w3.shape[1]
    tile = min(seq, 512)
    const = lambda shape: pl.BlockSpec(shape, lambda r: (0,) * len(shape))
    kern = functools.partial(_hyfilt_gen_kernel, seq=seq)
    return pl.pallas_call(
        kern,
        grid=(seq // tile,),
        in_specs=[const(w1p.shape), const(b1.shape), const(w2.shape), const(b2.shape), const(w3.shape),
                  const(freq.shape)],
        out_specs=[pl.BlockSpec((1, tile, ncol), lambda r: (0, r, 0)),
                   pl.BlockSpec((1, ncol), lambda r: (0, 0))],
        out_shape=[jax.ShapeDtypeStruct((1, seq, ncol), F32), jax.ShapeDtypeStruct((1, ncol), F32)],
        compiler_params=_cparams(("arbitrary",), 40),
        name="hyena_filter_gen",
    )(w1p, b1, w2, b2, w3, freq)


def _hyfilt_spec_kernel(tw1_ref, a0_ref, a1_ref, ss0_ref, ss1_ref, gb_ref, o_ref, *, n2):
    kblk = pl.program_id(2)
    kb = a0_ref.shape[1]
    cols = a0_ref.shape[-1]
    scale = lax.rsqrt(ss0_ref[...] + ss1_ref[...] + EPS)
    ang2 = (-2.0 * math.pi / n2) * lax.broadcasted_iota(jnp.int32, (n2, cols), 0).astype(F32)
    cr, ci = jnp.cos(ang2), jnp.sin(ang2)
    for k in range(kb):
        y0 = _dot(gb_ref[...], a0_ref[0, k].reshape(2 * n2, cols))
        y1 = _dot(gb_ref[...], a1_ref[0, k].reshape(2 * n2, cols))
        sr = tw1_ref[0, kblk * kb + k]
        si = tw1_ref[1, kblk * kb + k]
        wr, wi = cr * sr - ci * si, cr * si + ci * sr
        y1r, y1i = y1[0:n2, :], y1[n2:2 * n2, :]
        o_ref[0, k, 0] = (y0[0:n2, :] + (wr * y1r - wi * y1i)) * scale
        o_ref[0, k, 1] = (y0[n2:2 * n2, :] - (wr * y1i + wi * y1r)) * scale


def _hyfilt_spec(af, ss, plan):
    _, h1, _, n2, _ = af.shape
    ncb = MIX_WIDTH // FFT_COLS
    blk = (1, FFT_KB, 2, n2, FFT_COLS)
    kern = functools.partial(_hyfilt_spec_kernel, n2=n2)
    return pl.pallas_call(
        kern,
        grid=(HY_ORDER, ncb, h1 // FFT_KB),
        in_specs=[pl.BlockSpec(memory_space=pltpu.SMEM),
                  pl.BlockSpec(blk, lambda o, c, k: (0, k, 0, 0, o * 2 * ncb + c)),
                  pl.BlockSpec(blk, lambda o, c, k: (0, k, 0, 0, o * 2 * ncb + ncb + c)),
                  pl.BlockSpec((1, FFT_COLS), lambda o, c, k: (0, o * 2 * ncb + c)),
                  pl.BlockSpec((1, FFT_COLS), lambda o, c, k: (0, o * 2 * ncb + ncb + c)),
                  pl.BlockSpec(plan.gb.shape, lambda o, c, k: (0, 0))],
        out_specs=pl.BlockSpec(blk, lambda o, c, k: (o, k, 0, 0, c)),
        out_shape=jax.ShapeDtypeStruct((HY_ORDER, h1, 2, n2, MIX_WIDTH), F32),
        compiler_params=_cparams(("parallel", "parallel", "parallel"), 40),
        name="hyena_filter_spectrum",
    )(plan.tw1, af, af, ss, ss, plan.gb)


def _hyena(zc, lw, plan):
    bsz, seq, _ = zc.shape
    filt, ss = _hyfilt_gen(seq, lw["hy_w1p"], lw["hy_b1"], lw["hy_w2"], lw["hy_b2"], lw["hy_w3"], lw["hy_freq"])
    af = _fft_a(filt.reshape(1, plan.nh, plan.n2, filt.shape[-1]), plan)
    kf = _hyfilt_spec(af, ss, plan)
    zc4 = zc.reshape(bsz, plan.nh, plan.n2, 3 * MIX_WIDTH)
    a = _fft_a(zc4, plan, col_off=0, ncols=MIX_WIDTH)
    bm = _fft_mid(a, kf, 0, plan)
    z1 = _fft_c(bm, zc4, 0, zc4, MIX_WIDTH, lw["hy_bias"][0:1], plan)
    a = _fft_a(z1, plan)
    bm = _fft_mid(a, kf, 1, plan)
    z2 = _fft_c(bm, z1, 0, zc4, 2 * MIX_WIDTH, lw["hy_bias"][1:2], plan)
    return z2.reshape(bsz, seq, MIX_WIDTH)


def _gla_kernel(*refs, reverse):
    if reverse:
        x_ref, of_ref, wla_ref, bla_ref, ng_ref, o_ref, st_ref = refs
    else:
        x_ref, wla_ref, bla_ref, o_ref, st_ref = refs
    c = pl.program_id(1)

    @pl.when(c == 0)
    def _():
        st_ref[...] = jnp.zeros_like(st_ref)

    for bb in range(x_ref.shape[0]):
        _gla_sequence(bb, x_ref, of_ref if reverse else None, wla_ref, bla_ref, ng_ref if reverse else None, o_ref,
                      st_ref, reverse)


def _gla_sequence(bb, x_ref, of_ref, wla_ref, bla_ref, ng_ref, o_ref, st_ref, reverse):
    tile = GLA_TILE
    ck = GLA_CHUNK
    nck = tile // ck
    hw = V7X_LANES
    nh = GLA_HEADS
    q = x_ref[bb, :, 0:nh * hw] * (GLA_DK ** -0.5)
    k = x_ref[bb, :, nh * hw:2 * nh * hw]
    v = x_ref[bb, :, 2 * nh * hw:3 * nh * hw]
    lr = x_ref[bb, :, 4 * nh * hw:4 * nh * hw + hw]
    zl = _dot(lr.astype(BF16), wla_ref[...]) + bla_ref[...]
    la = (jnp.minimum(zl, 0.0) - jnp.log(1.0 + jnp.exp(-jnp.abs(zl)))) / GLA_TAU

    row = lax.broadcasted_iota(jnp.int32, la.shape, 0) % ck
    bcum = la
    d = 1
    while d < ck:
        if reverse:
            bcum = bcum + jnp.where(row < ck - d, pltpu.roll(bcum, tile - d, 0), 0.0)
        else:
            bcum = bcum + jnp.where(row >= d, pltpu.roll(bcum, d, 0), 0.0)
        d *= 2
    b3 = bcum.reshape(nck, ck, nh * hw)
    blast = b3[:, 0:1, :] if reverse else b3[:, ck - 1:ck, :]
    q_e = (q * jnp.exp(bcum)).astype(BF16)
    k_e = (k * jnp.exp(-bcum)).astype(BF16)
    k_d = (k.reshape(nck, ck, nh * hw) * jnp.exp(blast - b3)).reshape(tile, nh * hw).astype(BF16)
    gch = jnp.exp(blast)
    vb = v.astype(BF16)

    ri = lax.broadcasted_iota(jnp.int32, (ck, ck), 0)
    ci = lax.broadcasted_iota(jnp.int32, (ck, ck), 1)
    mask = (ri <= ci) if reverse else (ri >= ci)
    order = range(nck - 1, -1, -1) if reverse else range(nck)
    outs = [None] * nck
    for n in order:
        rs = slice(n * ck, (n + 1) * ck)
        heads = []
        for h in range(nh):
            ls = slice(h * hw, (h + 1) * hw)
            qe, ke, kd, vh = q_e[rs, ls], k_e[rs, ls], k_d[rs, ls], vb[rs, ls]
            st = st_ref[bb, h]
            sc = lax.dot_general(qe, ke, (((1,), (1,)), ((), ())), preferred_element_type=F32)
            sc = jnp.where(mask, sc, 0.0).astype(BF16)
            o = _dot(sc, vh) + lax.dot_general(qe, st.astype(BF16), (((1,), (1,)), ((), ())),
                                               preferred_element_type=F32)
            upd = lax.dot_general(vh, kd, (((0,), (0,)), ((), ())), preferred_element_type=F32)
            st_ref[bb, h] = st * gch[n, :, ls] + upd
            heads.append(o)
        outs[n] = jnp.concatenate(heads, axis=1)
    o_dir = jnp.concatenate(outs, axis=0)
    if not reverse:
        o_ref[bb] = o_dir
        return
    o = of_ref[bb] + o_dir
    g = x_ref[bb, :, 3 * nh * hw:4 * nh * hw]
    normed = []
    for h in range(nh):
        oh = o[:, h * hw:(h + 1) * hw]
        normed.append(oh * lax.rsqrt(jnp.mean(oh * oh, axis=-1, keepdims=True) + EPS))
    o = jnp.concatenate(normed, axis=1) * ng_ref[...]
    o_ref[bb] = o * _silu(g)


def _gla(gla_in, wla, bla, norm_g):
    bsz, seq, width = gla_in.shape
    tile = GLA_TILE
    nc = seq // tile
    nb = min(GLA_BATCH, bsz)
    const = lambda shape: pl.BlockSpec(shape, lambda b, c: (0,) * len(shape))
    out_shape = jax.ShapeDtypeStruct((bsz, seq, MIX_WIDTH), F32)
    scratch = [pltpu.VMEM((nb, GLA_HEADS, GLA_DV, V7X_LANES), F32)]
    o_f = pl.pallas_call(
        functools.partial(_gla_kernel, reverse=False),
        grid=(bsz // nb, nc),
        in_specs=[pl.BlockSpec((nb, tile, width), lambda b, c: (b, c, 0)), const(wla.shape[1:]), const(bla.shape[1:])],
        out_specs=pl.BlockSpec((nb, tile, MIX_WIDTH), lambda b, c: (b, c, 0)),
        out_shape=out_shape,
        scratch_shapes=scratch,
        compiler_params=_cparams(("parallel", "arbitrary"), 56),
        name="gla_fwd",
    )(gla_in, wla[0], bla[0])
    return pl.pallas_call(
        functools.partial(_gla_kernel, reverse=True),
        grid=(bsz // nb, nc),
        in_specs=[pl.BlockSpec((nb, tile, width), lambda b, c: (b, nc - 1 - c, 0)),
                  pl.BlockSpec((nb, tile, MIX_WIDTH), lambda b, c: (b, nc - 1 - c, 0)),
                  const(wla.shape[1:]), const(bla.shape[1:]), const(norm_g.shape)],
        out_specs=pl.BlockSpec((nb, tile, MIX_WIDTH), lambda b, c: (b, nc - 1 - c, 0)),
        out_shape=out_shape,
        scratch_shapes=scratch,
        compiler_params=_cparams(("parallel", "arbitrary"), 56),
        name="gla_bwd",
    )(gla_in, o_f, wla[1], bla[1], norm_g)


def _s5_kernel(u_ref, t_ref, e_ref, o_ref_w, mu_ref, mup_ref, y_ref):
    ch = S5_CHUNK
    sub = V7X_SUBLANES
    nrow = u_ref.shape[1] // ch
    half = S5_LANE_GROUPS * 2 * S5_STATE
    swap = lambda x: pltpu.roll(x, half // 2, 1)
    u = jnp.concatenate([u_ref[0, pl.ds(i, nrow, stride=ch), :] for i in range(ch)], axis=1).astype(BF16)
    y = _dot(u, t_ref[0])
    he = _dot(u, e_ref[0])
    row = lax.broadcasted_iota(jnp.int32, (nrow, half), 0)
    rsub = row % sub
    states = []
    for d in range(2):
        h = he[:, d * half:(d + 1) * half]
        for s in range(S5_SCAN_STEPS):
            step = 2 ** s
            if d == 0:
                hs = jnp.where(rsub >= step, pltpu.roll(h, step, 0), 0.0)
            else:
                hs = jnp.where(rsub < sub - step, pltpu.roll(h, nrow - step, 0), 0.0)
            h = h + hs * mu_ref[0, d, s, 0:1, :] + swap(hs) * mu_ref[0, d, s, 1:2, :]
        ngroup = nrow // sub
        carry = jnp.zeros((1, half), F32)
        out = [None] * ngroup
        for g in (range(ngroup) if d == 0 else range(ngroup - 1, -1, -1)):
            hg = h[g * sub:(g + 1) * sub, :] + carry * mup_ref[0, d, 0] + swap(carry) * mup_ref[0, d, 1]
            carry = hg[sub - 1:sub, :] if d == 0 else hg[0:1, :]
            out[g] = hg
        h = jnp.concatenate(out, axis=0)
        if d == 0:
            h = jnp.where(row >= 1, pltpu.roll(h, 1, 0), 0.0)
        else:
            h = jnp.where(row < nrow - 1, pltpu.roll(h, nrow - 1, 0), 0.0)
        states.append(h)
    hp = jnp.concatenate(states, axis=1).astype(BF16)
    y = y + lax.dot_general(hp, o_ref_w[0], (((1,), (1,)), ((), ())), preferred_element_type=F32)
    for j in range(ch):
        y_ref[0, pl.ds(j, nrow, stride=ch), :] = y[:, j * V7X_LANES:(j + 1) * V7X_LANES]


def _s5(s5_in, tblk, eblk, oblk, mu, mup):
    bsz, seq, width = s5_in.shape
    nb = width // V7X_LANES
    once = pl.Buffered(1)
    wspec = lambda arr: pl.BlockSpec((1,) + arr.shape[1:], lambda k, b: (k,) + (0,) * (arr.ndim - 1),
                                     pipeline_mode=once)
    xspec = pl.BlockSpec((1, seq, V7X_LANES), lambda k, b: (b, 0, k))
    return pl.pallas_call(
        _s5_kernel,
        grid=(nb, bsz),
        in_specs=[xspec, wspec(tblk), wspec(eblk), wspec(oblk), wspec(mu), wspec(mup)],
        out_specs=pl.BlockSpec((1, seq, V7X_LANES), lambda k, b: (b, 0, k)),
        out_shape=jax.ShapeDtypeStruct(s5_in.shape, F32),
        compiler_params=_cparams(("arbitrary", "arbitrary"), 60),
        name="s5",
    )(s5_in, tblk, eblk, oblk, mu, mup)


def _s5_tables(lam_re, lam_im, log_dt, b_re, b_im, c_re, c_im):
    ch = S5_CHUNK
    cmul = lambda x, y: (x[0] * y[0] - x[1] * y[1], x[0] * y[1] + x[1] * y[0])
    lr_, li_ = lam_re.astype(F32), lam_im.astype(F32)
    dt = jnp.exp(log_dt.astype(F32))[..., None]
    ar, ai = lr_ * dt, li_ * dt

    def lam_pow(tau):
        t = jnp.asarray(tau, F32)
        t = t.reshape(t.shape + (1,) * 3)
        mag = jnp.exp(t * ar)
        return mag * jnp.cos(t * ai), mag * jnp.sin(t * ai)

    lb = lam_pow(jnp.ones((), F32))
    num = (lb[0] - 1.0, lb[1])
    den = lr_ * lr_ + li_ * li_
    ratio = ((num[0] * lr_ + num[1] * li_) / den, (num[1] * lr_ - num[0] * li_) / den)
    b_bar = cmul((ratio[0][..., None], ratio[1][..., None]), (b_re.astype(F32), b_im.astype(F32)))
    cc = (c_re.astype(F32), c_im.astype(F32))

    nb, gb, hh, pp = S5_GROUPS // S5_LANE_GROUPS, S5_LANE_GROUPS, S5_GROUP, S5_STATE
    eye = jnp.eye(gb, dtype=F32)

    def b_base(x):
        x = jnp.transpose(x.reshape(2, nb, gb, pp, hh), (1, 2, 4, 0, 3))
        return (x[:, :, :, :, None, :] * eye[None, :, None, None, :, None]).reshape(nb, gb * hh, 2 * gb * pp)

    def c_base(x):
        x = jnp.transpose(x.reshape(2, nb, gb, hh, pp), (1, 2, 3, 0, 4))
        return (x[:, :, :, :, None, :] * eye[None, :, None, None, :, None]).reshape(nb, gb * hh, 2 * gb * pp)

    row = lambda x: jnp.transpose(x.reshape(2, nb, gb * pp), (1, 0, 2)).reshape(nb, 1, 2 * gb * pp)
    base = (b_base(b_bar[0]), b_base(b_bar[1]), c_base(cc[0]), c_base(cc[1]), row(ar), row(ai))
    dtab = _s5_lag_tables(*base)
    return _s5_block_tables(dtab, *base)


def _lam_pow(tau, ar, ai):
    mag = jnp.exp(tau * ar)
    return mag * jnp.cos(tau * ai), mag * jnp.sin(tau * ai)


def _s5_lag_kernel(btr_ref, bti_ref, ctr_ref, cti_ref, ar_ref, ai_ref, d_ref):
    ch = S5_CHUNK
    half = btr_ref.shape[-1] // 2
    def split(x):
        hi = x.astype(BF16)
        return hi, (x - hi.astype(F32)).astype(BF16)

    def tdot(a, b_split):
        ah, al = split(a)
        bh, bl = b_split
        return lax.dot_general(jnp.concatenate([ah, ah, al], axis=1), jnp.concatenate([bh, bl, bh], axis=1),
                               (((1,), (1,)), ((), ())), preferred_element_type=F32)

    for d in range(2):
        ls = slice(d * half, (d + 1) * half)
        btr, bti = btr_ref[0, :, ls], bti_ref[0, :, ls]
        ctr, cti = split(ctr_ref[0, :, ls]), split(cti_ref[0, :, ls])
        for lag in range(ch):
            lr, li = _lam_pow(float(lag), ar_ref[0, :, ls], ai_ref[0, :, ls])
            val = tdot(btr * lr - bti * li, ctr) - tdot(btr * li + bti * lr, cti)
            idx = ch - 1 + lag if d == 0 else ch - 1 - lag
            if d == 1 and lag == 0:
                d_ref[0, idx] = d_ref[0, idx] + val
            else:
                d_ref[0, idx] = val


def _s5_lag_tables(btr, bti, ctr, cti, ar, ai):
    nb = btr.shape[0]
    spec = lambda a: pl.BlockSpec((1,) + a.shape[1:], lambda k: (k, 0, 0))
    nlag = 2 * S5_CHUNK - 1
    return pl.pallas_call(
        _s5_lag_kernel,
        grid=(nb,),
        in_specs=[spec(a) for a in (btr, bti, ctr, cti, ar, ai)],
        out_specs=pl.BlockSpec((1, nlag, V7X_LANES, V7X_LANES), lambda k: (k, 0, 0, 0)),
        out_shape=jax.ShapeDtypeStruct((nb, nlag, V7X_LANES, V7X_LANES), F32),
        compiler_params=_cparams(("parallel",), 32),
        name="s5_lag_tables",
    )(btr, bti, ctr, cti, ar, ai)


def _s5_block_kernel(d_ref, btr_ref, bti_ref, ctr_ref, cti_ref, ar_ref, ai_ref, t_ref, e_ref, ot_ref, mu_ref,
                     mup_ref):
    ch = S5_CHUNK
    i = pl.program_id(1)
    half = btr_ref.shape[-1] // 2
    fi = i.astype(F32)
    for j in range(ch):
        t_ref[0, :, j * V7X_LANES:(j + 1) * V7X_LANES] = d_ref[0, j - i + ch - 1].astype(BF16)
    for d in range(2):
        ls = slice(d * half, (d + 1) * half)
        ar, ai = ar_ref[0, :, ls], ai_ref[0, :, ls]
        lr, li = _lam_pow(fi if d == 1 else (ch - 1.0) - fi, ar, ai)
        btr, bti = btr_ref[0, :, ls], bti_ref[0, :, ls]
        e_ref[0, :, 2 * d * half:(2 * d + 1) * half] = (btr * lr - bti * li).astype(BF16)
        e_ref[0, :, (2 * d + 1) * half:(2 * d + 2) * half] = (btr * li + bti * lr).astype(BF16)
        lr, li = _lam_pow(fi + 1.0 if d == 0 else ch - fi, ar, ai)
        ctr, cti = ctr_ref[0, :, ls], cti_ref[0, :, ls]
        ot_ref[0, :, 2 * d * half:(2 * d + 1) * half] = (ctr * lr - cti * li).astype(BF16)
        ot_ref[0, :, (2 * d + 1) * half:(2 * d + 2) * half] = (-(ctr * li + cti * lr)).astype(BF16)

    @pl.when(i == 0)
    def _():
        for d in range(2):
            ar, ai = ar_ref[0, :, d * half:(d + 1) * half], ai_ref[0, :, d * half:(d + 1) * half]
            for s in range(S5_SCAN_STEPS):
                lr, li = _lam_pow(float(ch * 2 ** s), ar, ai)
                mu_ref[0, d, s, 0:1, :] = jnp.concatenate([lr, lr], axis=1)
                mu_ref[0, d, s, 1:2, :] = jnp.concatenate([-li, li], axis=1)
            for r in range(V7X_SUBLANES):
                lr, li = _lam_pow(float(ch * (r + 1 if d == 0 else V7X_SUBLANES - r)), ar, ai)
                mup_ref[0, d, 0, r:r + 1, :] = jnp.concatenate([lr, lr], axis=1)
                mup_ref[0, d, 1, r:r + 1, :] = jnp.concatenate([-li, li], axis=1)


def _s5_block_tables(dtab, btr, bti, ctr, cti, ar, ai):
    nb = btr.shape[0]
    ch = S5_CHUNK
    big = ch * V7X_LANES
    wide = 2 * btr.shape[-1]
    spec = lambda a: pl.BlockSpec((1,) + a.shape[1:], lambda k, i: (k,) + (0,) * (a.ndim - 1))
    tile = lambda w: pl.BlockSpec((1, V7X_LANES, w), lambda k, i: (k, i, 0))
    return pl.pallas_call(
        _s5_block_kernel,
        grid=(nb, ch),
        in_specs=[spec(a) for a in (dtab, btr, bti, ctr, cti, ar, ai)],
        out_specs=[tile(big), tile(wide), tile(wide),
                   pl.BlockSpec((1, 2, S5_SCAN_STEPS, 2, wide // 2), lambda k, i: (k, 0, 0, 0, 0)),
                   pl.BlockSpec((1, 2, 2, V7X_SUBLANES, wide // 2), lambda k, i: (k, 0, 0, 0, 0))],
        out_shape=[jax.ShapeDtypeStruct((nb, big, big), BF16), jax.ShapeDtypeStruct((nb, big, wide), BF16),
                   jax.ShapeDtypeStruct((nb, big, wide), BF16),
                   jax.ShapeDtypeStruct((nb, 2, S5_SCAN_STEPS, 2, wide // 2), F32),
                   jax.ShapeDtypeStruct((nb, 2, 2, V7X_SUBLANES, wide // 2), F32)],
        compiler_params=_cparams(("parallel", "arbitrary"), 32),
        name="s5_block_tables",
    )(dtab, btr, bti, ctr, cti, ar, ai)


def _merge_kernel(x_ref, ng_ref, yaf_ref, yab_ref, yb_ref, yc_ref, y5_ref, u5_ref, d5_ref, gluw_ref, glub_ref,
                  wgate_ref, bgate_ref, wbr_ref, wout_ref, fng_ref, wr_ref, br_ref, o_ref, g_ref):
    x = x_ref[...]
    h = _rms(x, ng_ref[...]).astype(BF16)
    y_d = _gelu_tanh(u5_ref[...] * d5_ref[...] + y5_ref[...])
    y_d = y_d * _sigmoid(_dot(y_d.astype(BF16), gluw_ref[...]) + glub_ref[...])
    branches = (yaf_ref[...] + yab_ref[...], yb_ref[...], yc_ref[...], y_d)
    merged = jnp.zeros(x.shape, F32)
    for i, y in enumerate(branches):
        gate = _sigmoid(_dot(h, wgate_ref[i]) + bgate_ref[i])
        merged = merged + gate * _dot(y.astype(BF16), wbr_ref[i])
    x_new = x + _dot(merged.astype(BF16), wout_ref[...])
    o_ref[...] = x_new
    g_ref[...] = _top_group(_router_logits(x_new, fng_ref, wr_ref, br_ref)[1])


def _merge(xf, lw, yaf, yab, yb, yc, y5, u5):
    t = xf.shape[0]
    tok = lambda w: pl.BlockSpec((MERGE_TILE, w), lambda i: (i, 0))
    const = lambda arr: pl.BlockSpec(arr.shape, lambda i: (0,) * arr.ndim, pipeline_mode=pl.Buffered(1))
    weights = [lw["s5_d"], lw["s5_glu_w"], lw["s5_glu_b"], lw["w_gate"], lw["b_gate"], lw["w_branch"], lw["w_out"],
               lw["norm_ffn_g"], lw["w_router"], lw["b_router"]]
    return pl.pallas_call(
        _merge_kernel,
        grid=(t // MERGE_TILE,),
        in_specs=[tok(D_MODEL), const(lw["norm_mix_g"])] + [tok(MIX_WIDTH)] * 6 + [const(w) for w in weights],
        out_specs=[tok(D_MODEL), tok(1)],
        out_shape=[jax.ShapeDtypeStruct((t, D_MODEL), F32), jax.ShapeDtypeStruct((t, 1), jnp.int32)],
        compiler_params=_cparams(("parallel",), 56),
        name="merge",
    )(xf, lw["norm_mix_g"], yaf, yab, yb, yc, y5, u5, *weights)


def _router_logits(x, ng_ref, wr_ref, br_ref):
    hf = _rms(x, ng_ref[...])
    return hf, _dot3(hf, wr_ref[...]) + br_ref[...]


def _top_group(logits):
    lane = lax.broadcasted_iota(jnp.int32, logits.shape, 1).astype(F32)
    gl = jnp.where(lane < MOE_GROUPS, logits, -jnp.inf)
    gmax = jnp.max(gl, axis=1, keepdims=True)
    return jnp.min(jnp.where(gl == gmax, lane, float(V7X_LANES)), axis=1, keepdims=True).astype(jnp.int32)


def _moe_plan_kernel(pos_ref, src_ref):
    def clear(p, c):
        src_ref[p] = 0
        return c

    def place(t, c):
        src_ref[pos_ref[t]] = t
        return c

    lax.fori_loop(0, src_ref.shape[0], clear, 0, unroll=16)
    lax.fori_loop(0, pos_ref.shape[0], place, 0, unroll=16)


def _moe_plan(gidx, tile):
    t = gidx.shape[0]
    ntile = t // tile + MOE_GROUPS
    g = gidx.reshape(t)
    onehot = (g[:, None] == jnp.arange(MOE_GROUPS, dtype=jnp.int32)[None, :]).astype(jnp.int32)
    csum = jnp.cumsum(onehot, axis=0)
    rank = jnp.sum(onehot * (csum - 1), axis=1)
    count = csum[-1]
    gtiles = (count + tile - 1) // tile
    first = jnp.cumsum(gtiles) - gtiles
    pos = jnp.sum(onehot * first[None, :], axis=1) * tile + rank
    tid = jnp.arange(ntile, dtype=jnp.int32)
    tgroup = jnp.minimum(jnp.sum((tid[:, None] >= (first + gtiles)[None, :]).astype(jnp.int32), axis=1),
                         MOE_GROUPS - 1)
    oh_t = (tgroup[:, None] == jnp.arange(MOE_GROUPS, dtype=jnp.int32)[None, :]).astype(jnp.int32)
    tvalid = jnp.clip(jnp.sum(oh_t * count[None, :], axis=1) - (tid - jnp.sum(oh_t * first[None, :], axis=1)) * tile,
                      0, tile)
    src = pl.pallas_call(
        _moe_plan_kernel,
        in_specs=[pl.BlockSpec(memory_space=pltpu.SMEM)],
        out_specs=pl.BlockSpec(memory_space=pltpu.SMEM),
        out_shape=jax.ShapeDtypeStruct((ntile * tile,), jnp.int32),
        name="moe_plan",
    )(pos.astype(jnp.int32))
    return src, tgroup.astype(jnp.int32), tvalid.astype(jnp.int32)


def _moe_expert_kernel(src_ref, tg_ref, nv_ref, x_hbm, ng_ref, wr_ref, br_ref, w13_ref, w2_ref, fg_ref, o_hbm,
                       xbuf, ybuf, sem, *, final):
    i = pl.program_id(0)
    nt = pl.num_programs(0)
    tile = xbuf.shape[1]
    slot = i % 2
    nv = nv_ref[i]
    ng, ne, ff = MOE_GROUPS, MOE_EXPERTS, MOE_FF

    def rows(j, wait, row, whole):
        count = nv_ref[j]

        @pl.when(count == tile)
        def _():
            if wait:
                whole().wait()
            else:
                def body(r, c):
                    row(j * tile + r, r).start()
                    return c
                lax.fori_loop(0, tile, body, 0, unroll=16)

        @pl.when((count > 0) & (count < tile))
        def _():
            def body(r, c):
                cp = row(j * tile + r, r)
                cp.wait() if wait else cp.start()
                return c
            lax.fori_loop(0, count, body, 0)

    def gather(j, s, wait):
        rows(j, wait,
             lambda p, r: pltpu.make_async_copy(x_hbm.at[pl.ds(src_ref[p], 1), :], xbuf.at[s, pl.ds(r, 1), :],
                                                sem.at[0, s]),
             lambda: pltpu.make_async_copy(x_hbm.at[pl.ds(0, tile), :], xbuf.at[s], sem.at[0, s]))

    def scatter(j, s, wait):
        rows(j, wait,
             lambda p, r: pltpu.make_async_copy(ybuf.at[s, pl.ds(r, 1), :], o_hbm.at[pl.ds(src_ref[p], 1), :],
                                                sem.at[1, s]),
             lambda: pltpu.make_async_copy(ybuf.at[s], o_hbm.at[pl.ds(0, tile), :], sem.at[1, s]))

    @pl.when(i == 0)
    def _():
        xbuf[...] = jnp.zeros_like(xbuf)
        gather(0, 0, False)

    @pl.when(i + 1 < nt)
    def _():
        gather(i + 1, 1 - slot, False)

    @pl.when(i >= 2)
    def _():
        scatter(i - 2, slot, True)

    gather(i, slot, True)

    @pl.when(nv > 0)
    def _():
        x = xbuf[slot]
        hf, logits = _router_logits(x, ng_ref, wr_ref, br_ref)
        lane = lax.broadcasted_iota(jnp.int32, logits.shape, 1).astype(F32)
        neg = -jnp.inf
        big = float(V7X_LANES)
        grp = tg_ref[i].astype(F32)
        gl = jnp.where(lane < ng, logits, neg)
        gmax = jnp.max(gl, axis=1, keepdims=True)
        glog = jnp.sum(jnp.where(lane == grp, logits, 0.0), axis=1, keepdims=True)
        gprob = jnp.exp(glog - gmax) / jnp.sum(jnp.exp(gl - gmax), axis=1, keepdims=True)
        lo = ng + ne * grp
        sel = (lane >= lo) & (lane < lo + ne)
        m1 = jnp.max(jnp.where(sel, logits, neg), axis=1, keepdims=True)
        i1 = jnp.min(jnp.where(sel & (logits == m1), lane, big), axis=1, keepdims=True)
        sel2 = sel & (lane != i1)
        m2 = jnp.max(jnp.where(sel2, logits, neg), axis=1, keepdims=True)
        i2 = jnp.min(jnp.where(sel2 & (logits == m2), lane, big), axis=1, keepdims=True)
        e2 = jnp.exp(m2 - m1)
        w1 = gprob / (1.0 + e2)
        w2 = gprob * e2 / (1.0 + e2)
        gu = _dot(hf.astype(BF16), w13_ref[0])
        act = _silu(gu[:, 0:ne * ff]) * gu[:, ne * ff:2 * ne * ff]
        parts = []
        for e in range(ne):
            wcol = jnp.where(i1 == lo + e, w1, 0.0) + jnp.where(i2 == lo + e, w2, 0.0)
            parts.append(act[:, e * ff:(e + 1) * ff] * wcol)
        y = x + _dot(jnp.concatenate(parts, axis=1).astype(BF16), w2_ref[0])
        if final:
            y = _rms(y, fg_ref[...])
        ybuf[slot] = y
        scatter(i, slot, False)

    @pl.when(i == nt - 1)
    def _():
        @pl.when(i >= 1)
        def _():
            scatter(i - 1, 1 - slot, True)
        scatter(i, slot, True)


def _moe(xf, gidx, lw, final_g, final):
    t = xf.shape[0]
    tile = MOE_TILE
    src, tgroup, tvalid = _moe_plan(gidx, tile)
    ntile = tgroup.shape[0]
    const = lambda arr: pl.BlockSpec(arr.shape, lambda i, s, g, n: (0,) * arr.ndim)
    bygroup = lambda arr: pl.BlockSpec((1,) + arr.shape[1:], lambda i, s, g, n: (g[i],) + (0,) * (arr.ndim - 1))
    any_space = pl.BlockSpec(memory_space=pl.ANY)
    return pl.pallas_call(
        functools.partial(_moe_expert_kernel, final=final),
        grid_spec=pltpu.PrefetchScalarGridSpec(
            num_scalar_prefetch=3,
            grid=(ntile,),
            in_specs=[any_space, const(lw["norm_ffn_g"]), const(lw["w_router"]), const(lw["b_router"]),
                      bygroup(lw["w_e_13"]), bygroup(lw["w_e_2"]), const(final_g)],
            out_specs=any_space,
            scratch_shapes=[pltpu.VMEM((2, tile, D_MODEL), F32), pltpu.VMEM((2, tile, D_MODEL), F32),
                            pltpu.SemaphoreType.DMA((2, 2))],
        ),
        out_shape=jax.ShapeDtypeStruct((t, D_MODEL), F32),
        compiler_params=_cparams(("arbitrary",), 40),
        name="moe_experts",
    )(src, tgroup, tvalid, xf, lw["norm_ffn_g"], lw["w_router"], lw["b_router"], lw["w_e_13"], lw["w_e_2"], final_g)


def _block_diag(blocks):
    nb, bs, _ = blocks.shape
    eye = jnp.eye(nb, dtype=blocks.dtype)
    return jnp.einsum("nij,nm->nimj", blocks, eye).reshape(nb * bs, nb * bs)


def _pad_heads(w, axis=-1):
    shape = w.shape[:-1] + (GLA_HEADS, GLA_DK)
    w = w.reshape(shape)
    pad = [(0, 0)] * (w.ndim - 1) + [(0, V7X_LANES - GLA_DK)]
    return jnp.pad(w, pad).reshape(w.shape[:-2] + (GLA_HEADS * V7X_LANES,))


def _prep_layer(w, l):
    f = lambda name: w[name][l]
    w_in = f("w_in")
    cuts = np.cumsum([MIX_WIDTH, MIX_WIDTH, 3 * MIX_WIDTH, GLA_HEADS * GLA_DK, GLA_HEADS * GLA_DK, MIX_WIDTH,
                      MIX_WIDTH, 2 * GLA_RANK]).tolist()
    xa, ga, hy, q, k, v, g, lr, s5 = jnp.split(w_in, cuts, axis=-1)
    lr = jnp.pad(lr, ((0, 0), (0, V7X_LANES - 2 * GLA_RANK)))
    w_pack = jnp.concatenate([xa, ga, hy, _pad_heads(q), _pad_heads(k), v, g, lr, s5], axis=-1).astype(BF16)
    assert w_pack.shape[1] == N_PACK
    lw = {"w_pack": w_pack, "norm_mix_g": f("norm_mix_g")[None]}
    lw["lru_conv_w"] = f("lru_conv_w")
    lw["lru_conv_b"] = f("lru_conv_b")[None]
    wa, wx = f("lru_wa"), f("lru_wx")
    lw["lru_wg"] = jnp.stack([jnp.concatenate([_block_diag(wa[d]), _block_diag(wx[d])], axis=1)
                              for d in range(2)]).astype(BF16)
    lw["lru_bg"] = jnp.concatenate([f("lru_ba"), f("lru_bx")], axis=-1)[:, None, :]
    lw["lru_lam"] = f("lru_lambda")[:, None, :]
    lw["hy_conv_w"] = f("hy_conv_w")
    lw["hy_conv_b"] = f("hy_conv_b")[None]
    lw["hy_w1p"] = _split3(jnp.pad(f("hy_w1"), ((0, V7X_LANES - HY_EMB), (0, 0))))
    lw["hy_b1"] = f("hy_b1")[None]
    lw["hy_w2"] = _split3(f("hy_w2"))
    lw["hy_b2"] = f("hy_b2")[None]
    lw["hy_w3"] = _split3(f("hy_w3"))
    lw["hy_freq"] = f("hy_freq")[None]
    lw["hy_bias"] = f("hy_bias")
    wg2 = _pad_heads(f("gla_wg2"))
    wla = jnp.zeros((2, V7X_LANES, GLA_HEADS * V7X_LANES), F32)
    wla = wla.at[0, 0:GLA_RANK].set(wg2[0]).at[1, GLA_RANK:2 * GLA_RANK].set(wg2[1])
    lw["gla_wla"] = wla.astype(BF16)
    lw["gla_bla"] = _pad_heads(f("gla_bg"))[:, None, :]
    lw["gla_norm_g"] = jnp.tile(f("gla_norm_g"), GLA_HEADS)[None]
    lw["s5_tables"] = _s5_tables(f("s5_lam_re"), f("s5_lam_im"), f("s5_log_dt"), f("s5_b_re"), f("s5_b_im"),
                                 f("s5_c_re"), f("s5_c_im"))
    lw["s5_d"] = f("s5_d")[None]
    lw["s5_glu_w"] = f("s5_glu_w").astype(BF16)
    lw["s5_glu_b"] = f("s5_glu_b")[None]
    lw["w_gate"] = f("w_gate").astype(BF16)
    lw["b_gate"] = f("b_gate")[:, None, :]
    lw["w_branch"] = f("w_branch").astype(BF16)
    lw["w_out"] = f("w_out").astype(BF16)
    wr = jnp.concatenate([f("w_router_group"), jnp.transpose(f("w_router_expert"), (1, 0, 2)).reshape(D_MODEL, -1)],
                         axis=1)
    br = jnp.concatenate([f("b_router_group"), f("b_router_expert").reshape(-1)])
    nr = MOE_GROUPS + MOE_GROUPS * MOE_EXPERTS
    lw["w_router"] = _split3(jnp.pad(wr, ((0, 0), (0, V7X_LANES - nr))))
    lw["b_router"] = jnp.pad(br, (0, V7X_LANES - nr))[None]
    lw["norm_ffn_g"] = f("norm_ffn_g")[None]
    wide = lambda a: jnp.transpose(a, (0, 2, 1, 3)).reshape(MOE_GROUPS, D_MODEL, MOE_EXPERTS * MOE_FF)
    lw["w_e_13"] = jnp.concatenate([wide(f("w_e_gate")), wide(f("w_e_up"))], axis=-1).astype(BF16)
    lw["w_e_2"] = f("w_e_down").reshape(MOE_GROUPS, MOE_EXPERTS * MOE_FF, D_MODEL).astype(BF16)
    return lw


def _encoder(x, layers, final_g):
    bsz, seq, _ = x.shape
    plan = _FftPlan(seq)
    xf = x.reshape(bsz * seq, D_MODEL)
    for l, lw in enumerate(layers):
        lru_in, hy_in, gla_in, s5_in = _inproj(xf, lw["norm_mix_g"], lw["w_pack"], lw["hy_conv_w"],
                                               lw["hy_conv_b"], seq)
        shp = lambda a: a.reshape(bsz, seq, a.shape[-1])
        yaf, yab = _lru(shp(lru_in), lw["lru_conv_w"], lw["lru_conv_b"], lw["lru_wg"], lw["lru_bg"], lw["lru_lam"])
        yb = _hyena(shp(hy_in), lw, plan)
        yc = _gla(shp(gla_in), lw["gla_wla"], lw["gla_bla"], lw["gla_norm_g"])
        y5 = _s5(shp(s5_in), *lw["s5_tables"])
        flat = lambda a: a.reshape(bsz * seq, MIX_WIDTH)
        xf, gidx = _merge(xf, lw, flat(yaf), flat(yab), flat(yb), flat(yc), flat(y5), s5_in)
        xf = _moe(xf, gidx, lw, final_g, final=(l == len(layers) - 1))
    return xf.reshape(bsz, seq, D_MODEL)


def kernel(x_prompt, x_sample, norm_mix_g, w_in, lru_conv_w, lru_conv_b, lru_wa, lru_ba, lru_wx, lru_bx,
           lru_lambda, hy_conv_w, hy_conv_b, hy_w1, hy_b1, hy_w2, hy_b2, hy_w3, hy_freq, hy_bias,
           gla_wg2, gla_bg, gla_norm_g, s5_lam_re, s5_lam_im, s5_log_dt, s5_b_re, s5_b_im, s5_c_re, s5_c_im,
           s5_d, s5_glu_w, s5_glu_b, w_branch, w_gate, b_gate, w_out, norm_ffn_g, w_router_group,
           b_router_group, w_router_expert, b_router_expert, w_e_gate, w_e_up, w_e_down, final_norm_g):
    w = dict(norm_mix_g=norm_mix_g, w_in=w_in, lru_conv_w=lru_conv_w, lru_conv_b=lru_conv_b, lru_wa=lru_wa,
             lru_ba=lru_ba, lru_wx=lru_wx, lru_bx=lru_bx, lru_lambda=lru_lambda, hy_conv_w=hy_conv_w,
             hy_conv_b=hy_conv_b, hy_w1=hy_w1, hy_b1=hy_b1, hy_w2=hy_w2, hy_b2=hy_b2, hy_w3=hy_w3,
             hy_freq=hy_freq, hy_bias=hy_bias, gla_wg2=gla_wg2, gla_bg=gla_bg, gla_norm_g=gla_norm_g,
             s5_lam_re=s5_lam_re, s5_lam_im=s5_lam_im, s5_log_dt=s5_log_dt, s5_b_re=s5_b_re, s5_b_im=s5_b_im,
             s5_c_re=s5_c_re, s5_c_im=s5_c_im, s5_d=s5_d, s5_glu_w=s5_glu_w, s5_glu_b=s5_glu_b,
             w_branch=w_branch, w_gate=w_gate, b_gate=b_gate, w_out=w_out, norm_ffn_g=norm_ffn_g,
             w_router_group=w_router_group, b_router_group=b_router_group, w_router_expert=w_router_expert,
             b_router_expert=b_router_expert, w_e_gate=w_e_gate, w_e_up=w_e_up, w_e_down=w_e_down)
    layers = [_prep_layer(w, l) for l in range(norm_mix_g.shape[0])]
    fg = final_norm_g[None]
    return (_encoder(x_prompt, layers, fg), _encoder(x_sample, layers, fg))
```
